```python
import jax, jax.numpy as jnp
from jax import lax
import numpy as np

D_MODEL = 1024
BATCH = 4
SEQ = 4096
DEPTH = 2
DEC_BATCH = 128
DEC_SEQ = 4
PAST_LEN = 16384
PAGE_SIZE = 128

F32 = jnp.float32
HEAD_DIM = 64
ROT_DIM = HEAD_DIM // 4
ROPE_THETA = 500000.0
BLOCK = 128
A_Q_HEADS = 8
A_KV_HEADS = 2
A_GROUP = A_Q_HEADS // A_KV_HEADS
A_WINDOW = 128
B_DILATIONS = ((128, 1), (512, 4), (2048, 16))
N_B_GROUPS = 3
B_Q_HEADS = 8
B_KV_HEADS = 2
B_GROUP = B_Q_HEADS // B_KV_HEADS
A_QW = A_Q_HEADS * HEAD_DIM
A_KW = A_KV_HEADS * HEAD_DIM
B_QW = N_B_GROUPS * B_Q_HEADS * HEAD_DIM
B_KW = N_B_GROUPS * B_KV_HEADS * HEAD_DIM
MIX0 = A_QW + B_Q_HEADS * HEAD_DIM
IN0 = A_QW + 2 * A_KW + B_QW + 2 * B_KW + MIX0
C_HEADS = 4
C_HEAD_DIM = 256
MIX1 = C_HEADS * C_HEAD_DIM
IN1 = 5 * MIX1 + 2 * C_HEADS
CHUNK = 128
DN_ALPHA = (2 * DEPTH) ** 0.25
DN_BETA = (8 * DEPTH) ** -0.25
LN_EPS = 1e-5
MH_EPS = 1e-6
NEG_INF = -1e30
N_EVEN = (DEPTH + 1) // 2
N_ODD = DEPTH // 2

kernel_name = 'hybrid_swa_dilated_mlstm_deepnorm_step'


def layer_norm(x, g, b):
    xf = x.astype(F32)
    mu = xf.mean(-1, keepdims=True)
    var = jnp.square(xf - mu).mean(-1, keepdims=True)
    return ((xf - mu) * lax.rsqrt(var + LN_EPS) * g.astype(F32) + b.astype(F32)).astype(x.dtype)


def rope_partial(x, pos):
    half = ROT_DIM // 2
    inv = ROPE_THETA ** (-jnp.arange(half, dtype=F32) / half)
    ang = pos.astype(F32)[:, None] * inv[None, :]
    cos = jnp.cos(ang)[None, :, None, :]
    sin = jnp.sin(ang)[None, :, None, :]
    xr = x[..., :ROT_DIM].astype(F32)
    x1, x2 = xr[..., :half], xr[..., half:]
    rot = jnp.concatenate([x1 * cos - x2 * sin, x2 * cos + x1 * sin], axis=-1).astype(x.dtype)
    return jnp.concatenate([rot, x[..., ROT_DIM:]], axis=-1)


def _softmax_stats(s, mask, sink):
    s = jnp.where(mask, s, NEG_INF)
    m = s.max(-1)
    if sink is not None:
        m = jnp.maximum(m, sink)
    p = jnp.exp(s - m[..., None])
    den = p.sum(-1)
    if sink is not None:
        den = den + jnp.exp(sink - m)
    return p, den, m + jnp.log(den)


def banded_attention(q, k, v, span, sink=None):
    N, n = q.shape[0], q.shape[1]
    Hk, G, dh = q.shape[2], q.shape[3], q.shape[4]
    pad = (-n) % BLOCK
    if pad:
        q = jnp.pad(q, ((0, 0), (0, pad), (0, 0), (0, 0), (0, 0)))
        k = jnp.pad(k, ((0, 0), (0, pad), (0, 0), (0, 0)))
        v = jnp.pad(v, ((0, 0), (0, pad), (0, 0), (0, 0)))
    nb = (n + pad) // BLOCK
    qb = q.reshape(N, nb, BLOCK, Hk, G, dh)

    def with_prev(t):
        tb = t.reshape(N, nb, BLOCK, Hk, dh)
        prev = jnp.pad(tb, ((0, 0), (1, 0), (0, 0), (0, 0), (0, 0)))[:, :nb]
        return jnp.concatenate([prev, tb], axis=2)

    kk, vv = with_prev(k), with_prev(v)
    s = jnp.einsum('nbqhgd,nbkhd->nbhgqk', qb, kk, preferred_element_type=F32)
    qi = jnp.arange(BLOCK)[:, None] + BLOCK
    ki = jnp.arange(2 * BLOCK)[None, :]
    dist = qi - ki
    band = (dist >= 0) & (dist <= span)
    has_prev = (jnp.arange(nb)[:, None, None] > 0) | (ki[None] >= BLOCK)
    mask = (band[None] & has_prev)[None, :, None, None]
    sink_b = None if sink is None else sink.astype(F32)[None, None, :, :, None]
    p, den, lse = _softmax_stats(s, mask, sink_b)
    o = jnp.einsum('nbhgqk,nbkhd->nbqhgd', p, vv.astype(F32))
    o = o / jnp.moveaxis(den, -1, 2)[..., None]
    lse = jnp.moveaxis(lse, -1, 2)
    o = o.reshape(N, nb * BLOCK, Hk, G, dh)[:, :n]
    lse = lse.reshape(N, nb * BLOCK, Hk, G)[:, :n]
    return o, lse


def strided_gather_attention(q, k_all, v_all, n_past, span, dilation, sink=None):
    T = q.shape[1]
    idx = n_past + jnp.arange(T)[:, None] - dilation * jnp.arange(span + 1)[None, :]
    valid = idx >= 0
    idx = jnp.maximum(idx, 0)
    kg = k_all[:, idx]
    vg = v_all[:, idx]
    s = jnp.einsum('nthgd,ntjhd->nthgj', q, kg, preferred_element_type=F32)
    sink_b = None if sink is None else sink.astype(F32)[None, None]
    p, den, lse = _softmax_stats(s, valid[None, :, None, None, :], sink_b)
    o = jnp.einsum('nthgj,ntjhd->nthgd', p, vg.astype(F32)) / den[..., None]
    return o, lse


def _to_sub(t, dil):
    Bn, S = t.shape[0], t.shape[1]
    rest = t.shape[2:]
    return jnp.moveaxis(t.reshape((Bn, S // dil, dil) + rest), 2, 1).reshape((Bn * dil, S // dil) + rest)


def _from_sub(t, Bn, dil):
    n = t.shape[1]
    rest = t.shape[2:]
    return jnp.moveaxis(t.reshape((Bn, dil, n) + rest), 1, 2).reshape((Bn, n * dil) + rest)


def _layer0_project(x, pos, w_in):
    Bn, S, _ = x.shape
    h = jnp.einsum('bsd,de->bse', x, w_in)
    c1 = A_QW
    c2 = c1 + A_KW
    c3 = c2 + A_KW
    c4 = c3 + B_QW
    c5 = c4 + B_KW
    c6 = c5 + B_KW
    qa, ka, va, qb, kb, vb, z = jnp.split(h, [c1, c2, c3, c4, c5, c6], axis=-1)
    scale = HEAD_DIM ** -0.5
    qa = (rope_partial(qa.reshape(Bn, S, A_Q_HEADS, HEAD_DIM), pos) * scale).reshape(Bn, S, A_KV_HEADS, A_GROUP, HEAD_DIM)
    ka = rope_partial(ka.reshape(Bn, S, A_KV_HEADS, HEAD_DIM), pos)
    va = va.reshape(Bn, S, A_KV_HEADS, HEAD_DIM)
    qb = (rope_partial(qb.reshape(Bn, S, N_B_GROUPS * B_Q_HEADS, HEAD_DIM), pos) * scale).reshape(
        Bn, S, N_B_GROUPS, B_KV_HEADS, B_GROUP, HEAD_DIM)
    kb = rope_partial(kb.reshape(Bn, S, N_B_GROUPS * B_KV_HEADS, HEAD_DIM), pos).reshape(
        Bn, S, N_B_GROUPS, B_KV_HEADS, HEAD_DIM)
    vb = vb.reshape(Bn, S, N_B_GROUPS, B_KV_HEADS, HEAD_DIM)
    return qa, ka, va, qb, kb, vb, z


def _layer0_out(x, oa, obs, lses, z, w_out, ln_g, ln_b):
    Bn, S, _ = x.shape
    wts = jax.nn.softmax(jnp.stack(lses, axis=0), axis=0)
    ob = jnp.einsum('rbshg,rbshgd->bshgd', wts, jnp.stack(obs, axis=0))
    mix = jnp.concatenate([oa.reshape(Bn, S, A_QW), ob.reshape(Bn, S, MIX0 - A_QW)], axis=-1)
    mix = (mix * jax.nn.silu(z.astype(F32))).astype(x.dtype)
    y = jnp.einsum('bse,ed->bsd', mix, w_out)
    return layer_norm(DN_ALPHA * x + y, ln_g, ln_b)


def layer0_prompt(x, w_in, sinks, w_out, ln_g, ln_b):
    Bn, S, _ = x.shape
    pos = jnp.arange(S)
    qa, ka, va, qb, kb, vb, z = _layer0_project(x, pos, w_in)
    oa, _ = banded_attention(qa, ka, va, A_WINDOW, sinks.reshape(A_KV_HEADS, A_GROUP))
    la = min(A_WINDOW, S)
    a_state = jnp.stack([ka[:, S - la:], va[:, S - la:]], axis=2)
    obs, lses, b_states = [], [], []
    for g in range(N_B_GROUPS):
        win, dil = B_DILATIONS[g]
        o, lse = banded_attention(_to_sub(qb[:, :, g], dil), _to_sub(kb[:, :, g], dil),
                                  _to_sub(vb[:, :, g], dil), win // dil)
        obs.append(_from_sub(o, Bn, dil))
        lses.append(_from_sub(lse, Bn, dil))
        lg = min(win, S)
        b_states.append(jnp.stack([kb[:, S - lg:, g], vb[:, S - lg:, g]], axis=2))
    y = _layer0_out(x, oa, obs, lses, z, w_out, ln_g, ln_b)
    return y, a_state, b_states


def layer0_sample(x, cache_a, caches_b, w_in, sinks, w_out, ln_g, ln_b):
    Bn, T, _ = x.shape
    pos = PAST_LEN + jnp.arange(T)
    qa, ka, va, qb, kb, vb, z = _layer0_project(x, pos, w_in)
    kv_a = jnp.concatenate([cache_a.astype(ka.dtype), jnp.stack([ka, va], axis=2)], axis=1)
    oa, _ = strided_gather_attention(qa, kv_a[:, :, 0], kv_a[:, :, 1], cache_a.shape[1], A_WINDOW, 1,
                                     sinks.reshape(A_KV_HEADS, A_GROUP))
    a_state = kv_a[:, T:]
    obs, lses, b_states = [], [], []
    for g in range(N_B_GROUPS):
        win, dil = B_DILATIONS[g]
        cache = caches_b[g]
        kv_b = jnp.concatenate([cache.astype(kb.dtype), jnp.stack([kb[:, :, g], vb[:, :, g]], axis=2)], axis=1)
        o, lse = strided_gather_attention(qb[:, :, g], kv_b[:, :, 0], kv_b[:, :, 1], cache.shape[1], win // dil, dil)
        obs.append(o)
        lses.append(lse)
        b_states.append(kv_b[:, T:])
    y = _layer0_out(x, oa, obs, lses, z, w_out, ln_g, ln_b)
    return y, a_state, b_states


def _layer1_project(x, w_in, b_gates):
    Bn, S, _ = x.shape
    h = jnp.einsum('bsd,de->bse', x, w_in)
    q, k, v, o, z, gates = jnp.split(h, [MIX1, 2 * MIX1, 3 * MIX1, 4 * MIX1, 5 * MIX1], axis=-1)
    gates = gates.astype(F32) + b_gates.astype(F32)
    ig = gates[..., :C_HEADS]
    lf = jax.nn.log_sigmoid(gates[..., C_HEADS:])
    q = q.reshape(Bn, S, C_HEADS, C_HEAD_DIM).astype(F32)
    k = k.reshape(Bn, S, C_HEADS, C_HEAD_DIM).astype(F32) * (C_HEAD_DIM ** -0.5)
    v = v.reshape(Bn, S, C_HEADS, C_HEAD_DIM).astype(F32)
    return q, k, v, ig, lf, o, z


def _mlstm_chunk(carry, inp):
    c_mem, c_norm, c_max = carry
    q, k, v, ig, lf = inp
    L = q.shape[1]
    b = jnp.cumsum(lf, axis=1)
    causal = jnp.tril(jnp.ones((L, L), dtype=bool))[None, :, :, None]
    dmat = jnp.where(causal, b[:, :, None, :] - b[:, None, :, :] + ig[:, None, :, :], NEG_INF)
    inter = b + c_max[:, None, :]
    m = jnp.maximum(inter, dmat.max(axis=2))
    pw = jnp.exp(dmat - m[:, :, None, :])
    sc = jnp.einsum('bthd,bshd->btsh', q, k) * pw
    carry_w = jnp.exp(inter - m)
    num = jnp.einsum('btsh,bshd->bthd', sc, v) + carry_w[..., None] * jnp.einsum('bthk,bhkv->bthv', q, c_mem)
    den = sc.sum(axis=2) + carry_w * jnp.einsum('bthk,bhk->bth', q, c_norm)
    h = num / jnp.maximum(jnp.abs(den), jnp.exp(-m))[..., None]
    m_new = m[:, -1]
    ws = jnp.exp(b[:, -1:, :] - b + ig - m_new[:, None, :])
    decay = jnp.exp(b[:, -1] + c_max - m_new)
    c_mem_new = decay[..., None, None] * c_mem + jnp.einsum('bsh,bshk,bshv->bhkv', ws, k, v)
    c_norm_new = decay[..., None] * c_norm + jnp.einsum('bsh,bshk->bhk', ws, k)
    return (c_mem_new, c_norm_new, m_new), h


def _chunks(t):
    return jnp.moveaxis(t.reshape((t.shape[0], t.shape[1] // CHUNK, CHUNK) + t.shape[2:]), 1, 0)


def _layer1_out(x, h, o, z, mh_g, w_out, ln_g, ln_b):
    Bn, S, _ = x.shape
    h = jax.nn.sigmoid(o.astype(F32)).reshape(Bn, S, C_HEADS, C_HEAD_DIM) * h
    mu = h.mean(-1, keepdims=True)
    var = jnp.square(h - mu).mean(-1, keepdims=True)
    hn = ((h - mu) * lax.rsqrt(var + MH_EPS)).reshape(Bn, S, MIX1) * mh_g.astype(F32)
    mix = (hn * jax.nn.silu(z.astype(F32))).astype(x.dtype)
    y = jnp.einsum('bse,ed->bsd', mix, w_out)
    return layer_norm(DN_ALPHA * x + y, ln_g, ln_b)


def layer1_prompt(x, w_in, b_gates, mh_g, w_out, ln_g, ln_b):
    Bn, S, _ = x.shape
    q, k, v, ig, lf, o, z = _layer1_project(x, w_in, b_gates)
    init = (jnp.zeros((Bn, C_HEADS, C_HEAD_DIM, C_HEAD_DIM), F32),
            jnp.zeros((Bn, C_HEADS, C_HEAD_DIM), F32),
            jnp.zeros((Bn, C_HEADS), F32))
    state, h = lax.scan(_mlstm_chunk, init, (_chunks(q), _chunks(k), _chunks(v), _chunks(ig), _chunks(lf)))
    h = jnp.moveaxis(h, 0, 1).reshape(Bn, S, C_HEADS, C_HEAD_DIM)
    y = _layer1_out(x, h, o, z, mh_g, w_out, ln_g, ln_b)
    return y, state


def layer1_sample(x, c_mem, c_norm, c_max, w_in, b_gates, mh_g, w_out, ln_g, ln_b):
    q, k, v, ig, lf, o, z = _layer1_project(x, w_in, b_gates)
    state, h = _mlstm_chunk((c_mem.astype(F32), c_norm.astype(F32), c_max.astype(F32)), (q, k, v, ig, lf))
    y = _layer1_out(x, h, o, z, mh_g, w_out, ln_g, ln_b)
    return y, state


def setup_inputs(seed: int = 0) -> dict:
    key = jax.random.key(seed)
    ks = jax.random.split(key, 20)
    nrm = jax.random.normal
    x_prompt = nrm(ks[0], (BATCH, SEQ, D_MODEL), F32)
    x_sample = nrm(ks[1], (DEC_BATCH, DEC_SEQ, D_MODEL), F32)
    cache_a_kv = nrm(ks[2], (N_EVEN, DEC_BATCH, min(A_WINDOW, PAST_LEN), 2, A_KV_HEADS, HEAD_DIM), F32)
    cache_b0_kv = nrm(ks[3], (N_EVEN, DEC_BATCH, min(B_DILATIONS[0][0], PAST_LEN), 2, B_KV_HEADS, HEAD_DIM), F32)
    cache_b1_kv = nrm(ks[4], (N_EVEN, DEC_BATCH, min(B_DILATIONS[1][0], PAST_LEN), 2, B_KV_HEADS, HEAD_DIM), F32)
    cache_b2_kv = nrm(ks[5], (N_EVEN, DEC_BATCH, min(B_DILATIONS[2][0], PAST_LEN), 2, B_KV_HEADS, HEAD_DIM), F32)
    state_c_mem = 0.05 * nrm(ks[6], (N_ODD, DEC_BATCH, C_HEADS, C_HEAD_DIM, C_HEAD_DIM), F32)
    state_c_norm = 0.2 * nrm(ks[7], (N_ODD, DEC_BATCH, C_HEADS, C_HEAD_DIM), F32)
    state_c_max = nrm(ks[8], (N_ODD, DEC_BATCH, C_HEADS), F32)
    col0 = np.ones((IN0,), np.float32)
    va0 = A_QW + A_KW
    col0[va0:va0 + A_KW] = DN_BETA
    vb0 = A_QW + 2 * A_KW + B_QW + B_KW
    col0[vb0:vb0 + B_KW] = DN_BETA
    w_in0 = nrm(ks[9], (N_EVEN, D_MODEL, IN0), F32) * (D_MODEL ** -0.5) * jnp.asarray(col0)
    sinks0 = 0.5 * nrm(ks[10], (N_EVEN, A_Q_HEADS), F32)
    w_out0 = nrm(ks[11], (N_EVEN, MIX0, D_MODEL), F32) * (MIX0 ** -0.5 * DN_BETA)
    col1 = np.ones((IN1,), np.float32)
    col1[2 * MIX1:3 * MIX1] = DN_BETA
    w_in1 = nrm(ks[12], (N_ODD, D_MODEL, IN1), F32) * (D_MODEL ** -0.5) * jnp.asarray(col1)
    ig_bias = 0.1 * nrm(ks[13], (N_ODD, C_HEADS), F32)
    fg_bias = jnp.linspace(3.0, 6.0, C_HEADS, dtype=F32)[None] + 0.01 * nrm(ks[14], (N_ODD, C_HEADS), F32)
    b_gates1 = jnp.concatenate([ig_bias, fg_bias], axis=-1)
    mh_norm1 = 1.0 + 0.02 * nrm(ks[15], (N_ODD, MIX1), F32)
    w_out1 = nrm(ks[16], (N_ODD, MIX1, D_MODEL), F32) * (MIX1 ** -0.5 * DN_BETA)
    ln_g = 1.0 + 0.02 * nrm(ks[17], (DEPTH, D_MODEL), F32)
    ln_b = 0.02 * nrm(ks[18], (DEPTH, D_MODEL), F32)
    return {'x_prompt': x_prompt, 'x_sample': x_sample,
            'cache_a_kv': cache_a_kv, 'cache_b0_kv': cache_b0_kv, 'cache_b1_kv': cache_b1_kv,
            'cache_b2_kv': cache_b2_kv, 'state_c_mem': state_c_mem, 'state_c_norm': state_c_norm,
            'state_c_max': state_c_max, 'w_in0': w_in0, 'sinks0': sinks0, 'w_out0': w_out0,
            'w_in1': w_in1, 'b_gates1': b_gates1, 'mh_norm1': mh_norm1, 'w_out1': w_out1,
            'ln_g': ln_g, 'ln_b': ln_b}


def reference(x_prompt, x_sample, cache_a_kv, cache_b0_kv, cache_b1_kv, cache_b2_kv, state_c_mem,
              state_c_norm, state_c_max, w_in0, sinks0, w_out0, w_in1, b_gates1, mh_norm1, w_out1,
              ln_g, ln_b):
    yp, ys = x_prompt, x_sample
    ap, bp, cp = [], [[], [], []], [[], [], []]
    asm, bsm, csm = [], [[], [], []], [[], [], []]
    for layer in range(DEPTH):
        e = layer // 2
        if layer % 2 == 0:
            yp, a_new, b_new = layer0_prompt(yp, w_in0[e], sinks0[e], w_out0[e], ln_g[layer], ln_b[layer])
            ap.append(a_new)
            for g in range(N_B_GROUPS):
                bp[g].append(b_new[g])
            ys, a_new, b_new = layer0_sample(ys, cache_a_kv[e], (cache_b0_kv[e], cache_b1_kv[e], cache_b2_kv[e]),
                                             w_in0[e], sinks0[e], w_out0[e], ln_g[layer], ln_b[layer])
            asm.append(a_new)
            for g in range(N_B_GROUPS):
                bsm[g].append(b_new[g])
        else:
            yp, c_new = layer1_prompt(yp, w_in1[e], b_gates1[e], mh_norm1[e], w_out1[e], ln_g[layer], ln_b[layer])
            for i in range(3):
                cp[i].append(c_new[i])
            ys, c_new = layer1_sample(ys, state_c_mem[e], state_c_norm[e], state_c_max[e], w_in1[e], b_gates1[e],
                                      mh_norm1[e], w_out1[e], ln_g[layer], ln_b[layer])
            for i in range(3):
                csm[i].append(c_new[i])
    return (yp, ys,
            jnp.stack(ap, 0), jnp.stack(bp[0], 0), jnp.stack(bp[1], 0), jnp.stack(bp[2], 0),
            jnp.stack(cp[0], 0), jnp.stack(cp[1], 0), jnp.stack(cp[2], 0),
            jnp.stack(asm, 0), jnp.stack(bsm[0], 0), jnp.stack(bsm[1], 0), jnp.stack(bsm[2], 0),
            jnp.stack(csm[0], 0), jnp.stack(csm[1], 0), jnp.stack(csm[2], 0))
```

```python
import functools

import jax
import jax.numpy as jnp
from jax import lax
from jax.experimental import pallas as pl
from jax.experimental.pallas import tpu as pltpu

F32 = jnp.float32
BF16 = jnp.bfloat16

D_MODEL = 1024
PAST_LEN = 16384
HEAD_DIM = 64
ROT_DIM = HEAD_DIM // 4
ROPE_THETA = 500000.0
BLOCK = 128
Q_HEADS = 8
KV_HEADS = 2
GROUP = Q_HEADS // KV_HEADS
QW = Q_HEADS * HEAD_DIM
KW = KV_HEADS * HEAD_DIM
GW = QW + 2 * KW
DILATIONS = (1, 1, 4, 16)
N_GROUPS = 4
C_HEADS = 4
C_HEAD_DIM = 256
MIX = 1024
DEPTH = 2
DN_ALPHA = (2 * DEPTH) ** 0.25
LN_EPS = 1e-5
MH_EPS = 1e-6
NEG_INF = -1e30
LANES = 128
TPAD = 8
VMEM_LIMIT = 56 * 1024 * 1024


def _cparams(sem):
    return pltpu.CompilerParams(dimension_semantics=sem, vmem_limit_bytes=VMEM_LIMIT)


def _layer_norm_rows(v, g, b):
    mu = jnp.mean(v, axis=-1, keepdims=True)
    c = v - mu
    var = jnp.mean(c * c, axis=-1, keepdims=True)
    return c * lax.rsqrt(var + LN_EPS) * g + b


def _rope(slab, cos, sin_lo, sin_hi):
    return (slab * cos + pltpu.roll(slab, LANES - ROT_DIM // 2, 1) * sin_lo
            + pltpu.roll(slab, ROT_DIM // 2, 1) * sin_hi)


def _rope_tables(pos):
    half = ROT_DIM // 2
    inv = ROPE_THETA ** (-jnp.arange(half, dtype=F32) / half)
    ang = pos.astype(F32)[:, None] * inv[None, :]
    cos, sin = jnp.cos(ang), jnp.sin(ang)
    lane = jnp.arange(LANES) % HEAD_DIM
    f = lane % half
    cos_t = jnp.where(lane[None, :] < ROT_DIM, cos[:, f], 1.0)
    sin_lo = jnp.where(lane[None, :] < half, -sin[:, f], 0.0)
    sin_hi = jnp.where((lane[None, :] >= half) & (lane[None, :] < ROT_DIM), sin[:, f], 0.0)
    return cos_t.astype(F32), sin_lo.astype(F32), sin_hi.astype(F32)


def _group_attn_kernel(sink_ref, x_ref, w_ref, cos_ref, slo_ref, shi_ref,
                       o_ref, lse_ref, st_ref, kv_prev, *, has_sink, tq, nblk):
    i = pl.program_id(2)

    @pl.when(i == 0)
    def _():
        kv_prev[...] = jnp.zeros_like(kv_prev)

    x = x_ref[0].astype(BF16)
    h = jnp.dot(x, w_ref[...], preferred_element_type=F32)
    cos, slo, shi = cos_ref[...], slo_ref[...], shi_ref[...]
    scale = HEAD_DIM ** -0.5
    q_slabs = [_rope(h[:, j * LANES:(j + 1) * LANES], cos, slo, shi) * scale
               for j in range(QW // LANES)]
    k = _rope(h[:, QW:QW + KW], cos, slo, shi)
    v = h[:, QW + KW:GW]
    kb, vb = k.astype(BF16), v.astype(BF16)

    qi = lax.broadcasted_iota(jnp.int32, (BLOCK, 2 * BLOCK), 0) + BLOCK
    ki = lax.broadcasted_iota(jnp.int32, (BLOCK, 2 * BLOCK), 1)
    dist = qi - ki
    band = (dist >= 0) & (dist <= BLOCK)
    lane = lax.broadcasted_iota(jnp.int32, (BLOCK, LANES), 1)

    for sb in range(tq // BLOCK):
        rows = slice(sb * BLOCK, (sb + 1) * BLOCK)
        if sb == 0:
            k_prev, v_prev = kv_prev[:, :KW], kv_prev[:, KW:]
            mask = band & (ki >= jnp.where(i > 0, 0, BLOCK))
        else:
            prev = slice((sb - 1) * BLOCK, sb * BLOCK)
            k_prev, v_prev = kb[prev], vb[prev]
            mask = band
        kk = jnp.concatenate([k_prev, kb[rows]], axis=0)
        vv = jnp.concatenate([v_prev, vb[rows]], axis=0)
        lse_acc = jnp.zeros((BLOCK, LANES), F32)
        outs = []
        for hq in range(Q_HEADS):
            hk = hq // GROUP
            off = (hq % 2) * HEAD_DIM
            qh = q_slabs[hq // 2][rows, off:off + HEAD_DIM].astype(BF16)
            kh = kk[:, hk * HEAD_DIM:(hk + 1) * HEAD_DIM]
            vh = vv[:, hk * HEAD_DIM:(hk + 1) * HEAD_DIM]
            s = lax.dot_general(qh, kh, (((1,), (1,)), ((), ())), preferred_element_type=F32)
            s = jnp.where(mask, s, NEG_INF)
            m = jnp.max(s, axis=-1, keepdims=True)
            if has_sink:
                sink = sink_ref[hq]
                m = jnp.maximum(m, sink)
            p = jnp.exp(s - m)
            den = jnp.sum(p, axis=-1, keepdims=True)
            if has_sink:
                den = den + jnp.exp(sink - m)
            o = jnp.dot(p.astype(BF16), vh, preferred_element_type=F32) / den
            outs.append(o)
            lse_acc = jnp.where(lane == hq, m + jnp.log(den), lse_acc)
        for j in range(QW // LANES):
            o_ref[0, rows, j * LANES:(j + 1) * LANES] = jnp.concatenate(
                [outs[2 * j], outs[2 * j + 1]], axis=1).astype(o_ref.dtype)
        lse_ref[0, rows, :] = lse_acc

    kv_prev[:, :KW] = kb[tq - BLOCK:]
    kv_prev[:, KW:] = vb[tq - BLOCK:]

    @pl.when(i == nblk - 1)
    def _():
        st_ref[0, :, :KW] = k[tq - BLOCK:]
        st_ref[0, :, KW:] = v[tq - BLOCK:]


def _group_attn_prompt(x, w_g, sinks, tabs, dil, has_sink):
    bn, s, d = x.shape
    sub = s // dil
    tq = min(256, sub)
    nblk = sub // tq
    xv = x.reshape(bn, sub, dil * d)
    cos, slo, shi = (t.reshape(sub, dil * LANES) for t in tabs)
    tab_spec = pl.BlockSpec((tq, LANES), lambda b, r, i: (i, r))
    o, lse, st = pl.pallas_call(
        functools.partial(_group_attn_kernel, has_sink=has_sink, tq=tq, nblk=nblk),
        grid=(bn, dil, nblk),
        in_specs=[
            pl.BlockSpec(memory_space=pltpu.SMEM),
            pl.BlockSpec((1, tq, d), lambda b, r, i: (b, i, r)),
            pl.BlockSpec((d, GW), lambda b, r, i: (0, 0)),
            tab_spec, tab_spec, tab_spec,
        ],
        out_specs=[
            pl.BlockSpec((1, tq, QW), lambda b, r, i: (b, i, r)),
            pl.BlockSpec((1, tq, LANES), lambda b, r, i: (b, i, r)),
            pl.BlockSpec((1, BLOCK, 2 * KW), lambda b, r, i: (b, 0, r)),
        ],
        out_shape=[
            jax.ShapeDtypeStruct((bn, sub, dil * QW), BF16),
            jax.ShapeDtypeStruct((bn, sub, dil * LANES), F32),
            jax.ShapeDtypeStruct((bn, BLOCK, dil * 2 * KW), F32),
        ],
        scratch_shapes=[pltpu.VMEM((BLOCK, 2 * KW), BF16)],
        compiler_params=_cparams(("arbitrary", "arbitrary", "arbitrary")),
        name=f"l0_group_attn_d{dil}_{'sink' if has_sink else 'nosink'}",
    )(sinks, xv, w_g, cos, slo, shi)
    return (o.reshape(bn * s, QW), lse.reshape(bn * s, LANES),
            st.reshape(bn, BLOCK * dil, 2, KV_HEADS, HEAD_DIM))


def _l0_out_kernel(x_ref, oa_ref, o0_ref, o1_ref, o2_ref, l0_ref, l1_ref, l2_ref,
                   wz_ref, wo_ref, g_ref, b_ref, y_ref):
    x = x_ref[...]
    z = jnp.dot(x.astype(BF16), wz_ref[...], preferred_element_type=F32)
    l0, l1, l2 = l0_ref[...], l1_ref[...], l2_ref[...]
    m = jnp.maximum(jnp.maximum(l0, l1), l2)
    e0, e1, e2 = jnp.exp(l0 - m), jnp.exp(l1 - m), jnp.exp(l2 - m)
    inv = 1.0 / (e0 + e1 + e2)
    w0, w1, w2 = e0 * inv, e1 * inv, e2 * inv
    tm = x.shape[0]
    parts = []
    for hq in range(Q_HEADS):
        c = slice(hq * HEAD_DIM, (hq + 1) * HEAD_DIM)
        bshape = (tm, HEAD_DIM)
        parts.append(jnp.broadcast_to(w0[:, hq:hq + 1], bshape) * o0_ref[:, c].astype(F32)
                     + jnp.broadcast_to(w1[:, hq:hq + 1], bshape) * o1_ref[:, c].astype(F32)
                     + jnp.broadcast_to(w2[:, hq:hq + 1], bshape) * o2_ref[:, c].astype(F32))
    mix = jnp.concatenate([oa_ref[...].astype(F32)] + parts, axis=1)
    mix = (mix * (z * jax.nn.sigmoid(z))).astype(BF16)
    y = jnp.dot(mix, wo_ref[...], preferred_element_type=F32)
    y_ref[...] = _layer_norm_rows(DN_ALPHA * x + y, g_ref[...], b_ref[...])


def _l0_out(x2, oa, obs, lses, wz, wo, g, b):
    n, d = x2.shape
    tm = 256
    row = lambda w: pl.BlockSpec((tm, w), lambda i: (i, 0))
    full = lambda a: pl.BlockSpec(a.shape, lambda i: (0, 0))
    return pl.pallas_call(
        _l0_out_kernel,
        grid=(n // tm,),
        in_specs=[row(d), row(QW), row(QW), row(QW), row(QW), row(LANES), row(LANES), row(LANES),
                  full(wz), full(wo), full(g), full(b)],
        out_specs=row(d),
        out_shape=jax.ShapeDtypeStruct((n, d), F32),
        compiler_params=_cparams(("arbitrary",)),
        name="l0_out",
    )(x2, oa, *obs, *lses, wz, wo, g, b)


def _proj_kernel(x_ref, w_ref, cos_ref, slo_ref, shi_ref, h_ref, *, n_rope, n_scaled, tn):
    j = pl.program_id(0)
    h = jnp.dot(x_ref[...].astype(BF16), w_ref[...], preferred_element_type=F32)
    if n_rope:
        cos, slo, shi = cos_ref[...], slo_ref[...], shi_ref[...]
        roped = jnp.concatenate(
            [_rope(h[:, c * LANES:(c + 1) * LANES], cos, slo, shi) for c in range(tn // LANES)], axis=1)
        h = jnp.where(j < n_rope, roped, h)
        h = h * jnp.where(j < n_scaled, HEAD_DIM ** -0.5, 1.0)
    h_ref[...] = h


def _proj(x2, w, tabs, tn, n_rope=0, n_scaled=0):
    n, d = x2.shape
    ncol = w.shape[1]
    tab_spec = pl.BlockSpec((n, LANES), lambda j: (0, 0))
    return pl.pallas_call(
        functools.partial(_proj_kernel, n_rope=n_rope, n_scaled=n_scaled, tn=tn),
        grid=(ncol // tn,),
        in_specs=[pl.BlockSpec((n, d), lambda j: (0, 0)),
                  pl.BlockSpec((d, tn), lambda j: (0, j)),
                  tab_spec, tab_spec, tab_spec],
        out_specs=pl.BlockSpec((n, tn), lambda j: (0, j)),
        out_shape=jax.ShapeDtypeStruct((n, ncol), F32),
        compiler_params=_cparams(("arbitrary",)),
        name=f"decode_proj_{ncol}",
    )(x2, w, *tabs)


def _decode_attn_kernel(sink_ref, h_ref, ca_ref, c0_ref, c1_ref, c2_ref,
                        oa_ref, o0_ref, o1_ref, o2_ref, l0_ref, l1_ref, l2_ref, *, sb, t_new):
    nrow = Q_HEADS * TPAD
    ncache = BLOCK
    ncol = 2 * BLOCK
    kpad = jnp.zeros((ncol - ncache - TPAD, KW), F32)
    lane = lax.broadcasted_iota(jnp.int32, (TPAD, LANES), 1)
    lo_half = lane < HEAD_DIM
    tok = lax.broadcasted_iota(jnp.int32, (nrow, ncol), 0) % TPAD
    col = lax.broadcasted_iota(jnp.int32, (nrow, ncol), 1)
    new_j = col - ncache
    mask_d1 = ((col < ncache) & (col >= tok)) | ((col >= ncache) & (new_j <= tok))
    mask_dil = (col < ncache) | (new_j == tok)
    lane_r = lax.broadcasted_iota(jnp.int32, (TPAD, LANES), 1)
    k_base = N_GROUPS * QW
    v_base = k_base + N_GROUPS * KW
    caches = (ca_ref, c0_ref, c1_ref, c2_ref)
    o_refs = (oa_ref, o0_ref, o1_ref, o2_ref)
    l_refs = (None, l0_ref, l1_ref, l2_ref)

    for s_i in range(sb):
        rows = slice(s_i * TPAD, (s_i + 1) * TPAD)
        for g in range(N_GROUPS):
            dil = DILATIONS[g]
            blocks = []
            for j in range(QW // LANES):
                slab = h_ref[rows, g * QW + j * LANES: g * QW + (j + 1) * LANES]
                swapped = pltpu.roll(slab, HEAD_DIM, 1)
                if j < QW // LANES // 2:
                    blocks += [jnp.where(lo_half, slab, 0.0), jnp.where(lo_half, swapped, 0.0)]
                else:
                    blocks += [jnp.where(lo_half, 0.0, swapped), jnp.where(lo_half, 0.0, slab)]
            qm = jnp.concatenate(blocks, axis=0).astype(BF16)
            k_new = h_ref[rows, k_base + g * KW: k_base + (g + 1) * KW]
            v_new = h_ref[rows, v_base + g * KW: v_base + (g + 1) * KW]
            cref = caches[g]

            def keys_vals(i):
                c0 = i * 2 * KW
                kc = jnp.concatenate([cref[s_i, :, c0:c0 + KW], k_new, kpad], axis=0).astype(BF16)
                vc = jnp.concatenate([cref[s_i, :, c0 + KW:c0 + 2 * KW], v_new, kpad], axis=0).astype(BF16)
                return kc, vc

            nt = (((1,), (1,)), ((), ()))
            if dil == 1:
                kc, vc = keys_vals(0)
                s = lax.dot_general(qm, kc, nt, preferred_element_type=F32)
                s = jnp.where(mask_d1, s, NEG_INF)
                vcs = [vc]
            else:
                s = jnp.zeros((nrow, ncol), F32)
                vcs = []
                for i in range(t_new):
                    kc, vc = keys_vals(i)
                    s_tok = lax.dot_general(qm, kc, nt, preferred_element_type=F32)
                    s = jnp.where(tok == i, s_tok, s)
                    vcs.append(vc)
                s = jnp.where(mask_dil, s, NEG_INF)
            m = jnp.max(s, axis=-1, keepdims=True)
            if g == 0:
                sink = sink_ref[:, 0:1]
                m = jnp.maximum(m, sink)
            p = jnp.exp(s - m)
            den = jnp.sum(p, axis=-1, keepdims=True)
            if g == 0:
                den = den + jnp.exp(sink - m)
            if dil == 1:
                o = jnp.dot(p.astype(BF16), vcs[0], preferred_element_type=F32)
            else:
                o = jnp.zeros((nrow, KW), F32)
                for i in range(t_new):
                    p_tok = jnp.where(tok == i, p, 0.0).astype(BF16)
                    o = o + jnp.dot(p_tok, vcs[i], preferred_element_type=F32)
            o = o / den
            lse = m + jnp.log(den)
            lse_acc = jnp.zeros((TPAD, LANES), F32)
            for j in range(QW // LANES):
                ev = o[(2 * j) * TPAD:(2 * j + 1) * TPAD]
                od = o[(2 * j + 1) * TPAD:(2 * j + 2) * TPAD]
                if j < QW // LANES // 2:
                    slab = jnp.where(lo_half, ev, pltpu.roll(od, HEAD_DIM, 1))
                else:
                    slab = jnp.where(lo_half, pltpu.roll(ev, HEAD_DIM, 1), od)
                o_refs[g][rows, j * LANES:(j + 1) * LANES] = slab.astype(o_refs[g].dtype)
            if g > 0:
                for hq in range(Q_HEADS):
                    lse_acc = jnp.where(lane_r == hq, lse[hq * TPAD:(hq + 1) * TPAD], lse_acc)
                l_refs[g][rows, :] = lse_acc


def _decode_attn(h, cache_a, cache_b0, cache_b1, cache_b2, sink_rows, t_new):
    nseq = cache_a.shape[0]
    sb = 4
    cw = 2 * KW
    c1v = cache_b1.reshape(nseq, BLOCK, 4 * cw)
    c2v = cache_b2.reshape(nseq, BLOCK, 16 * cw)
    ca = cache_a.reshape(nseq, BLOCK, cw)
    c0 = cache_b0.reshape(nseq, BLOCK, cw)
    n = nseq * TPAD
    row = lambda w: pl.BlockSpec((sb * TPAD, w), lambda i: (i, 0))
    cspec = lambda w: pl.BlockSpec((sb, BLOCK, w), lambda i: (i, 0, 0))
    return pl.pallas_call(
        functools.partial(_decode_attn_kernel, sb=sb, t_new=t_new),
        grid=(nseq // sb,),
        in_specs=[pl.BlockSpec(sink_rows.shape, lambda i: (0, 0)),
                  row(h.shape[1]), cspec(cw), cspec(cw), cspec(t_new * cw), cspec(t_new * cw)],
        out_specs=[row(QW)] * 4 + [row(LANES)] * 3,
        out_shape=[jax.ShapeDtypeStruct((n, QW), BF16)] * 4 + [jax.ShapeDtypeStruct((n, LANES), F32)] * 3,
        compiler_params=_cparams(("arbitrary",)),
        name="l0_decode_attn",
    )(sink_rows, h, ca, c0, c1v, c2v)


def _mlstm_chunk(q, k, v, gates, hh, c_mem, c_norm, m_prev, n_valid):
    L = q.shape[0]
    ig_col = gates[:, hh:hh + 1]
    fg = gates[:, C_HEADS + hh:C_HEADS + hh + 1]
    lf_col = jnp.minimum(fg, 0.0) - jnp.log(1.0 + jnp.exp(-jnp.abs(fg)))
    ri = lax.broadcasted_iota(jnp.int32, (L, L), 0)
    ci = lax.broadcasted_iota(jnp.int32, (L, L), 1)
    tril = ri >= ci
    eye = ri == ci
    lf_row = jnp.sum(jnp.where(eye, lf_col, 0.0), axis=0, keepdims=True)
    ig_row = jnp.sum(jnp.where(eye, ig_col, 0.0), axis=0, keepdims=True)
    b_col = jnp.sum(jnp.where(tril, lf_row, 0.0), axis=1, keepdims=True)
    b_row = jnp.sum(jnp.where(ri <= ci, lf_col, 0.0), axis=0, keepdims=True)
    dmat = jnp.where(tril, b_col - b_row + ig_row, NEG_INF)
    inter = b_col + m_prev
    m = jnp.maximum(inter, jnp.max(dmat, axis=1, keepdims=True))
    pw = jnp.exp(dmat - m)
    qb, kb, vb = q.astype(BF16), k.astype(BF16), v.astype(BF16)
    sc = lax.dot_general(qb, kb, (((1,), (1,)), ((), ())), preferred_element_type=F32) * pw
    carry_w = jnp.exp(inter - m)
    num = (jnp.dot(sc.astype(BF16), vb, preferred_element_type=F32)
           + carry_w * jnp.dot(qb, c_mem.astype(BF16), preferred_element_type=F32))
    den = jnp.sum(sc, axis=1, keepdims=True) + carry_w * jnp.sum(q * c_norm, axis=1, keepdims=True)
    h = num / jnp.maximum(jnp.abs(den), jnp.exp(-m))
    last = n_valid - 1
    m_new = m[last:last + 1, :]
    b_last = b_col[last:last + 1, :]
    ws = jnp.exp(b_last - b_col + ig_col - m_new)
    if n_valid < L:
        ws = jnp.where(lax.broadcasted_iota(jnp.int32, (L, 1), 0) < n_valid, ws, 0.0)
    decay = jnp.exp(b_last + m_prev - m_new)
    wk = ws * k
    c_mem_new = decay * c_mem + lax.dot_general(
        wk.astype(BF16), vb, (((0,), (0,)), ((), ())), preferred_element_type=F32)
    c_norm_new = decay * c_norm + jnp.sum(wk, axis=0, keepdims=True)
    return h, c_mem_new, c_norm_new, m_new


def _l1_prompt_kernel(x_ref, w_ref, bias_ref, h_ref, cmem_ref, cnorm_ref, cmax_ref,
                      c_sc, n_sc, m_sc, *, tm, nblk):
    j = pl.program_id(1)

    @pl.when(j == 0)
    def _():
        c_sc[...] = jnp.zeros_like(c_sc)
        n_sc[...] = jnp.zeros_like(n_sc)
        m_sc[...] = jnp.zeros_like(m_sc)

    proj = jnp.dot(x_ref[0].astype(BF16), w_ref[...], preferred_element_type=F32)
    gates = proj[:, 3 * MIX:] + bias_ref[...]
    kscale = C_HEAD_DIM ** -0.5
    for c in range(tm // BLOCK):
        rows = slice(c * BLOCK, (c + 1) * BLOCK)
        for hh in range(C_HEADS):
            cols = slice(hh * C_HEAD_DIM, (hh + 1) * C_HEAD_DIM)
            q = proj[rows, cols]
            k = proj[rows, MIX + hh * C_HEAD_DIM: MIX + (hh + 1) * C_HEAD_DIM] * kscale
            v = proj[rows, 2 * MIX + hh * C_HEAD_DIM: 2 * MIX + (hh + 1) * C_HEAD_DIM]
            h, c_new, n_new, m_new = _mlstm_chunk(
                q, k, v, gates[rows], hh, c_sc[hh], n_sc[hh:hh + 1, :], m_sc[hh:hh + 1, 0:1], BLOCK)
            h_ref[0, rows, cols] = h
            c_sc[hh] = c_new
            n_sc[hh:hh + 1, :] = n_new
            m_sc[hh:hh + 1, :] = jnp.broadcast_to(m_new, (1, LANES))

    @pl.when(j == nblk - 1)
    def _():
        cmem_ref[0] = c_sc[...]
        cnorm_ref[0] = n_sc[...]
        cmax_ref[0] = m_sc[...]


def _l1_prompt(x, w_qkvg, bias_row):
    bn, s, d = x.shape
    tm = 512
    nblk = s // tm
    ncol = w_qkvg.shape[1]
    return pl.pallas_call(
        functools.partial(_l1_prompt_kernel, tm=tm, nblk=nblk),
        grid=(bn, nblk),
        in_specs=[pl.BlockSpec((1, tm, d), lambda b, j: (b, j, 0)),
                  pl.BlockSpec((d, ncol), lambda b, j: (0, 0)),
                  pl.BlockSpec((1, LANES), lambda b, j: (0, 0))],
        out_specs=[pl.BlockSpec((1, tm, MIX), lambda b, j: (b, j, 0)),
                   pl.BlockSpec((1, C_HEADS, C_HEAD_DIM, C_HEAD_DIM), lambda b, j: (b, 0, 0, 0)),
                   pl.BlockSpec((1, TPAD, C_HEAD_DIM), lambda b, j: (b, 0, 0)),
                   pl.BlockSpec((1, TPAD, LANES), lambda b, j: (b, 0, 0))],
        out_shape=[jax.ShapeDtypeStruct((bn, s, MIX), F32),
                   jax.ShapeDtypeStruct((bn, C_HEADS, C_HEAD_DIM, C_HEAD_DIM), F32),
                   jax.ShapeDtypeStruct((bn, TPAD, C_HEAD_DIM), F32),
                   jax.ShapeDtypeStruct((bn, TPAD, LANES), F32)],
        scratch_shapes=[pltpu.VMEM((C_HEADS, C_HEAD_DIM, C_HEAD_DIM), F32),
                        pltpu.VMEM((TPAD, C_HEAD_DIM), F32),
                        pltpu.VMEM((TPAD, LANES), F32)],
        compiler_params=_cparams(("arbitrary", "arbitrary")),
        name="l1_prompt_mlstm",
    )(x, w_qkvg, bias_row)


def _l1_decode_kernel(p_ref, bias_ref, cm_ref, cn_ref, cx_ref, h_ref, cmo_ref, cno_ref, cxo_ref,
                      *, sb, t_new):
    kscale = C_HEAD_DIM ** -0.5
    for s_i in range(sb):
        rows = slice(s_i * TPAD, (s_i + 1) * TPAD)
        gates = p_ref[rows, 3 * MIX:] + bias_ref[...]
        for hh in range(C_HEADS):
            cols = slice(hh * C_HEAD_DIM, (hh + 1) * C_HEAD_DIM)
            q = p_ref[rows, cols]
            k = p_ref[rows, MIX + hh * C_HEAD_DIM: MIX + (hh + 1) * C_HEAD_DIM] * kscale
            v = p_ref[rows, 2 * MIX + hh * C_HEAD_DIM: 2 * MIX + (hh + 1) * C_HEAD_DIM]
            h, c_new, n_new, m_new = _mlstm_chunk(
                q, k, v, gates, hh, cm_ref[s_i, hh], cn_ref[s_i, hh:hh + 1, :],
                cx_ref[s_i, hh:hh + 1, 0:1], t_new)
            h_ref[rows, cols] = h
            cmo_ref[s_i, hh] = c_new
            cno_ref[s_i, hh:hh + 1, :] = n_new
            cxo_ref[s_i, hh:hh + 1, :] = jnp.broadcast_to(m_new, (1, LANES))


def _l1_decode(proj, bias_row, c_mem, c_norm, c_max_pad, t_new):
    nseq = c_mem.shape[0]
    sb = 2
    n = nseq * TPAD
    return pl.pallas_call(
        functools.partial(_l1_decode_kernel, sb=sb, t_new=t_new),
        grid=(nseq // sb,),
        in_specs=[pl.BlockSpec((sb * TPAD, proj.shape[1]), lambda i: (i, 0)),
                  pl.BlockSpec((1, LANES), lambda i: (0, 0)),
                  pl.BlockSpec((sb, C_HEADS, C_HEAD_DIM, C_HEAD_DIM), lambda i: (i, 0, 0, 0)),
                  pl.BlockSpec((sb, C_HEADS, C_HEAD_DIM), lambda i: (i, 0, 0)),
                  pl.BlockSpec((sb, C_HEADS, LANES), lambda i: (i, 0, 0))],
        out_specs=[pl.BlockSpec((sb * TPAD, MIX), lambda i: (i, 0)),
                   pl.BlockSpec((sb, C_HEADS, C_HEAD_DIM, C_HEAD_DIM), lambda i: (i, 0, 0, 0)),
                   pl.BlockSpec((sb, C_HEADS, C_HEAD_DIM), lambda i: (i, 0, 0)),
                   pl.BlockSpec((sb, C_HEADS, LANES), lambda i: (i, 0, 0))],
        out_shape=[jax.ShapeDtypeStruct((n, MIX), F32),
                   jax.ShapeDtypeStruct(c_mem.shape, F32),
                   jax.ShapeDtypeStruct(c_norm.shape, F32),
                   jax.ShapeDtypeStruct(c_max_pad.shape, F32)],
        compiler_params=_cparams(("arbitrary",)),
        name="l1_decode_mlstm",
    )(proj, bias_row, c_mem, c_norm, c_max_pad)


def _l1_out_kernel(x_ref, h_ref, woz_ref, mhg_ref, wo_ref, g_ref, b_ref, y_ref):
    x = x_ref[...]
    oz = jnp.dot(x.astype(BF16), woz_ref[...], preferred_element_type=F32)
    parts = []
    for hh in range(C_HEADS):
        cols = slice(hh * C_HEAD_DIM, (hh + 1) * C_HEAD_DIM)
        hg = jax.nn.sigmoid(oz[:, cols]) * h_ref[:, cols]
        mu = jnp.mean(hg, axis=-1, keepdims=True)
        c = hg - mu
        var = jnp.mean(c * c, axis=-1, keepdims=True)
        parts.append(c * lax.rsqrt(var + MH_EPS))
    hn = jnp.concatenate(parts, axis=1) * mhg_ref[...]
    z = oz[:, MIX:]
    mix = (hn * (z * jax.nn.sigmoid(z))).astype(BF16)
    y = jnp.dot(mix, wo_ref[...], preferred_element_type=F32)
    y_ref[...] = _layer_norm_rows(DN_ALPHA * x + y, g_ref[...], b_ref[...])


def _l1_out(x2, h2, woz, mhg, wo, g, b):
    n, d = x2.shape
    tm = 256
    row = lambda w: pl.BlockSpec((tm, w), lambda i: (i, 0))
    full = lambda a: pl.BlockSpec(a.shape, lambda i: (0, 0))
    return pl.pallas_call(
        _l1_out_kernel,
        grid=(n // tm,),
        in_specs=[row(d), row(MIX), full(woz), full(mhg), full(wo), full(g), full(b)],
        out_specs=row(d),
        out_shape=jax.ShapeDtypeStruct((n, d), F32),
        compiler_params=_cparams(("arbitrary",)),
        name="l1_out",
    )(x2, h2, woz, mhg, wo, g, b)


def _group_cols(g):
    if g == 0:
        return [(0, QW), (QW, QW + KW), (QW + KW, GW)]
    b = g - 1
    qb0 = GW
    kb0 = GW + 3 * QW
    vb0 = kb0 + 3 * KW
    return [(qb0 + b * QW, qb0 + (b + 1) * QW), (kb0 + b * KW, kb0 + (b + 1) * KW),
            (vb0 + b * KW, vb0 + (b + 1) * KW)]


def kernel(x_prompt, x_sample, cache_a_kv, cache_b0_kv, cache_b1_kv, cache_b2_kv, state_c_mem,
           state_c_norm, state_c_max, w_in0, sinks0, w_out0, w_in1, b_gates1, mh_norm1, w_out1,
           ln_g, ln_b):
    bn, s, d = x_prompt.shape
    nseq, t_new, _ = x_sample.shape
    w0 = w_in0[0]
    z0 = 4 * GW
    w_groups = [jnp.concatenate([w0[:, a:b] for a, b in _group_cols(g)], axis=1).astype(BF16)
                for g in range(N_GROUPS)]
    wz0 = w0[:, z0:].astype(BF16)
    wo0 = w_out0[0].astype(BF16)
    sinks = sinks0[0].astype(F32)
    g0, b0 = ln_g[0][None, :], ln_b[0][None, :]
    g1, b1 = ln_g[1][None, :], ln_b[1][None, :]

    tabs_p = _rope_tables(jnp.arange(s))
    outs, lses, states_p = [], [], []
    for g in range(N_GROUPS):
        o, lse, st = _group_attn_prompt(x_prompt, w_groups[g], sinks, tabs_p, DILATIONS[g], g == 0)
        outs.append(o)
        lses.append(lse)
        states_p.append(st)
    xp2 = x_prompt.reshape(bn * s, d)
    y0p = _l0_out(xp2, outs[0], outs[1:], lses[1:], wz0, wo0, g0, b0)

    xs2 = jnp.pad(x_sample, ((0, 0), (0, TPAD - t_new), (0, 0))).reshape(nseq * TPAD, d)
    w_dec = jnp.concatenate(
        [w_groups[g][:, :QW] for g in range(N_GROUPS)] + [w_groups[g][:, QW:QW + KW] for g in range(N_GROUPS)]
        + [w_groups[g][:, QW + KW:] for g in range(N_GROUPS)], axis=1)
    tabs_s = tuple(jnp.tile(t, (nseq, 1)) for t in _rope_tables(PAST_LEN + jnp.arange(TPAD)))
    tn = 512
    hs = _proj(xs2, w_dec, tabs_s, tn, n_rope=(N_GROUPS * (QW + KW)) // tn, n_scaled=(N_GROUPS * QW) // tn)
    sink_rows = jnp.broadcast_to(jnp.repeat(sinks, TPAD)[:, None], (Q_HEADS * TPAD, LANES))
    caches = (cache_a_kv[0], cache_b0_kv[0], cache_b1_kv[0], cache_b2_kv[0])
    dec = _decode_attn(hs, *caches, sink_rows, t_new)
    y0s = _l0_out(xs2, dec[0], dec[1:4], dec[4:7], wz0, wo0, g0, b0)
    k_base = N_GROUPS * QW
    v_base = k_base + N_GROUPS * KW
    hs3 = hs.reshape(nseq, TPAD, -1)[:, :t_new]
    states_s = []
    for g in range(N_GROUPS):
        new = jnp.concatenate([hs3[:, :, k_base + g * KW:k_base + (g + 1) * KW],
                               hs3[:, :, v_base + g * KW:v_base + (g + 1) * KW]], axis=-1)
        new = new.reshape(nseq, t_new, 2, KV_HEADS, HEAD_DIM)
        states_s.append(jnp.concatenate([caches[g][:, t_new:], new], axis=1))

    w1 = w_in1[0]
    gate_pad = jnp.zeros((d, LANES - 2 * C_HEADS), F32)
    w_qkvg = jnp.concatenate([w1[:, :3 * MIX], w1[:, 5 * MIX:], gate_pad], axis=1).astype(BF16)
    w_oz = w1[:, 3 * MIX:5 * MIX].astype(BF16)
    wo1 = w_out1[0].astype(BF16)
    bias_row = jnp.pad(b_gates1[0].astype(F32), (0, LANES - 2 * C_HEADS))[None, :]
    mhg = mh_norm1[0][None, :]

    hp, cmem_p, cnorm_p, cmax_p = _l1_prompt(y0p.reshape(bn, s, d), w_qkvg, bias_row)
    y1p = _l1_out(y0p, hp.reshape(bn * s, MIX), w_oz, mhg, wo1, g1, b1)

    no_tabs = tuple(jnp.zeros((nseq * TPAD, LANES), F32) for _ in range(3))
    ps = _proj(y0s, w_qkvg, no_tabs, 640)
    cmax_pad = jnp.broadcast_to(state_c_max[0][:, :, None], (nseq, C_HEADS, LANES))
    hsd, cmem_s, cnorm_s, cmax_s = _l1_decode(ps, bias_row, state_c_mem[0], state_c_norm[0], cmax_pad, t_new)
    y1s = _l1_out(y0s, hsd, w_oz, mhg, wo1, g1, b1)

    y_prompt = y1p.reshape(bn, s, d)
    y_sample = y1s.reshape(nseq, TPAD, d)[:, :t_new]
    return (y_prompt, y_sample,
            states_p[0][None], states_p[1][None], states_p[2][None], states_p[3][None],
            cmem_p[None], cnorm_p[:, :C_HEADS][None], cmax_p[:, :C_HEADS, 0][None],
            states_s[0][None], states_s[1][None], states_s[2][None], states_s[3][None],
            cmem_s[None], cnorm_s[None], cmax_s[:, :, 0][None])
```

```python
import functools

import jax
import jax.numpy as jnp
from jax import lax
from jax.experimental import pallas as pl
from jax.experimental.pallas import tpu as pltpu

F32 = jnp.float32
BF16 = jnp.bfloat16

D_MODEL = 1024
PAST_LEN = 16384
HEAD_DIM = 64
ROT_DIM = HEAD_DIM // 4
ROPE_THETA = 500000.0
BLOCK = 128
Q_HEADS = 8
KV_HEADS = 2
GROUP = Q_HEADS // KV_HEADS
QW = Q_HEADS * HEAD_DIM
KW = KV_HEADS * HEAD_DIM
GW = QW + 2 * KW
DILATIONS = (1, 1, 4, 16)
N_GROUPS = 4
C_HEADS = 4
C_HEAD_DIM = 256
MIX = 1024
DEPTH = 2
DN_ALPHA = (2 * DEPTH) ** 0.25
LN_EPS = 1e-5
MH_EPS = 1e-6
NEG_INF = -1e30
LANES = 128
TPAD = 8
VMEM_LIMIT = 56 * 1024 * 1024


def _cparams(sem):
    return pltpu.CompilerParams(dimension_semantics=sem, vmem_limit_bytes=VMEM_LIMIT)


def _layer_norm_rows(v, g, b):
    mu = jnp.mean(v, axis=-1, keepdims=True)
    c = v - mu
    var = jnp.mean(c * c, axis=-1, keepdims=True)
    return c * lax.rsqrt(var + LN_EPS) * g + b


def _rope(slab, cos, sin_lo, sin_hi):
    return (slab * cos + pltpu.roll(slab, LANES - ROT_DIM // 2, 1) * sin_lo
            + pltpu.roll(slab, ROT_DIM // 2, 1) * sin_hi)


def _rope_tables(pos):
    half = ROT_DIM // 2
    inv = ROPE_THETA ** (-jnp.arange(half, dtype=F32) / half)
    ang = pos.astype(F32)[:, None] * inv[None, :]
    cos, sin = jnp.cos(ang), jnp.sin(ang)
    lane = jnp.arange(LANES) % HEAD_DIM
    f = lane % half
    cos_t = jnp.where(lane[None, :] < ROT_DIM, cos[:, f], 1.0)
    sin_lo = jnp.where(lane[None, :] < half, -sin[:, f], 0.0)
    sin_hi = jnp.where((lane[None, :] >= half) & (lane[None, :] < ROT_DIM), sin[:, f], 0.0)
    return cos_t.astype(F32), sin_lo.astype(F32), sin_hi.astype(F32)


PROJ_ROWS = 512


def _group_attn_kernel(sink_ref, x_ref, w_ref, cos_ref, slo_ref, shi_ref,
                       o_ref, lse_ref, st_ref, q_sc, kv_sc, prev_sc, o_sc,
                       *, has_sink, dil, tstep, nstep):
    j = pl.program_id(1)
    span = BLOCK * dil

    @pl.when(j == 0)
    def _():
        prev_sc[...] = jnp.zeros_like(prev_sc)

    scale = HEAD_DIM ** -0.5
    for rb in range(tstep // PROJ_ROWS):
        rows = slice(rb * PROJ_ROWS, (rb + 1) * PROJ_ROWS)
        h = jnp.dot(x_ref[0, rows, :].astype(BF16), w_ref[...], preferred_element_type=F32)
        cos, slo, shi = cos_ref[rows, :], slo_ref[rows, :], shi_ref[rows, :]
        for c in range(QW // LANES):
            q_sc[c, rows, :] = _rope(h[:, c * LANES:(c + 1) * LANES], cos, slo, shi) * scale
        kv_sc[0, rows, :] = _rope(h[:, QW:QW + KW], cos, slo, shi)
        kv_sc[1, rows, :] = h[:, QW + KW:GW]

    qi = lax.broadcasted_iota(jnp.int32, (BLOCK, 2 * BLOCK), 0) + BLOCK
    ki = lax.broadcasted_iota(jnp.int32, (BLOCK, 2 * BLOCK), 1)
    dist = qi - ki
    band = (dist >= 0) & (dist <= BLOCK)
    band_first = band & (ki >= jnp.where(j > 0, 0, BLOCK))
    lane = lax.broadcasted_iota(jnp.int32, (BLOCK, LANES), 1)

    for u in range(tstep // BLOCK):
        if dil == 1:
            cur = pl.ds(u * BLOCK, BLOCK)
            from_prev_step = u == 0
            prv = pl.ds(0, BLOCK) if from_prev_step else pl.ds((u - 1) * BLOCK, BLOCK)
        else:
            cur = pl.ds(u, BLOCK, stride=dil)
            from_prev_step = True
            prv = cur
        p_sc = prev_sc if from_prev_step else kv_sc
        mask = band_first if from_prev_step else band
        q_slabs = [q_sc[c, cur, :] for c in range(QW // LANES)]
        kk = jnp.concatenate([p_sc[0, prv, :], kv_sc[0, cur, :]], axis=0).astype(BF16)
        vv = jnp.concatenate([p_sc[1, prv, :], kv_sc[1, cur, :]], axis=0).astype(BF16)
        lse_acc = jnp.zeros((BLOCK, LANES), F32)
        outs = []
        for hq in range(Q_HEADS):
            hk = hq // GROUP
            off = (hq % 2) * HEAD_DIM
            qh = q_slabs[hq // 2][:, off:off + HEAD_DIM].astype(BF16)
            kh = kk[:, hk * HEAD_DIM:(hk + 1) * HEAD_DIM]
            vh = vv[:, hk * HEAD_DIM:(hk + 1) * HEAD_DIM]
            s = lax.dot_general(qh, kh, (((1,), (1,)), ((), ())), preferred_element_type=F32)
            s = jnp.where(mask, s, NEG_INF)
            m = jnp.max(s, axis=-1, keepdims=True)
            if has_sink:
                sink = sink_ref[hq]
                m = jnp.maximum(m, sink)
            p = jnp.exp(s - m)
            den = jnp.sum(p, axis=-1, keepdims=True)
            if has_sink:
                den = den + jnp.exp(sink - m)
            outs.append(jnp.dot(p.astype(BF16), vh, preferred_element_type=F32) / den)
            lse_acc = jnp.where(lane == hq, m + jnp.log(den), lse_acc)
        for c in range(QW // LANES):
            o_sc[c, cur, :] = jnp.concatenate([outs[2 * c], outs[2 * c + 1]], axis=1)
        lse_ref[0, cur, :] = lse_acc

    for c in range(QW // LANES):
        o_ref[0, :, c * LANES:(c + 1) * LANES] = o_sc[c].astype(o_ref.dtype)
    prev_sc[...] = kv_sc[:, tstep - span:, :]

    @pl.when(j == nstep - 1)
    def _():
        st_ref[0, :KW, :] = kv_sc[0, tstep - span:, :].T
        st_ref[0, KW:, :] = kv_sc[1, tstep - span:, :].T


def _group_attn_prompt(x, w_g, sinks, tabs, dil, has_sink):
    bn, s, d = x.shape
    span = BLOCK * dil
    tstep = max(PROJ_ROWS, span)
    nstep = s // tstep
    tab_spec = pl.BlockSpec((tstep, LANES), lambda b, j: (j, 0))
    o, lse, st = pl.pallas_call(
        functools.partial(_group_attn_kernel, has_sink=has_sink, dil=dil, tstep=tstep, nstep=nstep),
        grid=(bn, nstep),
        in_specs=[
            pl.BlockSpec(memory_space=pltpu.SMEM),
            pl.BlockSpec((1, tstep, d), lambda b, j: (b, j, 0)),
            pl.BlockSpec((d, GW), lambda b, j: (0, 0)),
            tab_spec, tab_spec, tab_spec,
        ],
        out_specs=[
            pl.BlockSpec((1, tstep, QW), lambda b, j: (b, j, 0)),
            pl.BlockSpec((1, tstep, LANES), lambda b, j: (b, j, 0)),
            pl.BlockSpec((1, 2 * KW, span), lambda b, j: (b, 0, 0)),
        ],
        out_shape=[
            jax.ShapeDtypeStruct((bn, s, QW), BF16),
            jax.ShapeDtypeStruct((bn, s, LANES), F32),
            jax.ShapeDtypeStruct((bn, 2 * KW, span), F32),
        ],
        scratch_shapes=[pltpu.VMEM((QW // LANES, tstep, LANES), F32), pltpu.VMEM((2, tstep, LANES), F32),
                        pltpu.VMEM((2, span, LANES), F32), pltpu.VMEM((QW // LANES, tstep, LANES), F32)],
        compiler_params=_cparams(("arbitrary", "arbitrary")),
        name=f"l0_group_attn_d{dil}_{'sink' if has_sink else 'nosink'}",
    )(sinks, x, w_g, *tabs)
    st = jnp.transpose(st.reshape(bn, 2, KV_HEADS, HEAD_DIM, span), (0, 4, 1, 2, 3))
    return o.reshape(bn * s, QW), lse.reshape(bn * s, LANES), st


def _l0_out_kernel(x_ref, oa_ref, o0_ref, o1_ref, o2_ref, l0_ref, l1_ref, l2_ref,
                   wz_ref, wo_ref, g_ref, b_ref, y_ref):
    x = x_ref[...]
    z = jnp.dot(x.astype(BF16), wz_ref[...], preferred_element_type=F32)
    l0, l1, l2 = l0_ref[...], l1_ref[...], l2_ref[...]
    m = jnp.maximum(jnp.maximum(l0, l1), l2)
    e0, e1, e2 = jnp.exp(l0 - m), jnp.exp(l1 - m), jnp.exp(l2 - m)
    inv = 1.0 / (e0 + e1 + e2)
    w0, w1, w2 = e0 * inv, e1 * inv, e2 * inv
    tm = x.shape[0]
    parts = []
    for hq in range(Q_HEADS):
        c = slice(hq * HEAD_DIM, (hq + 1) * HEAD_DIM)
        bshape = (tm, HEAD_DIM)
        parts.append(jnp.broadcast_to(w0[:, hq:hq + 1], bshape) * o0_ref[:, c].astype(F32)
                     + jnp.broadcast_to(w1[:, hq:hq + 1], bshape) * o1_ref[:, c].astype(F32)
                     + jnp.broadcast_to(w2[:, hq:hq + 1], bshape) * o2_ref[:, c].astype(F32))
    mix = jnp.concatenate([oa_ref[...].astype(F32)] + parts, axis=1)
    mix = (mix * (z * jax.nn.sigmoid(z))).astype(BF16)
    y = jnp.dot(mix, wo_ref[...], preferred_element_type=F32)
    y_ref[...] = _layer_norm_rows(DN_ALPHA * x + y, g_ref[...], b_ref[...])


def _l0_out(x2, oa, obs, lses, wz, wo, g, b):
    n, d = x2.shape
    tm = 256
    row = lambda w: pl.BlockSpec((tm, w), lambda i: (i, 0))
    full = lambda a: pl.BlockSpec(a.shape, lambda i: (0, 0))
    return pl.pallas_call(
        _l0_out_kernel,
        grid=(n // tm,),
        in_specs=[row(d), row(QW), row(QW), row(QW), row(QW), row(LANES), row(LANES), row(LANES),
                  full(wz), full(wo), full(g), full(b)],
        out_specs=row(d),
        out_shape=jax.ShapeDtypeStruct((n, d), F32),
        compiler_params=_cparams(("arbitrary",)),
        name="l0_out",
    )(x2, oa, *obs, *lses, wz, wo, g, b)


def _proj_kernel(x_ref, w_ref, cos_ref, slo_ref, shi_ref, h_ref, *, n_rope, n_scaled, tn):
    j = pl.program_id(0)
    h = jnp.dot(x_ref[...].astype(BF16), w_ref[...], preferred_element_type=F32)
    if n_rope:
        cos, slo, shi = cos_ref[...], slo_ref[...], shi_ref[...]
        roped = jnp.concatenate(
            [_rope(h[:, c * LANES:(c + 1) * LANES], cos, slo, shi) for c in range(tn // LANES)], axis=1)
        h = jnp.where(j < n_rope, roped, h)
        h = h * jnp.where(j < n_scaled, HEAD_DIM ** -0.5, 1.0)
    h_ref[...] = h


def _proj(x2, w, tabs, tn, n_rope=0, n_scaled=0):
    n, d = x2.shape
    ncol = w.shape[1]
    tab_spec = pl.BlockSpec((n, LANES), lambda j: (0, 0))
    return pl.pallas_call(
        functools.partial(_proj_kernel, n_rope=n_rope, n_scaled=n_scaled, tn=tn),
        grid=(ncol // tn,),
        in_specs=[pl.BlockSpec((n, d), lambda j: (0, 0)),
                  pl.BlockSpec((d, tn), lambda j: (0, j)),
                  tab_spec, tab_spec, tab_spec],
        out_specs=pl.BlockSpec((n, tn), lambda j: (0, j)),
        out_shape=jax.ShapeDtypeStruct((n, ncol), F32),
        compiler_params=_cparams(("arbitrary",)),
        name=f"decode_proj_{ncol}",
    )(x2, w, *tabs)


def _decode_attn_kernel(sink_ref, h_ref, ca_ref, c0_ref, c1_ref, c2_ref,
                        oa_ref, o0_ref, o1_ref, o2_ref, l0_ref, l1_ref, l2_ref,
                        na_ref, n0_ref, n1_ref, n2_ref, *, sb, t_new):
    nrow = Q_HEADS * TPAD
    lane8 = lax.broadcasted_iota(jnp.int32, (TPAD, LANES), 1)
    lo_half = lane8 < HEAD_DIM
    lane_c = lax.broadcasted_iota(jnp.int32, (2 * KW, LANES), 1)
    k_base = N_GROUPS * QW
    v_base = k_base + N_GROUPS * KW
    caches = (ca_ref, c0_ref, c1_ref, c2_ref)
    new_caches = (na_ref, n0_ref, n1_ref, n2_ref)
    o_refs = (oa_ref, o0_ref, o1_ref, o2_ref)
    l_refs = (None, l0_ref, l1_ref, l2_ref)
    zpad = jnp.zeros((LANES - TPAD, 2 * KW), F32)
    nt = (((1,), (1,)), ((), ()))

    for g in range(N_GROUPS):
        dil = DILATIONS[g]
        nkeys = BLOCK * dil
        tok = lax.broadcasted_iota(jnp.int32, (nrow, nkeys), 0) % TPAD
        col = lax.broadcasted_iota(jnp.int32, (nrow, nkeys), 1)
        mask_c = (((col - tok) & (dil - 1)) == 0) & (col >= tok)
        tok_n = lax.broadcasted_iota(jnp.int32, (nrow, LANES), 0) % TPAD
        col_n = lax.broadcasted_iota(jnp.int32, (nrow, LANES), 1)
        mask_n = (((tok_n - col_n) & (dil - 1)) == 0) & (col_n <= tok_n) & (col_n < t_new)
        cref, nref = caches[g], new_caches[g]
        for s_i in range(sb):
            rows = slice(s_i * TPAD, (s_i + 1) * TPAD)
            blocks = []
            for c in range(QW // LANES):
                slab = h_ref[rows, g * QW + c * LANES: g * QW + (c + 1) * LANES]
                swapped = pltpu.roll(slab, HEAD_DIM, 1)
                if c < QW // LANES // 2:
                    blocks += [jnp.where(lo_half, slab, 0.0), jnp.where(lo_half, swapped, 0.0)]
                else:
                    blocks += [jnp.where(lo_half, 0.0, swapped), jnp.where(lo_half, 0.0, slab)]
            qm = jnp.concatenate(blocks, axis=0).astype(BF16)
            k_new = h_ref[rows, k_base + g * KW: k_base + (g + 1) * KW]
            v_new = h_ref[rows, v_base + g * KW: v_base + (g + 1) * KW]
            kv_new = jnp.concatenate([jnp.concatenate([k_new, v_new], axis=1), zpad], axis=0)
            kv_new_b = kv_new.astype(BF16)
            cache = cref[s_i]
            kt = cache[:KW].astype(BF16)
            vt = cache[KW:].astype(BF16)
            s_c = jnp.where(mask_c, jnp.dot(qm, kt, preferred_element_type=F32), NEG_INF)
            s_n = jnp.where(mask_n, lax.dot_general(qm, kv_new_b[:, :KW], nt, preferred_element_type=F32), NEG_INF)
            m = jnp.maximum(jnp.max(s_c, axis=-1, keepdims=True), jnp.max(s_n, axis=-1, keepdims=True))
            if g == 0:
                sink = sink_ref[:, 0:1]
                m = jnp.maximum(m, sink)
            p_c = jnp.exp(s_c - m)
            p_n = jnp.exp(s_n - m)
            den = jnp.sum(p_c, axis=-1, keepdims=True) + jnp.sum(p_n, axis=-1, keepdims=True)
            if g == 0:
                den = den + jnp.exp(sink - m)
            o = (lax.dot_general(p_c.astype(BF16), vt, nt, preferred_element_type=F32)
                 + jnp.dot(p_n.astype(BF16), kv_new_b[:, KW:], preferred_element_type=F32)) / den
            for c in range(QW // LANES):
                ev = o[(2 * c) * TPAD:(2 * c + 1) * TPAD]
                od = o[(2 * c + 1) * TPAD:(2 * c + 2) * TPAD]
                if c < QW // LANES // 2:
                    slab = jnp.where(lo_half, ev, pltpu.roll(od, HEAD_DIM, 1))
                else:
                    slab = jnp.where(lo_half, pltpu.roll(ev, HEAD_DIM, 1), od)
                o_refs[g][rows, c * LANES:(c + 1) * LANES] = slab.astype(o_refs[g].dtype)
            if g > 0:
                lse = m + jnp.log(den)
                lse_acc = jnp.zeros((TPAD, LANES), F32)
                for hq in range(Q_HEADS):
                    lse_acc = jnp.where(lane8 == hq, lse[hq * TPAD:(hq + 1) * TPAD], lse_acc)
                l_refs[g][rows, :] = lse_acc
            shifted = pltpu.roll(cache, nkeys - t_new, 1)
            new_cols = pltpu.roll(kv_new.T, LANES - t_new, 1)
            nref[s_i] = shifted
            nref[s_i, :, nkeys - LANES:] = jnp.where(lane_c >= LANES - t_new, new_cols, shifted[:, nkeys - LANES:])


def _cache_rows_minor(c):
    nseq, rows = c.shape[1], c.shape[2]
    return jnp.transpose(c[0], (0, 2, 3, 4, 1)).reshape(nseq, 2 * KW, rows)


def _cache_rows_major(c):
    nseq, _, rows = c.shape
    return jnp.transpose(c.reshape(nseq, 2, KV_HEADS, HEAD_DIM, rows), (0, 4, 1, 2, 3))[None]


def _decode_attn(h, caches, sink_rows, t_new):
    cts = [_cache_rows_minor(c) for c in caches]
    nseq = cts[0].shape[0]
    sb = 2
    n = nseq * TPAD
    row = lambda w: pl.BlockSpec((sb * TPAD, w), lambda i: (i, 0))
    cspec = lambda c: pl.BlockSpec((sb,) + c.shape[1:], lambda i: (i, 0, 0))
    outs = pl.pallas_call(
        functools.partial(_decode_attn_kernel, sb=sb, t_new=t_new),
        grid=(nseq // sb,),
        in_specs=[pl.BlockSpec(sink_rows.shape, lambda i: (0, 0)), row(h.shape[1])] + [cspec(c) for c in cts],
        out_specs=[row(QW)] * 4 + [row(LANES)] * 3 + [cspec(c) for c in cts],
        out_shape=([jax.ShapeDtypeStruct((n, QW), BF16)] * 4 + [jax.ShapeDtypeStruct((n, LANES), F32)] * 3
                   + [jax.ShapeDtypeStruct(c.shape, F32) for c in cts]),
        compiler_params=_cparams(("arbitrary",)),
        name="l0_decode_attn",
    )(sink_rows, h, *cts)
    return outs[:4], outs[4:7], [_cache_rows_major(c) for c in outs[7:]]


def _mlstm_chunk(q, k, v, gates, hh, c_mem, c_norm, m_prev, n_valid):
    L = q.shape[0]
    ig_col = gates[:, hh:hh + 1]
    fg = gates[:, C_HEADS + hh:C_HEADS + hh + 1]
    lf_col = jnp.minimum(fg, 0.0) - jnp.log(1.0 + jnp.exp(-jnp.abs(fg)))
    ri = lax.broadcasted_iota(jnp.int32, (L, L), 0)
    ci = lax.broadcasted_iota(jnp.int32, (L, L), 1)
    tril = ri >= ci
    eye = ri == ci
    lf_row = jnp.sum(jnp.where(eye, lf_col, 0.0), axis=0, keepdims=True)
    ig_row = jnp.sum(jnp.where(eye, ig_col, 0.0), axis=0, keepdims=True)
    b_col = jnp.sum(jnp.where(tril, lf_row, 0.0), axis=1, keepdims=True)
    b_row = jnp.sum(jnp.where(ri <= ci, lf_col, 0.0), axis=0, keepdims=True)
    dmat = jnp.where(tril, b_col - b_row + ig_row, NEG_INF)
    inter = b_col + m_prev
    m = jnp.maximum(inter, jnp.max(dmat, axis=1, keepdims=True))
    pw = jnp.exp(dmat - m)
    qb, kb, vb = q.astype(BF16), k.astype(BF16), v.astype(BF16)
    sc = lax.dot_general(qb, kb, (((1,), (1,)), ((), ())), preferred_element_type=F32) * pw
    carry_w = jnp.exp(inter - m)
    num = (jnp.dot(sc.astype(BF16), vb, preferred_element_type=F32)
           + carry_w * jnp.dot(qb, c_mem.astype(BF16), preferred_element_type=F32))
    den = jnp.sum(sc, axis=1, keepdims=True) + carry_w * jnp.sum(q * c_norm, axis=1, keepdims=True)
    h = num / jnp.maximum(jnp.abs(den), jnp.exp(-m))
    last = n_valid - 1
    m_new = m[last:last + 1, :]
    b_last = b_col[last:last + 1, :]
    ws = jnp.exp(b_last - b_col + ig_col - m_new)
    if n_valid < L:
        ws = jnp.where(lax.broadcasted_iota(jnp.int32, (L, 1), 0) < n_valid, ws, 0.0)
    decay = jnp.exp(b_last + m_prev - m_new)
    wk = ws * k
    c_mem_new = decay * c_mem + lax.dot_general(
        wk.astype(BF16), vb, (((0,), (0,)), ((), ())), preferred_element_type=F32)
    c_norm_new = decay * c_norm + jnp.sum(wk, axis=0, keepdims=True)
    return h, c_mem_new, c_norm_new, m_new


def _l1_prompt_kernel(x_ref, w_ref, bias_ref, h_ref, cmem_ref, cnorm_ref, cmax_ref,
                      c_sc, n_sc, m_sc, *, tm, nblk):
    j = pl.program_id(1)

    @pl.when(j == 0)
    def _():
        c_sc[...] = jnp.zeros_like(c_sc)
        n_sc[...] = jnp.zeros_like(n_sc)
        m_sc[...] = jnp.zeros_like(m_sc)

    proj = jnp.dot(x_ref[0].astype(BF16), w_ref[...], preferred_element_type=F32)
    gates = proj[:, 3 * MIX:] + bias_ref[...]
    kscale = C_HEAD_DIM ** -0.5
    for c in range(tm // BLOCK):
        rows = slice(c * BLOCK, (c + 1) * BLOCK)
        for hh in range(C_HEADS):
            cols = slice(hh * C_HEAD_DIM, (hh + 1) * C_HEAD_DIM)
            q = proj[rows, cols]
            k = proj[rows, MIX + hh * C_HEAD_DIM: MIX + (hh + 1) * C_HEAD_DIM] * kscale
            v = proj[rows, 2 * MIX + hh * C_HEAD_DIM: 2 * MIX + (hh + 1) * C_HEAD_DIM]
            h, c_new, n_new, m_new = _mlstm_chunk(
                q, k, v, gates[rows], hh, c_sc[hh], n_sc[hh:hh + 1, :], m_sc[hh:hh + 1, 0:1], BLOCK)
            h_ref[0, rows, cols] = h
            c_sc[hh] = c_new
            n_sc[hh:hh + 1, :] = n_new
            m_sc[hh:hh + 1, :] = jnp.broadcast_to(m_new, (1, LANES))

    @pl.when(j == nblk - 1)
    def _():
        cmem_ref[0] = c_sc[...]
        cnorm_ref[0] = n_sc[...]
        cmax_ref[0] = m_sc[...]


def _l1_prompt(x, w_qkvg, bias_row):
    bn, s, d = x.shape
    tm = 512
    nblk = s // tm
    ncol = w_qkvg.shape[1]
    return pl.pallas_call(
        functools.partial(_l1_prompt_kernel, tm=tm, nblk=nblk),
        grid=(bn, nblk),
        in_specs=[pl.BlockSpec((1, tm, d), lambda b, j: (b, j, 0)),
                  pl.BlockSpec((d, ncol), lambda b, j: (0, 0)),
                  pl.BlockSpec((1, LANES), lambda b, j: (0, 0))],
        out_specs=[pl.BlockSpec((1, tm, MIX), lambda b, j: (b, j, 0)),
                   pl.BlockSpec((1, C_HEADS, C_HEAD_DIM, C_HEAD_DIM), lambda b, j: (b, 0, 0, 0)),
                   pl.BlockSpec((1, TPAD, C_HEAD_DIM), lambda b, j: (b, 0, 0)),
                   pl.BlockSpec((1, TPAD, LANES), lambda b, j: (b, 0, 0))],
        out_shape=[jax.ShapeDtypeStruct((bn, s, MIX), F32),
                   jax.ShapeDtypeStruct((bn, C_HEADS, C_HEAD_DIM, C_HEAD_DIM), F32),
                   jax.ShapeDtypeStruct((bn, TPAD, C_HEAD_DIM), F32),
                   jax.ShapeDtypeStruct((bn, TPAD, LANES), F32)],
        scratch_shapes=[pltpu.VMEM((C_HEADS, C_HEAD_DIM, C_HEAD_DIM), F32),
                        pltpu.VMEM((TPAD, C_HEAD_DIM), F32),
                        pltpu.VMEM((TPAD, LANES), F32)],
        compiler_params=_cparams(("arbitrary", "arbitrary")),
        name="l1_prompt_mlstm",
    )(x, w_qkvg, bias_row)


def _l1_decode_kernel(p_ref, bias_ref, cm_ref, cn_ref, cx_ref, h_ref, cmo_ref, cno_ref, cxo_ref,
                      *, sb, t_new):
    kscale = C_HEAD_DIM ** -0.5
    for s_i in range(sb):
        rows = slice(s_i * TPAD, (s_i + 1) * TPAD)
        gates = p_ref[rows, 3 * MIX:] + bias_ref[...]
        for hh in range(C_HEADS):
            cols = slice(hh * C_HEAD_DIM, (hh + 1) * C_HEAD_DIM)
            q = p_ref[rows, cols]
            k = p_ref[rows, MIX + hh * C_HEAD_DIM: MIX + (hh + 1) * C_HEAD_DIM] * kscale
            v = p_ref[rows, 2 * MIX + hh * C_HEAD_DIM: 2 * MIX + (hh + 1) * C_HEAD_DIM]
            h, c_new, n_new, m_new = _mlstm_chunk(
                q, k, v, gates, hh, cm_ref[s_i, hh], cn_ref[s_i, hh:hh + 1, :],
                cx_ref[s_i, hh:hh + 1, 0:1], t_new)
            h_ref[rows, cols] = h
            cmo_ref[s_i, hh] = c_new
            cno_ref[s_i, hh:hh + 1, :] = n_new
            cxo_ref[s_i, hh:hh + 1, :] = jnp.broadcast_to(m_new, (1, LANES))


def _l1_decode(proj, bias_row, c_mem, c_norm, c_max_pad, t_new):
    nseq = c_mem.shape[0]
    sb = 2
    n = nseq * TPAD
    return pl.pallas_call(
        functools.partial(_l1_decode_kernel, sb=sb, t_new=t_new),
        grid=(nseq // sb,),
        in_specs=[pl.BlockSpec((sb * TPAD, proj.shape[1]), lambda i: (i, 0)),
                  pl.BlockSpec((1, LANES), lambda i: (0, 0)),
                  pl.BlockSpec((sb, C_HEADS, C_HEAD_DIM, C_HEAD_DIM), lambda i: (i, 0, 0, 0)),
                  pl.BlockSpec((sb, C_HEADS, C_HEAD_DIM), lambda i: (i, 0, 0)),
                  pl.BlockSpec((sb, C_HEADS, LANES), lambda i: (i, 0, 0))],
        out_specs=[pl.BlockSpec((sb * TPAD, MIX), lambda i: (i, 0)),
                   pl.BlockSpec((sb, C_HEADS, C_HEAD_DIM, C_HEAD_DIM), lambda i: (i, 0, 0, 0)),
                   pl.BlockSpec((sb, C_HEADS, C_HEAD_DIM), lambda i: (i, 0, 0)),
                   pl.BlockSpec((sb, C_HEADS, LANES), lambda i: (i, 0, 0))],
        out_shape=[jax.ShapeDtypeStruct((n, MIX), F32),
                   jax.ShapeDtypeStruct(c_mem.shape, F32),
                   jax.ShapeDtypeStruct(c_norm.shape, F32),
                   jax.ShapeDtypeStruct(c_max_pad.shape, F32)],
        compiler_params=_cparams(("arbitrary",)),
        name="l1_decode_mlstm",
    )(proj, bias_row, c_mem, c_norm, c_max_pad)


def _l1_out_kernel(x_ref, h_ref, woz_ref, mhg_ref, wo_ref, g_ref, b_ref, y_ref):
    x = x_ref[...]
    oz = jnp.dot(x.astype(BF16), woz_ref[...], preferred_element_type=F32)
    parts = []
    for hh in range(C_HEADS):
        cols = slice(hh * C_HEAD_DIM, (hh + 1) * C_HEAD_DIM)
        hg = jax.nn.sigmoid(oz[:, cols]) * h_ref[:, cols]
        mu = jnp.mean(hg, axis=-1, keepdims=True)
        c = hg - mu
        var = jnp.mean(c * c, axis=-1, keepdims=True)
        parts.append(c * lax.rsqrt(var + MH_EPS))
    hn = jnp.concatenate(parts, axis=1) * mhg_ref[...]
    z = oz[:, MIX:]
    mix = (hn * (z * jax.nn.sigmoid(z))).astype(BF16)
    y = jnp.dot(mix, wo_ref[...], preferred_element_type=F32)
    y_ref[...] = _layer_norm_rows(DN_ALPHA * x + y, g_ref[...], b_ref[...])


def _l1_out(x2, h2, woz, mhg, wo, g, b):
    n, d = x2.shape
    tm = 256
    row = lambda w: pl.BlockSpec((tm, w), lambda i: (i, 0))
    full = lambda a: pl.BlockSpec(a.shape, lambda i: (0, 0))
    return pl.pallas_call(
        _l1_out_kernel,
        grid=(n // tm,),
        in_specs=[row(d), row(MIX), full(woz), full(mhg), full(wo), full(g), full(b)],
        out_specs=row(d),
        out_shape=jax.ShapeDtypeStruct((n, d), F32),
        compiler_params=_cparams(("arbitrary",)),
        name="l1_out",
    )(x2, h2, woz, mhg, wo, g, b)


def _group_cols(g):
    if g == 0:
        return [(0, QW), (QW, QW + KW), (QW + KW, GW)]
    b = g - 1
    qb0 = GW
    kb0 = GW + 3 * QW
    vb0 = kb0 + 3 * KW
    return [(qb0 + b * QW, qb0 + (b + 1) * QW), (kb0 + b * KW, kb0 + (b + 1) * KW),
            (vb0 + b * KW, vb0 + (b + 1) * KW)]


def kernel(x_prompt, x_sample, cache_a_kv, cache_b0_kv, cache_b1_kv, cache_b2_kv, state_c_mem,
           state_c_norm, state_c_max, w_in0, sinks0, w_out0, w_in1, b_gates1, mh_norm1, w_out1,
           ln_g, ln_b):
    bn, s, d = x_prompt.shape
    nseq, t_new, _ = x_sample.shape
    w0 = w_in0[0]
    z0 = 4 * GW
    w_groups = [jnp.concatenate([w0[:, a:b] for a, b in _group_cols(g)], axis=1).astype(BF16)
                for g in range(N_GROUPS)]
    wz0 = w0[:, z0:].astype(BF16)
    wo0 = w_out0[0].astype(BF16)
    sinks = sinks0[0].astype(F32)
    g0, b0 = ln_g[0][None, :], ln_b[0][None, :]
    g1, b1 = ln_g[1][None, :], ln_b[1][None, :]

    tabs_p = _rope_tables(jnp.arange(s))
    outs, lses, states_p = [], [], []
    for g in range(N_GROUPS):
        o, lse, st = _group_attn_prompt(x_prompt, w_groups[g], sinks, tabs_p, DILATIONS[g], g == 0)
        outs.append(o)
        lses.append(lse)
        states_p.append(st)
    xp2 = x_prompt.reshape(bn * s, d)
    y0p = _l0_out(xp2, outs[0], outs[1:], lses[1:], wz0, wo0, g0, b0)

    xs2 = jnp.pad(x_sample, ((0, 0), (0, TPAD - t_new), (0, 0))).reshape(nseq * TPAD, d)
    w_dec = jnp.concatenate(
        [w_groups[g][:, :QW] for g in range(N_GROUPS)] + [w_groups[g][:, QW:QW + KW] for g in range(N_GROUPS)]
        + [w_groups[g][:, QW + KW:] for g in range(N_GROUPS)], axis=1)
    tabs_s = tuple(jnp.tile(t, (nseq, 1)) for t in _rope_tables(PAST_LEN + jnp.arange(TPAD)))
    tn = 512
    hs = _proj(xs2, w_dec, tabs_s, tn, n_rope=(N_GROUPS * (QW + KW)) // tn, n_scaled=(N_GROUPS * QW) // tn)
    sink_rows = jnp.broadcast_to(jnp.repeat(sinks, TPAD)[:, None], (Q_HEADS * TPAD, LANES))
    dec_o, dec_l, states_s = _decode_attn(
        hs, (cache_a_kv, cache_b0_kv, cache_b1_kv, cache_b2_kv), sink_rows, t_new)
    y0s = _l0_out(xs2, dec_o[0], dec_o[1:], dec_l, wz0, wo0, g0, b0)

    w1 = w_in1[0]
    gate_pad = jnp.zeros((d, LANES - 2 * C_HEADS), F32)
    w_qkvg = jnp.concatenate([w1[:, :3 * MIX], w1[:, 5 * MIX:], gate_pad], axis=1).astype(BF16)
    w_oz = w1[:, 3 * MIX:5 * MIX].astype(BF16)
    wo1 = w_out1[0].astype(BF16)
    bias_row = jnp.pad(b_gates1[0].astype(F32), (0, LANES - 2 * C_HEADS))[None, :]
    mhg = mh_norm1[0][None, :]

    hp, cmem_p, cnorm_p, cmax_p = _l1_prompt(y0p.reshape(bn, s, d), w_qkvg, bias_row)
    y1p = _l1_out(y0p, hp.reshape(bn * s, MIX), w_oz, mhg, wo1, g1, b1)

    no_tabs = tuple(jnp.zeros((nseq * TPAD, LANES), F32) for _ in range(3))
    ps = _proj(y0s, w_qkvg, no_tabs, 640)
    cmax_pad = jnp.broadcast_to(state_c_max[0][:, :, None], (nseq, C_HEADS, LANES))
    hsd, cmem_s, cnorm_s, cmax_s = _l1_decode(ps, bias_row, state_c_mem[0], state_c_norm[0], cmax_pad, t_new)
    y1s = _l1_out(y0s, hsd, w_oz, mhg, wo1, g1, b1)

    y_prompt = y1p.reshape(bn, s, d)
    y_sample = y1s.reshape(nseq, TPAD, d)[:, :t_new]
    return (y_prompt, y_sample,
            states_p[0][None], states_p[1][None], states_p[2][None], states_p[3][None],
            cmem_p[None], cnorm_p[:, :C_HEADS][None], cmax_p[:, :C_HEADS, 0][None],
            states_s[0], states_s[1], states_s[2], states_s[3],
            cmem_s[None], cnorm_s[None], cmax_s[:, :, 0][None])
```

```python
import functools

import jax
import jax.numpy as jnp
from jax import lax
from jax.experimental import pallas as pl
from jax.experimental.pallas import tpu as pltpu

F32 = jnp.float32
BF16 = jnp.bfloat16

D_MODEL = 1024
PAST_LEN = 16384
HEAD_DIM = 64
ROT_DIM = HEAD_DIM // 4
ROPE_THETA = 500000.0
BLOCK = 128
Q_HEADS = 8
KV_HEADS = 2
GROUP = Q_HEADS // KV_HEADS
QW = Q_HEADS * HEAD_DIM
KW = KV_HEADS * HEAD_DIM
GW = QW + 2 * KW
DILATIONS = (1, 1, 4, 16)
N_GROUPS = 4
C_HEADS = 4
C_HEAD_DIM = 256
MIX = 1024
DEPTH = 2
DN_ALPHA = (2 * DEPTH) ** 0.25
LN_EPS = 1e-5
MH_EPS = 1e-6
NEG_INF = -1e30
LOG2E = 1.4426950408889634
LN2 = 0.6931471805599453
LANES = 128
TPAD = 8
VMEM_LIMIT = 56 * 1024 * 1024


def _cparams(sem):
    return pltpu.CompilerParams(dimension_semantics=sem, vmem_limit_bytes=VMEM_LIMIT)


def _layer_norm_rows(v, g, b):
    mu = jnp.mean(v, axis=-1, keepdims=True)
    c = v - mu
    var = jnp.mean(c * c, axis=-1, keepdims=True)
    return c * lax.rsqrt(var + LN_EPS) * g + b


def _rope(slab, cos, sin_lo, sin_hi):
    return (slab * cos + pltpu.roll(slab, LANES - ROT_DIM // 2, 1) * sin_lo
            + pltpu.roll(slab, ROT_DIM // 2, 1) * sin_hi)


def _rope_tables(pos):
    half = ROT_DIM // 2
    inv = ROPE_THETA ** (-jnp.arange(half, dtype=F32) / half)
    ang = pos.astype(F32)[:, None] * inv[None, :]
    cos, sin = jnp.cos(ang), jnp.sin(ang)
    lane = jnp.arange(LANES) % HEAD_DIM
    f = lane % half
    cos_t = jnp.where(lane[None, :] < ROT_DIM, cos[:, f], 1.0)
    sin_lo = jnp.where(lane[None, :] < half, -sin[:, f], 0.0)
    sin_hi = jnp.where((lane[None, :] >= half) & (lane[None, :] < ROT_DIM), sin[:, f], 0.0)
    return cos_t.astype(F32), sin_lo.astype(F32), sin_hi.astype(F32)


STEP_TOKENS = 512
X_SLABS = D_MODEL // LANES
UNITS_PER_PROJ = 4
ONES_ROWS = 16


def _rope_t(blk, cos, sin):
    half = ROT_DIM // 2
    x1, x2 = blk[:half], blk[half:ROT_DIM]
    return jnp.concatenate([x1 * cos - x2 * sin, x2 * cos + x1 * sin, blk[ROT_DIM:]], axis=0)


def _group_attn_kernel(sink_ref, *refs, has_sink, dil, tstep, nstep):
    x_refs = refs[:X_SLABS]
    (wt_ref, cos_ref, sin_ref, o_ref, lse_ref, st_ref,
     kprev_sc, vprev_sc, kv_nat_sc, o_sc, st_sc, pt_sc, bias_sc) = refs[X_SLABS:]
    j = pl.program_id(1)
    n_units = tstep // BLOCK
    is_last = j == nstep - 1

    @pl.when(j == 0)
    def _():
        kprev_sc[...] = jnp.zeros_like(kprev_sc)
        vprev_sc[...] = jnp.zeros_like(vprev_sc)

    scale = HEAD_DIM ** -0.5 * LOG2E
    kidx = lax.broadcasted_iota(jnp.int32, (2 * BLOCK, 2 * BLOCK), 0)
    qidx = lax.broadcasted_iota(jnp.int32, (2 * BLOCK, 2 * BLOCK), 1) % BLOCK
    dist = qidx + BLOCK - kidx
    band = (dist >= 0) & (dist <= BLOCK)
    bias_sc[0] = jnp.where(band, 0.0, NEG_INF)
    bias_sc[1] = jnp.where(band & (kidx >= jnp.where(j > 0, 0, BLOCK)), 0.0, NEG_INF)
    lane2 = lax.broadcasted_iota(jnp.int32, (1, 2 * BLOCK), 1)
    ones_rows = jnp.ones((ONES_ROWS, 2 * BLOCK), F32)
    zeros_half = jnp.zeros((HEAD_DIM, 2 * BLOCK), F32)
    nt = (((1,), (1,)), ((), ()))

    def unit_rows(u):
        return pl.ds(u * BLOCK, BLOCK) if dil == 1 else pl.ds(u, BLOCK, stride=dil)

    carried = None
    for qd in range(n_units // UNITS_PER_PROJ):
        units = [UNITS_PER_PROJ * qd + w for w in range(UNITS_PER_PROJ)]
        x_quad = jnp.concatenate(
            [jnp.concatenate([x_refs[c][0, unit_rows(u), :] for c in range(X_SLABS)], axis=1) for u in units],
            axis=0).astype(BF16)
        ht = lax.dot_general(wt_ref[...], x_quad, nt, preferred_element_type=F32)
        cols = slice(qd * UNITS_PER_PROJ * BLOCK, (qd + 1) * UNITS_PER_PROJ * BLOCK)
        cos, sin = cos_ref[:, cols], sin_ref[:, cols]
        qt = [_rope_t(ht[h * HEAD_DIM:(h + 1) * HEAD_DIM], cos, sin) * scale for h in range(Q_HEADS)]
        kt = jnp.concatenate([_rope_t(ht[QW + h * HEAD_DIM:QW + (h + 1) * HEAD_DIM], cos, sin)
                              for h in range(KV_HEADS)], axis=0)
        vt = ht[QW + KW:GW]
        k_rm = kt.T

        if dil > 1:
            @pl.when(is_last)
            def _():
                v_rm = vt.T
                for w, u in enumerate(units):
                    kv_nat_sc[0, unit_rows(u), :] = k_rm[w * BLOCK:(w + 1) * BLOCK]
                    kv_nat_sc[1, unit_rows(u), :] = v_rm[w * BLOCK:(w + 1) * BLOCK]

        def scores(w, u, carried):
            lanes = slice(w * BLOCK, (w + 1) * BLOCK)
            k_cur = k_rm[w * BLOCK:(w + 1) * BLOCK].astype(BF16)
            vt_cur = vt[:, lanes].astype(BF16)
            if dil == 1 and u > 0:
                k_prev, vt_prev = carried
                mask = 0
            else:
                slot = 0 if dil == 1 else u
                k_prev, vt_prev = kprev_sc[slot], vprev_sc[slot]
                mask = 1
            k_cat = jnp.concatenate([k_prev, k_cur], axis=0)
            wq = []
            for c in range(QW // LANES):
                top = jnp.concatenate([qt[2 * c][:, lanes], qt[2 * c + 1][:, lanes]], axis=1)
                wq.append(jnp.concatenate([top, zeros_half] if c < GROUP // 2 else [zeros_half, top], axis=0))
            wq = jnp.concatenate(wq, axis=1).astype(BF16)
            st_sc[u % 2] = jnp.dot(k_cat, wq, preferred_element_type=F32)
            return dict(w=w, u=u, lanes=lanes, k_cur=k_cur, vt_cur=vt_cur, vt_prev=vt_prev, mask=mask)

        def softmax(t):
            par = t["u"] % 2
            ms = []
            for c in range(QW // LANES):
                pc = slice(c * 2 * BLOCK, (c + 1) * 2 * BLOCK)
                st = st_sc[par, :, pc] + bias_sc[t["mask"]]
                m = jnp.max(st, axis=0, keepdims=True)
                if has_sink:
                    sink = jnp.where(lane2 < BLOCK, sink_ref[2 * c], sink_ref[2 * c + 1]) * LOG2E
                    m = jnp.maximum(m, sink)
                ms.append(m)
                pt_sc[par, :, pc] = jnp.exp2(st - m).astype(BF16)
            t["ms"] = ms

        def values(t):
            u, par, lanes = t["u"], t["u"] % 2, t["lanes"]
            lse_rows = []
            for hk in range(KV_HEADS):
                hd = slice(hk * HEAD_DIM, (hk + 1) * HEAD_DIM)
                vt_aug = jnp.concatenate(
                    [jnp.concatenate([t["vt_prev"][hd], t["vt_cur"][hd]], axis=1), ones_rows.astype(BF16)], axis=0)
                acc2 = jnp.dot(vt_aug, pt_sc[par, :, hk * 4 * BLOCK:(hk + 1) * 4 * BLOCK],
                               preferred_element_type=F32)
                for i in range(GROUP // 2):
                    c = hk * (GROUP // 2) + i
                    acc = acc2[:, i * 2 * BLOCK:(i + 1) * 2 * BLOCK]
                    m = t["ms"][c]
                    den = acc[HEAD_DIM:HEAD_DIM + 1]
                    if has_sink:
                        sink = jnp.where(lane2 < BLOCK, sink_ref[2 * c], sink_ref[2 * c + 1]) * LOG2E
                        den = den + jnp.exp2(sink - m)
                    ot = acc[:HEAD_DIM] / den
                    lse = m * LN2 + jnp.log(den)
                    lse_rows += [lse[:, :BLOCK], lse[:, BLOCK:]]
                    slab_t = jnp.concatenate([ot[:, :BLOCK], ot[:, BLOCK:]], axis=0)
                    o_sc[c, unit_rows(u), :] = slab_t.T
            lse_t = jnp.concatenate(lse_rows + [jnp.zeros((LANES - Q_HEADS, BLOCK), F32)], axis=0)
            lse_ref[0, unit_rows(u), :] = lse_t.T
            if dil > 1:
                kprev_sc[u] = t["k_cur"]
                vprev_sc[u] = t["vt_cur"]
            elif u == n_units - 1:
                kprev_sc[0] = t["k_cur"]
                vprev_sc[0] = t["vt_cur"]

                @pl.when(is_last)
                def _():
                    st_ref[0, :KW, :] = kt[:, lanes]
                    st_ref[0, KW:, :] = vt[:, lanes]

        pending = None
        for w, u in enumerate(units):
            t = scores(w, u, carried)
            carried = (t["k_cur"], t["vt_cur"])
            if pending is not None:
                softmax(pending)
                values(pending)
            pending = t
        softmax(pending)
        values(pending)

    for c in range(QW // LANES):
        o_ref[0, :, c * LANES:(c + 1) * LANES] = o_sc[c].astype(o_ref.dtype)

    if dil > 1:
        @pl.when(is_last)
        def _():
            st_ref[0, :KW, :] = kv_nat_sc[0].T
            st_ref[0, KW:, :] = kv_nat_sc[1].T


def _group_attn_prompt(x, w_g, sinks, dil, has_sink):
    bn, s, d = x.shape
    span = BLOCK * dil
    tstep = max(STEP_TOKENS, span)
    nstep = s // tstep
    if dil == 1:
        pos = jnp.arange(s)
    else:
        pos = (jnp.arange(nstep)[:, None, None] * tstep + jnp.arange(BLOCK)[None, None, :] * dil
               + jnp.arange(dil)[None, :, None]).reshape(s)
    half = ROT_DIM // 2
    inv = ROPE_THETA ** (-jnp.arange(half, dtype=F32) / half)
    ang = pos.astype(F32)[:, None] * inv[None, :]
    cos_t, sin_t = jnp.cos(ang).T, jnp.sin(ang).T
    tab_spec = pl.BlockSpec((half, tstep), lambda b, j: (0, j))
    nprev = 1 if dil == 1 else dil
    o, lse, st = pl.pallas_call(
        functools.partial(_group_attn_kernel, has_sink=has_sink, dil=dil, tstep=tstep, nstep=nstep),
        grid=(bn, nstep),
        in_specs=([pl.BlockSpec(memory_space=pltpu.SMEM)]
                  + [pl.BlockSpec((1, tstep, LANES), functools.partial(lambda c, b, j: (b, j, c), c))
                     for c in range(X_SLABS)]
                  + [pl.BlockSpec((GW, d), lambda b, j: (0, 0)), tab_spec, tab_spec]),
        out_specs=[
            pl.BlockSpec((1, tstep, QW), lambda b, j: (b, j, 0)),
            pl.BlockSpec((1, tstep, LANES), lambda b, j: (b, j, 0)),
            pl.BlockSpec((1, 2 * KW, span), lambda b, j: (b, 0, 0)),
        ],
        out_shape=[
            jax.ShapeDtypeStruct((bn, s, QW), BF16),
            jax.ShapeDtypeStruct((bn, s, LANES), F32),
            jax.ShapeDtypeStruct((bn, 2 * KW, span), F32),
        ],
        scratch_shapes=[pltpu.VMEM((nprev, BLOCK, KW), BF16), pltpu.VMEM((nprev, KW, BLOCK), BF16),
                        pltpu.VMEM((2, span, LANES), F32), pltpu.VMEM((QW // LANES, tstep, LANES), F32),
                        pltpu.VMEM((2, 2 * BLOCK, Q_HEADS * BLOCK), F32),
                        pltpu.VMEM((2, 2 * BLOCK, Q_HEADS * BLOCK), BF16),
                        pltpu.VMEM((2, 2 * BLOCK, 2 * BLOCK), F32)],
        compiler_params=_cparams(("arbitrary", "arbitrary")),
        name=f"l0_group_attn_d{dil}_{'sink' if has_sink else 'nosink'}",
    )(sinks, *([x] * X_SLABS), w_g.T, cos_t, sin_t)
    st = jnp.transpose(st.reshape(bn, 2, KV_HEADS, HEAD_DIM, span), (0, 4, 1, 2, 3))
    return o.reshape(bn * s, QW), lse.reshape(bn * s, LANES), st


def _l0_out_kernel(x_ref, oa_ref, o0_ref, o1_ref, o2_ref, l0_ref, l1_ref, l2_ref,
                   wz_ref, wo_ref, g_ref, b_ref, y_ref):
    x = x_ref[...]
    z = jnp.dot(x.astype(BF16), wz_ref[...], preferred_element_type=F32)
    l0, l1, l2 = l0_ref[...], l1_ref[...], l2_ref[...]
    m = jnp.maximum(jnp.maximum(l0, l1), l2)
    e0, e1, e2 = jnp.exp(l0 - m), jnp.exp(l1 - m), jnp.exp(l2 - m)
    inv = 1.0 / (e0 + e1 + e2)
    w0, w1, w2 = e0 * inv, e1 * inv, e2 * inv
    tm = x.shape[0]
    parts = []
    for hq in range(Q_HEADS):
        c = slice(hq * HEAD_DIM, (hq + 1) * HEAD_DIM)
        bshape = (tm, HEAD_DIM)
        parts.append(jnp.broadcast_to(w0[:, hq:hq + 1], bshape) * o0_ref[:, c].astype(F32)
                     + jnp.broadcast_to(w1[:, hq:hq + 1], bshape) * o1_ref[:, c].astype(F32)
                     + jnp.broadcast_to(w2[:, hq:hq + 1], bshape) * o2_ref[:, c].astype(F32))
    mix = jnp.concatenate([oa_ref[...].astype(F32)] + parts, axis=1)
    mix = (mix * (z * jax.nn.sigmoid(z))).astype(BF16)
    y = jnp.dot(mix, wo_ref[...], preferred_element_type=F32)
    y_ref[...] = _layer_norm_rows(DN_ALPHA * x + y, g_ref[...], b_ref[...])


def _l0_out(x2, oa, obs, lses, wz, wo, g, b):
    n, d = x2.shape
    tm = 256
    row = lambda w: pl.BlockSpec((tm, w), lambda i: (i, 0))
    full = lambda a: pl.BlockSpec(a.shape, lambda i: (0, 0))
    return pl.pallas_call(
        _l0_out_kernel,
        grid=(n // tm,),
        in_specs=[row(d), row(QW), row(QW), row(QW), row(QW), row(LANES), row(LANES), row(LANES),
                  full(wz), full(wo), full(g), full(b)],
        out_specs=row(d),
        out_shape=jax.ShapeDtypeStruct((n, d), F32),
        compiler_params=_cparams(("arbitrary",)),
        name="l0_out",
    )(x2, oa, *obs, *lses, wz, wo, g, b)


def _proj_kernel(x_ref, w_ref, cos_ref, slo_ref, shi_ref, h_ref, *, n_rope, n_scaled, tn):
    j = pl.program_id(0)
    h = jnp.dot(x_ref[...].astype(BF16), w_ref[...], preferred_element_type=F32)
    if n_rope:
        cos, slo, shi = cos_ref[...], slo_ref[...], shi_ref[...]
        roped = jnp.concatenate(
            [_rope(h[:, c * LANES:(c + 1) * LANES], cos, slo, shi) for c in range(tn // LANES)], axis=1)
        h = jnp.where(j < n_rope, roped, h)
        h = h * jnp.where(j < n_scaled, HEAD_DIM ** -0.5, 1.0)
    h_ref[...] = h


def _proj(x2, w, tabs, tn, n_rope=0, n_scaled=0):
    n, d = x2.shape
    ncol = w.shape[1]
    tab_spec = pl.BlockSpec((n, LANES), lambda j: (0, 0))
    return pl.pallas_call(
        functools.partial(_proj_kernel, n_rope=n_rope, n_scaled=n_scaled, tn=tn),
        grid=(ncol // tn,),
        in_specs=[pl.BlockSpec((n, d), lambda j: (0, 0)),
                  pl.BlockSpec((d, tn), lambda j: (0, j)),
                  tab_spec, tab_spec, tab_spec],
        out_specs=pl.BlockSpec((n, tn), lambda j: (0, j)),
        out_shape=jax.ShapeDtypeStruct((n, ncol), F32),
        compiler_params=_cparams(("arbitrary",)),
        name=f"decode_proj_{ncol}",
    )(x2, w, *tabs)


def _decode_attn_kernel(sink_ref, h_ref, ca_ref, c0_ref, c1_ref, c2_ref,
                        oa_ref, o0_ref, o1_ref, o2_ref, l0_ref, l1_ref, l2_ref,
                        na_ref, n0_ref, n1_ref, n2_ref, *, sb, t_new):
    nrow = Q_HEADS * TPAD
    lane8 = lax.broadcasted_iota(jnp.int32, (TPAD, LANES), 1)
    lo_half = lane8 < HEAD_DIM
    lane_c = lax.broadcasted_iota(jnp.int32, (2 * KW, LANES), 1)
    k_base = N_GROUPS * QW
    v_base = k_base + N_GROUPS * KW
    caches = (ca_ref, c0_ref, c1_ref, c2_ref)
    new_caches = (na_ref, n0_ref, n1_ref, n2_ref)
    o_refs = (oa_ref, o0_ref, o1_ref, o2_ref)
    l_refs = (None, l0_ref, l1_ref, l2_ref)
    zpad = jnp.zeros((LANES - TPAD, 2 * KW), F32)
    nt = (((1,), (1,)), ((), ()))

    for g in range(N_GROUPS):
        dil = DILATIONS[g]
        nkeys = BLOCK * dil
        tok = lax.broadcasted_iota(jnp.int32, (nrow, nkeys), 0) % TPAD
        col = lax.broadcasted_iota(jnp.int32, (nrow, nkeys), 1)
        mask_c = (((col - tok) & (dil - 1)) == 0) & (col >= tok)
        tok_n = lax.broadcasted_iota(jnp.int32, (nrow, LANES), 0) % TPAD
        col_n = lax.broadcasted_iota(jnp.int32, (nrow, LANES), 1)
        mask_n = (((tok_n - col_n) & (dil - 1)) == 0) & (col_n <= tok_n) & (col_n < t_new)
        cref, nref = caches[g], new_caches[g]
        for s_i in range(sb):
            rows = slice(s_i * TPAD, (s_i + 1) * TPAD)
            blocks = []
            for c in range(QW // LANES):
                slab = h_ref[rows, g * QW + c * LANES: g * QW + (c + 1) * LANES]
                swapped = pltpu.roll(slab, HEAD_DIM, 1)
                if c < QW // LANES // 2:
                    blocks += [jnp.where(lo_half, slab, 0.0), jnp.where(lo_half, swapped, 0.0)]
                else:
                    blocks += [jnp.where(lo_half, 0.0, swapped), jnp.where(lo_half, 0.0, slab)]
            qm = jnp.concatenate(blocks, axis=0).astype(BF16)
            k_new = h_ref[rows, k_base + g * KW: k_base + (g + 1) * KW]
            v_new = h_ref[rows, v_base + g * KW: v_base + (g + 1) * KW]
            kv_new = jnp.concatenate([jnp.concatenate([k_new, v_new], axis=1), zpad], axis=0)
            kv_new_b = kv_new.astype(BF16)
            cache = cref[s_i]
            kt = cache[:KW].astype(BF16)
            vt = cache[KW:].astype(BF16)
            s_c = jnp.where(mask_c, jnp.dot(qm, kt, preferred_element_type=F32), NEG_INF)
            s_n = jnp.where(mask_n, lax.dot_general(qm, kv_new_b[:, :KW], nt, preferred_element_type=F32), NEG_INF)
            m = jnp.maximum(jnp.max(s_c, axis=-1, keepdims=True), jnp.max(s_n, axis=-1, keepdims=True))
            if g == 0:
                sink = sink_ref[:, 0:1]
                m = jnp.maximum(m, sink)
            p_c = jnp.exp(s_c - m)
            p_n = jnp.exp(s_n - m)
            den = jnp.sum(p_c, axis=-1, keepdims=True) + jnp.sum(p_n, axis=-1, keepdims=True)
            if g == 0:
                den = den + jnp.exp(sink - m)
            o = (lax.dot_general(p_c.astype(BF16), vt, nt, preferred_element_type=F32)
                 + jnp.dot(p_n.astype(BF16), kv_new_b[:, KW:], preferred_element_type=F32)) / den
            for c in range(QW // LANES):
                ev = o[(2 * c) * TPAD:(2 * c + 1) * TPAD]
                od = o[(2 * c + 1) * TPAD:(2 * c + 2) * TPAD]
                if c < QW // LANES // 2:
                    slab = jnp.where(lo_half, ev, pltpu.roll(od, HEAD_DIM, 1))
                else:
                    slab = jnp.where(lo_half, pltpu.roll(ev, HEAD_DIM, 1), od)
                o_refs[g][rows, c * LANES:(c + 1) * LANES] = slab.astype(o_refs[g].dtype)
            if g > 0:
                lse = m + jnp.log(den)
                lse_acc = jnp.zeros((TPAD, LANES), F32)
                for hq in range(Q_HEADS):
                    lse_acc = jnp.where(lane8 == hq, lse[hq * TPAD:(hq + 1) * TPAD], lse_acc)
                l_refs[g][rows, :] = lse_acc
            shifted = pltpu.roll(cache, nkeys - t_new, 1)
            new_cols = pltpu.roll(kv_new.T, LANES - t_new, 1)
            nref[s_i] = shifted
            nref[s_i, :, nkeys - LANES:] = jnp.where(lane_c >= LANES - t_new, new_cols, shifted[:, nkeys - LANES:])


def _cache_rows_minor(c):
    nseq, rows = c.shape[1], c.shape[2]
    return jnp.transpose(c[0], (0, 2, 3, 4, 1)).reshape(nseq, 2 * KW, rows)


def _cache_rows_major(c):
    nseq, _, rows = c.shape
    return jnp.transpose(c.reshape(nseq, 2, KV_HEADS, HEAD_DIM, rows), (0, 4, 1, 2, 3))[None]


def _decode_attn(h, caches, sink_rows, t_new):
    cts = [_cache_rows_minor(c) for c in caches]
    nseq = cts[0].shape[0]
    sb = 2
    n = nseq * TPAD
    row = lambda w: pl.BlockSpec((sb * TPAD, w), lambda i: (i, 0))
    cspec = lambda c: pl.BlockSpec((sb,) + c.shape[1:], lambda i: (i, 0, 0))
    outs = pl.pallas_call(
        functools.partial(_decode_attn_kernel, sb=sb, t_new=t_new),
        grid=(nseq // sb,),
        in_specs=[pl.BlockSpec(sink_rows.shape, lambda i: (0, 0)), row(h.shape[1])] + [cspec(c) for c in cts],
        out_specs=[row(QW)] * 4 + [row(LANES)] * 3 + [cspec(c) for c in cts],
        out_shape=([jax.ShapeDtypeStruct((n, QW), BF16)] * 4 + [jax.ShapeDtypeStruct((n, LANES), F32)] * 3
                   + [jax.ShapeDtypeStruct(c.shape, F32) for c in cts]),
        compiler_params=_cparams(("arbitrary",)),
        name="l0_decode_attn",
    )(sink_rows, h, *cts)
    return outs[:4], outs[4:7], [_cache_rows_major(c) for c in outs[7:]]


def _mlstm_chunk(q, k, v, gates, hh, c_mem, c_norm, m_prev, n_valid):
    L = q.shape[0]
    ig_col = gates[:, hh:hh + 1]
    fg = gates[:, C_HEADS + hh:C_HEADS + hh + 1]
    lf_col = jnp.minimum(fg, 0.0) - jnp.log(1.0 + jnp.exp(-jnp.abs(fg)))
    ri = lax.broadcasted_iota(jnp.int32, (L, L), 0)
    ci = lax.broadcasted_iota(jnp.int32, (L, L), 1)
    tril = ri >= ci
    eye = ri == ci
    lf_row = jnp.sum(jnp.where(eye, lf_col, 0.0), axis=0, keepdims=True)
    ig_row = jnp.sum(jnp.where(eye, ig_col, 0.0), axis=0, keepdims=True)
    b_col = jnp.sum(jnp.where(tril, lf_row, 0.0), axis=1, keepdims=True)
    b_row = jnp.sum(jnp.where(ri <= ci, lf_col, 0.0), axis=0, keepdims=True)
    dmat = jnp.where(tril, b_col - b_row + ig_row, NEG_INF)
    inter = b_col + m_prev
    m = jnp.maximum(inter, jnp.max(dmat, axis=1, keepdims=True))
    pw = jnp.exp(dmat - m)
    qb, kb, vb = q.astype(BF16), k.astype(BF16), v.astype(BF16)
    sc = lax.dot_general(qb, kb, (((1,), (1,)), ((), ())), preferred_element_type=F32) * pw
    carry_w = jnp.exp(inter - m)
    num = (jnp.dot(sc.astype(BF16), vb, preferred_element_type=F32)
           + carry_w * jnp.dot(qb, c_mem.astype(BF16), preferred_element_type=F32))
    den = jnp.sum(sc, axis=1, keepdims=True) + carry_w * jnp.sum(q * c_norm, axis=1, keepdims=True)
    h = num / jnp.maximum(jnp.abs(den), jnp.exp(-m))
    last = n_valid - 1
    m_new = m[last:last + 1, :]
    b_last = b_col[last:last + 1, :]
    ws = jnp.exp(b_last - b_col + ig_col - m_new)
    if n_valid < L:
        ws = jnp.where(lax.broadcasted_iota(jnp.int32, (L, 1), 0) < n_valid, ws, 0.0)
    decay = jnp.exp(b_last + m_prev - m_new)
    wk = ws * k
    c_mem_new = decay * c_mem + lax.dot_general(
        wk.astype(BF16), vb, (((0,), (0,)), ((), ())), preferred_element_type=F32)
    c_norm_new = decay * c_norm + jnp.sum(wk, axis=0, keepdims=True)
    return h, c_mem_new, c_norm_new, m_new


def _l1_prompt_kernel(x_ref, w_ref, bias_ref, h_ref, cmem_ref, cnorm_ref, cmax_ref,
                      c_sc, n_sc, m_sc, *, tm, nblk):
    j = pl.program_id(1)

    @pl.when(j == 0)
    def _():
        c_sc[...] = jnp.zeros_like(c_sc)
        n_sc[...] = jnp.zeros_like(n_sc)
        m_sc[...] = jnp.zeros_like(m_sc)

    proj = jnp.dot(x_ref[0].astype(BF16), w_ref[...], preferred_element_type=F32)
    gates = proj[:, 3 * MIX:] + bias_ref[...]
    kscale = C_HEAD_DIM ** -0.5
    for c in range(tm // BLOCK):
        rows = slice(c * BLOCK, (c + 1) * BLOCK)
        for hh in range(C_HEADS):
            cols = slice(hh * C_HEAD_DIM, (hh + 1) * C_HEAD_DIM)
            q = proj[rows, cols]
            k = proj[rows, MIX + hh * C_HEAD_DIM: MIX + (hh + 1) * C_HEAD_DIM] * kscale
            v = proj[rows, 2 * MIX + hh * C_HEAD_DIM: 2 * MIX + (hh + 1) * C_HEAD_DIM]
            h, c_new, n_new, m_new = _mlstm_chunk(
                q, k, v, gates[rows], hh, c_sc[hh], n_sc[hh:hh + 1, :], m_sc[hh:hh + 1, 0:1], BLOCK)
            h_ref[0, rows, cols] = h
            c_sc[hh] = c_new
            n_sc[hh:hh + 1, :] = n_new
            m_sc[hh:hh + 1, :] = jnp.broadcast_to(m_new, (1, LANES))

    @pl.when(j == nblk - 1)
    def _():
        cmem_ref[0] = c_sc[...]
        cnorm_ref[0] = n_sc[...]
        cmax_ref[0] = m_sc[...]


def _l1_prompt(x, w_qkvg, bias_row):
    bn, s, d = x.shape
    tm = 512
    nblk = s // tm
    ncol = w_qkvg.shape[1]
    return pl.pallas_call(
        functools.partial(_l1_prompt_kernel, tm=tm, nblk=nblk),
        grid=(bn, nblk),
        in_specs=[pl.BlockSpec((1, tm, d), lambda b, j: (b, j, 0)),
                  pl.BlockSpec((d, ncol), lambda b, j: (0, 0)),
                  pl.BlockSpec((1, LANES), lambda b, j: (0, 0))],
        out_specs=[pl.BlockSpec((1, tm, MIX), lambda b, j: (b, j, 0)),
                   pl.BlockSpec((1, C_HEADS, C_HEAD_DIM, C_HEAD_DIM), lambda b, j: (b, 0, 0, 0)),
                   pl.BlockSpec((1, TPAD, C_HEAD_DIM), lambda b, j: (b, 0, 0)),
                   pl.BlockSpec((1, TPAD, LANES), lambda b, j: (b, 0, 0))],
        out_shape=[jax.ShapeDtypeStruct((bn, s, MIX), F32),
                   jax.ShapeDtypeStruct((bn, C_HEADS, C_HEAD_DIM, C_HEAD_DIM), F32),
                   jax.ShapeDtypeStruct((bn, TPAD, C_HEAD_DIM), F32),
                   jax.ShapeDtypeStruct((bn, TPAD, LANES), F32)],
        scratch_shapes=[pltpu.VMEM((C_HEADS, C_HEAD_DIM, C_HEAD_DIM), F32),
                        pltpu.VMEM((TPAD, C_HEAD_DIM), F32),
                        pltpu.VMEM((TPAD, LANES), F32)],
        compiler_params=_cparams(("arbitrary", "arbitrary")),
        name="l1_prompt_mlstm",
    )(x, w_qkvg, bias_row)


def _l1_decode_kernel(p_ref, bias_ref, cm_ref, cn_ref, cx_ref, h_ref, cmo_ref, cno_ref, cxo_ref,
                      *, sb, t_new):
    kscale = C_HEAD_DIM ** -0.5
    for s_i in range(sb):
        rows = slice(s_i * TPAD, (s_i + 1) * TPAD)
        gates = p_ref[rows, 3 * MIX:] + bias_ref[...]
        for hh in range(C_HEADS):
            cols = slice(hh * C_HEAD_DIM, (hh + 1) * C_HEAD_DIM)
            q = p_ref[rows, cols]
            k = p_ref[rows, MIX + hh * C_HEAD_DIM: MIX + (hh + 1) * C_HEAD_DIM] * kscale
            v = p_ref[rows, 2 * MIX + hh * C_HEAD_DIM: 2 * MIX + (hh + 1) * C_HEAD_DIM]
            h, c_new, n_new, m_new = _mlstm_chunk(
                q, k, v, gates, hh, cm_ref[s_i, hh], cn_ref[s_i, hh:hh + 1, :],
                cx_ref[s_i, hh:hh + 1, 0:1], t_new)
            h_ref[rows, cols] = h
            cmo_ref[s_i, hh] = c_new
            cno_ref[s_i, hh:hh + 1, :] = n_new
            cxo_ref[s_i, hh:hh + 1, :] = jnp.broadcast_to(m_new, (1, LANES))


def _l1_decode(proj, bias_row, c_mem, c_norm, c_max_pad, t_new):
    nseq = c_mem.shape[0]
    sb = 2
    n = nseq * TPAD
    return pl.pallas_call(
        functools.partial(_l1_decode_kernel, sb=sb, t_new=t_new),
        grid=(nseq // sb,),
        in_specs=[pl.BlockSpec((sb * TPAD, proj.shape[1]), lambda i: (i, 0)),
                  pl.BlockSpec((1, LANES), lambda i: (0, 0)),
                  pl.BlockSpec((sb, C_HEADS, C_HEAD_DIM, C_HEAD_DIM), lambda i: (i, 0, 0, 0)),
                  pl.BlockSpec((sb, C_HEADS, C_HEAD_DIM), lambda i: (i, 0, 0)),
                  pl.BlockSpec((sb, C_HEADS, LANES), lambda i: (i, 0, 0))],
        out_specs=[pl.BlockSpec((sb * TPAD, MIX), lambda i: (i, 0)),
                   pl.BlockSpec((sb, C_HEADS, C_HEAD_DIM, C_HEAD_DIM), lambda i: (i, 0, 0, 0)),
                   pl.BlockSpec((sb, C_HEADS, C_HEAD_DIM), lambda i: (i, 0, 0)),
                   pl.BlockSpec((sb, C_HEADS, LANES), lambda i: (i, 0, 0))],
        out_shape=[jax.ShapeDtypeStruct((n, MIX), F32),
                   jax.ShapeDtypeStruct(c_mem.shape, F32),
                   jax.ShapeDtypeStruct(c_norm.shape, F32),
                   jax.ShapeDtypeStruct(c_max_pad.shape, F32)],
        compiler_params=_cparams(("arbitrary",)),
        name="l1_decode_mlstm",
    )(proj, bias_row, c_mem, c_norm, c_max_pad)


def _l1_out_kernel(x_ref, h_ref, woz_ref, mhg_ref, wo_ref, g_ref, b_ref, y_ref):
    x = x_ref[...]
    oz = jnp.dot(x.astype(BF16), woz_ref[...], preferred_element_type=F32)
    parts = []
    for hh in range(C_HEADS):
        cols = slice(hh * C_HEAD_DIM, (hh + 1) * C_HEAD_DIM)
        hg = jax.nn.sigmoid(oz[:, cols]) * h_ref[:, cols]
        mu = jnp.mean(hg, axis=-1, keepdims=True)
        c = hg - mu
        var = jnp.mean(c * c, axis=-1, keepdims=True)
        parts.append(c * lax.rsqrt(var + MH_EPS))
    hn = jnp.concatenate(parts, axis=1) * mhg_ref[...]
    z = oz[:, MIX:]
    mix = (hn * (z * jax.nn.sigmoid(z))).astype(BF16)
    y = jnp.dot(mix, wo_ref[...], preferred_element_type=F32)
    y_ref[...] = _layer_norm_rows(DN_ALPHA * x + y, g_ref[...], b_ref[...])


def _l1_out(x2, h2, woz, mhg, wo, g, b):
    n, d = x2.shape
    tm = 256
    row = lambda w: pl.BlockSpec((tm, w), lambda i: (i, 0))
    full = lambda a: pl.BlockSpec(a.shape, lambda i: (0, 0))
    return pl.pallas_call(
        _l1_out_kernel,
        grid=(n // tm,),
        in_specs=[row(d), row(MIX), full(woz), full(mhg), full(wo), full(g), full(b)],
        out_specs=row(d),
        out_shape=jax.ShapeDtypeStruct((n, d), F32),
        compiler_params=_cparams(("arbitrary",)),
        name="l1_out",
    )(x2, h2, woz, mhg, wo, g, b)


def _group_cols(g):
    if g == 0:
        return [(0, QW), (QW, QW + KW), (QW + KW, GW)]
    b = g - 1
    qb0 = GW
    kb0 = GW + 3 * QW
    vb0 = kb0 + 3 * KW
    return [(qb0 + b * QW, qb0 + (b + 1) * QW), (kb0 + b * KW, kb0 + (b + 1) * KW),
            (vb0 + b * KW, vb0 + (b + 1) * KW)]


def kernel(x_prompt, x_sample, cache_a_kv, cache_b0_kv, cache_b1_kv, cache_b2_kv, state_c_mem,
           state_c_norm, state_c_max, w_in0, sinks0, w_out0, w_in1, b_gates1, mh_norm1, w_out1,
           ln_g, ln_b):
    bn, s, d = x_prompt.shape
    nseq, t_new, _ = x_sample.shape
    w0 = w_in0[0]
    z0 = 4 * GW
    w_groups = [jnp.concatenate([w0[:, a:b] for a, b in _group_cols(g)], axis=1).astype(BF16)
                for g in range(N_GROUPS)]
    wz0 = w0[:, z0:].astype(BF16)
    wo0 = w_out0[0].astype(BF16)
    sinks = sinks0[0].astype(F32)
    g0, b0 = ln_g[0][None, :], ln_b[0][None, :]
    g1, b1 = ln_g[1][None, :], ln_b[1][None, :]

    outs, lses, states_p = [], [], []
    for g in range(N_GROUPS):
        o, lse, st = _group_attn_prompt(x_prompt, w_groups[g], sinks, DILATIONS[g], g == 0)
        outs.append(o)
        lses.append(lse)
        states_p.append(st)
    xp2 = x_prompt.reshape(bn * s, d)
    y0p = _l0_out(xp2, outs[0], outs[1:], lses[1:], wz0, wo0, g0, b0)

    xs2 = jnp.pad(x_sample, ((0, 0), (0, TPAD - t_new), (0, 0))).reshape(nseq * TPAD, d)
    w_dec = jnp.concatenate(
        [w_groups[g][:, :QW] for g in range(N_GROUPS)] + [w_groups[g][:, QW:QW + KW] for g in range(N_GROUPS)]
        + [w_groups[g][:, QW + KW:] for g in range(N_GROUPS)], axis=1)
    tabs_s = tuple(jnp.tile(t, (nseq, 1)) for t in _rope_tables(PAST_LEN + jnp.arange(TPAD)))
    tn = 512
    hs = _proj(xs2, w_dec, tabs_s, tn, n_rope=(N_GROUPS * (QW + KW)) // tn, n_scaled=(N_GROUPS * QW) // tn)
    sink_rows = jnp.broadcast_to(jnp.repeat(sinks, TPAD)[:, None], (Q_HEADS * TPAD, LANES))
    dec_o, dec_l, states_s = _decode_attn(
        hs, (cache_a_kv, cache_b0_kv, cache_b1_kv, cache_b2_kv), sink_rows, t_new)
    y0s = _l0_out(xs2, dec_o[0], dec_o[1:], dec_l, wz0, wo0, g0, b0)

    w1 = w_in1[0]
    gate_pad = jnp.zeros((d, LANES - 2 * C_HEADS), F32)
    w_qkvg = jnp.concatenate([w1[:, :3 * MIX], w1[:, 5 * MIX:], gate_pad], axis=1).astype(BF16)
    w_oz = w1[:, 3 * MIX:5 * MIX].astype(BF16)
    wo1 = w_out1[0].astype(BF16)
    bias_row = jnp.pad(b_gates1[0].astype(F32), (0, LANES - 2 * C_HEADS))[None, :]
    mhg = mh_norm1[0][None, :]

    hp, cmem_p, cnorm_p, cmax_p = _l1_prompt(y0p.reshape(bn, s, d), w_qkvg, bias_row)
    y1p = _l1_out(y0p, hp.reshape(bn * s, MIX), w_oz, mhg, wo1, g1, b1)

    no_tabs = tuple(jnp.zeros((nseq * TPAD, LANES), F32) for _ in range(3))
    ps = _proj(y0s, w_qkvg, no_tabs, 640)
    cmax_pad = jnp.broadcast_to(state_c_max[0][:, :, None], (nseq, C_HEADS, LANES))
    hsd, cmem_s, cnorm_s, cmax_s = _l1_decode(ps, bias_row, state_c_mem[0], state_c_norm[0], cmax_pad, t_new)
    y1s = _l1_out(y0s, hsd, w_oz, mhg, wo1, g1, b1)

    y_prompt = y1p.reshape(bn, s, d)
    y_sample = y1s.reshape(nseq, TPAD, d)[:, :t_new]
    return (y_prompt, y_sample,
            states_p[0][None], states_p[1][None], states_p[2][None], states_p[3][None],
            cmem_p[None], cnorm_p[:, :C_HEADS][None], cmax_p[:, :C_HEADS, 0][None],
            states_s[0], states_s[1], states_s[2], states_s[3],
            cmem_s[None], cnorm_s[None], cmax_s[:, :, 0][None])
```

```python
import functools

import jax
import jax.numpy as jnp
from jax import lax
from jax.experimental import pallas as pl
from jax.experimental.pallas import tpu as pltpu

F32 = jnp.float32
BF16 = jnp.bfloat16

D_MODEL = 1024
PAST_LEN = 16384
HEAD_DIM = 64
ROT_DIM = HEAD_DIM // 4
ROPE_THETA = 500000.0
BLOCK = 128
Q_HEADS = 8
KV_HEADS = 2
GROUP = Q_HEADS // KV_HEADS
QW = Q_HEADS * HEAD_DIM
KW = KV_HEADS * HEAD_DIM
GW = QW + 2 * KW
DILATIONS = (1, 1, 4, 16)
N_GROUPS = 4
C_HEADS = 4
C_HEAD_DIM = 256
MIX = 1024
DEPTH = 2
DN_ALPHA = (2 * DEPTH) ** 0.25
LN_EPS = 1e-5
MH_EPS = 1e-6
NEG_INF = -1e30
LOG2E = 1.4426950408889634
LN2 = 0.6931471805599453
LANES = 128
TPAD = 8
VMEM_LIMIT = 56 * 1024 * 1024


def _cparams(sem):
    return pltpu.CompilerParams(dimension_semantics=sem, vmem_limit_bytes=VMEM_LIMIT)


def _layer_norm_rows(v, g, b):
    mu = jnp.mean(v, axis=-1, keepdims=True)
    c = v - mu
    var = jnp.mean(c * c, axis=-1, keepdims=True)
    return c * lax.rsqrt(var + LN_EPS) * g + b


def _rope(slab, cos, sin_lo, sin_hi):
    return (slab * cos + pltpu.roll(slab, LANES - ROT_DIM // 2, 1) * sin_lo
            + pltpu.roll(slab, ROT_DIM // 2, 1) * sin_hi)


def _rope_tables(pos):
    half = ROT_DIM // 2
    inv = ROPE_THETA ** (-jnp.arange(half, dtype=F32) / half)
    ang = pos.astype(F32)[:, None] * inv[None, :]
    cos, sin = jnp.cos(ang), jnp.sin(ang)
    lane = jnp.arange(LANES) % HEAD_DIM
    f = lane % half
    cos_t = jnp.where(lane[None, :] < ROT_DIM, cos[:, f], 1.0)
    sin_lo = jnp.where(lane[None, :] < half, -sin[:, f], 0.0)
    sin_hi = jnp.where((lane[None, :] >= half) & (lane[None, :] < ROT_DIM), sin[:, f], 0.0)
    return cos_t.astype(F32), sin_lo.astype(F32), sin_hi.astype(F32)


STEP_TOKENS = 512
X_SLABS = D_MODEL // LANES
UNITS_PER_PROJ = 4
ONES_ROWS = 16


def _rope_t(blk, cos, sin):
    half = ROT_DIM // 2
    x1, x2 = blk[:half], blk[half:ROT_DIM]
    return jnp.concatenate([x1 * cos - x2 * sin, x2 * cos + x1 * sin, blk[ROT_DIM:]], axis=0)


def _group_attn_kernel(sink_ref, *refs, has_sink, dil, tstep, nstep):
    x_refs = refs[:X_SLABS]
    (wt_ref, cos_ref, sin_ref, o_ref, lse_ref, st_ref,
     kprev_sc, vprev_sc, kv_nat_sc, o_sc, st_sc, pt_sc, bias_sc) = refs[X_SLABS:]
    j = pl.program_id(1)
    n_units = tstep // BLOCK
    is_last = j == nstep - 1

    @pl.when(j == 0)
    def _():
        kprev_sc[...] = jnp.zeros_like(kprev_sc)
        vprev_sc[...] = jnp.zeros_like(vprev_sc)

    scale = HEAD_DIM ** -0.5 * LOG2E
    kidx = lax.broadcasted_iota(jnp.int32, (2 * BLOCK, 2 * BLOCK), 0)
    qidx = lax.broadcasted_iota(jnp.int32, (2 * BLOCK, 2 * BLOCK), 1) % BLOCK
    dist = qidx + BLOCK - kidx
    band = (dist >= 0) & (dist <= BLOCK)
    bias_sc[0] = jnp.where(band, 0.0, NEG_INF)
    bias_sc[1] = jnp.where(band & (kidx >= jnp.where(j > 0, 0, BLOCK)), 0.0, NEG_INF)
    lane2 = lax.broadcasted_iota(jnp.int32, (1, 2 * BLOCK), 1)
    ones_rows = jnp.ones((ONES_ROWS, 2 * BLOCK), F32)
    zeros_half = jnp.zeros((HEAD_DIM, 2 * BLOCK), F32)
    nt = (((1,), (1,)), ((), ()))

    def unit_rows(u):
        return pl.ds(u * BLOCK, BLOCK) if dil == 1 else pl.ds(u, BLOCK, stride=dil)

    carried = None
    for qd in range(n_units // UNITS_PER_PROJ):
        units = [UNITS_PER_PROJ * qd + w for w in range(UNITS_PER_PROJ)]
        x_quad = jnp.concatenate(
            [jnp.concatenate([x_refs[c][0, unit_rows(u), :] for c in range(X_SLABS)], axis=1) for u in units],
            axis=0).astype(BF16)
        ht = lax.dot_general(wt_ref[...], x_quad, nt, preferred_element_type=F32)
        cols = slice(qd * UNITS_PER_PROJ * BLOCK, (qd + 1) * UNITS_PER_PROJ * BLOCK)
        cos, sin = cos_ref[:, cols], sin_ref[:, cols]
        qt = [_rope_t(ht[h * HEAD_DIM:(h + 1) * HEAD_DIM], cos, sin) * scale for h in range(Q_HEADS)]
        kt = jnp.concatenate([_rope_t(ht[QW + h * HEAD_DIM:QW + (h + 1) * HEAD_DIM], cos, sin)
                              for h in range(KV_HEADS)], axis=0)
        vt = ht[QW + KW:GW]
        k_rm = kt.T

        if dil > 1:
            @pl.when(is_last)
            def _():
                v_rm = vt.T
                for w, u in enumerate(units):
                    kv_nat_sc[0, unit_rows(u), :] = k_rm[w * BLOCK:(w + 1) * BLOCK]
                    kv_nat_sc[1, unit_rows(u), :] = v_rm[w * BLOCK:(w + 1) * BLOCK]

        def scores(w, u, carried):
            lanes = slice(w * BLOCK, (w + 1) * BLOCK)
            k_cur = k_rm[w * BLOCK:(w + 1) * BLOCK].astype(BF16)
            vt_cur = vt[:, lanes].astype(BF16)
            if dil == 1 and u > 0:
                k_prev, vt_prev = carried
                mask = 0
            else:
                slot = 0 if dil == 1 else u
                k_prev, vt_prev = kprev_sc[slot], vprev_sc[slot]
                mask = 1
            k_cat = jnp.concatenate([k_prev, k_cur], axis=0)
            wq = []
            for c in range(QW // LANES):
                top = jnp.concatenate([qt[2 * c][:, lanes], qt[2 * c + 1][:, lanes]], axis=1)
                wq.append(jnp.concatenate([top, zeros_half] if c < GROUP // 2 else [zeros_half, top], axis=0))
            wq = jnp.concatenate(wq, axis=1).astype(BF16)
            st_sc[u % 2] = jnp.dot(k_cat, wq, preferred_element_type=F32)
            return dict(w=w, u=u, lanes=lanes, k_cur=k_cur, vt_cur=vt_cur, vt_prev=vt_prev, mask=mask)

        def softmax(t):
            par = t["u"] % 2
            ms = []
            for c in range(QW // LANES):
                pc = slice(c * 2 * BLOCK, (c + 1) * 2 * BLOCK)
                st = st_sc[par, :, pc] + bias_sc[t["mask"]]
                m = jnp.max(st, axis=0, keepdims=True)
                if has_sink:
                    sink = jnp.where(lane2 < BLOCK, sink_ref[2 * c], sink_ref[2 * c + 1]) * LOG2E
                    m = jnp.maximum(m, sink)
                ms.append(m)
                pt_sc[par, :, pc] = jnp.exp2(st - m).astype(BF16)
            t["ms"] = ms

        def values(t):
            u, par, lanes = t["u"], t["u"] % 2, t["lanes"]
            lse_rows = []
            for hk in range(KV_HEADS):
                hd = slice(hk * HEAD_DIM, (hk + 1) * HEAD_DIM)
                vt_aug = jnp.concatenate(
                    [jnp.concatenate([t["vt_prev"][hd], t["vt_cur"][hd]], axis=1), ones_rows.astype(BF16)], axis=0)
                acc2 = jnp.dot(vt_aug, pt_sc[par, :, hk * 4 * BLOCK:(hk + 1) * 4 * BLOCK],
                               preferred_element_type=F32)
                for i in range(GROUP // 2):
                    c = hk * (GROUP // 2) + i
                    acc = acc2[:, i * 2 * BLOCK:(i + 1) * 2 * BLOCK]
                    m = t["ms"][c]
                    den = acc[HEAD_DIM:HEAD_DIM + 1]
                    if has_sink:
                        sink = jnp.where(lane2 < BLOCK, sink_ref[2 * c], sink_ref[2 * c + 1]) * LOG2E
                        den = den + jnp.exp2(sink - m)
                    ot = acc[:HEAD_DIM] / den
                    lse = m * LN2 + jnp.log(den)
                    lse_rows += [lse[:, :BLOCK], lse[:, BLOCK:]]
                    slab_t = jnp.concatenate([ot[:, :BLOCK], ot[:, BLOCK:]], axis=0)
                    o_sc[c, unit_rows(u), :] = slab_t.T
            lse_t = jnp.concatenate(lse_rows + [jnp.zeros((LANES - Q_HEADS, BLOCK), F32)], axis=0)
            lse_ref[0, unit_rows(u), :] = lse_t.T
            if dil > 1:
                kprev_sc[u] = t["k_cur"]
                vprev_sc[u] = t["vt_cur"]
            elif u == n_units - 1:
                kprev_sc[0] = t["k_cur"]
                vprev_sc[0] = t["vt_cur"]

                @pl.when(is_last)
                def _():
                    st_ref[0, :KW, :] = kt[:, lanes]
                    st_ref[0, KW:, :] = vt[:, lanes]

        pending = None
        for w, u in enumerate(units):
            t = scores(w, u, carried)
            carried = (t["k_cur"], t["vt_cur"])
            if pending is not None:
                softmax(pending)
                values(pending)
            pending = t
        softmax(pending)
        values(pending)

    for c in range(QW // LANES):
        o_ref[0, :, c * LANES:(c + 1) * LANES] = o_sc[c].astype(o_ref.dtype)

    if dil > 1:
        @pl.when(is_last)
        def _():
            st_ref[0, :KW, :] = kv_nat_sc[0].T
            st_ref[0, KW:, :] = kv_nat_sc[1].T


def _group_attn_prompt(x, w_g, sinks, dil, has_sink):
    bn, s, d = x.shape
    span = BLOCK * dil
    tstep = max(STEP_TOKENS, span)
    nstep = s // tstep
    if dil == 1:
        pos = jnp.arange(s)
    else:
        pos = (jnp.arange(nstep)[:, None, None] * tstep + jnp.arange(BLOCK)[None, None, :] * dil
               + jnp.arange(dil)[None, :, None]).reshape(s)
    half = ROT_DIM // 2
    inv = ROPE_THETA ** (-jnp.arange(half, dtype=F32) / half)
    ang = pos.astype(F32)[:, None] * inv[None, :]
    cos_t, sin_t = jnp.cos(ang).T, jnp.sin(ang).T
    tab_spec = pl.BlockSpec((half, tstep), lambda b, j: (0, j))
    nprev = 1 if dil == 1 else dil
    o, lse, st = pl.pallas_call(
        functools.partial(_group_attn_kernel, has_sink=has_sink, dil=dil, tstep=tstep, nstep=nstep),
        grid=(bn, nstep),
        in_specs=([pl.BlockSpec(memory_space=pltpu.SMEM)]
                  + [pl.BlockSpec((1, tstep, LANES), functools.partial(lambda c, b, j: (b, j, c), c))
                     for c in range(X_SLABS)]
                  + [pl.BlockSpec((GW, d), lambda b, j: (0, 0)), tab_spec, tab_spec]),
        out_specs=[
            pl.BlockSpec((1, tstep, QW), lambda b, j: (b, j, 0)),
            pl.BlockSpec((1, tstep, LANES), lambda b, j: (b, j, 0)),
            pl.BlockSpec((1, 2 * KW, span), lambda b, j: (b, 0, 0)),
        ],
        out_shape=[
            jax.ShapeDtypeStruct((bn, s, QW), BF16),
            jax.ShapeDtypeStruct((bn, s, LANES), F32),
            jax.ShapeDtypeStruct((bn, 2 * KW, span), F32),
        ],
        scratch_shapes=[pltpu.VMEM((nprev, BLOCK, KW), BF16), pltpu.VMEM((nprev, KW, BLOCK), BF16),
                        pltpu.VMEM((2, span, LANES), F32), pltpu.VMEM((QW // LANES, tstep, LANES), F32),
                        pltpu.VMEM((2, 2 * BLOCK, Q_HEADS * BLOCK), F32),
                        pltpu.VMEM((2, 2 * BLOCK, Q_HEADS * BLOCK), BF16),
                        pltpu.VMEM((2, 2 * BLOCK, 2 * BLOCK), F32)],
        compiler_params=_cparams(("arbitrary", "arbitrary")),
        name=f"l0_group_attn_d{dil}_{'sink' if has_sink else 'nosink'}",
    )(sinks, *([x] * X_SLABS), w_g.T, cos_t, sin_t)
    st = jnp.transpose(st.reshape(bn, 2, KV_HEADS, HEAD_DIM, span), (0, 4, 1, 2, 3))
    return o.reshape(bn * s, QW), lse.reshape(bn * s, LANES), st


def _l0_out_kernel(x_ref, oa_ref, o0_ref, o1_ref, o2_ref, l0_ref, l1_ref, l2_ref,
                   wz_ref, wo_ref, g_ref, b_ref, y_ref):
    x = x_ref[...]
    z = jnp.dot(x.astype(BF16), wz_ref[...], preferred_element_type=F32)
    l0, l1, l2 = l0_ref[...], l1_ref[...], l2_ref[...]
    m = jnp.maximum(jnp.maximum(l0, l1), l2)
    e0, e1, e2 = jnp.exp(l0 - m), jnp.exp(l1 - m), jnp.exp(l2 - m)
    inv = 1.0 / (e0 + e1 + e2)
    w0, w1, w2 = e0 * inv, e1 * inv, e2 * inv
    tm = x.shape[0]
    parts = []
    for hq in range(Q_HEADS):
        c = slice(hq * HEAD_DIM, (hq + 1) * HEAD_DIM)
        bshape = (tm, HEAD_DIM)
        parts.append(jnp.broadcast_to(w0[:, hq:hq + 1], bshape) * o0_ref[:, c].astype(F32)
                     + jnp.broadcast_to(w1[:, hq:hq + 1], bshape) * o1_ref[:, c].astype(F32)
                     + jnp.broadcast_to(w2[:, hq:hq + 1], bshape) * o2_ref[:, c].astype(F32))
    mix = jnp.concatenate([oa_ref[...].astype(F32)] + parts, axis=1)
    mix = (mix * (z * jax.nn.sigmoid(z))).astype(BF16)
    y = jnp.dot(mix, wo_ref[...], preferred_element_type=F32)
    y_ref[...] = _layer_norm_rows(DN_ALPHA * x + y, g_ref[...], b_ref[...])


def _l0_out(x2, oa, obs, lses, wz, wo, g, b):
    n, d = x2.shape
    tm = 512
    row = lambda w: pl.BlockSpec((tm, w), lambda i: (i, 0))
    full = lambda a: pl.BlockSpec(a.shape, lambda i: (0, 0))
    return pl.pallas_call(
        _l0_out_kernel,
        grid=(n // tm,),
        in_specs=[row(d), row(QW), row(QW), row(QW), row(QW), row(LANES), row(LANES), row(LANES),
                  full(wz), full(wo), full(g), full(b)],
        out_specs=row(d),
        out_shape=jax.ShapeDtypeStruct((n, d), F32),
        compiler_params=_cparams(("arbitrary",)),
        name="l0_out",
    )(x2, oa, *obs, *lses, wz, wo, g, b)


def _proj_kernel(x_ref, w_ref, cos_ref, slo_ref, shi_ref, h_ref, *, n_rope, n_scaled, tn):
    j = pl.program_id(0)
    h = jnp.dot(x_ref[...].astype(BF16), w_ref[...], preferred_element_type=F32)
    if n_rope:
        cos, slo, shi = cos_ref[...], slo_ref[...], shi_ref[...]
        roped = jnp.concatenate(
            [_rope(h[:, c * LANES:(c + 1) * LANES], cos, slo, shi) for c in range(tn // LANES)], axis=1)
        h = jnp.where(j < n_rope, roped, h)
        h = h * jnp.where(j < n_scaled, HEAD_DIM ** -0.5, 1.0)
    h_ref[...] = h


def _proj(x2, w, tabs, tn, n_rope=0, n_scaled=0):
    n, d = x2.shape
    ncol = w.shape[1]
    tab_spec = pl.BlockSpec((n, LANES), lambda j: (0, 0))
    return pl.pallas_call(
        functools.partial(_proj_kernel, n_rope=n_rope, n_scaled=n_scaled, tn=tn),
        grid=(ncol // tn,),
        in_specs=[pl.BlockSpec((n, d), lambda j: (0, 0)),
                  pl.BlockSpec((d, tn), lambda j: (0, j)),
                  tab_spec, tab_spec, tab_spec],
        out_specs=pl.BlockSpec((n, tn), lambda j: (0, j)),
        out_shape=jax.ShapeDtypeStruct((n, ncol), F32),
        compiler_params=_cparams(("arbitrary",)),
        name=f"decode_proj_{ncol}",
    )(x2, w, *tabs)


def _decode_attn_kernel(sink_ref, h_ref, ca_ref, c0_ref, c1_ref, c2_ref,
                        oa_ref, o0_ref, o1_ref, o2_ref, l0_ref, l1_ref, l2_ref,
                        na_ref, n0_ref, n1_ref, n2_ref, sa_sc, s0_sc, s1_sc, s2_sc, *, sb, t_new):
    nrow = Q_HEADS * TPAD
    lane8 = lax.broadcasted_iota(jnp.int32, (TPAD, LANES), 1)
    lo_half = lane8 < HEAD_DIM
    lane_k = lax.broadcasted_iota(jnp.int32, (KW, LANES), 1)
    k_base = N_GROUPS * QW
    v_base = k_base + N_GROUPS * KW
    caches = (ca_ref, c0_ref, c1_ref, c2_ref)
    new_caches = (na_ref, n0_ref, n1_ref, n2_ref)
    o_refs = (oa_ref, o0_ref, o1_ref, o2_ref)
    l_refs = (None, l0_ref, l1_ref, l2_ref)
    sc_refs = (sa_sc, s0_sc, s1_sc, s2_sc)
    zpad = jnp.zeros((LANES - TPAD, 2 * KW), F32)
    nt = (((1,), (1,)), ((), ()))

    work = []
    for g in range(N_GROUPS):
        dil = DILATIONS[g]
        nkeys = BLOCK * dil
        cref = caches[g]
        for s_i in range(sb):
            rows = slice(s_i * TPAD, (s_i + 1) * TPAD)
            blocks = []
            for c in range(QW // LANES):
                slab = h_ref[rows, g * QW + c * LANES: g * QW + (c + 1) * LANES]
                swapped = pltpu.roll(slab, HEAD_DIM, 1)
                if c < QW // LANES // 2:
                    blocks += [jnp.where(lo_half, slab, 0.0), jnp.where(lo_half, swapped, 0.0)]
                else:
                    blocks += [jnp.where(lo_half, 0.0, swapped), jnp.where(lo_half, 0.0, slab)]
            qm = jnp.concatenate(blocks, axis=0).astype(BF16)
            k_new = h_ref[rows, k_base + g * KW: k_base + (g + 1) * KW]
            v_new = h_ref[rows, v_base + g * KW: v_base + (g + 1) * KW]
            kv_new = jnp.concatenate([jnp.concatenate([k_new, v_new], axis=1), zpad], axis=0)
            kv_new_b = kv_new.astype(BF16)
            sc_refs[g][s_i, :, :nkeys] = jnp.dot(qm, cref[s_i, :KW, :].astype(BF16), preferred_element_type=F32)
            sc_refs[g][s_i, :, nkeys:] = lax.dot_general(qm, kv_new_b[:, :KW], nt, preferred_element_type=F32)
            work.append((g, s_i, rows, kv_new, kv_new_b))

    for g, s_i, rows, kv_new, kv_new_b in work:
        nkeys = BLOCK * DILATIONS[g]
        cref, nref = caches[g], new_caches[g]
        new_cols = pltpu.roll(kv_new.T, LANES - t_new, 1)
        for half in range(2):
            hrows = slice(half * KW, (half + 1) * KW)
            shifted = pltpu.roll(cref[s_i, hrows, :], nkeys - t_new, 1)
            nref[s_i, hrows, :] = shifted
            nref[s_i, hrows, nkeys - LANES:] = jnp.where(lane_k >= LANES - t_new, new_cols[hrows],
                                                         shifted[:, nkeys - LANES:])

    for g, s_i, rows, kv_new, kv_new_b in work:
        dil = DILATIONS[g]
        nkeys = BLOCK * dil
        ncol = nkeys + LANES
        tok = lax.broadcasted_iota(jnp.int32, (nrow, ncol), 0) % TPAD
        col = lax.broadcasted_iota(jnp.int32, (nrow, ncol), 1)
        new_j = col - nkeys
        valid = (((col < nkeys) & (((col - tok) & (dil - 1)) == 0) & (col >= tok))
                 | ((col >= nkeys) & (((tok - new_j) & (dil - 1)) == 0) & (new_j <= tok) & (new_j < t_new)))
        s_all = jnp.where(valid, sc_refs[g][s_i], NEG_INF)
        m = jnp.max(s_all, axis=-1, keepdims=True)
        if g == 0:
            sink = sink_ref[:, 0:1]
            m = jnp.maximum(m, sink)
        p = jnp.exp(s_all - m)
        den = jnp.sum(p, axis=-1, keepdims=True)
        if g == 0:
            den = den + jnp.exp(sink - m)
        pb = p.astype(BF16)
        o = (lax.dot_general(pb[:, :nkeys], caches[g][s_i, KW:, :].astype(BF16), nt, preferred_element_type=F32)
             + jnp.dot(pb[:, nkeys:], kv_new_b[:, KW:], preferred_element_type=F32)) / den
        for c in range(QW // LANES):
            ev = o[(2 * c) * TPAD:(2 * c + 1) * TPAD]
            od = o[(2 * c + 1) * TPAD:(2 * c + 2) * TPAD]
            if c < QW // LANES // 2:
                slab = jnp.where(lo_half, ev, pltpu.roll(od, HEAD_DIM, 1))
            else:
                slab = jnp.where(lo_half, pltpu.roll(ev, HEAD_DIM, 1), od)
            o_refs[g][rows, c * LANES:(c + 1) * LANES] = slab.astype(o_refs[g].dtype)
        if g > 0:
            lse = m + jnp.log(den)
            lse_acc = jnp.zeros((TPAD, LANES), F32)
            for hq in range(Q_HEADS):
                lse_acc = jnp.where(lane8 == hq, lse[hq * TPAD:(hq + 1) * TPAD], lse_acc)
            l_refs[g][rows, :] = lse_acc


def _cache_rows_minor(c):
    nseq, rows = c.shape[1], c.shape[2]
    return jnp.transpose(c[0], (0, 2, 3, 4, 1)).reshape(nseq, 2 * KW, rows)


def _cache_rows_major(c):
    nseq, _, rows = c.shape
    return jnp.transpose(c.reshape(nseq, 2, KV_HEADS, HEAD_DIM, rows), (0, 4, 1, 2, 3))[None]


def _decode_attn(h, caches, sink_rows, t_new):
    cts = [_cache_rows_minor(c) for c in caches]
    nseq = cts[0].shape[0]
    sb = 2
    n = nseq * TPAD
    row = lambda w: pl.BlockSpec((sb * TPAD, w), lambda i: (i, 0))
    cspec = lambda c: pl.BlockSpec((sb,) + c.shape[1:], lambda i: (i, 0, 0))
    outs = pl.pallas_call(
        functools.partial(_decode_attn_kernel, sb=sb, t_new=t_new),
        grid=(nseq // sb,),
        in_specs=[pl.BlockSpec(sink_rows.shape, lambda i: (0, 0)), row(h.shape[1])] + [cspec(c) for c in cts],
        out_specs=[row(QW)] * 4 + [row(LANES)] * 3 + [cspec(c) for c in cts],
        out_shape=([jax.ShapeDtypeStruct((n, QW), BF16)] * 4 + [jax.ShapeDtypeStruct((n, LANES), F32)] * 3
                   + [jax.ShapeDtypeStruct(c.shape, F32) for c in cts]),
        scratch_shapes=[pltpu.VMEM((sb, Q_HEADS * TPAD, c.shape[2] + LANES), F32) for c in cts],
        compiler_params=_cparams(("arbitrary",)),
        name="l0_decode_attn",
    )(sink_rows, h, *cts)
    return outs[:4], outs[4:7], [_cache_rows_major(c) for c in outs[7:]]


def _mlstm_chunk(q, k, v, gates, hh, c_mem, c_norm, m_prev, n_valid):
    L = q.shape[0]
    ig_col = gates[:, hh:hh + 1]
    fg = gates[:, C_HEADS + hh:C_HEADS + hh + 1]
    lf_col = jnp.minimum(fg, 0.0) - jnp.log(1.0 + jnp.exp(-jnp.abs(fg)))
    ri = lax.broadcasted_iota(jnp.int32, (L, L), 0)
    ci = lax.broadcasted_iota(jnp.int32, (L, L), 1)
    tril = ri >= ci
    eye = ri == ci
    lf_row = jnp.sum(jnp.where(eye, lf_col, 0.0), axis=0, keepdims=True)
    ig_row = jnp.sum(jnp.where(eye, ig_col, 0.0), axis=0, keepdims=True)
    b_col = jnp.sum(jnp.where(tril, lf_row, 0.0), axis=1, keepdims=True)
    b_row = jnp.sum(jnp.where(ri <= ci, lf_col, 0.0), axis=0, keepdims=True)
    dmat = jnp.where(tril, b_col - b_row + ig_row, NEG_INF)
    inter = b_col + m_prev
    m = jnp.maximum(inter, jnp.max(dmat, axis=1, keepdims=True))
    pw = jnp.exp(dmat - m)
    qb, kb, vb = q.astype(BF16), k.astype(BF16), v.astype(BF16)
    sc = lax.dot_general(qb, kb, (((1,), (1,)), ((), ())), preferred_element_type=F32) * pw
    carry_w = jnp.exp(inter - m)
    num = (jnp.dot(sc.astype(BF16), vb, preferred_element_type=F32)
           + carry_w * jnp.dot(qb, c_mem.astype(BF16), preferred_element_type=F32))
    den = jnp.sum(sc, axis=1, keepdims=True) + carry_w * jnp.sum(q * c_norm, axis=1, keepdims=True)
    h = num / jnp.maximum(jnp.abs(den), jnp.exp(-m))
    last = n_valid - 1
    m_new = m[last:last + 1, :]
    b_last = b_col[last:last + 1, :]
    ws = jnp.exp(b_last - b_col + ig_col - m_new)
    if n_valid < L:
        ws = jnp.where(lax.broadcasted_iota(jnp.int32, (L, 1), 0) < n_valid, ws, 0.0)
    decay = jnp.exp(b_last + m_prev - m_new)
    wk = ws * k
    c_mem_new = decay * c_mem + lax.dot_general(
        wk.astype(BF16), vb, (((0,), (0,)), ((), ())), preferred_element_type=F32)
    c_norm_new = decay * c_norm + jnp.sum(wk, axis=0, keepdims=True)
    return h, c_mem_new, c_norm_new, m_new


def _l1_prompt_kernel(x_ref, w_ref, bias_ref, h_ref, cmem_ref, cnorm_ref, cmax_ref,
                      c_sc, n_sc, m_sc, *, tm, nblk):
    j = pl.program_id(1)

    @pl.when(j == 0)
    def _():
        c_sc[...] = jnp.zeros_like(c_sc)
        n_sc[...] = jnp.zeros_like(n_sc)
        m_sc[...] = jnp.zeros_like(m_sc)

    proj = jnp.dot(x_ref[0].astype(BF16), w_ref[...], preferred_element_type=F32)
    gates = proj[:, 3 * MIX:] + bias_ref[...]
    kscale = C_HEAD_DIM ** -0.5
    for c in range(tm // BLOCK):
        rows = slice(c * BLOCK, (c + 1) * BLOCK)
        for hh in range(C_HEADS):
            cols = slice(hh * C_HEAD_DIM, (hh + 1) * C_HEAD_DIM)
            q = proj[rows, cols]
            k = proj[rows, MIX + hh * C_HEAD_DIM: MIX + (hh + 1) * C_HEAD_DIM] * kscale
            v = proj[rows, 2 * MIX + hh * C_HEAD_DIM: 2 * MIX + (hh + 1) * C_HEAD_DIM]
            h, c_new, n_new, m_new = _mlstm_chunk(
                q, k, v, gates[rows], hh, c_sc[hh], n_sc[hh:hh + 1, :], m_sc[hh:hh + 1, 0:1], BLOCK)
            h_ref[0, rows, cols] = h
            c_sc[hh] = c_new
            n_sc[hh:hh + 1, :] = n_new
            m_sc[hh:hh + 1, :] = jnp.broadcast_to(m_new, (1, LANES))

    @pl.when(j == nblk - 1)
    def _():
        cmem_ref[0] = c_sc[...]
        cnorm_ref[0] = n_sc[...]
        cmax_ref[0] = m_sc[...]


def _l1_prompt(x, w_qkvg, bias_row):
    bn, s, d = x.shape
    tm = 512
    nblk = s // tm
    ncol = w_qkvg.shape[1]
    return pl.pallas_call(
        functools.partial(_l1_prompt_kernel, tm=tm, nblk=nblk),
        grid=(bn, nblk),
        in_specs=[pl.BlockSpec((1, tm, d), lambda b, j: (b, j, 0)),
                  pl.BlockSpec((d, ncol), lambda b, j: (0, 0)),
                  pl.BlockSpec((1, LANES), lambda b, j: (0, 0))],
        out_specs=[pl.BlockSpec((1, tm, MIX), lambda b, j: (b, j, 0)),
                   pl.BlockSpec((1, C_HEADS, C_HEAD_DIM, C_HEAD_DIM), lambda b, j: (b, 0, 0, 0)),
                   pl.BlockSpec((1, TPAD, C_HEAD_DIM), lambda b, j: (b, 0, 0)),
                   pl.BlockSpec((1, TPAD, LANES), lambda b, j: (b, 0, 0))],
        out_shape=[jax.ShapeDtypeStruct((bn, s, MIX), F32),
                   jax.ShapeDtypeStruct((bn, C_HEADS, C_HEAD_DIM, C_HEAD_DIM), F32),
                   jax.ShapeDtypeStruct((bn, TPAD, C_HEAD_DIM), F32),
                   jax.ShapeDtypeStruct((bn, TPAD, LANES), F32)],
        scratch_shapes=[pltpu.VMEM((C_HEADS, C_HEAD_DIM, C_HEAD_DIM), F32),
                        pltpu.VMEM((TPAD, C_HEAD_DIM), F32),
                        pltpu.VMEM((TPAD, LANES), F32)],
        compiler_params=_cparams(("arbitrary", "arbitrary")),
        name="l1_prompt_mlstm",
    )(x, w_qkvg, bias_row)


def _l1_decode_kernel(p_ref, bias_ref, cm_ref, cn_ref, cx_ref, h_ref, cmo_ref, cno_ref, cxo_ref,
                      *, sb, t_new):
    kscale = C_HEAD_DIM ** -0.5
    for s_i in range(sb):
        rows = slice(s_i * TPAD, (s_i + 1) * TPAD)
        gates = p_ref[rows, 3 * MIX:] + bias_ref[...]
        for hh in range(C_HEADS):
            cols = slice(hh * C_HEAD_DIM, (hh + 1) * C_HEAD_DIM)
            q = p_ref[rows, cols]
            k = p_ref[rows, MIX + hh * C_HEAD_DIM: MIX + (hh + 1) * C_HEAD_DIM] * kscale
            v = p_ref[rows, 2 * MIX + hh * C_HEAD_DIM: 2 * MIX + (hh + 1) * C_HEAD_DIM]
            h, c_new, n_new, m_new = _mlstm_chunk(
                q, k, v, gates, hh, cm_ref[s_i, hh], cn_ref[s_i, hh:hh + 1, :],
                cx_ref[s_i, hh:hh + 1, 0:1], t_new)
            h_ref[rows, cols] = h
            cmo_ref[s_i, hh] = c_new
            cno_ref[s_i, hh:hh + 1, :] = n_new
            cxo_ref[s_i, hh:hh + 1, :] = jnp.broadcast_to(m_new, (1, LANES))


def _l1_decode(proj, bias_row, c_mem, c_norm, c_max_pad, t_new):
    nseq = c_mem.shape[0]
    sb = 4
    n = nseq * TPAD
    return pl.pallas_call(
        functools.partial(_l1_decode_kernel, sb=sb, t_new=t_new),
        grid=(nseq // sb,),
        in_specs=[pl.BlockSpec((sb * TPAD, proj.shape[1]), lambda i: (i, 0)),
                  pl.BlockSpec((1, LANES), lambda i: (0, 0)),
                  pl.BlockSpec((sb, C_HEADS, C_HEAD_DIM, C_HEAD_DIM), lambda i: (i, 0, 0, 0)),
                  pl.BlockSpec((sb, C_HEADS, C_HEAD_DIM), lambda i: (i, 0, 0)),
                  pl.BlockSpec((sb, C_HEADS, LANES), lambda i: (i, 0, 0))],
        out_specs=[pl.BlockSpec((sb * TPAD, MIX), lambda i: (i, 0)),
                   pl.BlockSpec((sb, C_HEADS, C_HEAD_DIM, C_HEAD_DIM), lambda i: (i, 0, 0, 0)),
                   pl.BlockSpec((sb, C_HEADS, C_HEAD_DIM), lambda i: (i, 0, 0)),
                   pl.BlockSpec((sb, C_HEADS, LANES), lambda i: (i, 0, 0))],
        out_shape=[jax.ShapeDtypeStruct((n, MIX), F32),
                   jax.ShapeDtypeStruct(c_mem.shape, F32),
                   jax.ShapeDtypeStruct(c_norm.shape, F32),
                   jax.ShapeDtypeStruct(c_max_pad.shape, F32)],
        compiler_params=_cparams(("arbitrary",)),
        name="l1_decode_mlstm",
    )(proj, bias_row, c_mem, c_norm, c_max_pad)


def _l1_out_kernel(x_ref, h_ref, woz_ref, mhg_ref, wo_ref, g_ref, b_ref, y_ref):
    x = x_ref[...]
    oz = jnp.dot(x.astype(BF16), woz_ref[...], preferred_element_type=F32)
    parts = []
    for hh in range(C_HEADS):
        cols = slice(hh * C_HEAD_DIM, (hh + 1) * C_HEAD_DIM)
        hg = jax.nn.sigmoid(oz[:, cols]) * h_ref[:, cols]
        mu = jnp.mean(hg, axis=-1, keepdims=True)
        c = hg - mu
        var = jnp.mean(c * c, axis=-1, keepdims=True)
        parts.append(c * lax.rsqrt(var + MH_EPS))
    hn = jnp.concatenate(parts, axis=1) * mhg_ref[...]
    z = oz[:, MIX:]
    mix = (hn * (z * jax.nn.sigmoid(z))).astype(BF16)
    y = jnp.dot(mix, wo_ref[...], preferred_element_type=F32)
    y_ref[...] = _layer_norm_rows(DN_ALPHA * x + y, g_ref[...], b_ref[...])


def _l1_out(x2, h2, woz, mhg, wo, g, b):
    n, d = x2.shape
    tm = 512
    row = lambda w: pl.BlockSpec((tm, w), lambda i: (i, 0))
    full = lambda a: pl.BlockSpec(a.shape, lambda i: (0, 0))
    return pl.pallas_call(
        _l1_out_kernel,
        grid=(n // tm,),
        in_specs=[row(d), row(MIX), full(woz), full(mhg), full(wo), full(g), full(b)],
        out_specs=row(d),
        out_shape=jax.ShapeDtypeStruct((n, d), F32),
        compiler_params=_cparams(("arbitrary",)),
        name="l1_out",
    )(x2, h2, woz, mhg, wo, g, b)


def _group_cols(g):
    if g == 0:
        return [(0, QW), (QW, QW + KW), (QW + KW, GW)]
    b = g - 1
    qb0 = GW
    kb0 = GW + 3 * QW
    vb0 = kb0 + 3 * KW
    return [(qb0 + b * QW, qb0 + (b + 1) * QW), (kb0 + b * KW, kb0 + (b + 1) * KW),
            (vb0 + b * KW, vb0 + (b + 1) * KW)]


def kernel(x_prompt, x_sample, cache_a_kv, cache_b0_kv, cache_b1_kv, cache_b2_kv, state_c_mem,
           state_c_norm, state_c_max, w_in0, sinks0, w_out0, w_in1, b_gates1, mh_norm1, w_out1,
           ln_g, ln_b):
    bn, s, d = x_prompt.shape
    nseq, t_new, _ = x_sample.shape
    w0 = w_in0[0]
    z0 = 4 * GW
    w_groups = [jnp.concatenate([w0[:, a:b] for a, b in _group_cols(g)], axis=1).astype(BF16)
                for g in range(N_GROUPS)]
    wz0 = w0[:, z0:].astype(BF16)
    wo0 = w_out0[0].astype(BF16)
    sinks = sinks0[0].astype(F32)
    g0, b0 = ln_g[0][None, :], ln_b[0][None, :]
    g1, b1 = ln_g[1][None, :], ln_b[1][None, :]

    outs, lses, states_p = [], [], []
    for g in range(N_GROUPS):
        o, lse, st = _group_attn_prompt(x_prompt, w_groups[g], sinks, DILATIONS[g], g == 0)
        outs.append(o)
        lses.append(lse)
        states_p.append(st)
    xp2 = x_prompt.reshape(bn * s, d)
    y0p = _l0_out(xp2, outs[0], outs[1:], lses[1:], wz0, wo0, g0, b0)

    xs2 = jnp.pad(x_sample, ((0, 0), (0, TPAD - t_new), (0, 0))).reshape(nseq * TPAD, d)
    w_dec = jnp.concatenate(
        [w_groups[g][:, :QW] for g in range(N_GROUPS)] + [w_groups[g][:, QW:QW + KW] for g in range(N_GROUPS)]
        + [w_groups[g][:, QW + KW:] for g in range(N_GROUPS)], axis=1)
    tabs_s = tuple(jnp.tile(t, (nseq, 1)) for t in _rope_tables(PAST_LEN + jnp.arange(TPAD)))
    tn = 512
    hs = _proj(xs2, w_dec, tabs_s, tn, n_rope=(N_GROUPS * (QW + KW)) // tn, n_scaled=(N_GROUPS * QW) // tn)
    sink_rows = jnp.broadcast_to(jnp.repeat(sinks, TPAD)[:, None], (Q_HEADS * TPAD, LANES))
    dec_o, dec_l, states_s = _decode_attn(
        hs, (cache_a_kv, cache_b0_kv, cache_b1_kv, cache_b2_kv), sink_rows, t_new)
    y0s = _l0_out(xs2, dec_o[0], dec_o[1:], dec_l, wz0, wo0, g0, b0)

    w1 = w_in1[0]
    gate_pad = jnp.zeros((d, LANES - 2 * C_HEADS), F32)
    w_qkvg = jnp.concatenate([w1[:, :3 * MIX], w1[:, 5 * MIX:], gate_pad], axis=1).astype(BF16)
    w_oz = w1[:, 3 * MIX:5 * MIX].astype(BF16)
    wo1 = w_out1[0].astype(BF16)
    bias_row = jnp.pad(b_gates1[0].astype(F32), (0, LANES - 2 * C_HEADS))[None, :]
    mhg = mh_norm1[0][None, :]

    hp, cmem_p, cnorm_p, cmax_p = _l1_prompt(y0p.reshape(bn, s, d), w_qkvg, bias_row)
    y1p = _l1_out(y0p, hp.reshape(bn * s, MIX), w_oz, mhg, wo1, g1, b1)

    no_tabs = tuple(jnp.zeros((nseq * TPAD, LANES), F32) for _ in range(3))
    ps = _proj(y0s, w_qkvg, no_tabs, 640)
    cmax_pad = jnp.broadcast_to(state_c_max[0][:, :, None], (nseq, C_HEADS, LANES))
    hsd, cmem_s, cnorm_s, cmax_s = _l1_decode(ps, bias_row, state_c_mem[0], state_c_norm[0], cmax_pad, t_new)
    y1s = _l1_out(y0s, hsd, w_oz, mhg, wo1, g1, b1)

    y_prompt = y1p.reshape(bn, s, d)
    y_sample = y1s.reshape(nseq, TPAD, d)[:, :t_new]
    return (y_prompt, y_sample,
            states_p[0][None], states_p[1][None], states_p[2][None], states_p[3][None],
            cmem_p[None], cnorm_p[:, :C_HEADS][None], cmax_p[:, :C_HEADS, 0][None],
            states_s[0], states_s[1], states_s[2], states_s[3],
            cmem_s[None], cnorm_s[None], cmax_s[:, :, 0][None])
```

```python
import functools

import jax
import jax.numpy as jnp
from jax import lax
from jax.experimental import pallas as pl
from jax.experimental.pallas import tpu as pltpu

F32 = jnp.float32
BF16 = jnp.bfloat16

D_MODEL = 1024
PAST_LEN = 16384
HEAD_DIM = 64
ROT_DIM = HEAD_DIM // 4
ROPE_THETA = 500000.0
BLOCK = 128
Q_HEADS = 8
KV_HEADS = 2
GROUP = Q_HEADS // KV_HEADS
QW = Q_HEADS * HEAD_DIM
KW = KV_HEADS * HEAD_DIM
GW = QW + 2 * KW
DILATIONS = (1, 1, 4, 16)
N_GROUPS = 4
C_HEADS = 4
C_HEAD_DIM = 256
MIX = 1024
DEPTH = 2
DN_ALPHA = (2 * DEPTH) ** 0.25
LN_EPS = 1e-5
MH_EPS = 1e-6
NEG_INF = -1e30
LOG2E = 1.4426950408889634
LN2 = 0.6931471805599453
LANES = 128
TPAD = 8
VMEM_LIMIT = 56 * 1024 * 1024


def _cparams(sem):
    return pltpu.CompilerParams(dimension_semantics=sem, vmem_limit_bytes=VMEM_LIMIT)


def _layer_norm_rows(v, g, b):
    mu = jnp.mean(v, axis=-1, keepdims=True)
    c = v - mu
    var = jnp.mean(c * c, axis=-1, keepdims=True)
    return c * lax.rsqrt(var + LN_EPS) * g + b


def _rope(slab, cos, sin_lo, sin_hi):
    return (slab * cos + pltpu.roll(slab, LANES - ROT_DIM // 2, 1) * sin_lo
            + pltpu.roll(slab, ROT_DIM // 2, 1) * sin_hi)


def _rope_tables(pos):
    half = ROT_DIM // 2
    inv = ROPE_THETA ** (-jnp.arange(half, dtype=F32) / half)
    ang = pos.astype(F32)[:, None] * inv[None, :]
    cos, sin = jnp.cos(ang), jnp.sin(ang)
    lane = jnp.arange(LANES) % HEAD_DIM
    f = lane % half
    cos_t = jnp.where(lane[None, :] < ROT_DIM, cos[:, f], 1.0)
    sin_lo = jnp.where(lane[None, :] < half, -sin[:, f], 0.0)
    sin_hi = jnp.where((lane[None, :] >= half) & (lane[None, :] < ROT_DIM), sin[:, f], 0.0)
    return cos_t.astype(F32), sin_lo.astype(F32), sin_hi.astype(F32)


STEP_TOKENS = 512
X_SLABS = D_MODEL // LANES
UNITS_PER_PROJ = 4
ONES_ROWS = 16


def _rope_t(blk, cos, sin):
    half = ROT_DIM // 2
    x1, x2 = blk[:half], blk[half:ROT_DIM]
    return jnp.concatenate([x1 * cos - x2 * sin, x2 * cos + x1 * sin, blk[ROT_DIM:]], axis=0)


def _group_attn_kernel(sink_ref, *refs, has_sink, dil, tstep, nstep):
    x_refs = refs[:X_SLABS]
    (wt_ref, cos_ref, sin_ref, o_ref, lse_ref, st_ref,
     kprev_sc, vprev_sc, kv_nat_sc, o_sc, st_sc, pt_sc, bias_sc) = refs[X_SLABS:]
    j = pl.program_id(1)
    n_units = tstep // BLOCK
    is_last = j == nstep - 1

    @pl.when(j == 0)
    def _():
        kprev_sc[...] = jnp.zeros_like(kprev_sc)
        vprev_sc[...] = jnp.zeros_like(vprev_sc)

    scale = HEAD_DIM ** -0.5 * LOG2E
    kidx = lax.broadcasted_iota(jnp.int32, (2 * BLOCK, 2 * BLOCK), 0)
    qidx = lax.broadcasted_iota(jnp.int32, (2 * BLOCK, 2 * BLOCK), 1) % BLOCK
    dist = qidx + BLOCK - kidx
    band = (dist >= 0) & (dist <= BLOCK)
    bias_sc[0] = jnp.where(band, 0.0, NEG_INF)
    bias_sc[1] = jnp.where(band & (kidx >= jnp.where(j > 0, 0, BLOCK)), 0.0, NEG_INF)
    lane2 = lax.broadcasted_iota(jnp.int32, (1, 2 * BLOCK), 1)
    ones_rows = jnp.ones((ONES_ROWS, 2 * BLOCK), F32)
    zeros_half = jnp.zeros((HEAD_DIM, 2 * BLOCK), F32)
    nt = (((1,), (1,)), ((), ()))

    def unit_rows(u):
        return pl.ds(u * BLOCK, BLOCK) if dil == 1 else pl.ds(u, BLOCK, stride=dil)

    carried = None
    for qd in range(n_units // UNITS_PER_PROJ):
        units = [UNITS_PER_PROJ * qd + w for w in range(UNITS_PER_PROJ)]
        x_quad = jnp.concatenate(
            [jnp.concatenate([x_refs[c][0, unit_rows(u), :] for c in range(X_SLABS)], axis=1) for u in units],
            axis=0).astype(BF16)
        ht = lax.dot_general(wt_ref[...], x_quad, nt, preferred_element_type=F32)
        cols = slice(qd * UNITS_PER_PROJ * BLOCK, (qd + 1) * UNITS_PER_PROJ * BLOCK)
        cos, sin = cos_ref[:, cols], sin_ref[:, cols]
        qt = [_rope_t(ht[h * HEAD_DIM:(h + 1) * HEAD_DIM], cos, sin) * scale for h in range(Q_HEADS)]
        kt = jnp.concatenate([_rope_t(ht[QW + h * HEAD_DIM:QW + (h + 1) * HEAD_DIM], cos, sin)
                              for h in range(KV_HEADS)], axis=0)
        vt = ht[QW + KW:GW]
        k_rm = kt.T

        if dil > 1:
            @pl.when(is_last)
            def _():
                v_rm = vt.T
                for w, u in enumerate(units):
                    kv_nat_sc[0, unit_rows(u), :] = k_rm[w * BLOCK:(w + 1) * BLOCK]
                    kv_nat_sc[1, unit_rows(u), :] = v_rm[w * BLOCK:(w + 1) * BLOCK]

        def scores(w, u, carried):
            lanes = slice(w * BLOCK, (w + 1) * BLOCK)
            k_cur = k_rm[w * BLOCK:(w + 1) * BLOCK].astype(BF16)
            vt_cur = vt[:, lanes].astype(BF16)
            if dil == 1 and u > 0:
                k_prev, vt_prev = carried
                mask = 0
            else:
                slot = 0 if dil == 1 else u
                k_prev, vt_prev = kprev_sc[slot], vprev_sc[slot]
                mask = 1
            k_cat = jnp.concatenate([k_prev, k_cur], axis=0)
            wq = []
            for c in range(QW // LANES):
                top = jnp.concatenate([qt[2 * c][:, lanes], qt[2 * c + 1][:, lanes]], axis=1)
                wq.append(jnp.concatenate([top, zeros_half] if c < GROUP // 2 else [zeros_half, top], axis=0))
            wq = jnp.concatenate(wq, axis=1).astype(BF16)
            st_sc[u % 2] = jnp.dot(k_cat, wq, preferred_element_type=F32)
            return dict(w=w, u=u, lanes=lanes, k_cur=k_cur, vt_cur=vt_cur, vt_prev=vt_prev, mask=mask)

        def softmax(t):
            par = t["u"] % 2
            ms = []
            for c in range(QW // LANES):
                pc = slice(c * 2 * BLOCK, (c + 1) * 2 * BLOCK)
                st = st_sc[par, :, pc] + bias_sc[t["mask"]]
                m = jnp.max(st, axis=0, keepdims=True)
                if has_sink:
                    sink = jnp.where(lane2 < BLOCK, sink_ref[2 * c], sink_ref[2 * c + 1]) * LOG2E
                    m = jnp.maximum(m, sink)
                ms.append(m)
                pt_sc[par, :, pc] = jnp.exp2(st - m).astype(BF16)
            t["ms"] = ms

        def values(t):
            u, par, lanes = t["u"], t["u"] % 2, t["lanes"]
            lse_rows = []
            for hk in range(KV_HEADS):
                hd = slice(hk * HEAD_DIM, (hk + 1) * HEAD_DIM)
                vt_aug = jnp.concatenate(
                    [jnp.concatenate([t["vt_prev"][hd], t["vt_cur"][hd]], axis=1), ones_rows.astype(BF16)], axis=0)
                acc2 = jnp.dot(vt_aug, pt_sc[par, :, hk * 4 * BLOCK:(hk + 1) * 4 * BLOCK],
                               preferred_element_type=F32)
                for i in range(GROUP // 2):
                    c = hk * (GROUP // 2) + i
                    acc = acc2[:, i * 2 * BLOCK:(i + 1) * 2 * BLOCK]
                    m = t["ms"][c]
                    den = acc[HEAD_DIM:HEAD_DIM + 1]
                    if has_sink:
                        sink = jnp.where(lane2 < BLOCK, sink_ref[2 * c], sink_ref[2 * c + 1]) * LOG2E
                        den = den + jnp.exp2(sink - m)
                    ot = acc[:HEAD_DIM] / den
                    lse = m * LN2 + jnp.log(den)
                    lse_rows += [lse[:, :BLOCK], lse[:, BLOCK:]]
                    slab_t = jnp.concatenate([ot[:, :BLOCK], ot[:, BLOCK:]], axis=0)
                    o_sc[c, unit_rows(u), :] = slab_t.T
            lse_t = jnp.concatenate(lse_rows + [jnp.zeros((LANES - Q_HEADS, BLOCK), F32)], axis=0)
            lse_ref[0, unit_rows(u), :] = lse_t.T
            if dil > 1:
                kprev_sc[u] = t["k_cur"]
                vprev_sc[u] = t["vt_cur"]
            elif u == n_units - 1:
                kprev_sc[0] = t["k_cur"]
                vprev_sc[0] = t["vt_cur"]

                @pl.when(is_last)
                def _():
                    st_ref[0, :KW, :] = kt[:, lanes]
                    st_ref[0, KW:, :] = vt[:, lanes]

        pending = None
        for w, u in enumerate(units):
            t = scores(w, u, carried)
            carried = (t["k_cur"], t["vt_cur"])
            if pending is not None:
                softmax(pending)
                values(pending)
            pending = t
        softmax(pending)
        values(pending)

    for c in range(QW // LANES):
        o_ref[0, :, c * LANES:(c + 1) * LANES] = o_sc[c].astype(o_ref.dtype)

    if dil > 1:
        @pl.when(is_last)
        def _():
            st_ref[0, :KW, :] = kv_nat_sc[0].T
            st_ref[0, KW:, :] = kv_nat_sc[1].T


def _group_attn_prompt(x, w_g, sinks, dil, has_sink):
    bn, s, d = x.shape
    span = BLOCK * dil
    tstep = max(STEP_TOKENS, span)
    nstep = s // tstep
    if dil == 1:
        pos = jnp.arange(s)
    else:
        pos = (jnp.arange(nstep)[:, None, None] * tstep + jnp.arange(BLOCK)[None, None, :] * dil
               + jnp.arange(dil)[None, :, None]).reshape(s)
    half = ROT_DIM // 2
    inv = ROPE_THETA ** (-jnp.arange(half, dtype=F32) / half)
    ang = pos.astype(F32)[:, None] * inv[None, :]
    cos_t, sin_t = jnp.cos(ang).T, jnp.sin(ang).T
    tab_spec = pl.BlockSpec((half, tstep), lambda b, j: (0, j))
    nprev = 1 if dil == 1 else dil
    o, lse, st = pl.pallas_call(
        functools.partial(_group_attn_kernel, has_sink=has_sink, dil=dil, tstep=tstep, nstep=nstep),
        grid=(bn, nstep),
        in_specs=([pl.BlockSpec(memory_space=pltpu.SMEM)]
                  + [pl.BlockSpec((1, tstep, LANES), functools.partial(lambda c, b, j: (b, j, c), c))
                     for c in range(X_SLABS)]
                  + [pl.BlockSpec((GW, d), lambda b, j: (0, 0)), tab_spec, tab_spec]),
        out_specs=[
            pl.BlockSpec((1, tstep, QW), lambda b, j: (b, j, 0)),
            pl.BlockSpec((1, tstep, LANES), lambda b, j: (b, j, 0)),
            pl.BlockSpec((1, 2 * KW, span), lambda b, j: (b, 0, 0)),
        ],
        out_shape=[
            jax.ShapeDtypeStruct((bn, s, QW), BF16),
            jax.ShapeDtypeStruct((bn, s, LANES), F32),
            jax.ShapeDtypeStruct((bn, 2 * KW, span), F32),
        ],
        scratch_shapes=[pltpu.VMEM((nprev, BLOCK, KW), BF16), pltpu.VMEM((nprev, KW, BLOCK), BF16),
                        pltpu.VMEM((2, span, LANES), F32), pltpu.VMEM((QW // LANES, tstep, LANES), F32),
                        pltpu.VMEM((2, 2 * BLOCK, Q_HEADS * BLOCK), F32),
                        pltpu.VMEM((2, 2 * BLOCK, Q_HEADS * BLOCK), BF16),
                        pltpu.VMEM((2, 2 * BLOCK, 2 * BLOCK), F32)],
        compiler_params=_cparams(("arbitrary", "arbitrary")),
        name=f"l0_group_attn_d{dil}_{'sink' if has_sink else 'nosink'}",
    )(sinks, *([x] * X_SLABS), w_g.T, cos_t, sin_t)
    st = jnp.transpose(st.reshape(bn, 2, KV_HEADS, HEAD_DIM, span), (0, 4, 1, 2, 3))
    return o.reshape(bn * s, QW), lse.reshape(bn * s, LANES), st


def _l0_out_kernel(x_ref, oa_ref, o0_ref, o1_ref, o2_ref, l0_ref, l1_ref, l2_ref,
                   wz_ref, wo_ref, g_ref, b_ref, y_ref):
    x = x_ref[...]
    z = jnp.dot(x.astype(BF16), wz_ref[...], preferred_element_type=F32)
    l0, l1, l2 = l0_ref[...], l1_ref[...], l2_ref[...]
    m = jnp.maximum(jnp.maximum(l0, l1), l2)
    e0, e1, e2 = jnp.exp(l0 - m), jnp.exp(l1 - m), jnp.exp(l2 - m)
    inv = 1.0 / (e0 + e1 + e2)
    w0, w1, w2 = e0 * inv, e1 * inv, e2 * inv
    tm = x.shape[0]
    parts = []
    for hq in range(Q_HEADS):
        c = slice(hq * HEAD_DIM, (hq + 1) * HEAD_DIM)
        bshape = (tm, HEAD_DIM)
        parts.append(jnp.broadcast_to(w0[:, hq:hq + 1], bshape) * o0_ref[:, c].astype(F32)
                     + jnp.broadcast_to(w1[:, hq:hq + 1], bshape) * o1_ref[:, c].astype(F32)
                     + jnp.broadcast_to(w2[:, hq:hq + 1], bshape) * o2_ref[:, c].astype(F32))
    mix = jnp.concatenate([oa_ref[...].astype(F32)] + parts, axis=1)
    mix = (mix * (z * jax.nn.sigmoid(z))).astype(BF16)
    y = jnp.dot(mix, wo_ref[...], preferred_element_type=F32)
    y_ref[...] = _layer_norm_rows(DN_ALPHA * x + y, g_ref[...], b_ref[...])


def _l0_out(x2, oa, obs, lses, wz, wo, g, b):
    n, d = x2.shape
    tm = 512
    row = lambda w: pl.BlockSpec((tm, w), lambda i: (i, 0))
    full = lambda a: pl.BlockSpec(a.shape, lambda i: (0, 0))
    return pl.pallas_call(
        _l0_out_kernel,
        grid=(n // tm,),
        in_specs=[row(d), row(QW), row(QW), row(QW), row(QW), row(LANES), row(LANES), row(LANES),
                  full(wz), full(wo), full(g), full(b)],
        out_specs=row(d),
        out_shape=jax.ShapeDtypeStruct((n, d), F32),
        compiler_params=_cparams(("arbitrary",)),
        name="l0_out",
    )(x2, oa, *obs, *lses, wz, wo, g, b)


def _proj_kernel(x_ref, w_ref, cos_ref, slo_ref, shi_ref, h_ref, *, n_rope, n_scaled, tn):
    j = pl.program_id(0)
    h = jnp.dot(x_ref[...].astype(BF16), w_ref[...], preferred_element_type=F32)
    if n_rope:
        cos, slo, shi = cos_ref[...], slo_ref[...], shi_ref[...]
        roped = jnp.concatenate(
            [_rope(h[:, c * LANES:(c + 1) * LANES], cos, slo, shi) for c in range(tn // LANES)], axis=1)
        h = jnp.where(j < n_rope, roped, h)
        h = h * jnp.where(j < n_scaled, HEAD_DIM ** -0.5, 1.0)
    h_ref[...] = h


def _proj(x2, w, tabs, tn, n_rope=0, n_scaled=0):
    n, d = x2.shape
    ncol = w.shape[1]
    tab_spec = pl.BlockSpec((n, LANES), lambda j: (0, 0))
    return pl.pallas_call(
        functools.partial(_proj_kernel, n_rope=n_rope, n_scaled=n_scaled, tn=tn),
        grid=(ncol // tn,),
        in_specs=[pl.BlockSpec((n, d), lambda j: (0, 0)),
                  pl.BlockSpec((d, tn), lambda j: (0, j)),
                  tab_spec, tab_spec, tab_spec],
        out_specs=pl.BlockSpec((n, tn), lambda j: (0, j)),
        out_shape=jax.ShapeDtypeStruct((n, ncol), F32),
        compiler_params=_cparams(("arbitrary",)),
        name=f"decode_proj_{ncol}",
    )(x2, w, *tabs)


def _decode_attn_kernel(sink_ref, h_ref, ca_ref, c0_ref, c1_ref, c2_ref,
                        oa_ref, o0_ref, o1_ref, o2_ref, l0_ref, l1_ref, l2_ref,
                        na_ref, n0_ref, n1_ref, n2_ref, sa_sc, s0_sc, s1_sc, s2_sc, *, sb, t_new):
    nrow = Q_HEADS * TPAD
    lane8 = lax.broadcasted_iota(jnp.int32, (TPAD, LANES), 1)
    lo_half = lane8 < HEAD_DIM
    lane_k = lax.broadcasted_iota(jnp.int32, (KW, LANES), 1)
    k_base = N_GROUPS * QW
    v_base = k_base + N_GROUPS * KW
    caches = (ca_ref, c0_ref, c1_ref, c2_ref)
    new_caches = (na_ref, n0_ref, n1_ref, n2_ref)
    o_refs = (oa_ref, o0_ref, o1_ref, o2_ref)
    l_refs = (None, l0_ref, l1_ref, l2_ref)
    sc_refs = (sa_sc, s0_sc, s1_sc, s2_sc)
    zpad = jnp.zeros((LANES - TPAD, 2 * KW), F32)
    nt = (((1,), (1,)), ((), ()))

    work = []
    for g in range(N_GROUPS):
        dil = DILATIONS[g]
        nkeys = BLOCK * dil
        cref = caches[g]
        for s_i in range(sb):
            rows = slice(s_i * TPAD, (s_i + 1) * TPAD)
            blocks = []
            for c in range(QW // LANES):
                slab = h_ref[rows, g * QW + c * LANES: g * QW + (c + 1) * LANES]
                swapped = pltpu.roll(slab, HEAD_DIM, 1)
                if c < QW // LANES // 2:
                    blocks += [jnp.where(lo_half, slab, 0.0), jnp.where(lo_half, swapped, 0.0)]
                else:
                    blocks += [jnp.where(lo_half, 0.0, swapped), jnp.where(lo_half, 0.0, slab)]
            qm = jnp.concatenate(blocks, axis=0).astype(BF16)
            k_new = h_ref[rows, k_base + g * KW: k_base + (g + 1) * KW]
            v_new = h_ref[rows, v_base + g * KW: v_base + (g + 1) * KW]
            kv_new = jnp.concatenate([jnp.concatenate([k_new, v_new], axis=1), zpad], axis=0)
            kv_new_b = kv_new.astype(BF16)
            sc_refs[g][s_i, :, :nkeys] = jnp.dot(qm, cref[s_i, :KW, :].astype(BF16), preferred_element_type=F32)
            sc_refs[g][s_i, :, nkeys:] = lax.dot_general(qm, kv_new_b[:, :KW], nt, preferred_element_type=F32)
            work.append((g, s_i, rows, kv_new, kv_new_b))

    for g, s_i, rows, kv_new, kv_new_b in work:
        nkeys = BLOCK * DILATIONS[g]
        cref, nref = caches[g], new_caches[g]
        new_cols = pltpu.roll(kv_new.T, LANES - t_new, 1)
        for half in range(2):
            hrows = slice(half * KW, (half + 1) * KW)
            shifted = pltpu.roll(cref[s_i, hrows, :], nkeys - t_new, 1)
            nref[s_i, hrows, :] = shifted
            nref[s_i, hrows, nkeys - LANES:] = jnp.where(lane_k >= LANES - t_new, new_cols[hrows],
                                                         shifted[:, nkeys - LANES:])

    for g, s_i, rows, kv_new, kv_new_b in work:
        dil = DILATIONS[g]
        nkeys = BLOCK * dil
        ncol = nkeys + LANES
        tok = lax.broadcasted_iota(jnp.int32, (nrow, ncol), 0) % TPAD
        col = lax.broadcasted_iota(jnp.int32, (nrow, ncol), 1)
        new_j = col - nkeys
        valid = (((col < nkeys) & (((col - tok) & (dil - 1)) == 0) & (col >= tok))
                 | ((col >= nkeys) & (((tok - new_j) & (dil - 1)) == 0) & (new_j <= tok) & (new_j < t_new)))
        s_all = jnp.where(valid, sc_refs[g][s_i], NEG_INF)
        m = jnp.max(s_all, axis=-1, keepdims=True)
        if g == 0:
            sink = sink_ref[:, 0:1]
            m = jnp.maximum(m, sink)
        p = jnp.exp(s_all - m)
        den = jnp.sum(p, axis=-1, keepdims=True)
        if g == 0:
            den = den + jnp.exp(sink - m)
        pb = p.astype(BF16)
        o = (lax.dot_general(pb[:, :nkeys], caches[g][s_i, KW:, :].astype(BF16), nt, preferred_element_type=F32)
             + jnp.dot(pb[:, nkeys:], kv_new_b[:, KW:], preferred_element_type=F32)) / den
        for c in range(QW // LANES):
            ev = o[(2 * c) * TPAD:(2 * c + 1) * TPAD]
            od = o[(2 * c + 1) * TPAD:(2 * c + 2) * TPAD]
            if c < QW // LANES // 2:
                slab = jnp.where(lo_half, ev, pltpu.roll(od, HEAD_DIM, 1))
            else:
                slab = jnp.where(lo_half, pltpu.roll(ev, HEAD_DIM, 1), od)
            o_refs[g][rows, c * LANES:(c + 1) * LANES] = slab.astype(o_refs[g].dtype)
        if g > 0:
            lse = m + jnp.log(den)
            lse_acc = jnp.zeros((TPAD, LANES), F32)
            for hq in range(Q_HEADS):
                lse_acc = jnp.where(lane8 == hq, lse[hq * TPAD:(hq + 1) * TPAD], lse_acc)
            l_refs[g][rows, :] = lse_acc


def _cache_rows_minor(c):
    nseq, rows = c.shape[1], c.shape[2]
    return jnp.transpose(c[0], (0, 2, 3, 4, 1)).reshape(nseq, 2 * KW, rows)


def _cache_rows_major(c):
    nseq, _, rows = c.shape
    return jnp.transpose(c.reshape(nseq, 2, KV_HEADS, HEAD_DIM, rows), (0, 4, 1, 2, 3))[None]


def _decode_attn(h, caches, sink_rows, t_new):
    cts = [_cache_rows_minor(c) for c in caches]
    nseq = cts[0].shape[0]
    sb = 2
    n = nseq * TPAD
    row = lambda w: pl.BlockSpec((sb * TPAD, w), lambda i: (i, 0))
    cspec = lambda c: pl.BlockSpec((sb,) + c.shape[1:], lambda i: (i, 0, 0))
    outs = pl.pallas_call(
        functools.partial(_decode_attn_kernel, sb=sb, t_new=t_new),
        grid=(nseq // sb,),
        in_specs=[pl.BlockSpec(sink_rows.shape, lambda i: (0, 0)), row(h.shape[1])] + [cspec(c) for c in cts],
        out_specs=[row(QW)] * 4 + [row(LANES)] * 3 + [cspec(c) for c in cts],
        out_shape=([jax.ShapeDtypeStruct((n, QW), BF16)] * 4 + [jax.ShapeDtypeStruct((n, LANES), F32)] * 3
                   + [jax.ShapeDtypeStruct(c.shape, F32) for c in cts]),
        scratch_shapes=[pltpu.VMEM((sb, Q_HEADS * TPAD, c.shape[2] + LANES), F32) for c in cts],
        compiler_params=_cparams(("arbitrary",)),
        name="l0_decode_attn",
    )(sink_rows, h, *cts)
    return outs[:4], outs[4:7], [_cache_rows_major(c) for c in outs[7:]]


def _mlstm_chunk(q, k, v, gates, hh, c_mem, c_norm, m_prev, n_valid):
    L = q.shape[0]
    ig_col = gates[:, hh:hh + 1]
    fg = gates[:, C_HEADS + hh:C_HEADS + hh + 1]
    lf_col = jnp.minimum(fg, 0.0) - jnp.log(1.0 + jnp.exp(-jnp.abs(fg)))
    ri = lax.broadcasted_iota(jnp.int32, (L, L), 0)
    ci = lax.broadcasted_iota(jnp.int32, (L, L), 1)
    tril = ri >= ci
    eye = ri == ci
    lf_row = jnp.sum(jnp.where(eye, lf_col, 0.0), axis=0, keepdims=True)
    ig_row = jnp.sum(jnp.where(eye, ig_col, 0.0), axis=0, keepdims=True)
    b_col = jnp.sum(jnp.where(tril, lf_row, 0.0), axis=1, keepdims=True)
    b_row = jnp.sum(jnp.where(ri <= ci, lf_col, 0.0), axis=0, keepdims=True)
    dmat = jnp.where(tril, b_col - b_row + ig_row, NEG_INF)
    inter = b_col + m_prev
    m = jnp.maximum(inter, jnp.max(dmat, axis=1, keepdims=True))
    pw = jnp.exp(dmat - m)
    qb, kb, vb = q.astype(BF16), k.astype(BF16), v.astype(BF16)
    sc = lax.dot_general(qb, kb, (((1,), (1,)), ((), ())), preferred_element_type=F32) * pw
    carry_w = jnp.exp(inter - m)
    num = (jnp.dot(sc.astype(BF16), vb, preferred_element_type=F32)
           + carry_w * jnp.dot(qb, c_mem.astype(BF16), preferred_element_type=F32))
    den = jnp.sum(sc, axis=1, keepdims=True) + carry_w * jnp.sum(q * c_norm, axis=1, keepdims=True)
    h = num / jnp.maximum(jnp.abs(den), jnp.exp(-m))
    last = n_valid - 1
    m_new = m[last:last + 1, :]
    b_last = b_col[last:last + 1, :]
    ws = jnp.exp(b_last - b_col + ig_col - m_new)
    if n_valid < L:
        ws = jnp.where(lax.broadcasted_iota(jnp.int32, (L, 1), 0) < n_valid, ws, 0.0)
    decay = jnp.exp(b_last + m_prev - m_new)
    wk = ws * k
    c_mem_new = decay * c_mem + lax.dot_general(
        wk.astype(BF16), vb, (((0,), (0,)), ((), ())), preferred_element_type=F32)
    c_norm_new = decay * c_norm + jnp.sum(wk, axis=0, keepdims=True)
    return h, c_mem_new, c_norm_new, m_new


AUG = LANES


def _l1_prompt_kernel(x_ref, w_ref, bias_ref, h_ref, cmem_ref, cnorm_ref, cmax_ref,
                      caug_sc, m_sc, s_sc, lhs_sc, *, tm, nblk):
    j = pl.program_id(1)

    @pl.when(j == 0)
    def _():
        caug_sc[...] = jnp.zeros_like(caug_sc)
        m_sc[...] = jnp.zeros_like(m_sc)

    proj = jnp.dot(x_ref[0].astype(BF16), w_ref[...], preferred_element_type=F32)
    g_t = (proj[:, 3 * MIX:] + bias_ref[...]).T[0:TPAD]
    lf_t = jnp.minimum(g_t, 0.0) - jnp.log(1.0 + jnp.exp(-jnp.abs(g_t)))
    ri = lax.broadcasted_iota(jnp.int32, (BLOCK, BLOCK), 0)
    ci = lax.broadcasted_iota(jnp.int32, (BLOCK, BLOCK), 1)
    tril = ri >= ci
    upper = (ri <= ci).astype(F32)
    ones_blk = jnp.ones((BLOCK, AUG), BF16)
    kscale = C_HEAD_DIM ** -0.5
    nt = (((1,), (1,)), ((), ()))

    def gate_stage(c, m_prevs):
        rows = slice(c * BLOCK, (c + 1) * BLOCK)
        par = c % 2
        b_t = jnp.dot(lf_t[:, rows], upper, preferred_element_type=F32, precision=lax.Precision.HIGHEST)
        qs, ks, vs = [], [], []
        for hh in range(C_HEADS):
            q = proj[rows, hh * C_HEAD_DIM:(hh + 1) * C_HEAD_DIM]
            k = proj[rows, MIX + hh * C_HEAD_DIM: MIX + (hh + 1) * C_HEAD_DIM] * kscale
            v = proj[rows, 2 * MIX + hh * C_HEAD_DIM: 2 * MIX + (hh + 1) * C_HEAD_DIM]
            qs.append(q)
            ks.append(k)
            vs.append(v.astype(BF16))
            s_sc[par, hh] = lax.dot_general(q.astype(BF16), k.astype(BF16), nt, preferred_element_type=F32)
        stats = []
        for hh in range(C_HEADS):
            ig_row = g_t[hh:hh + 1, rows]
            lf_row = lf_t[C_HEADS + hh:C_HEADS + hh + 1, rows]
            b_row = b_t[C_HEADS + hh:C_HEADS + hh + 1, :]
            m_prev = m_prevs[hh]
            b_col = jnp.sum(jnp.where(tril, lf_row, 0.0), axis=1, keepdims=True)
            dmat = jnp.where(tril, b_col - (b_row - ig_row), NEG_INF)
            inter = b_col + m_prev
            m = jnp.maximum(inter, jnp.max(dmat, axis=1, keepdims=True))
            sc = s_sc[par, hh] * jnp.exp(dmat - m)
            carry_w = jnp.exp(inter - m)
            lhs_sc[par, hh, :, :BLOCK] = sc.astype(BF16)
            lhs_sc[par, hh, :, BLOCK:] = (carry_w * qs[hh]).astype(BF16)
            m_new = m[BLOCK - 1:BLOCK, :]
            b_last = b_col[BLOCK - 1:BLOCK, :]
            ws_row = jnp.exp(b_last - b_row + ig_row - m_new)
            decay = jnp.exp(b_last + m_prev - m_new)
            stats.append((m, m_new, ws_row, decay))
        return dict(rows=rows, par=par, ks=ks, vs=vs, stats=stats)

    def memory_stage(t):
        rows, par = t["rows"], t["par"]
        for hh in range(C_HEADS):
            cols = slice(hh * C_HEAD_DIM, (hh + 1) * C_HEAD_DIM)
            m, m_new, ws_row, decay = t["stats"][hh]
            caug = caug_sc[hh]
            v_aug = jnp.concatenate([t["vs"][hh], ones_blk], axis=1)
            rhs = jnp.concatenate([v_aug, caug.astype(BF16)], axis=0)
            res = jnp.dot(lhs_sc[par, hh], rhs, preferred_element_type=F32)
            den = jnp.maximum(jnp.abs(res[:, C_HEAD_DIM:]), jnp.exp(-m))
            h_ref[0, rows, cols] = res[:, :C_HEAD_DIM] / jnp.concatenate([den, den], axis=1)
            wk_t = (t["ks"][hh].T * ws_row).astype(BF16)
            caug_sc[hh] = decay * caug + jnp.dot(wk_t, v_aug, preferred_element_type=F32)

    m_prevs = [m_sc[hh:hh + 1, 0:1] for hh in range(C_HEADS)]
    pending = None
    for c in range(tm // BLOCK):
        t = gate_stage(c, m_prevs)
        m_prevs = [st[1] for st in t["stats"]]
        if pending is not None:
            memory_stage(pending)
        pending = t
    memory_stage(pending)
    for hh in range(C_HEADS):
        m_sc[hh:hh + 1, :] = jnp.broadcast_to(m_prevs[hh], (1, LANES))

    @pl.when(j == nblk - 1)
    def _():
        for hh in range(C_HEADS):
            cmem_ref[0, hh] = caug_sc[hh, :, :C_HEAD_DIM]
            cnorm_ref[0, hh:hh + 1, :] = caug_sc[hh, :, C_HEAD_DIM:].T[0:1, :]
        cmax_ref[0] = m_sc[...]


def _l1_prompt(x, w_qkvg, bias_row):
    bn, s, d = x.shape
    tm = 512
    nblk = s // tm
    ncol = w_qkvg.shape[1]
    return pl.pallas_call(
        functools.partial(_l1_prompt_kernel, tm=tm, nblk=nblk),
        grid=(bn, nblk),
        in_specs=[pl.BlockSpec((1, tm, d), lambda b, j: (b, j, 0)),
                  pl.BlockSpec((d, ncol), lambda b, j: (0, 0)),
                  pl.BlockSpec((1, LANES), lambda b, j: (0, 0))],
        out_specs=[pl.BlockSpec((1, tm, MIX), lambda b, j: (b, j, 0)),
                   pl.BlockSpec((1, C_HEADS, C_HEAD_DIM, C_HEAD_DIM), lambda b, j: (b, 0, 0, 0)),
                   pl.BlockSpec((1, C_HEADS, C_HEAD_DIM), lambda b, j: (b, 0, 0)),
                   pl.BlockSpec((1, TPAD, LANES), lambda b, j: (b, 0, 0))],
        out_shape=[jax.ShapeDtypeStruct((bn, s, MIX), F32),
                   jax.ShapeDtypeStruct((bn, C_HEADS, C_HEAD_DIM, C_HEAD_DIM), F32),
                   jax.ShapeDtypeStruct((bn, C_HEADS, C_HEAD_DIM), F32),
                   jax.ShapeDtypeStruct((bn, TPAD, LANES), F32)],
        scratch_shapes=[pltpu.VMEM((C_HEADS, C_HEAD_DIM, C_HEAD_DIM + AUG), F32),
                        pltpu.VMEM((TPAD, LANES), F32),
                        pltpu.VMEM((2, C_HEADS, BLOCK, BLOCK), F32),
                        pltpu.VMEM((2, C_HEADS, BLOCK, BLOCK + C_HEAD_DIM), BF16)],
        compiler_params=_cparams(("arbitrary", "arbitrary")),
        name="l1_prompt_mlstm",
    )(x, w_qkvg, bias_row)


def _l1_decode_kernel(p_ref, bias_ref, cm_ref, cn_ref, cx_ref, h_ref, cmo_ref, cno_ref, cxo_ref,
                      *, sb, t_new):
    kscale = C_HEAD_DIM ** -0.5
    for s_i in range(sb):
        rows = slice(s_i * TPAD, (s_i + 1) * TPAD)
        gates = p_ref[rows, 3 * MIX:] + bias_ref[...]
        for hh in range(C_HEADS):
            cols = slice(hh * C_HEAD_DIM, (hh + 1) * C_HEAD_DIM)
            q = p_ref[rows, cols]
            k = p_ref[rows, MIX + hh * C_HEAD_DIM: MIX + (hh + 1) * C_HEAD_DIM] * kscale
            v = p_ref[rows, 2 * MIX + hh * C_HEAD_DIM: 2 * MIX + (hh + 1) * C_HEAD_DIM]
            h, c_new, n_new, m_new = _mlstm_chunk(
                q, k, v, gates, hh, cm_ref[s_i, hh], cn_ref[s_i, hh:hh + 1, :],
                cx_ref[s_i, hh:hh + 1, 0:1], t_new)
            h_ref[rows, cols] = h
            cmo_ref[s_i, hh] = c_new
            cno_ref[s_i, hh:hh + 1, :] = n_new
            cxo_ref[s_i, hh:hh + 1, :] = jnp.broadcast_to(m_new, (1, LANES))


def _l1_decode(proj, bias_row, c_mem, c_norm, c_max_pad, t_new):
    nseq = c_mem.shape[0]
    sb = 4
    n = nseq * TPAD
    return pl.pallas_call(
        functools.partial(_l1_decode_kernel, sb=sb, t_new=t_new),
        grid=(nseq // sb,),
        in_specs=[pl.BlockSpec((sb * TPAD, proj.shape[1]), lambda i: (i, 0)),
                  pl.BlockSpec((1, LANES), lambda i: (0, 0)),
                  pl.BlockSpec((sb, C_HEADS, C_HEAD_DIM, C_HEAD_DIM), lambda i: (i, 0, 0, 0)),
                  pl.BlockSpec((sb, C_HEADS, C_HEAD_DIM), lambda i: (i, 0, 0)),
                  pl.BlockSpec((sb, C_HEADS, LANES), lambda i: (i, 0, 0))],
        out_specs=[pl.BlockSpec((sb * TPAD, MIX), lambda i: (i, 0)),
                   pl.BlockSpec((sb, C_HEADS, C_HEAD_DIM, C_HEAD_DIM), lambda i: (i, 0, 0, 0)),
                   pl.BlockSpec((sb, C_HEADS, C_HEAD_DIM), lambda i: (i, 0, 0)),
                   pl.BlockSpec((sb, C_HEADS, LANES), lambda i: (i, 0, 0))],
        out_shape=[jax.ShapeDtypeStruct((n, MIX), F32),
                   jax.ShapeDtypeStruct(c_mem.shape, F32),
                   jax.ShapeDtypeStruct(c_norm.shape, F32),
                   jax.ShapeDtypeStruct(c_max_pad.shape, F32)],
        compiler_params=_cparams(("arbitrary",)),
        name="l1_decode_mlstm",
    )(proj, bias_row, c_mem, c_norm, c_max_pad)


def _l1_out_kernel(x_ref, h_ref, woz_ref, mhg_ref, wo_ref, g_ref, b_ref, y_ref):
    x = x_ref[...]
    oz = jnp.dot(x.astype(BF16), woz_ref[...], preferred_element_type=F32)
    parts = []
    for hh in range(C_HEADS):
        cols = slice(hh * C_HEAD_DIM, (hh + 1) * C_HEAD_DIM)
        hg = jax.nn.sigmoid(oz[:, cols]) * h_ref[:, cols]
        mu = jnp.mean(hg, axis=-1, keepdims=True)
        c = hg - mu
        var = jnp.mean(c * c, axis=-1, keepdims=True)
        parts.append(c * lax.rsqrt(var + MH_EPS))
    hn = jnp.concatenate(parts, axis=1) * mhg_ref[...]
    z = oz[:, MIX:]
    mix = (hn * (z * jax.nn.sigmoid(z))).astype(BF16)
    y = jnp.dot(mix, wo_ref[...], preferred_element_type=F32)
    y_ref[...] = _layer_norm_rows(DN_ALPHA * x + y, g_ref[...], b_ref[...])


def _l1_out(x2, h2, woz, mhg, wo, g, b):
    n, d = x2.shape
    tm = 512
    row = lambda w: pl.BlockSpec((tm, w), lambda i: (i, 0))
    full = lambda a: pl.BlockSpec(a.shape, lambda i: (0, 0))
    return pl.pallas_call(
        _l1_out_kernel,
        grid=(n // tm,),
        in_specs=[row(d), row(MIX), full(woz), full(mhg), full(wo), full(g), full(b)],
        out_specs=row(d),
        out_shape=jax.ShapeDtypeStruct((n, d), F32),
        compiler_params=_cparams(("arbitrary",)),
        name="l1_out",
    )(x2, h2, woz, mhg, wo, g, b)


def _group_cols(g):
    if g == 0:
        return [(0, QW), (QW, QW + KW), (QW + KW, GW)]
    b = g - 1
    qb0 = GW
    kb0 = GW + 3 * QW
    vb0 = kb0 + 3 * KW
    return [(qb0 + b * QW, qb0 + (b + 1) * QW), (kb0 + b * KW, kb0 + (b + 1) * KW),
            (vb0 + b * KW, vb0 + (b + 1) * KW)]


def kernel(x_prompt, x_sample, cache_a_kv, cache_b0_kv, cache_b1_kv, cache_b2_kv, state_c_mem,
           state_c_norm, state_c_max, w_in0, sinks0, w_out0, w_in1, b_gates1, mh_norm1, w_out1,
           ln_g, ln_b):
    bn, s, d = x_prompt.shape
    nseq, t_new, _ = x_sample.shape
    w0 = w_in0[0]
    z0 = 4 * GW
    w_groups = [jnp.concatenate([w0[:, a:b] for a, b in _group_cols(g)], axis=1).astype(BF16)
                for g in range(N_GROUPS)]
    wz0 = w0[:, z0:].astype(BF16)
    wo0 = w_out0[0].astype(BF16)
    sinks = sinks0[0].astype(F32)
    g0, b0 = ln_g[0][None, :], ln_b[0][None, :]
    g1, b1 = ln_g[1][None, :], ln_b[1][None, :]

    outs, lses, states_p = [], [], []
    for g in range(N_GROUPS):
        o, lse, st = _group_attn_prompt(x_prompt, w_groups[g], sinks, DILATIONS[g], g == 0)
        outs.append(o)
        lses.append(lse)
        states_p.append(st)
    xp2 = x_prompt.reshape(bn * s, d)
    y0p = _l0_out(xp2, outs[0], outs[1:], lses[1:], wz0, wo0, g0, b0)

    xs2 = jnp.pad(x_sample, ((0, 0), (0, TPAD - t_new), (0, 0))).reshape(nseq * TPAD, d)
    w_dec = jnp.concatenate(
        [w_groups[g][:, :QW] for g in range(N_GROUPS)] + [w_groups[g][:, QW:QW + KW] for g in range(N_GROUPS)]
        + [w_groups[g][:, QW + KW:] for g in range(N_GROUPS)], axis=1)
    tabs_s = tuple(jnp.tile(t, (nseq, 1)) for t in _rope_tables(PAST_LEN + jnp.arange(TPAD)))
    tn = 512
    hs = _proj(xs2, w_dec, tabs_s, tn, n_rope=(N_GROUPS * (QW + KW)) // tn, n_scaled=(N_GROUPS * QW) // tn)
    sink_rows = jnp.broadcast_to(jnp.repeat(sinks, TPAD)[:, None], (Q_HEADS * TPAD, LANES))
    dec_o, dec_l, states_s = _decode_attn(
        hs, (cache_a_kv, cache_b0_kv, cache_b1_kv, cache_b2_kv), sink_rows, t_new)
    y0s = _l0_out(xs2, dec_o[0], dec_o[1:], dec_l, wz0, wo0, g0, b0)

    w1 = w_in1[0]
    gate_pad = jnp.zeros((d, LANES - 2 * C_HEADS), F32)
    w_qkvg = jnp.concatenate([w1[:, :3 * MIX], w1[:, 5 * MIX:], gate_pad], axis=1).astype(BF16)
    w_oz = w1[:, 3 * MIX:5 * MIX].astype(BF16)
    wo1 = w_out1[0].astype(BF16)
    bias_row = jnp.pad(b_gates1[0].astype(F32), (0, LANES - 2 * C_HEADS))[None, :]
    mhg = mh_norm1[0][None, :]

    hp, cmem_p, cnorm_p, cmax_p = _l1_prompt(y0p.reshape(bn, s, d), w_qkvg, bias_row)
    y1p = _l1_out(y0p, hp.reshape(bn * s, MIX), w_oz, mhg, wo1, g1, b1)

    no_tabs = tuple(jnp.zeros((nseq * TPAD, LANES), F32) for _ in range(3))
    ps = _proj(y0s, w_qkvg, no_tabs, 640)
    cmax_pad = jnp.broadcast_to(state_c_max[0][:, :, None], (nseq, C_HEADS, LANES))
    hsd, cmem_s, cnorm_s, cmax_s = _l1_decode(ps, bias_row, state_c_mem[0], state_c_norm[0], cmax_pad, t_new)
    y1s = _l1_out(y0s, hsd, w_oz, mhg, wo1, g1, b1)

    y_prompt = y1p.reshape(bn, s, d)
    y_sample = y1s.reshape(nseq, TPAD, d)[:, :t_new]
    return (y_prompt, y_sample,
            states_p[0][None], states_p[1][None], states_p[2][None], states_p[3][None],
            cmem_p[None], cnorm_p[None], cmax_p[:, :C_HEADS, 0][None],
            states_s[0], states_s[1], states_s[2], states_s[3],
            cmem_s[None], cnorm_s[None], cmax_s[:, :, 0][None])
```

```python
import functools

import jax
import jax.numpy as jnp
from jax import lax
from jax.experimental import pallas as pl
from jax.experimental.pallas import tpu as pltpu

F32 = jnp.float32
BF16 = jnp.bfloat16

D_MODEL = 1024
PAST_LEN = 16384
HEAD_DIM = 64
ROT_DIM = HEAD_DIM // 4
ROPE_THETA = 500000.0
BLOCK = 128
Q_HEADS = 8
KV_HEADS = 2
GROUP = Q_HEADS // KV_HEADS
QW = Q_HEADS * HEAD_DIM
KW = KV_HEADS * HEAD_DIM
GW = QW + 2 * KW
DILATIONS = (1, 1, 4, 16)
N_GROUPS = 4
C_HEADS = 4
C_HEAD_DIM = 256
MIX = 1024
DEPTH = 2
DN_ALPHA = (2 * DEPTH) ** 0.25
LN_EPS = 1e-5
MH_EPS = 1e-6
NEG_INF = -1e30
LOG2E = 1.4426950408889634
LN2 = 0.6931471805599453
LANES = 128
TPAD = 8
VMEM_LIMIT = 56 * 1024 * 1024


def _cparams(sem):
    return pltpu.CompilerParams(dimension_semantics=sem, vmem_limit_bytes=VMEM_LIMIT)


def _layer_norm_rows(v, g, b):
    mu = jnp.mean(v, axis=-1, keepdims=True)
    c = v - mu
    var = jnp.mean(c * c, axis=-1, keepdims=True)
    return c * lax.rsqrt(var + LN_EPS) * g + b


def _rope(slab, cos, sin_lo, sin_hi):
    return (slab * cos + pltpu.roll(slab, LANES - ROT_DIM // 2, 1) * sin_lo
            + pltpu.roll(slab, ROT_DIM // 2, 1) * sin_hi)


def _rope_tables(pos):
    half = ROT_DIM // 2
    inv = ROPE_THETA ** (-jnp.arange(half, dtype=F32) / half)
    ang = pos.astype(F32)[:, None] * inv[None, :]
    cos, sin = jnp.cos(ang), jnp.sin(ang)
    lane = jnp.arange(LANES) % HEAD_DIM
    f = lane % half
    cos_t = jnp.where(lane[None, :] < ROT_DIM, cos[:, f], 1.0)
    sin_lo = jnp.where(lane[None, :] < half, -sin[:, f], 0.0)
    sin_hi = jnp.where((lane[None, :] >= half) & (lane[None, :] < ROT_DIM), sin[:, f], 0.0)
    return cos_t.astype(F32), sin_lo.astype(F32), sin_hi.astype(F32)


STEP_TOKENS = 2048
X_SLABS = D_MODEL // LANES
UNITS_PER_PROJ = 4
ONES_ROWS = 16


def _rope_t(blk, cos, sin):
    half = ROT_DIM // 2
    x1, x2 = blk[:half], blk[half:ROT_DIM]
    return jnp.concatenate([x1 * cos - x2 * sin, x2 * cos + x1 * sin, blk[ROT_DIM:]], axis=0)


def _group_attn_kernel(sink_ref, *refs, has_sink, dil, tstep, nstep):
    x_refs = refs[:X_SLABS]
    (wt_ref, cos_ref, sin_ref, o_ref, lse_ref, st_ref,
     kprev_sc, vprev_sc, kv_nat_sc, o_sc, st_sc, pt_sc, bias_sc, xq_sc, ht_sc) = refs[X_SLABS:]
    j = pl.program_id(1)
    n_units = tstep // BLOCK
    is_last = j == nstep - 1

    @pl.when(j == 0)
    def _():
        kprev_sc[...] = jnp.zeros_like(kprev_sc)
        vprev_sc[...] = jnp.zeros_like(vprev_sc)

    scale = HEAD_DIM ** -0.5 * LOG2E
    kidx = lax.broadcasted_iota(jnp.int32, (2 * BLOCK, 2 * BLOCK), 0)
    qidx = lax.broadcasted_iota(jnp.int32, (2 * BLOCK, 2 * BLOCK), 1) % BLOCK
    dist = qidx + BLOCK - kidx
    band = (dist >= 0) & (dist <= BLOCK)
    bias_sc[0] = jnp.where(band, 0.0, NEG_INF)
    bias_sc[1] = jnp.where(band & (kidx >= jnp.where(j > 0, 0, BLOCK)), 0.0, NEG_INF)
    lane2 = lax.broadcasted_iota(jnp.int32, (1, 2 * BLOCK), 1)
    ones_rows = jnp.ones((ONES_ROWS, 2 * BLOCK), F32)
    zeros_half = jnp.zeros((HEAD_DIM, 2 * BLOCK), F32)
    nt = (((1,), (1,)), ((), ()))

    span = BLOCK * dil

    def unit_rows(u):
        start = (u // dil) * span + u % dil
        return pl.ds(start, BLOCK) if dil == 1 else pl.ds(start, BLOCK, stride=dil)

    def state_rows(u):
        return pl.ds(u % dil, BLOCK) if dil == 1 else pl.ds(u % dil, BLOCK, stride=dil)

    n_quads = n_units // UNITS_PER_PROJ
    k_slab = D_MODEL // UNITS_PER_PROJ

    def quad_units(qd):
        return [UNITS_PER_PROJ * qd + w for w in range(UNITS_PER_PROJ)]

    def load_x(qd):
        xq_sc[qd % 2] = jnp.concatenate(
            [jnp.concatenate([x_refs[c][0, unit_rows(u), :] for c in range(X_SLABS)], axis=1)
             for u in quad_units(qd)], axis=0).astype(BF16)

    def project_piece(qd, kp):
        ks = slice(kp * k_slab, (kp + 1) * k_slab)
        part = lax.dot_general(wt_ref[:, ks], xq_sc[qd % 2, :, ks], nt, preferred_element_type=F32)
        if kp == 0:
            ht_sc[qd % 2] = part
        else:
            ht_sc[qd % 2] += part

    load_x(0)
    for kp in range(UNITS_PER_PROJ):
        project_piece(0, kp)
    carried = {}
    for qd in range(n_quads):
        units = quad_units(qd)
        ht = ht_sc[qd % 2]
        cols = slice(qd * UNITS_PER_PROJ * BLOCK, (qd + 1) * UNITS_PER_PROJ * BLOCK)
        cos, sin = cos_ref[:, cols], sin_ref[:, cols]
        qt = [_rope_t(ht[h * HEAD_DIM:(h + 1) * HEAD_DIM], cos, sin) * scale for h in range(Q_HEADS)]
        kt = jnp.concatenate([_rope_t(ht[QW + h * HEAD_DIM:QW + (h + 1) * HEAD_DIM], cos, sin)
                              for h in range(KV_HEADS)], axis=0)
        vt = ht[QW + KW:GW]
        k_rm = kt.T

        if dil > 1 and units[0] >= n_units - dil:
            @pl.when(is_last)
            def _():
                v_rm = vt.T
                for w, u in enumerate(units):
                    kv_nat_sc[0, state_rows(u), :] = k_rm[w * BLOCK:(w + 1) * BLOCK]
                    kv_nat_sc[1, state_rows(u), :] = v_rm[w * BLOCK:(w + 1) * BLOCK]

        def scores(w, u, carried):
            lanes = slice(w * BLOCK, (w + 1) * BLOCK)
            k_cur = k_rm[w * BLOCK:(w + 1) * BLOCK].astype(BF16)
            vt_cur = vt[:, lanes].astype(BF16)
            if u >= dil:
                k_prev, vt_prev = carried[u - dil]
                mask = 0
            else:
                k_prev, vt_prev = kprev_sc[u], vprev_sc[u]
                mask = 1
            k_cat = jnp.concatenate([k_prev, k_cur], axis=0)
            wq = []
            for c in range(QW // LANES):
                top = jnp.concatenate([qt[2 * c][:, lanes], qt[2 * c + 1][:, lanes]], axis=1)
                wq.append(jnp.concatenate([top, zeros_half] if c < GROUP // 2 else [zeros_half, top], axis=0))
            wq = jnp.concatenate(wq, axis=1).astype(BF16)
            st_sc[u % 2] = jnp.dot(k_cat, wq, preferred_element_type=F32)
            return dict(w=w, u=u, lanes=lanes, k_cur=k_cur, vt_cur=vt_cur, vt_prev=vt_prev, mask=mask)

        def softmax(t):
            par = t["u"] % 2
            ms = []
            for c in range(QW // LANES):
                pc = slice(c * 2 * BLOCK, (c + 1) * 2 * BLOCK)
                st = st_sc[par, :, pc] + bias_sc[t["mask"]]
                m = jnp.max(st, axis=0, keepdims=True)
                if has_sink:
                    sink = jnp.where(lane2 < BLOCK, sink_ref[2 * c], sink_ref[2 * c + 1]) * LOG2E
                    m = jnp.maximum(m, sink)
                ms.append(m)
                pt_sc[par, :, pc] = jnp.exp2(st - m).astype(BF16)
            t["ms"] = ms

        def values(t):
            u, par, lanes = t["u"], t["u"] % 2, t["lanes"]
            lse_rows = []
            for hk in range(KV_HEADS):
                hd = slice(hk * HEAD_DIM, (hk + 1) * HEAD_DIM)
                vt_aug = jnp.concatenate(
                    [jnp.concatenate([t["vt_prev"][hd], t["vt_cur"][hd]], axis=1), ones_rows.astype(BF16)], axis=0)
                acc2 = jnp.dot(vt_aug, pt_sc[par, :, hk * 4 * BLOCK:(hk + 1) * 4 * BLOCK],
                               preferred_element_type=F32)
                for i in range(GROUP // 2):
                    c = hk * (GROUP // 2) + i
                    acc = acc2[:, i * 2 * BLOCK:(i + 1) * 2 * BLOCK]
                    m = t["ms"][c]
                    den = acc[HEAD_DIM:HEAD_DIM + 1]
                    if has_sink:
                        sink = jnp.where(lane2 < BLOCK, sink_ref[2 * c], sink_ref[2 * c + 1]) * LOG2E
                        den = den + jnp.exp2(sink - m)
                    ot = acc[:HEAD_DIM] / den
                    lse = m * LN2 + jnp.log(den)
                    lse_rows += [lse[:, :BLOCK], lse[:, BLOCK:]]
                    slab_t = jnp.concatenate([ot[:, :BLOCK], ot[:, BLOCK:]], axis=0)
                    o_sc[c, unit_rows(u), :] = slab_t.T
            lse_t = jnp.concatenate(lse_rows + [jnp.zeros((LANES - Q_HEADS, BLOCK), F32)], axis=0)
            lse_ref[0, unit_rows(u), :] = lse_t.T
            if u >= n_units - dil:
                kprev_sc[u % dil] = t["k_cur"]
                vprev_sc[u % dil] = t["vt_cur"]
                if dil == 1:
                    @pl.when(is_last)
                    def _():
                        st_ref[0, :KW, :] = kt[:, lanes]
                        st_ref[0, KW:, :] = vt[:, lanes]

        prefetch = qd + 1 < n_quads
        pending = None
        for w, u in enumerate(units):
            t = scores(w, u, carried)
            carried[u] = (t["k_cur"], t["vt_cur"])
            if prefetch and w == 0:
                load_x(qd + 1)
            if pending is not None:
                softmax(pending)
            if prefetch:
                project_piece(qd + 1, w)
            if pending is not None:
                values(pending)
            pending = t
        softmax(pending)
        values(pending)

    for c in range(QW // LANES):
        o_ref[0, :, c * LANES:(c + 1) * LANES] = o_sc[c].astype(o_ref.dtype)

    if dil > 1:
        @pl.when(is_last)
        def _():
            st_ref[0, :KW, :] = kv_nat_sc[0].T
            st_ref[0, KW:, :] = kv_nat_sc[1].T


def _group_attn_prompt(x, w_g, sinks, dil, has_sink):
    bn, s, d = x.shape
    span = BLOCK * dil
    tstep = max(STEP_TOKENS, span)
    nstep = s // tstep
    pos = (jnp.arange(s // span)[:, None, None] * span + jnp.arange(dil)[None, :, None]
           + jnp.arange(BLOCK)[None, None, :] * dil).reshape(s)
    half = ROT_DIM // 2
    inv = ROPE_THETA ** (-jnp.arange(half, dtype=F32) / half)
    ang = pos.astype(F32)[:, None] * inv[None, :]
    cos_t, sin_t = jnp.cos(ang).T, jnp.sin(ang).T
    tab_spec = pl.BlockSpec((half, tstep), lambda b, j: (0, j))
    o, lse, st = pl.pallas_call(
        functools.partial(_group_attn_kernel, has_sink=has_sink, dil=dil, tstep=tstep, nstep=nstep),
        grid=(bn, nstep),
        in_specs=([pl.BlockSpec(memory_space=pltpu.SMEM)]
                  + [pl.BlockSpec((1, tstep, LANES), functools.partial(lambda c, b, j: (b, j, c), c))
                     for c in range(X_SLABS)]
                  + [pl.BlockSpec((GW, d), lambda b, j: (0, 0)), tab_spec, tab_spec]),
        out_specs=[
            pl.BlockSpec((1, tstep, QW), lambda b, j: (b, j, 0)),
            pl.BlockSpec((1, tstep, LANES), lambda b, j: (b, j, 0)),
            pl.BlockSpec((1, 2 * KW, span), lambda b, j: (b, 0, 0)),
        ],
        out_shape=[
            jax.ShapeDtypeStruct((bn, s, QW), BF16),
            jax.ShapeDtypeStruct((bn, s, LANES), F32),
            jax.ShapeDtypeStruct((bn, 2 * KW, span), F32),
        ],
        scratch_shapes=[pltpu.VMEM((dil, BLOCK, KW), BF16), pltpu.VMEM((dil, KW, BLOCK), BF16),
                        pltpu.VMEM((2, span, LANES), F32), pltpu.VMEM((QW // LANES, tstep, LANES), F32),
                        pltpu.VMEM((2, 2 * BLOCK, Q_HEADS * BLOCK), F32),
                        pltpu.VMEM((2, 2 * BLOCK, Q_HEADS * BLOCK), BF16),
                        pltpu.VMEM((2, 2 * BLOCK, 2 * BLOCK), F32),
                        pltpu.VMEM((2, UNITS_PER_PROJ * BLOCK, d), BF16),
                        pltpu.VMEM((2, GW, UNITS_PER_PROJ * BLOCK), F32)],
        compiler_params=_cparams(("arbitrary", "arbitrary")),
        name=f"l0_group_attn_d{dil}_{'sink' if has_sink else 'nosink'}",
    )(sinks, *([x] * X_SLABS), w_g.T, cos_t, sin_t)
    st = jnp.transpose(st.reshape(bn, 2, KV_HEADS, HEAD_DIM, span), (0, 4, 1, 2, 3))
    return o.reshape(bn * s, QW), lse.reshape(bn * s, LANES), st


SUB_ROWS = 256


def _l0_out_kernel(x_ref, oa_ref, o0_ref, o1_ref, o2_ref, l0_ref, l1_ref, l2_ref,
                   wz_ref, wo_ref, g_ref, b_ref, y_ref, z_sc, mix_sc):
    tm = x_ref.shape[0]
    n_sub = tm // SUB_ROWS

    def gate_proj(i):
        r = slice(i * SUB_ROWS, (i + 1) * SUB_ROWS)
        z_sc[r, :] = jnp.dot(x_ref[r, :].astype(BF16), wz_ref[...], preferred_element_type=F32)

    def mix_stage(i):
        r = slice(i * SUB_ROWS, (i + 1) * SUB_ROWS)
        l0, l1, l2 = l0_ref[r, :], l1_ref[r, :], l2_ref[r, :]
        m = jnp.maximum(jnp.maximum(l0, l1), l2)
        e0, e1, e2 = jnp.exp(l0 - m), jnp.exp(l1 - m), jnp.exp(l2 - m)
        inv = 1.0 / (e0 + e1 + e2)
        w0, w1, w2 = e0 * inv, e1 * inv, e2 * inv
        parts = []
        for hq in range(Q_HEADS):
            c = slice(hq * HEAD_DIM, (hq + 1) * HEAD_DIM)
            bshape = (SUB_ROWS, HEAD_DIM)
            parts.append(jnp.broadcast_to(w0[:, hq:hq + 1], bshape) * o0_ref[r, c].astype(F32)
                         + jnp.broadcast_to(w1[:, hq:hq + 1], bshape) * o1_ref[r, c].astype(F32)
                         + jnp.broadcast_to(w2[:, hq:hq + 1], bshape) * o2_ref[r, c].astype(F32))
        mix = jnp.concatenate([oa_ref[r, :].astype(F32)] + parts, axis=1)
        z = z_sc[r, :]
        mix_sc[r, :] = (mix * (z * jax.nn.sigmoid(z))).astype(BF16)

    def out_stage(i):
        r = slice(i * SUB_ROWS, (i + 1) * SUB_ROWS)
        y = jnp.dot(mix_sc[r, :], wo_ref[...], preferred_element_type=F32)
        y_ref[r, :] = _layer_norm_rows(DN_ALPHA * x_ref[r, :] + y, g_ref[...], b_ref[...])

    gate_proj(0)
    for i in range(n_sub):
        if i + 1 < n_sub:
            gate_proj(i + 1)
        mix_stage(i)
        out_stage(i)


def _l0_out(x2, oa, obs, lses, wz, wo, g, b):
    n, d = x2.shape
    tm = 512
    row = lambda w: pl.BlockSpec((tm, w), lambda i: (i, 0))
    full = lambda a: pl.BlockSpec(a.shape, lambda i: (0, 0))
    return pl.pallas_call(
        _l0_out_kernel,
        grid=(n // tm,),
        in_specs=[row(d), row(QW), row(QW), row(QW), row(QW), row(LANES), row(LANES), row(LANES),
                  full(wz), full(wo), full(g), full(b)],
        out_specs=row(d),
        out_shape=jax.ShapeDtypeStruct((n, d), F32),
        scratch_shapes=[pltpu.VMEM((tm, MIX), F32), pltpu.VMEM((tm, MIX), BF16)],
        compiler_params=_cparams(("arbitrary",)),
        name="l0_out",
    )(x2, oa, *obs, *lses, wz, wo, g, b)


def _proj_kernel(x_ref, w_ref, cos_ref, slo_ref, shi_ref, h_ref, *, n_rope, n_scaled, tn):
    j = pl.program_id(0)
    h = jnp.dot(x_ref[...].astype(BF16), w_ref[...], preferred_element_type=F32)
    if n_rope:
        cos, slo, shi = cos_ref[...], slo_ref[...], shi_ref[...]
        roped = jnp.concatenate(
            [_rope(h[:, c * LANES:(c + 1) * LANES], cos, slo, shi) for c in range(tn // LANES)], axis=1)
        h = jnp.where(j < n_rope, roped, h)
        h = h * jnp.where(j < n_scaled, HEAD_DIM ** -0.5, 1.0)
    h_ref[...] = h


def _proj(x2, w, tabs, tn, n_rope=0, n_scaled=0):
    n, d = x2.shape
    ncol = w.shape[1]
    tab_spec = pl.BlockSpec((n, LANES), lambda j: (0, 0))
    return pl.pallas_call(
        functools.partial(_proj_kernel, n_rope=n_rope, n_scaled=n_scaled, tn=tn),
        grid=(ncol // tn,),
        in_specs=[pl.BlockSpec((n, d), lambda j: (0, 0)),
                  pl.BlockSpec((d, tn), lambda j: (0, j)),
                  tab_spec, tab_spec, tab_spec],
        out_specs=pl.BlockSpec((n, tn), lambda j: (0, j)),
        out_shape=jax.ShapeDtypeStruct((n, ncol), F32),
        compiler_params=_cparams(("arbitrary",)),
        name=f"decode_proj_{ncol}",
    )(x2, w, *tabs)


def _decode_attn_kernel(sink_ref, h_ref, ca_ref, c0_ref, c1_ref, c2_ref,
                        oa_ref, o0_ref, o1_ref, o2_ref, l0_ref, l1_ref, l2_ref,
                        na_ref, n0_ref, n1_ref, n2_ref, sa_sc, s0_sc, s1_sc, s2_sc, *, sb, t_new):
    nrow = Q_HEADS * TPAD
    lane8 = lax.broadcasted_iota(jnp.int32, (TPAD, LANES), 1)
    lo_half = lane8 < HEAD_DIM
    lane_k = lax.broadcasted_iota(jnp.int32, (KW, LANES), 1)
    k_base = N_GROUPS * QW
    v_base = k_base + N_GROUPS * KW
    caches = (ca_ref, c0_ref, c1_ref, c2_ref)
    new_caches = (na_ref, n0_ref, n1_ref, n2_ref)
    o_refs = (oa_ref, o0_ref, o1_ref, o2_ref)
    l_refs = (None, l0_ref, l1_ref, l2_ref)
    sc_refs = (sa_sc, s0_sc, s1_sc, s2_sc)
    zpad = jnp.zeros((LANES - TPAD, 2 * KW), F32)
    nt = (((1,), (1,)), ((), ()))

    work = []
    for g in range(N_GROUPS):
        dil = DILATIONS[g]
        nkeys = BLOCK * dil
        cref = caches[g]
        for s_i in range(sb):
            rows = slice(s_i * TPAD, (s_i + 1) * TPAD)
            blocks = []
            for c in range(QW // LANES):
                slab = h_ref[rows, g * QW + c * LANES: g * QW + (c + 1) * LANES]
                swapped = pltpu.roll(slab, HEAD_DIM, 1)
                if c < QW // LANES // 2:
                    blocks += [jnp.where(lo_half, slab, 0.0), jnp.where(lo_half, swapped, 0.0)]
                else:
                    blocks += [jnp.where(lo_half, 0.0, swapped), jnp.where(lo_half, 0.0, slab)]
            qm = jnp.concatenate(blocks, axis=0).astype(BF16)
            k_new = h_ref[rows, k_base + g * KW: k_base + (g + 1) * KW]
            v_new = h_ref[rows, v_base + g * KW: v_base + (g + 1) * KW]
            kv_new = jnp.concatenate([jnp.concatenate([k_new, v_new], axis=1), zpad], axis=0)
            kv_new_b = kv_new.astype(BF16)
            sc_refs[g][s_i, :, :nkeys] = jnp.dot(qm, cref[s_i, :KW, :].astype(BF16), preferred_element_type=F32)
            sc_refs[g][s_i, :, nkeys:] = lax.dot_general(qm, kv_new_b[:, :KW], nt, preferred_element_type=F32)
            work.append((g, s_i, rows, kv_new, kv_new_b))

    for g, s_i, rows, kv_new, kv_new_b in work:
        nkeys = BLOCK * DILATIONS[g]
        cref, nref = caches[g], new_caches[g]
        new_cols = pltpu.roll(kv_new.T, LANES - t_new, 1)
        for half in range(2):
            hrows = slice(half * KW, (half + 1) * KW)
            shifted = pltpu.roll(cref[s_i, hrows, :], nkeys - t_new, 1)
            nref[s_i, hrows, :] = shifted
            nref[s_i, hrows, nkeys - LANES:] = jnp.where(lane_k >= LANES - t_new, new_cols[hrows],
                                                         shifted[:, nkeys - LANES:])

    for g, s_i, rows, kv_new, kv_new_b in work:
        dil = DILATIONS[g]
        nkeys = BLOCK * dil
        ncol = nkeys + LANES
        tok = lax.broadcasted_iota(jnp.int32, (nrow, ncol), 0) % TPAD
        col = lax.broadcasted_iota(jnp.int32, (nrow, ncol), 1)
        new_j = col - nkeys
        valid = (((col < nkeys) & (((col - tok) & (dil - 1)) == 0) & (col >= tok))
                 | ((col >= nkeys) & (((tok - new_j) & (dil - 1)) == 0) & (new_j <= tok) & (new_j < t_new)))
        s_all = jnp.where(valid, sc_refs[g][s_i], NEG_INF)
        m = jnp.max(s_all, axis=-1, keepdims=True)
        if g == 0:
            sink = sink_ref[:, 0:1]
            m = jnp.maximum(m, sink)
        p = jnp.exp(s_all - m)
        den = jnp.sum(p, axis=-1, keepdims=True)
        if g == 0:
            den = den + jnp.exp(sink - m)
        pb = p.astype(BF16)
        o = (lax.dot_general(pb[:, :nkeys], caches[g][s_i, KW:, :].astype(BF16), nt, preferred_element_type=F32)
             + jnp.dot(pb[:, nkeys:], kv_new_b[:, KW:], preferred_element_type=F32)) / den
        for c in range(QW // LANES):
            ev = o[(2 * c) * TPAD:(2 * c + 1) * TPAD]
            od = o[(2 * c + 1) * TPAD:(2 * c + 2) * TPAD]
            if c < QW // LANES // 2:
                slab = jnp.where(lo_half, ev, pltpu.roll(od, HEAD_DIM, 1))
            else:
                slab = jnp.where(lo_half, pltpu.roll(ev, HEAD_DIM, 1), od)
            o_refs[g][rows, c * LANES:(c + 1) * LANES] = slab.astype(o_refs[g].dtype)
        if g > 0:
            lse = m + jnp.log(den)
            lse_acc = jnp.zeros((TPAD, LANES), F32)
            for hq in range(Q_HEADS):
                lse_acc = jnp.where(lane8 == hq, lse[hq * TPAD:(hq + 1) * TPAD], lse_acc)
            l_refs[g][rows, :] = lse_acc


def _cache_rows_minor(c):
    nseq, rows = c.shape[1], c.shape[2]
    return jnp.transpose(c[0], (0, 2, 3, 4, 1)).reshape(nseq, 2 * KW, rows)


def _cache_rows_major(c):
    nseq, _, rows = c.shape
    return jnp.transpose(c.reshape(nseq, 2, KV_HEADS, HEAD_DIM, rows), (0, 4, 1, 2, 3))[None]


def _decode_attn(h, caches, sink_rows, t_new):
    cts = [_cache_rows_minor(c) for c in caches]
    nseq = cts[0].shape[0]
    sb = 2
    n = nseq * TPAD
    row = lambda w: pl.BlockSpec((sb * TPAD, w), lambda i: (i, 0))
    cspec = lambda c: pl.BlockSpec((sb,) + c.shape[1:], lambda i: (i, 0, 0))
    outs = pl.pallas_call(
        functools.partial(_decode_attn_kernel, sb=sb, t_new=t_new),
        grid=(nseq // sb,),
        in_specs=[pl.BlockSpec(sink_rows.shape, lambda i: (0, 0)), row(h.shape[1])] + [cspec(c) for c in cts],
        out_specs=[row(QW)] * 4 + [row(LANES)] * 3 + [cspec(c) for c in cts],
        out_shape=([jax.ShapeDtypeStruct((n, QW), BF16)] * 4 + [jax.ShapeDtypeStruct((n, LANES), F32)] * 3
                   + [jax.ShapeDtypeStruct(c.shape, F32) for c in cts]),
        scratch_shapes=[pltpu.VMEM((sb, Q_HEADS * TPAD, c.shape[2] + LANES), F32) for c in cts],
        compiler_params=_cparams(("arbitrary",)),
        name="l0_decode_attn",
    )(sink_rows, h, *cts)
    return outs[:4], outs[4:7], [_cache_rows_major(c) for c in outs[7:]]


def _mlstm_chunk(q, k, v, gates, hh, c_mem, c_norm, m_prev, n_valid):
    L = q.shape[0]
    ig_col = gates[:, hh:hh + 1]
    fg = gates[:, C_HEADS + hh:C_HEADS + hh + 1]
    lf_col = jnp.minimum(fg, 0.0) - jnp.log(1.0 + jnp.exp(-jnp.abs(fg)))
    ri = lax.broadcasted_iota(jnp.int32, (L, L), 0)
    ci = lax.broadcasted_iota(jnp.int32, (L, L), 1)
    tril = ri >= ci
    eye = ri == ci
    lf_row = jnp.sum(jnp.where(eye, lf_col, 0.0), axis=0, keepdims=True)
    ig_row = jnp.sum(jnp.where(eye, ig_col, 0.0), axis=0, keepdims=True)
    b_col = jnp.sum(jnp.where(tril, lf_row, 0.0), axis=1, keepdims=True)
    b_row = jnp.sum(jnp.where(ri <= ci, lf_col, 0.0), axis=0, keepdims=True)
    dmat = jnp.where(tril, b_col - b_row + ig_row, NEG_INF)
    inter = b_col + m_prev
    m = jnp.maximum(inter, jnp.max(dmat, axis=1, keepdims=True))
    pw = jnp.exp(dmat - m)
    qb, kb, vb = q.astype(BF16), k.astype(BF16), v.astype(BF16)
    sc = lax.dot_general(qb, kb, (((1,), (1,)), ((), ())), preferred_element_type=F32) * pw
    carry_w = jnp.exp(inter - m)
    num = (jnp.dot(sc.astype(BF16), vb, preferred_element_type=F32)
           + carry_w * jnp.dot(qb, c_mem.astype(BF16), preferred_element_type=F32))
    den = jnp.sum(sc, axis=1, keepdims=True) + carry_w * jnp.sum(q * c_norm, axis=1, keepdims=True)
    h = num / jnp.maximum(jnp.abs(den), jnp.exp(-m))
    last = n_valid - 1
    m_new = m[last:last + 1, :]
    b_last = b_col[last:last + 1, :]
    ws = jnp.exp(b_last - b_col + ig_col - m_new)
    if n_valid < L:
        ws = jnp.where(lax.broadcasted_iota(jnp.int32, (L, 1), 0) < n_valid, ws, 0.0)
    decay = jnp.exp(b_last + m_prev - m_new)
    wk = ws * k
    c_mem_new = decay * c_mem + lax.dot_general(
        wk.astype(BF16), vb, (((0,), (0,)), ((), ())), preferred_element_type=F32)
    c_norm_new = decay * c_norm + jnp.sum(wk, axis=0, keepdims=True)
    return h, c_mem_new, c_norm_new, m_new


AUG = LANES
PROJ_COLS = 256


def _l1_prompt_kernel(x_ref, w_ref, bias_ref, h_ref, cmem_ref, cnorm_ref, cmax_ref,
                      caug_sc, m_sc, s_sc, lhs_sc, proj_a, proj_b, xb_sc, *, tm, nblk):
    i = pl.program_id(0)
    blk = i - 1

    @pl.when(i == 0)
    def _():
        proj_b[...] = jnp.zeros_like(proj_b)

    @pl.when(blk % nblk == 0)
    def _():
        caug_sc[...] = jnp.zeros_like(caug_sc)
        m_sc[...] = jnp.zeros_like(m_sc)

    ri = lax.broadcasted_iota(jnp.int32, (BLOCK, BLOCK), 0)
    ci = lax.broadcasted_iota(jnp.int32, (BLOCK, BLOCK), 1)
    tril = ri >= ci
    upper = (ri <= ci).astype(F32)
    ones_blk = jnp.ones((BLOCK, AUG), BF16)
    kscale = C_HEAD_DIM ** -0.5
    nt = (((1,), (1,)), ((), ()))
    ncol = w_ref.shape[1]
    slabs = [(c0, min(c0 + PROJ_COLS, ncol)) for c0 in range(0, ncol, PROJ_COLS)]
    n_chunks = tm // BLOCK

    def body(src, dst):
        xb_sc[...] = x_ref[...].astype(BF16)
        pieces = iter(slabs)

        def project_piece():
            c = next(pieces, None)
            if c is not None:
                dst[:, c[0]:c[1]] = jnp.dot(xb_sc[...], w_ref[:, c[0]:c[1]], preferred_element_type=F32)

        g_t = (src[:, 3 * MIX:] + bias_ref[...]).T[0:TPAD]
        lf_t = jnp.minimum(g_t, 0.0) - jnp.log(1.0 + jnp.exp(-jnp.abs(g_t)))

        def score_stage(c):
            rows = slice(c * BLOCK, (c + 1) * BLOCK)
            qs, ks, vs = [], [], []
            for hh in range(C_HEADS):
                q = src[rows, hh * C_HEAD_DIM:(hh + 1) * C_HEAD_DIM]
                k = src[rows, MIX + hh * C_HEAD_DIM: MIX + (hh + 1) * C_HEAD_DIM] * kscale
                v = src[rows, 2 * MIX + hh * C_HEAD_DIM: 2 * MIX + (hh + 1) * C_HEAD_DIM]
                qs.append(q)
                ks.append(k)
                vs.append(v.astype(BF16))
                s_sc[c % 2, hh] = lax.dot_general(q.astype(BF16), k.astype(BF16), nt, preferred_element_type=F32)
            return dict(rows=rows, par=c % 2, qs=qs, ks=ks, vs=vs)

        def gate_stage(t, m_prevs):
            rows, par = t["rows"], t["par"]
            b_t = jnp.dot(lf_t[:, rows], upper, preferred_element_type=F32, precision=lax.Precision.HIGHEST)
            stats = []
            for hh in range(C_HEADS):
                ig_row = g_t[hh:hh + 1, rows]
                lf_row = lf_t[C_HEADS + hh:C_HEADS + hh + 1, rows]
                b_row = b_t[C_HEADS + hh:C_HEADS + hh + 1, :]
                m_prev = m_prevs[hh]
                b_col = jnp.sum(jnp.where(tril, lf_row, 0.0), axis=1, keepdims=True)
                dmat = jnp.where(tril, b_col - (b_row - ig_row), NEG_INF)
                inter = b_col + m_prev
                m = jnp.maximum(inter, jnp.max(dmat, axis=1, keepdims=True))
                sc = s_sc[par, hh] * jnp.exp(dmat - m)
                carry_w = jnp.exp(inter - m)
                lhs_sc[par, hh, :, :BLOCK] = sc.astype(BF16)
                lhs_sc[par, hh, :, BLOCK:] = (carry_w * t["qs"][hh]).astype(BF16)
                m_new = m[BLOCK - 1:BLOCK, :]
                b_last = b_col[BLOCK - 1:BLOCK, :]
                ws_row = jnp.exp(b_last - b_row + ig_row - m_new)
                decay = jnp.exp(b_last + m_prev - m_new)
                stats.append((m, m_new, ws_row, decay))
            t["stats"] = stats

        def memory_stage(t):
            rows, par = t["rows"], t["par"]
            for hh in range(C_HEADS):
                cols = slice(hh * C_HEAD_DIM, (hh + 1) * C_HEAD_DIM)
                m, m_new, ws_row, decay = t["stats"][hh]
                caug = caug_sc[hh]
                v_aug = jnp.concatenate([t["vs"][hh], ones_blk], axis=1)
                rhs = jnp.concatenate([v_aug, caug.astype(BF16)], axis=0)
                res = jnp.dot(lhs_sc[par, hh], rhs, preferred_element_type=F32)
                den = jnp.maximum(jnp.abs(res[:, C_HEAD_DIM:]), jnp.exp(-m))
                h_ref[rows, cols] = res[:, :C_HEAD_DIM] / jnp.concatenate([den, den], axis=1)
                wk_t = (t["ks"][hh].T * ws_row).astype(BF16)
                caug_sc[hh] = decay * caug + jnp.dot(wk_t, v_aug, preferred_element_type=F32)

        m_prevs = [m_sc[hh:hh + 1, 0:1] for hh in range(C_HEADS)]
        ts = [score_stage(0), score_stage(1)]
        project_piece()
        for c in range(n_chunks):
            gate_stage(ts[c], m_prevs)
            m_prevs = [st[1] for st in ts[c]["stats"]]
            project_piece()
            if c + 2 < n_chunks:
                ts.append(score_stage(c + 2))
                project_piece()
            if c >= 1:
                memory_stage(ts[c - 1])
                project_piece()
        memory_stage(ts[n_chunks - 1])
        for _ in slabs:
            project_piece()
        for hh in range(C_HEADS):
            m_sc[hh:hh + 1, :] = jnp.broadcast_to(m_prevs[hh], (1, LANES))

    @pl.when(i % 2 == 0)
    def _():
        body(proj_b, proj_a)

    @pl.when(i % 2 == 1)
    def _():
        body(proj_a, proj_b)

    @pl.when((blk >= 0) & (blk % nblk == nblk - 1))
    def _():
        for hh in range(C_HEADS):
            cmem_ref[0, hh] = caug_sc[hh, :, :C_HEAD_DIM]
            cnorm_ref[0, hh:hh + 1, :] = caug_sc[hh, :, C_HEAD_DIM:].T[0:1, :]
        cmax_ref[0] = m_sc[...]


def _l1_prompt(x, w_qkvg, bias_row):
    bn, s, d = x.shape
    tm = 512
    nblk = s // tm
    ntot = bn * nblk
    ncol = w_qkvg.shape[1]
    state_blk = lambda i: (jnp.maximum(i - 1, 0) // nblk, 0, 0)
    return pl.pallas_call(
        functools.partial(_l1_prompt_kernel, tm=tm, nblk=nblk),
        grid=(ntot + 1,),
        in_specs=[pl.BlockSpec((tm, d), lambda i: (jnp.minimum(i, ntot - 1), 0)),
                  pl.BlockSpec((d, ncol), lambda i: (0, 0)),
                  pl.BlockSpec((1, LANES), lambda i: (0, 0))],
        out_specs=[pl.BlockSpec((tm, MIX), lambda i: (jnp.maximum(i - 1, 0), 0)),
                   pl.BlockSpec((1, C_HEADS, C_HEAD_DIM, C_HEAD_DIM), lambda i: state_blk(i) + (0,)),
                   pl.BlockSpec((1, C_HEADS, C_HEAD_DIM), state_blk),
                   pl.BlockSpec((1, TPAD, LANES), state_blk)],
        out_shape=[jax.ShapeDtypeStruct((bn * s, MIX), F32),
                   jax.ShapeDtypeStruct((bn, C_HEADS, C_HEAD_DIM, C_HEAD_DIM), F32),
                   jax.ShapeDtypeStruct((bn, C_HEADS, C_HEAD_DIM), F32),
                   jax.ShapeDtypeStruct((bn, TPAD, LANES), F32)],
        scratch_shapes=[pltpu.VMEM((C_HEADS, C_HEAD_DIM, C_HEAD_DIM + AUG), F32),
                        pltpu.VMEM((TPAD, LANES), F32),
                        pltpu.VMEM((2, C_HEADS, BLOCK, BLOCK), F32),
                        pltpu.VMEM((2, C_HEADS, BLOCK, BLOCK + C_HEAD_DIM), BF16),
                        pltpu.VMEM((tm, ncol), F32), pltpu.VMEM((tm, ncol), F32),
                        pltpu.VMEM((tm, d), BF16)],
        compiler_params=_cparams(("arbitrary",)),
        name="l1_prompt_mlstm",
    )(x.reshape(bn * s, d), w_qkvg, bias_row)


def _l1_decode_kernel(p_ref, bias_ref, cm_ref, cn_ref, cx_ref, h_ref, cmo_ref, cno_ref, cxo_ref,
                      *, sb, t_new):
    kscale = C_HEAD_DIM ** -0.5
    for s_i in range(sb):
        rows = slice(s_i * TPAD, (s_i + 1) * TPAD)
        gates = p_ref[rows, 3 * MIX:] + bias_ref[...]
        for hh in range(C_HEADS):
            cols = slice(hh * C_HEAD_DIM, (hh + 1) * C_HEAD_DIM)
            q = p_ref[rows, cols]
            k = p_ref[rows, MIX + hh * C_HEAD_DIM: MIX + (hh + 1) * C_HEAD_DIM] * kscale
            v = p_ref[rows, 2 * MIX + hh * C_HEAD_DIM: 2 * MIX + (hh + 1) * C_HEAD_DIM]
            h, c_new, n_new, m_new = _mlstm_chunk(
                q, k, v, gates, hh, cm_ref[s_i, hh], cn_ref[s_i, hh:hh + 1, :],
                cx_ref[s_i, hh:hh + 1, 0:1], t_new)
            h_ref[rows, cols] = h
            cmo_ref[s_i, hh] = c_new
            cno_ref[s_i, hh:hh + 1, :] = n_new
            cxo_ref[s_i, hh:hh + 1, :] = jnp.broadcast_to(m_new, (1, LANES))


def _l1_decode(proj, bias_row, c_mem, c_norm, c_max_pad, t_new):
    nseq = c_mem.shape[0]
    sb = 8
    n = nseq * TPAD
    return pl.pallas_call(
        functools.partial(_l1_decode_kernel, sb=sb, t_new=t_new),
        grid=(nseq // sb,),
        in_specs=[pl.BlockSpec((sb * TPAD, proj.shape[1]), lambda i: (i, 0)),
                  pl.BlockSpec((1, LANES), lambda i: (0, 0)),
                  pl.BlockSpec((sb, C_HEADS, C_HEAD_DIM, C_HEAD_DIM), lambda i: (i, 0, 0, 0)),
                  pl.BlockSpec((sb, C_HEADS, C_HEAD_DIM), lambda i: (i, 0, 0)),
                  pl.BlockSpec((sb, C_HEADS, LANES), lambda i: (i, 0, 0))],
        out_specs=[pl.BlockSpec((sb * TPAD, MIX), lambda i: (i, 0)),
                   pl.BlockSpec((sb, C_HEADS, C_HEAD_DIM, C_HEAD_DIM), lambda i: (i, 0, 0, 0)),
                   pl.BlockSpec((sb, C_HEADS, C_HEAD_DIM), lambda i: (i, 0, 0)),
                   pl.BlockSpec((sb, C_HEADS, LANES), lambda i: (i, 0, 0))],
        out_shape=[jax.ShapeDtypeStruct((n, MIX), F32),
                   jax.ShapeDtypeStruct(c_mem.shape, F32),
                   jax.ShapeDtypeStruct(c_norm.shape, F32),
                   jax.ShapeDtypeStruct(c_max_pad.shape, F32)],
        compiler_params=_cparams(("arbitrary",)),
        name="l1_decode_mlstm",
    )(proj, bias_row, c_mem, c_norm, c_max_pad)


def _l1_out_kernel(x_ref, h_ref, woz_ref, mhg_ref, wo_ref, g_ref, b_ref, y_ref, oz_sc, mix_sc):
    tm = x_ref.shape[0]
    n_sub = tm // SUB_ROWS

    def gate_proj(i):
        r = slice(i * SUB_ROWS, (i + 1) * SUB_ROWS)
        oz_sc[r, :] = jnp.dot(x_ref[r, :].astype(BF16), woz_ref[...], preferred_element_type=F32)

    def mix_stage(i):
        r = slice(i * SUB_ROWS, (i + 1) * SUB_ROWS)
        parts = []
        for hh in range(C_HEADS):
            cols = slice(hh * C_HEAD_DIM, (hh + 1) * C_HEAD_DIM)
            hg = jax.nn.sigmoid(oz_sc[r, cols]) * h_ref[r, cols]
            mu = jnp.mean(hg, axis=-1, keepdims=True)
            c = hg - mu
            var = jnp.mean(c * c, axis=-1, keepdims=True)
            parts.append(c * lax.rsqrt(var + MH_EPS))
        hn = jnp.concatenate(parts, axis=1) * mhg_ref[...]
        z = oz_sc[r, MIX:]
        mix_sc[r, :] = (hn * (z * jax.nn.sigmoid(z))).astype(BF16)

    def out_stage(i):
        r = slice(i * SUB_ROWS, (i + 1) * SUB_ROWS)
        y = jnp.dot(mix_sc[r, :], wo_ref[...], preferred_element_type=F32)
        y_ref[r, :] = _layer_norm_rows(DN_ALPHA * x_ref[r, :] + y, g_ref[...], b_ref[...])

    gate_proj(0)
    for i in range(n_sub):
        if i + 1 < n_sub:
            gate_proj(i + 1)
        mix_stage(i)
        out_stage(i)


def _l1_out(x2, h2, woz, mhg, wo, g, b):
    n, d = x2.shape
    tm = 512
    row = lambda w: pl.BlockSpec((tm, w), lambda i: (i, 0))
    full = lambda a: pl.BlockSpec(a.shape, lambda i: (0, 0))
    return pl.pallas_call(
        _l1_out_kernel,
        grid=(n // tm,),
        in_specs=[row(d), row(MIX), full(woz), full(mhg), full(wo), full(g), full(b)],
        out_specs=row(d),
        out_shape=jax.ShapeDtypeStruct((n, d), F32),
        scratch_shapes=[pltpu.VMEM((tm, 2 * MIX), F32), pltpu.VMEM((tm, MIX), BF16)],
        compiler_params=_cparams(("arbitrary",)),
        name="l1_out",
    )(x2, h2, woz, mhg, wo, g, b)


def _group_cols(g):
    if g == 0:
        return [(0, QW), (QW, QW + KW), (QW + KW, GW)]
    b = g - 1
    qb0 = GW
    kb0 = GW + 3 * QW
    vb0 = kb0 + 3 * KW
    return [(qb0 + b * QW, qb0 + (b + 1) * QW), (kb0 + b * KW, kb0 + (b + 1) * KW),
            (vb0 + b * KW, vb0 + (b + 1) * KW)]


def kernel(x_prompt, x_sample, cache_a_kv, cache_b0_kv, cache_b1_kv, cache_b2_kv, state_c_mem,
           state_c_norm, state_c_max, w_in0, sinks0, w_out0, w_in1, b_gates1, mh_norm1, w_out1,
           ln_g, ln_b):
    bn, s, d = x_prompt.shape
    nseq, t_new, _ = x_sample.shape
    w0 = w_in0[0]
    z0 = 4 * GW
    w_groups = [jnp.concatenate([w0[:, a:b] for a, b in _group_cols(g)], axis=1).astype(BF16)
                for g in range(N_GROUPS)]
    wz0 = w0[:, z0:].astype(BF16)
    wo0 = w_out0[0].astype(BF16)
    sinks = sinks0[0].astype(F32)
    g0, b0 = ln_g[0][None, :], ln_b[0][None, :]
    g1, b1 = ln_g[1][None, :], ln_b[1][None, :]

    outs, lses, states_p = [], [], []
    for g in range(N_GROUPS):
        o, lse, st = _group_attn_prompt(x_prompt, w_groups[g], sinks, DILATIONS[g], g == 0)
        outs.append(o)
        lses.append(lse)
        states_p.append(st)
    xp2 = x_prompt.reshape(bn * s, d)
    y0p = _l0_out(xp2, outs[0], outs[1:], lses[1:], wz0, wo0, g0, b0)

    xs2 = jnp.pad(x_sample, ((0, 0), (0, TPAD - t_new), (0, 0))).reshape(nseq * TPAD, d)
    w_dec = jnp.concatenate(
        [w_groups[g][:, :QW] for g in range(N_GROUPS)] + [w_groups[g][:, QW:QW + KW] for g in range(N_GROUPS)]
        + [w_groups[g][:, QW + KW:] for g in range(N_GROUPS)], axis=1)
    tabs_s = tuple(jnp.tile(t, (nseq, 1)) for t in _rope_tables(PAST_LEN + jnp.arange(TPAD)))
    tn = 512
    hs = _proj(xs2, w_dec, tabs_s, tn, n_rope=(N_GROUPS * (QW + KW)) // tn, n_scaled=(N_GROUPS * QW) // tn)
    sink_rows = jnp.broadcast_to(jnp.repeat(sinks, TPAD)[:, None], (Q_HEADS * TPAD, LANES))
    dec_o, dec_l, states_s = _decode_attn(
        hs, (cache_a_kv, cache_b0_kv, cache_b1_kv, cache_b2_kv), sink_rows, t_new)
    y0s = _l0_out(xs2, dec_o[0], dec_o[1:], dec_l, wz0, wo0, g0, b0)

    w1 = w_in1[0]
    gate_pad = jnp.zeros((d, LANES - 2 * C_HEADS), F32)
    w_qkvg = jnp.concatenate([w1[:, :3 * MIX], w1[:, 5 * MIX:], gate_pad], axis=1).astype(BF16)
    w_oz = w1[:, 3 * MIX:5 * MIX].astype(BF16)
    wo1 = w_out1[0].astype(BF16)
    bias_row = jnp.pad(b_gates1[0].astype(F32), (0, LANES - 2 * C_HEADS))[None, :]
    mhg = mh_norm1[0][None, :]

    hp, cmem_p, cnorm_p, cmax_p = _l1_prompt(y0p.reshape(bn, s, d), w_qkvg, bias_row)
    y1p = _l1_out(y0p, hp, w_oz, mhg, wo1, g1, b1)

    no_tabs = tuple(jnp.zeros((nseq * TPAD, LANES), F32) for _ in range(3))
    ps = _proj(y0s, w_qkvg, no_tabs, 640)
    cmax_pad = jnp.broadcast_to(state_c_max[0][:, :, None], (nseq, C_HEADS, LANES))
    hsd, cmem_s, cnorm_s, cmax_s = _l1_decode(ps, bias_row, state_c_mem[0], state_c_norm[0], cmax_pad, t_new)
    y1s = _l1_out(y0s, hsd, w_oz, mhg, wo1, g1, b1)

    y_prompt = y1p.reshape(bn, s, d)
    y_sample = y1s.reshape(nseq, TPAD, d)[:, :t_new]
    return (y_prompt, y_sample,
            states_p[0][None], states_p[1][None], states_p[2][None], states_p[3][None],
            cmem_p[None], cnorm_p[None], cmax_p[:, :C_HEADS, 0][None],
            states_s[0], states_s[1], states_s[2], states_s[3],
            cmem_s[None], cnorm_s[None], cmax_s[:, :, 0][None])
```

```python
import functools

import jax
import jax.numpy as jnp
from jax import lax
from jax.experimental import pallas as pl
from jax.experimental.pallas import tpu as pltpu

F32 = jnp.float32
BF16 = jnp.bfloat16

D_MODEL = 1024
PAST_LEN = 16384
HEAD_DIM = 64
ROT_DIM = HEAD_DIM // 4
ROPE_THETA = 500000.0
BLOCK = 128
Q_HEADS = 8
KV_HEADS = 2
GROUP = Q_HEADS // KV_HEADS
QW = Q_HEADS * HEAD_DIM
KW = KV_HEADS * HEAD_DIM
GW = QW + 2 * KW
DILATIONS = (1, 1, 4, 16)
N_GROUPS = 4
C_HEADS = 4
C_HEAD_DIM = 256
MIX = 1024
DEPTH = 2
DN_ALPHA = (2 * DEPTH) ** 0.25
LN_EPS = 1e-5
MH_EPS = 1e-6
NEG_INF = -1e30
LOG2E = 1.4426950408889634
LN2 = 0.6931471805599453
LANES = 128
TPAD = 8
VMEM_LIMIT = 56 * 1024 * 1024


def _cparams(sem):
    return pltpu.CompilerParams(dimension_semantics=sem, vmem_limit_bytes=VMEM_LIMIT)


def _layer_norm_rows(v, g, b):
    mu = jnp.mean(v, axis=-1, keepdims=True)
    c = v - mu
    var = jnp.mean(c * c, axis=-1, keepdims=True)
    return c * lax.rsqrt(var + LN_EPS) * g + b


def _rope(slab, cos, sin_lo, sin_hi):
    return (slab * cos + pltpu.roll(slab, LANES - ROT_DIM // 2, 1) * sin_lo
            + pltpu.roll(slab, ROT_DIM // 2, 1) * sin_hi)


def _rope_tables(pos):
    half = ROT_DIM // 2
    inv = ROPE_THETA ** (-jnp.arange(half, dtype=F32) / half)
    ang = pos.astype(F32)[:, None] * inv[None, :]
    cos, sin = jnp.cos(ang), jnp.sin(ang)
    lane = jnp.arange(LANES) % HEAD_DIM
    f = lane % half
    cos_t = jnp.where(lane[None, :] < ROT_DIM, cos[:, f], 1.0)
    sin_lo = jnp.where(lane[None, :] < half, -sin[:, f], 0.0)
    sin_hi = jnp.where((lane[None, :] >= half) & (lane[None, :] < ROT_DIM), sin[:, f], 0.0)
    return cos_t.astype(F32), sin_lo.astype(F32), sin_hi.astype(F32)


STEP_TOKENS = 2048
X_SLABS = D_MODEL // LANES
UNITS_PER_PROJ = 4
ONES_ROWS = 16


def _rope_t(blk, cos, sin):
    half = ROT_DIM // 2
    x1, x2 = blk[:half], blk[half:ROT_DIM]
    return jnp.concatenate([x1 * cos - x2 * sin, x2 * cos + x1 * sin, blk[ROT_DIM:]], axis=0)


def _group_attn_kernel(sink_ref, *refs, has_sink, dil, tstep, nstep):
    x_refs = refs[:X_SLABS]
    (wt_ref, cos_ref, sin_ref, o_ref, lse_ref, st_ref,
     kprev_sc, vprev_sc, kv_nat_sc, o_sc, st_sc, pt_sc, bias_sc, xq_sc, ht_sc) = refs[X_SLABS:]
    j = pl.program_id(1)
    n_units = tstep // BLOCK
    is_last = j == nstep - 1

    @pl.when(j == 0)
    def _():
        kprev_sc[...] = jnp.zeros_like(kprev_sc)
        vprev_sc[...] = jnp.zeros_like(vprev_sc)

    scale = HEAD_DIM ** -0.5 * LOG2E
    kidx = lax.broadcasted_iota(jnp.int32, (2 * BLOCK, 2 * BLOCK), 0)
    qidx = lax.broadcasted_iota(jnp.int32, (2 * BLOCK, 2 * BLOCK), 1) % BLOCK
    dist = qidx + BLOCK - kidx
    band = (dist >= 0) & (dist <= BLOCK)
    bias_sc[0] = jnp.where(band, 0.0, NEG_INF)
    bias_sc[1] = jnp.where(band & (kidx >= jnp.where(j > 0, 0, BLOCK)), 0.0, NEG_INF)
    lane2 = lax.broadcasted_iota(jnp.int32, (1, 2 * BLOCK), 1)
    ones_rows = jnp.ones((ONES_ROWS, 2 * BLOCK), F32)
    zeros_half = jnp.zeros((HEAD_DIM, 2 * BLOCK), F32)
    nt = (((1,), (1,)), ((), ()))

    span = BLOCK * dil

    def unit_rows(u):
        start = (u // dil) * span + u % dil
        return pl.ds(start, BLOCK) if dil == 1 else pl.ds(start, BLOCK, stride=dil)

    def state_rows(u):
        return pl.ds(u % dil, BLOCK) if dil == 1 else pl.ds(u % dil, BLOCK, stride=dil)

    n_quads = n_units // UNITS_PER_PROJ
    k_slab = D_MODEL // UNITS_PER_PROJ

    def quad_units(qd):
        return [UNITS_PER_PROJ * qd + w for w in range(UNITS_PER_PROJ)]

    def load_x(qd):
        xq_sc[qd % 2] = jnp.concatenate(
            [jnp.concatenate([x_refs[c][0, unit_rows(u), :] for c in range(X_SLABS)], axis=1)
             for u in quad_units(qd)], axis=0).astype(BF16)

    def project_piece(qd, kp):
        ks = slice(kp * k_slab, (kp + 1) * k_slab)
        part = lax.dot_general(wt_ref[:, ks], xq_sc[qd % 2, :, ks], nt, preferred_element_type=F32)
        if kp == 0:
            ht_sc[qd % 2] = part
        else:
            ht_sc[qd % 2] += part

    load_x(0)
    for kp in range(UNITS_PER_PROJ):
        project_piece(0, kp)
    carried = {}
    for qd in range(n_quads):
        units = quad_units(qd)
        ht = ht_sc[qd % 2]
        cols = slice(qd * UNITS_PER_PROJ * BLOCK, (qd + 1) * UNITS_PER_PROJ * BLOCK)
        cos, sin = cos_ref[:, cols], sin_ref[:, cols]
        qt = [_rope_t(ht[h * HEAD_DIM:(h + 1) * HEAD_DIM], cos, sin) * scale for h in range(Q_HEADS)]
        kt = jnp.concatenate([_rope_t(ht[QW + h * HEAD_DIM:QW + (h + 1) * HEAD_DIM], cos, sin)
                              for h in range(KV_HEADS)], axis=0)
        vt = ht[QW + KW:GW]
        k_rm = kt.T

        if dil > 1 and units[0] >= n_units - dil:
            @pl.when(is_last)
            def _():
                v_rm = vt.T
                for w, u in enumerate(units):
                    kv_nat_sc[0, state_rows(u), :] = k_rm[w * BLOCK:(w + 1) * BLOCK]
                    kv_nat_sc[1, state_rows(u), :] = v_rm[w * BLOCK:(w + 1) * BLOCK]

        def scores(w, u, carried):
            lanes = slice(w * BLOCK, (w + 1) * BLOCK)
            k_cur = k_rm[w * BLOCK:(w + 1) * BLOCK].astype(BF16)
            vt_cur = vt[:, lanes].astype(BF16)
            if u >= dil:
                k_prev, vt_prev = carried[u - dil]
                mask = 0
            else:
                k_prev, vt_prev = kprev_sc[u], vprev_sc[u]
                mask = 1
            k_cat = jnp.concatenate([k_prev, k_cur], axis=0)
            wq = []
            for c in range(QW // LANES):
                top = jnp.concatenate([qt[2 * c][:, lanes], qt[2 * c + 1][:, lanes]], axis=1)
                wq.append(jnp.concatenate([top, zeros_half] if c < GROUP // 2 else [zeros_half, top], axis=0))
            wq = jnp.concatenate(wq, axis=1).astype(BF16)
            st_sc[u % 2] = jnp.dot(k_cat, wq, preferred_element_type=F32)
            return dict(w=w, u=u, lanes=lanes, k_cur=k_cur, vt_cur=vt_cur, vt_prev=vt_prev, mask=mask)

        def softmax(t):
            par = t["u"] % 2
            ms = []
            for c in range(QW // LANES):
                pc = slice(c * 2 * BLOCK, (c + 1) * 2 * BLOCK)
                st = st_sc[par, :, pc] + bias_sc[t["mask"]]
                m = jnp.max(st, axis=0, keepdims=True)
                if has_sink:
                    sink = jnp.where(lane2 < BLOCK, sink_ref[2 * c], sink_ref[2 * c + 1]) * LOG2E
                    m = jnp.maximum(m, sink)
                ms.append(m)
                pt_sc[par, :, pc] = jnp.exp2(st - m).astype(BF16)
            t["ms"] = ms

        def values(t):
            u, par, lanes = t["u"], t["u"] % 2, t["lanes"]
            lse_rows = []
            for hk in range(KV_HEADS):
                hd = slice(hk * HEAD_DIM, (hk + 1) * HEAD_DIM)
                vt_aug = jnp.concatenate(
                    [jnp.concatenate([t["vt_prev"][hd], t["vt_cur"][hd]], axis=1), ones_rows.astype(BF16)], axis=0)
                acc2 = jnp.dot(vt_aug, pt_sc[par, :, hk * 4 * BLOCK:(hk + 1) * 4 * BLOCK],
                               preferred_element_type=F32)
                for i in range(GROUP // 2):
                    c = hk * (GROUP // 2) + i
                    acc = acc2[:, i * 2 * BLOCK:(i + 1) * 2 * BLOCK]
                    m = t["ms"][c]
                    den = acc[HEAD_DIM:HEAD_DIM + 1]
                    if has_sink:
                        sink = jnp.where(lane2 < BLOCK, sink_ref[2 * c], sink_ref[2 * c + 1]) * LOG2E
                        den = den + jnp.exp2(sink - m)
                    ot = acc[:HEAD_DIM] / den
                    lse = m * LN2 + jnp.log(den)
                    lse_rows += [lse[:, :BLOCK], lse[:, BLOCK:]]
                    slab_t = jnp.concatenate([ot[:, :BLOCK], ot[:, BLOCK:]], axis=0)
                    o_sc[c, unit_rows(u), :] = slab_t.T
            lse_t = jnp.concatenate(lse_rows + [jnp.zeros((LANES - Q_HEADS, BLOCK), F32)], axis=0)
            lse_ref[0, unit_rows(u), :] = lse_t.T
            if u >= n_units - dil:
                kprev_sc[u % dil] = t["k_cur"]
                vprev_sc[u % dil] = t["vt_cur"]
                if dil == 1:
                    @pl.when(is_last)
                    def _():
                        st_ref[0, :KW, :] = kt[:, lanes]
                        st_ref[0, KW:, :] = vt[:, lanes]

        prefetch = qd + 1 < n_quads
        pending = None
        for w, u in enumerate(units):
            t = scores(w, u, carried)
            carried[u] = (t["k_cur"], t["vt_cur"])
            if prefetch and w == 0:
                load_x(qd + 1)
            if pending is not None:
                softmax(pending)
            if prefetch:
                project_piece(qd + 1, w)
            if pending is not None:
                values(pending)
            pending = t
        softmax(pending)
        values(pending)

    for c in range(QW // LANES):
        o_ref[0, :, c * LANES:(c + 1) * LANES] = o_sc[c].astype(o_ref.dtype)

    if dil > 1:
        @pl.when(is_last)
        def _():
            st_ref[0, :KW, :] = kv_nat_sc[0].T
            st_ref[0, KW:, :] = kv_nat_sc[1].T


def _group_attn_prompt(x, w_g, sinks, dil, has_sink):
    bn, s, d = x.shape
    span = BLOCK * dil
    tstep = max(STEP_TOKENS, span)
    nstep = s // tstep
    pos = (jnp.arange(s // span)[:, None, None] * span + jnp.arange(dil)[None, :, None]
           + jnp.arange(BLOCK)[None, None, :] * dil).reshape(s)
    half = ROT_DIM // 2
    inv = ROPE_THETA ** (-jnp.arange(half, dtype=F32) / half)
    ang = pos.astype(F32)[:, None] * inv[None, :]
    cos_t, sin_t = jnp.cos(ang).T, jnp.sin(ang).T
    tab_spec = pl.BlockSpec((half, tstep), lambda b, j: (0, j))
    o, lse, st = pl.pallas_call(
        functools.partial(_group_attn_kernel, has_sink=has_sink, dil=dil, tstep=tstep, nstep=nstep),
        grid=(bn, nstep),
        in_specs=([pl.BlockSpec(memory_space=pltpu.SMEM)]
                  + [pl.BlockSpec((1, tstep, LANES), functools.partial(lambda c, b, j: (b, j, c), c))
                     for c in range(X_SLABS)]
                  + [pl.BlockSpec((GW, d), lambda b, j: (0, 0)), tab_spec, tab_spec]),
        out_specs=[
            pl.BlockSpec((1, tstep, QW), lambda b, j: (b, j, 0)),
            pl.BlockSpec((1, tstep, LANES), lambda b, j: (b, j, 0)),
            pl.BlockSpec((1, 2 * KW, span), lambda b, j: (b, 0, 0)),
        ],
        out_shape=[
            jax.ShapeDtypeStruct((bn, s, QW), BF16),
            jax.ShapeDtypeStruct((bn, s, LANES), F32),
            jax.ShapeDtypeStruct((bn, 2 * KW, span), F32),
        ],
        scratch_shapes=[pltpu.VMEM((dil, BLOCK, KW), BF16), pltpu.VMEM((dil, KW, BLOCK), BF16),
                        pltpu.VMEM((2, span, LANES), F32), pltpu.VMEM((QW // LANES, tstep, LANES), F32),
                        pltpu.VMEM((2, 2 * BLOCK, Q_HEADS * BLOCK), F32),
                        pltpu.VMEM((2, 2 * BLOCK, Q_HEADS * BLOCK), BF16),
                        pltpu.VMEM((2, 2 * BLOCK, 2 * BLOCK), F32),
                        pltpu.VMEM((2, UNITS_PER_PROJ * BLOCK, d), BF16),
                        pltpu.VMEM((2, GW, UNITS_PER_PROJ * BLOCK), F32)],
        compiler_params=_cparams(("arbitrary", "arbitrary")),
        name=f"l0_group_attn_d{dil}_{'sink' if has_sink else 'nosink'}",
    )(sinks, *([x] * X_SLABS), w_g.T, cos_t, sin_t)
    st = jnp.transpose(st.reshape(bn, 2, KV_HEADS, HEAD_DIM, span), (0, 4, 1, 2, 3))
    return o.reshape(bn * s, QW), lse.reshape(bn * s, LANES), st


SUB_ROWS = 256


def _l0_out_kernel(x_ref, oa_ref, o0_ref, o1_ref, o2_ref, l0_ref, l1_ref, l2_ref,
                   wz_ref, wo_ref, g_ref, b_ref, y_ref, z_sc, mix_sc):
    tm = x_ref.shape[0]
    n_sub = tm // SUB_ROWS

    def gate_proj(i):
        r = slice(i * SUB_ROWS, (i + 1) * SUB_ROWS)
        z_sc[r, :] = jnp.dot(x_ref[r, :].astype(BF16), wz_ref[...], preferred_element_type=F32)

    def mix_stage(i):
        r = slice(i * SUB_ROWS, (i + 1) * SUB_ROWS)
        l0, l1, l2 = l0_ref[r, :], l1_ref[r, :], l2_ref[r, :]
        m = jnp.maximum(jnp.maximum(l0, l1), l2)
        e0, e1, e2 = jnp.exp(l0 - m), jnp.exp(l1 - m), jnp.exp(l2 - m)
        inv = 1.0 / (e0 + e1 + e2)
        w0, w1, w2 = e0 * inv, e1 * inv, e2 * inv
        parts = []
        for hq in range(Q_HEADS):
            c = slice(hq * HEAD_DIM, (hq + 1) * HEAD_DIM)
            bshape = (SUB_ROWS, HEAD_DIM)
            parts.append(jnp.broadcast_to(w0[:, hq:hq + 1], bshape) * o0_ref[r, c].astype(F32)
                         + jnp.broadcast_to(w1[:, hq:hq + 1], bshape) * o1_ref[r, c].astype(F32)
                         + jnp.broadcast_to(w2[:, hq:hq + 1], bshape) * o2_ref[r, c].astype(F32))
        mix = jnp.concatenate([oa_ref[r, :].astype(F32)] + parts, axis=1)
        z = z_sc[r, :]
        mix_sc[r, :] = (mix * (z * jax.nn.sigmoid(z))).astype(BF16)

    def out_stage(i):
        r = slice(i * SUB_ROWS, (i + 1) * SUB_ROWS)
        y = jnp.dot(mix_sc[r, :], wo_ref[...], preferred_element_type=F32)
        y_ref[r, :] = _layer_norm_rows(DN_ALPHA * x_ref[r, :] + y, g_ref[...], b_ref[...])

    gate_proj(0)
    for i in range(n_sub):
        if i + 1 < n_sub:
            gate_proj(i + 1)
        mix_stage(i)
        out_stage(i)


def _l0_out(x2, oa, obs, lses, wz, wo, g, b):
    n, d = x2.shape
    tm = 512
    row = lambda w: pl.BlockSpec((tm, w), lambda i: (i, 0))
    full = lambda a: pl.BlockSpec(a.shape, lambda i: (0, 0))
    return pl.pallas_call(
        _l0_out_kernel,
        grid=(n // tm,),
        in_specs=[row(d), row(QW), row(QW), row(QW), row(QW), row(LANES), row(LANES), row(LANES),
                  full(wz), full(wo), full(g), full(b)],
        out_specs=row(d),
        out_shape=jax.ShapeDtypeStruct((n, d), F32),
        scratch_shapes=[pltpu.VMEM((tm, MIX), F32), pltpu.VMEM((tm, MIX), BF16)],
        compiler_params=_cparams(("arbitrary",)),
        name="l0_out",
    )(x2, oa, *obs, *lses, wz, wo, g, b)


def _proj_kernel(x_ref, w_ref, cos_ref, slo_ref, shi_ref, h_ref, *, n_rope, n_scaled, tn):
    j = pl.program_id(0)
    h = jnp.dot(x_ref[...].astype(BF16), w_ref[...], preferred_element_type=F32)
    if n_rope:
        cos, slo, shi = cos_ref[...], slo_ref[...], shi_ref[...]
        roped = jnp.concatenate(
            [_rope(h[:, c * LANES:(c + 1) * LANES], cos, slo, shi) for c in range(tn // LANES)], axis=1)
        h = jnp.where(j < n_rope, roped, h)
        h = h * jnp.where(j < n_scaled, HEAD_DIM ** -0.5, 1.0)
    h_ref[...] = h


def _proj(x2, w, tabs, tn, n_rope=0, n_scaled=0):
    n, d = x2.shape
    ncol = w.shape[1]
    tab_spec = pl.BlockSpec((n, LANES), lambda j: (0, 0))
    return pl.pallas_call(
        functools.partial(_proj_kernel, n_rope=n_rope, n_scaled=n_scaled, tn=tn),
        grid=(ncol // tn,),
        in_specs=[pl.BlockSpec((n, d), lambda j: (0, 0)),
                  pl.BlockSpec((d, tn), lambda j: (0, j)),
                  tab_spec, tab_spec, tab_spec],
        out_specs=pl.BlockSpec((n, tn), lambda j: (0, j)),
        out_shape=jax.ShapeDtypeStruct((n, ncol), F32),
        compiler_params=_cparams(("arbitrary",)),
        name=f"decode_proj_{ncol}",
    )(x2, w, *tabs)


def _dec_prepare(h_ref, rows, g):
    lo_half = lax.broadcasted_iota(jnp.int32, (TPAD, LANES), 1) < HEAD_DIM
    k_base = N_GROUPS * QW
    v_base = k_base + N_GROUPS * KW
    blocks = []
    for c in range(QW // LANES):
        slab = h_ref[rows, g * QW + c * LANES: g * QW + (c + 1) * LANES]
        swapped = pltpu.roll(slab, HEAD_DIM, 1)
        if c < QW // LANES // 2:
            blocks += [jnp.where(lo_half, slab, 0.0), jnp.where(lo_half, swapped, 0.0)]
        else:
            blocks += [jnp.where(lo_half, 0.0, swapped), jnp.where(lo_half, 0.0, slab)]
    qm = jnp.concatenate(blocks, axis=0).astype(BF16)
    k_new = h_ref[rows, k_base + g * KW: k_base + (g + 1) * KW]
    v_new = h_ref[rows, v_base + g * KW: v_base + (g + 1) * KW]
    zpad = jnp.zeros((LANES - TPAD, 2 * KW), F32)
    kv_new = jnp.concatenate([jnp.concatenate([k_new, v_new], axis=1), zpad], axis=0)
    return qm, kv_new


def _dec_scores(qm, kv_new, cache_ref, sc_ref, s_i):
    nkeys = cache_ref.shape[2]
    nt = (((1,), (1,)), ((), ()))
    sc_ref[s_i, :, :nkeys] = jnp.dot(qm, cache_ref[s_i, :KW, :].astype(BF16), preferred_element_type=F32)
    sc_ref[s_i, :, nkeys:] = lax.dot_general(qm, kv_new[:, :KW].astype(BF16), nt, preferred_element_type=F32)


def _dec_shift(kv_new, cache_ref, new_ref, s_i, t_new):
    nkeys = cache_ref.shape[2]
    lane_k = lax.broadcasted_iota(jnp.int32, (KW, LANES), 1)
    new_cols = pltpu.roll(kv_new.T, LANES - t_new, 1)
    for half in range(2):
        hrows = slice(half * KW, (half + 1) * KW)
        shifted = pltpu.roll(cache_ref[s_i, hrows, :], nkeys - t_new, 1)
        new_ref[s_i, hrows, :] = shifted
        new_ref[s_i, hrows, nkeys - LANES:] = jnp.where(lane_k >= LANES - t_new, new_cols[hrows],
                                                        shifted[:, nkeys - LANES:])


def _dec_values(kv_new, cache_ref, sc_ref, s_i, rows, dil, t_new, sink_ref, o_ref, l_ref):
    nrow = Q_HEADS * TPAD
    nkeys = cache_ref.shape[2]
    ncol = nkeys + LANES
    nt = (((1,), (1,)), ((), ()))
    lane8 = lax.broadcasted_iota(jnp.int32, (TPAD, LANES), 1)
    lo_half = lane8 < HEAD_DIM
    tok = lax.broadcasted_iota(jnp.int32, (nrow, ncol), 0) % TPAD
    col = lax.broadcasted_iota(jnp.int32, (nrow, ncol), 1)
    new_j = col - nkeys
    valid = (((col < nkeys) & (((col - tok) & (dil - 1)) == 0) & (col >= tok))
             | ((col >= nkeys) & (((tok - new_j) & (dil - 1)) == 0) & (new_j <= tok) & (new_j < t_new)))
    s_all = jnp.where(valid, sc_ref[s_i], NEG_INF)
    m = jnp.max(s_all, axis=-1, keepdims=True)
    if sink_ref is not None:
        sink = sink_ref[:, 0:1]
        m = jnp.maximum(m, sink)
    p = jnp.exp(s_all - m)
    den = jnp.sum(p, axis=-1, keepdims=True)
    if sink_ref is not None:
        den = den + jnp.exp(sink - m)
    pb = p.astype(BF16)
    o = (lax.dot_general(pb[:, :nkeys], cache_ref[s_i, KW:, :].astype(BF16), nt, preferred_element_type=F32)
         + jnp.dot(pb[:, nkeys:], kv_new[:, KW:].astype(BF16), preferred_element_type=F32)) / den
    for c in range(QW // LANES):
        ev = o[(2 * c) * TPAD:(2 * c + 1) * TPAD]
        od = o[(2 * c + 1) * TPAD:(2 * c + 2) * TPAD]
        if c < QW // LANES // 2:
            slab = jnp.where(lo_half, ev, pltpu.roll(od, HEAD_DIM, 1))
        else:
            slab = jnp.where(lo_half, pltpu.roll(ev, HEAD_DIM, 1), od)
        o_ref[rows, c * LANES:(c + 1) * LANES] = slab.astype(o_ref.dtype)
    if l_ref is not None:
        lse = m + jnp.log(den)
        lse_acc = jnp.zeros((TPAD, LANES), F32)
        for hq in range(Q_HEADS):
            lse_acc = jnp.where(lane8 == hq, lse[hq * TPAD:(hq + 1) * TPAD], lse_acc)
        l_ref[rows, :] = lse_acc


N_DEC_GROUPS = 3


def _decode_attn_kernel(sink_ref, h_ref, *refs, sb, t_new):
    n = N_DEC_GROUPS
    caches, o_refs = refs[:n], refs[n:2 * n]
    l_refs = (None,) + refs[2 * n:3 * n - 1]
    new_caches, sc_refs = refs[3 * n - 1:4 * n - 1], refs[4 * n - 1:]
    work = []
    for g in range(n):
        for s_i in range(sb):
            rows = slice(s_i * TPAD, (s_i + 1) * TPAD)
            qm, kv_new = _dec_prepare(h_ref, rows, g)
            _dec_scores(qm, kv_new, caches[g], sc_refs[g], s_i)
            work.append((g, s_i, rows, kv_new))
    for g, s_i, rows, kv_new in work:
        _dec_shift(kv_new, caches[g], new_caches[g], s_i, t_new)
    for g, s_i, rows, kv_new in work:
        _dec_values(kv_new, caches[g], sc_refs[g], s_i, rows, DILATIONS[g], t_new,
                    sink_ref if g == 0 else None, o_refs[g], l_refs[g])


def _cache_rows_minor(c):
    nseq, rows = c.shape[1], c.shape[2]
    return jnp.transpose(c[0], (0, 2, 3, 4, 1)).reshape(nseq, 2 * KW, rows)


def _cache_rows_major(c):
    nseq, _, rows = c.shape
    return jnp.transpose(c.reshape(nseq, 2, KV_HEADS, HEAD_DIM, rows), (0, 4, 1, 2, 3))[None]


def _decode_attn(h, caches, sink_rows, t_new):
    cts = [_cache_rows_minor(c) for c in caches]
    nseq = cts[0].shape[0]
    sb = 2
    n = nseq * TPAD
    ng = len(cts)
    row = lambda w: pl.BlockSpec((sb * TPAD, w), lambda i: (i, 0))
    cspec = lambda c: pl.BlockSpec((sb,) + c.shape[1:], lambda i: (i, 0, 0))
    outs = pl.pallas_call(
        functools.partial(_decode_attn_kernel, sb=sb, t_new=t_new),
        grid=(nseq // sb,),
        in_specs=[pl.BlockSpec(sink_rows.shape, lambda i: (0, 0)), row(h.shape[1])] + [cspec(c) for c in cts],
        out_specs=[row(QW)] * ng + [row(LANES)] * (ng - 1) + [cspec(c) for c in cts],
        out_shape=([jax.ShapeDtypeStruct((n, QW), BF16)] * ng + [jax.ShapeDtypeStruct((n, LANES), F32)] * (ng - 1)
                   + [jax.ShapeDtypeStruct(c.shape, F32) for c in cts]),
        scratch_shapes=[pltpu.VMEM((sb, Q_HEADS * TPAD, c.shape[2] + LANES), F32) for c in cts],
        compiler_params=_cparams(("arbitrary",)),
        name="l0_decode_attn",
    )(sink_rows, h, *cts)
    return outs[:ng], outs[ng:2 * ng - 1], [_cache_rows_major(c) for c in outs[2 * ng - 1:]]


def _mlstm_chunk(q, k, v, gates, hh, c_mem, c_norm, m_prev, n_valid):
    L = q.shape[0]
    ig_col = gates[:, hh:hh + 1]
    fg = gates[:, C_HEADS + hh:C_HEADS + hh + 1]
    lf_col = jnp.minimum(fg, 0.0) - jnp.log(1.0 + jnp.exp(-jnp.abs(fg)))
    ri = lax.broadcasted_iota(jnp.int32, (L, L), 0)
    ci = lax.broadcasted_iota(jnp.int32, (L, L), 1)
    tril = ri >= ci
    eye = ri == ci
    lf_row = jnp.sum(jnp.where(eye, lf_col, 0.0), axis=0, keepdims=True)
    ig_row = jnp.sum(jnp.where(eye, ig_col, 0.0), axis=0, keepdims=True)
    b_col = jnp.sum(jnp.where(tril, lf_row, 0.0), axis=1, keepdims=True)
    b_row = jnp.sum(jnp.where(ri <= ci, lf_col, 0.0), axis=0, keepdims=True)
    dmat = jnp.where(tril, b_col - b_row + ig_row, NEG_INF)
    inter = b_col + m_prev
    m = jnp.maximum(inter, jnp.max(dmat, axis=1, keepdims=True))
    pw = jnp.exp(dmat - m)
    qb, kb, vb = q.astype(BF16), k.astype(BF16), v.astype(BF16)
    sc = lax.dot_general(qb, kb, (((1,), (1,)), ((), ())), preferred_element_type=F32) * pw
    carry_w = jnp.exp(inter - m)
    num = (jnp.dot(sc.astype(BF16), vb, preferred_element_type=F32)
           + carry_w * jnp.dot(qb, c_mem.astype(BF16), preferred_element_type=F32))
    den = jnp.sum(sc, axis=1, keepdims=True) + carry_w * jnp.sum(q * c_norm, axis=1, keepdims=True)
    h = num / jnp.maximum(jnp.abs(den), jnp.exp(-m))
    last = n_valid - 1
    m_new = m[last:last + 1, :]
    b_last = b_col[last:last + 1, :]
    ws = jnp.exp(b_last - b_col + ig_col - m_new)
    if n_valid < L:
        ws = jnp.where(lax.broadcasted_iota(jnp.int32, (L, 1), 0) < n_valid, ws, 0.0)
    decay = jnp.exp(b_last + m_prev - m_new)
    wk = ws * k
    c_mem_new = decay * c_mem + lax.dot_general(
        wk.astype(BF16), vb, (((0,), (0,)), ((), ())), preferred_element_type=F32)
    c_norm_new = decay * c_norm + jnp.sum(wk, axis=0, keepdims=True)
    return h, c_mem_new, c_norm_new, m_new


AUG = LANES
PROJ_COLS = 256


def _l1_prompt_kernel(x_ref, w_ref, bias_ref, hs_ref, c2_ref, h_ref, cmem_ref, cnorm_ref, cmax_ref,
                      o2_ref, l2_ref, n2_ref,
                      caug_sc, m_sc, s_sc, lhs_sc, proj_a, proj_b, xb_sc, s2_sc, *, tm, nblk, dec_sb, t_new):
    i = pl.program_id(0)
    blk = i - 1

    @pl.when(i == 0)
    def _():
        proj_b[...] = jnp.zeros_like(proj_b)

    @pl.when(blk % nblk == 0)
    def _():
        caug_sc[...] = jnp.zeros_like(caug_sc)
        m_sc[...] = jnp.zeros_like(m_sc)

    ri = lax.broadcasted_iota(jnp.int32, (BLOCK, BLOCK), 0)
    ci = lax.broadcasted_iota(jnp.int32, (BLOCK, BLOCK), 1)
    tril = ri >= ci
    upper = (ri <= ci).astype(F32)
    ones_blk = jnp.ones((BLOCK, AUG), BF16)
    kscale = C_HEAD_DIM ** -0.5
    nt = (((1,), (1,)), ((), ()))
    ncol = w_ref.shape[1]
    slabs = [(c0, min(c0 + PROJ_COLS, ncol)) for c0 in range(0, ncol, PROJ_COLS)]
    n_chunks = tm // BLOCK

    def body(src, dst):
        xb_sc[...] = x_ref[...].astype(BF16)
        pieces = iter(slabs)

        def project_piece():
            c = next(pieces, None)
            if c is not None:
                dst[:, c[0]:c[1]] = jnp.dot(xb_sc[...], w_ref[:, c[0]:c[1]], preferred_element_type=F32)

        g_t = (src[:, 3 * MIX:] + bias_ref[...]).T[0:TPAD]
        lf_t = jnp.minimum(g_t, 0.0) - jnp.log(1.0 + jnp.exp(-jnp.abs(g_t)))

        def score_stage(c):
            rows = slice(c * BLOCK, (c + 1) * BLOCK)
            qs, ks, vs = [], [], []
            for hh in range(C_HEADS):
                q = src[rows, hh * C_HEAD_DIM:(hh + 1) * C_HEAD_DIM]
                k = src[rows, MIX + hh * C_HEAD_DIM: MIX + (hh + 1) * C_HEAD_DIM] * kscale
                v = src[rows, 2 * MIX + hh * C_HEAD_DIM: 2 * MIX + (hh + 1) * C_HEAD_DIM]
                qs.append(q)
                ks.append(k)
                vs.append(v.astype(BF16))
                s_sc[c % 2, hh] = lax.dot_general(q.astype(BF16), k.astype(BF16), nt, preferred_element_type=F32)
            return dict(rows=rows, par=c % 2, qs=qs, ks=ks, vs=vs)

        def gate_stage(t, m_prevs):
            rows, par = t["rows"], t["par"]
            b_t = jnp.dot(lf_t[:, rows], upper, preferred_element_type=F32, precision=lax.Precision.HIGHEST)
            stats = []
            for hh in range(C_HEADS):
                ig_row = g_t[hh:hh + 1, rows]
                lf_row = lf_t[C_HEADS + hh:C_HEADS + hh + 1, rows]
                b_row = b_t[C_HEADS + hh:C_HEADS + hh + 1, :]
                m_prev = m_prevs[hh]
                b_col = jnp.sum(jnp.where(tril, lf_row, 0.0), axis=1, keepdims=True)
                dmat = jnp.where(tril, b_col - (b_row - ig_row), NEG_INF)
                inter = b_col + m_prev
                m = jnp.maximum(inter, jnp.max(dmat, axis=1, keepdims=True))
                sc = s_sc[par, hh] * jnp.exp(dmat - m)
                carry_w = jnp.exp(inter - m)
                lhs_sc[par, hh, :, :BLOCK] = sc.astype(BF16)
                lhs_sc[par, hh, :, BLOCK:] = (carry_w * t["qs"][hh]).astype(BF16)
                m_new = m[BLOCK - 1:BLOCK, :]
                b_last = b_col[BLOCK - 1:BLOCK, :]
                ws_row = jnp.exp(b_last - b_row + ig_row - m_new)
                decay = jnp.exp(b_last + m_prev - m_new)
                stats.append((m, m_new, ws_row, decay))
            t["stats"] = stats

        def memory_stage(t):
            rows, par = t["rows"], t["par"]
            for hh in range(C_HEADS):
                cols = slice(hh * C_HEAD_DIM, (hh + 1) * C_HEAD_DIM)
                m, m_new, ws_row, decay = t["stats"][hh]
                caug = caug_sc[hh]
                v_aug = jnp.concatenate([t["vs"][hh], ones_blk], axis=1)
                rhs = jnp.concatenate([v_aug, caug.astype(BF16)], axis=0)
                res = jnp.dot(lhs_sc[par, hh], rhs, preferred_element_type=F32)
                den = jnp.maximum(jnp.abs(res[:, C_HEAD_DIM:]), jnp.exp(-m))
                h_ref[rows, cols] = res[:, :C_HEAD_DIM] / jnp.concatenate([den, den], axis=1)
                wk_t = (t["ks"][hh].T * ws_row).astype(BF16)
                caug_sc[hh] = decay * caug + jnp.dot(wk_t, v_aug, preferred_element_type=F32)

        m_prevs = [m_sc[hh:hh + 1, 0:1] for hh in range(C_HEADS)]
        ts = [score_stage(0), score_stage(1)]
        g2 = N_GROUPS - 1
        dec = []
        for s_i in range(dec_sb):
            rows = slice(s_i * TPAD, (s_i + 1) * TPAD)
            qm, kv_new = _dec_prepare(hs_ref, rows, g2)
            _dec_scores(qm, kv_new, c2_ref, s2_sc, s_i)
            dec.append((s_i, rows, kv_new))
        shifts = iter(dec)
        project_piece()
        for c in range(n_chunks):
            gate_stage(ts[c], m_prevs)
            m_prevs = [st[1] for st in ts[c]["stats"]]
            project_piece()
            if c + 2 < n_chunks:
                ts.append(score_stage(c + 2))
                project_piece()
            d = next(shifts, None)
            if d is not None:
                _dec_shift(d[2], c2_ref, n2_ref, d[0], t_new)
                project_piece()
            if c >= 1:
                memory_stage(ts[c - 1])
                project_piece()
        memory_stage(ts[n_chunks - 1])
        project_piece()
        for d in shifts:
            _dec_shift(d[2], c2_ref, n2_ref, d[0], t_new)
            project_piece()
        for s_i, rows, kv_new in dec:
            _dec_values(kv_new, c2_ref, s2_sc, s_i, rows, DILATIONS[g2], t_new, None, o2_ref, l2_ref)
            project_piece()
        for _ in slabs:
            project_piece()
        for hh in range(C_HEADS):
            m_sc[hh:hh + 1, :] = jnp.broadcast_to(m_prevs[hh], (1, LANES))

    @pl.when(i % 2 == 0)
    def _():
        body(proj_b, proj_a)

    @pl.when(i % 2 == 1)
    def _():
        body(proj_a, proj_b)

    @pl.when((blk >= 0) & (blk % nblk == nblk - 1))
    def _():
        for hh in range(C_HEADS):
            cmem_ref[0, hh] = caug_sc[hh, :, :C_HEAD_DIM]
            cnorm_ref[0, hh:hh + 1, :] = caug_sc[hh, :, C_HEAD_DIM:].T[0:1, :]
        cmax_ref[0] = m_sc[...]


def _l1_prompt(x, w_qkvg, bias_row, hs, cache_b2, t_new):
    bn, s, d = x.shape
    c2 = _cache_rows_minor(cache_b2)
    nseq = c2.shape[0]
    dec_sb = 2
    ndec = nseq // dec_sb
    tm = (bn * s) // ndec
    nblk = s // tm
    ntot = bn * nblk
    ncol = w_qkvg.shape[1]
    state_blk = lambda i: (jnp.maximum(i - 1, 0) // nblk, 0, 0)
    dec_blk = lambda i: jnp.minimum(i, ndec - 1)
    drow = lambda w: pl.BlockSpec((dec_sb * TPAD, w), lambda i: (dec_blk(i), 0))
    cspec = pl.BlockSpec((dec_sb,) + c2.shape[1:], lambda i: (dec_blk(i), 0, 0))
    outs = pl.pallas_call(
        functools.partial(_l1_prompt_kernel, tm=tm, nblk=nblk, dec_sb=dec_sb, t_new=t_new),
        grid=(ntot + 1,),
        in_specs=[pl.BlockSpec((tm, d), lambda i: (jnp.minimum(i, ntot - 1), 0)),
                  pl.BlockSpec((d, ncol), lambda i: (0, 0)),
                  pl.BlockSpec((1, LANES), lambda i: (0, 0)),
                  drow(hs.shape[1]), cspec],
        out_specs=[pl.BlockSpec((tm, MIX), lambda i: (jnp.maximum(i - 1, 0), 0)),
                   pl.BlockSpec((1, C_HEADS, C_HEAD_DIM, C_HEAD_DIM), lambda i: state_blk(i) + (0,)),
                   pl.BlockSpec((1, C_HEADS, C_HEAD_DIM), state_blk),
                   pl.BlockSpec((1, TPAD, LANES), state_blk),
                   drow(QW), drow(LANES), cspec],
        out_shape=[jax.ShapeDtypeStruct((bn * s, MIX), F32),
                   jax.ShapeDtypeStruct((bn, C_HEADS, C_HEAD_DIM, C_HEAD_DIM), F32),
                   jax.ShapeDtypeStruct((bn, C_HEADS, C_HEAD_DIM), F32),
                   jax.ShapeDtypeStruct((bn, TPAD, LANES), F32),
                   jax.ShapeDtypeStruct((nseq * TPAD, QW), BF16),
                   jax.ShapeDtypeStruct((nseq * TPAD, LANES), F32),
                   jax.ShapeDtypeStruct(c2.shape, F32)],
        scratch_shapes=[pltpu.VMEM((C_HEADS, C_HEAD_DIM, C_HEAD_DIM + AUG), F32),
                        pltpu.VMEM((TPAD, LANES), F32),
                        pltpu.VMEM((2, C_HEADS, BLOCK, BLOCK), F32),
                        pltpu.VMEM((2, C_HEADS, BLOCK, BLOCK + C_HEAD_DIM), BF16),
                        pltpu.VMEM((tm, ncol), F32), pltpu.VMEM((tm, ncol), F32),
                        pltpu.VMEM((tm, d), BF16),
                        pltpu.VMEM((dec_sb, Q_HEADS * TPAD, c2.shape[2] + LANES), F32)],
        compiler_params=_cparams(("arbitrary",)),
        name="l1_prompt_mlstm_b2",
    )(x.reshape(bn * s, d), w_qkvg, bias_row, hs, c2)
    return outs[:6] + (_cache_rows_major(outs[6]),)


def _l1_decode_kernel(p_ref, bias_ref, cm_ref, cn_ref, cx_ref, h_ref, cmo_ref, cno_ref, cxo_ref,
                      *, sb, t_new):
    kscale = C_HEAD_DIM ** -0.5
    for s_i in range(sb):
        rows = slice(s_i * TPAD, (s_i + 1) * TPAD)
        gates = p_ref[rows, 3 * MIX:] + bias_ref[...]
        for hh in range(C_HEADS):
            cols = slice(hh * C_HEAD_DIM, (hh + 1) * C_HEAD_DIM)
            q = p_ref[rows, cols]
            k = p_ref[rows, MIX + hh * C_HEAD_DIM: MIX + (hh + 1) * C_HEAD_DIM] * kscale
            v = p_ref[rows, 2 * MIX + hh * C_HEAD_DIM: 2 * MIX + (hh + 1) * C_HEAD_DIM]
            h, c_new, n_new, m_new = _mlstm_chunk(
                q, k, v, gates, hh, cm_ref[s_i, hh], cn_ref[s_i, hh:hh + 1, :],
                cx_ref[s_i, hh:hh + 1, 0:1], t_new)
            h_ref[rows, cols] = h
            cmo_ref[s_i, hh] = c_new
            cno_ref[s_i, hh:hh + 1, :] = n_new
            cxo_ref[s_i, hh:hh + 1, :] = jnp.broadcast_to(m_new, (1, LANES))


def _l1_decode(proj, bias_row, c_mem, c_norm, c_max_pad, t_new):
    nseq = c_mem.shape[0]
    sb = 8
    n = nseq * TPAD
    return pl.pallas_call(
        functools.partial(_l1_decode_kernel, sb=sb, t_new=t_new),
        grid=(nseq // sb,),
        in_specs=[pl.BlockSpec((sb * TPAD, proj.shape[1]), lambda i: (i, 0)),
                  pl.BlockSpec((1, LANES), lambda i: (0, 0)),
                  pl.BlockSpec((sb, C_HEADS, C_HEAD_DIM, C_HEAD_DIM), lambda i: (i, 0, 0, 0)),
                  pl.BlockSpec((sb, C_HEADS, C_HEAD_DIM), lambda i: (i, 0, 0)),
                  pl.BlockSpec((sb, C_HEADS, LANES), lambda i: (i, 0, 0))],
        out_specs=[pl.BlockSpec((sb * TPAD, MIX), lambda i: (i, 0)),
                   pl.BlockSpec((sb, C_HEADS, C_HEAD_DIM, C_HEAD_DIM), lambda i: (i, 0, 0, 0)),
                   pl.BlockSpec((sb, C_HEADS, C_HEAD_DIM), lambda i: (i, 0, 0)),
                   pl.BlockSpec((sb, C_HEADS, LANES), lambda i: (i, 0, 0))],
        out_shape=[jax.ShapeDtypeStruct((n, MIX), F32),
                   jax.ShapeDtypeStruct(c_mem.shape, F32),
                   jax.ShapeDtypeStruct(c_norm.shape, F32),
                   jax.ShapeDtypeStruct(c_max_pad.shape, F32)],
        compiler_params=_cparams(("arbitrary",)),
        name="l1_decode_mlstm",
    )(proj, bias_row, c_mem, c_norm, c_max_pad)


def _l1_out_kernel(x_ref, h_ref, woz_ref, mhg_ref, wo_ref, g_ref, b_ref, y_ref, oz_sc, mix_sc):
    tm = x_ref.shape[0]
    n_sub = tm // SUB_ROWS

    def gate_proj(i):
        r = slice(i * SUB_ROWS, (i + 1) * SUB_ROWS)
        oz_sc[r, :] = jnp.dot(x_ref[r, :].astype(BF16), woz_ref[...], preferred_element_type=F32)

    def mix_stage(i):
        r = slice(i * SUB_ROWS, (i + 1) * SUB_ROWS)
        parts = []
        for hh in range(C_HEADS):
            cols = slice(hh * C_HEAD_DIM, (hh + 1) * C_HEAD_DIM)
            hg = jax.nn.sigmoid(oz_sc[r, cols]) * h_ref[r, cols]
            mu = jnp.mean(hg, axis=-1, keepdims=True)
            c = hg - mu
            var = jnp.mean(c * c, axis=-1, keepdims=True)
            parts.append(c * lax.rsqrt(var + MH_EPS))
        hn = jnp.concatenate(parts, axis=1) * mhg_ref[...]
        z = oz_sc[r, MIX:]
        mix_sc[r, :] = (hn * (z * jax.nn.sigmoid(z))).astype(BF16)

    def out_stage(i):
        r = slice(i * SUB_ROWS, (i + 1) * SUB_ROWS)
        y = jnp.dot(mix_sc[r, :], wo_ref[...], preferred_element_type=F32)
        y_ref[r, :] = _layer_norm_rows(DN_ALPHA * x_ref[r, :] + y, g_ref[...], b_ref[...])

    gate_proj(0)
    for i in range(n_sub):
        if i + 1 < n_sub:
            gate_proj(i + 1)
        mix_stage(i)
        out_stage(i)


def _l1_out(x2, h2, woz, mhg, wo, g, b):
    n, d = x2.shape
    tm = 512
    row = lambda w: pl.BlockSpec((tm, w), lambda i: (i, 0))
    full = lambda a: pl.BlockSpec(a.shape, lambda i: (0, 0))
    return pl.pallas_call(
        _l1_out_kernel,
        grid=(n // tm,),
        in_specs=[row(d), row(MIX), full(woz), full(mhg), full(wo), full(g), full(b)],
        out_specs=row(d),
        out_shape=jax.ShapeDtypeStruct((n, d), F32),
        scratch_shapes=[pltpu.VMEM((tm, 2 * MIX), F32), pltpu.VMEM((tm, MIX), BF16)],
        compiler_params=_cparams(("arbitrary",)),
        name="l1_out",
    )(x2, h2, woz, mhg, wo, g, b)


def _group_cols(g):
    if g == 0:
        return [(0, QW), (QW, QW + KW), (QW + KW, GW)]
    b = g - 1
    qb0 = GW
    kb0 = GW + 3 * QW
    vb0 = kb0 + 3 * KW
    return [(qb0 + b * QW, qb0 + (b + 1) * QW), (kb0 + b * KW, kb0 + (b + 1) * KW),
            (vb0 + b * KW, vb0 + (b + 1) * KW)]


def kernel(x_prompt, x_sample, cache_a_kv, cache_b0_kv, cache_b1_kv, cache_b2_kv, state_c_mem,
           state_c_norm, state_c_max, w_in0, sinks0, w_out0, w_in1, b_gates1, mh_norm1, w_out1,
           ln_g, ln_b):
    bn, s, d = x_prompt.shape
    nseq, t_new, _ = x_sample.shape
    w0 = w_in0[0]
    z0 = 4 * GW
    w_groups = [jnp.concatenate([w0[:, a:b] for a, b in _group_cols(g)], axis=1).astype(BF16)
                for g in range(N_GROUPS)]
    wz0 = w0[:, z0:].astype(BF16)
    wo0 = w_out0[0].astype(BF16)
    sinks = sinks0[0].astype(F32)
    g0, b0 = ln_g[0][None, :], ln_b[0][None, :]
    g1, b1 = ln_g[1][None, :], ln_b[1][None, :]

    outs, lses, states_p = [], [], []
    for g in range(N_GROUPS):
        o, lse, st = _group_attn_prompt(x_prompt, w_groups[g], sinks, DILATIONS[g], g == 0)
        outs.append(o)
        lses.append(lse)
        states_p.append(st)
    xp2 = x_prompt.reshape(bn * s, d)
    y0p = _l0_out(xp2, outs[0], outs[1:], lses[1:], wz0, wo0, g0, b0)

    xs2 = jnp.pad(x_sample, ((0, 0), (0, TPAD - t_new), (0, 0))).reshape(nseq * TPAD, d)
    w_dec = jnp.concatenate(
        [w_groups[g][:, :QW] for g in range(N_GROUPS)] + [w_groups[g][:, QW:QW + KW] for g in range(N_GROUPS)]
        + [w_groups[g][:, QW + KW:] for g in range(N_GROUPS)], axis=1)
    tabs_s = tuple(jnp.tile(t, (nseq, 1)) for t in _rope_tables(PAST_LEN + jnp.arange(TPAD)))
    tn = 512
    hs = _proj(xs2, w_dec, tabs_s, tn, n_rope=(N_GROUPS * (QW + KW)) // tn, n_scaled=(N_GROUPS * QW) // tn)
    sink_rows = jnp.broadcast_to(jnp.repeat(sinks, TPAD)[:, None], (Q_HEADS * TPAD, LANES))
    dec_o, dec_l, states_s = _decode_attn(hs, (cache_a_kv, cache_b0_kv, cache_b1_kv), sink_rows, t_new)

    w1 = w_in1[0]
    gate_pad = jnp.zeros((d, LANES - 2 * C_HEADS), F32)
    w_qkvg = jnp.concatenate([w1[:, :3 * MIX], w1[:, 5 * MIX:], gate_pad], axis=1).astype(BF16)
    w_oz = w1[:, 3 * MIX:5 * MIX].astype(BF16)
    wo1 = w_out1[0].astype(BF16)
    bias_row = jnp.pad(b_gates1[0].astype(F32), (0, LANES - 2 * C_HEADS))[None, :]
    mhg = mh_norm1[0][None, :]

    hp, cmem_p, cnorm_p, cmax_p, o2_s, l2_s, state_b2 = _l1_prompt(
        y0p.reshape(bn, s, d), w_qkvg, bias_row, hs, cache_b2_kv, t_new)
    y1p = _l1_out(y0p, hp, w_oz, mhg, wo1, g1, b1)
    states_s.append(state_b2)

    y0s = _l0_out(xs2, dec_o[0], list(dec_o[1:]) + [o2_s], list(dec_l) + [l2_s], wz0, wo0, g0, b0)

    no_tabs = tuple(jnp.zeros((nseq * TPAD, LANES), F32) for _ in range(3))
    ps = _proj(y0s, w_qkvg, no_tabs, 640)
    cmax_pad = jnp.broadcast_to(state_c_max[0][:, :, None], (nseq, C_HEADS, LANES))
    hsd, cmem_s, cnorm_s, cmax_s = _l1_decode(ps, bias_row, state_c_mem[0], state_c_norm[0], cmax_pad, t_new)
    y1s = _l1_out(y0s, hsd, w_oz, mhg, wo1, g1, b1)

    y_prompt = y1p.reshape(bn, s, d)
    y_sample = y1s.reshape(nseq, TPAD, d)[:, :t_new]
    return (y_prompt, y_sample,
            states_p[0][None], states_p[1][None], states_p[2][None], states_p[3][None],
            cmem_p[None], cnorm_p[None], cmax_p[:, :C_HEADS, 0][None],
            states_s[0], states_s[1], states_s[2], states_s[3],
            cmem_s[None], cnorm_s[None], cmax_s[:, :, 0][None])
```

```python
import functools

import jax
import jax.numpy as jnp
from jax import lax
from jax.experimental import pallas as pl
from jax.experimental.pallas import tpu as pltpu

F32 = jnp.float32
BF16 = jnp.bfloat16

D_MODEL = 1024
PAST_LEN = 16384
HEAD_DIM = 64
ROT_DIM = HEAD_DIM // 4
ROPE_THETA = 500000.0
BLOCK = 128
Q_HEADS = 8
KV_HEADS = 2
GROUP = Q_HEADS // KV_HEADS
QW = Q_HEADS * HEAD_DIM
KW = KV_HEADS * HEAD_DIM
GW = QW + 2 * KW
DILATIONS = (1, 1, 4, 16)
N_GROUPS = 4
C_HEADS = 4
C_HEAD_DIM = 256
MIX = 1024
DEPTH = 2
DN_ALPHA = (2 * DEPTH) ** 0.25
LN_EPS = 1e-5
MH_EPS = 1e-6
NEG_INF = -1e30
LOG2E = 1.4426950408889634
LN2 = 0.6931471805599453
LANES = 128
TPAD = 8
VMEM_LIMIT = 56 * 1024 * 1024


def _cparams(sem):
    return pltpu.CompilerParams(dimension_semantics=sem, vmem_limit_bytes=VMEM_LIMIT)


def _layer_norm_rows(v, g, b):
    mu = jnp.mean(v, axis=-1, keepdims=True)
    c = v - mu
    var = jnp.mean(c * c, axis=-1, keepdims=True)
    return c * lax.rsqrt(var + LN_EPS) * g + b


def _rope(slab, cos, sin_lo, sin_hi):
    return (slab * cos + pltpu.roll(slab, LANES - ROT_DIM // 2, 1) * sin_lo
            + pltpu.roll(slab, ROT_DIM // 2, 1) * sin_hi)


def _rope_tables(pos):
    half = ROT_DIM // 2
    inv = ROPE_THETA ** (-jnp.arange(half, dtype=F32) / half)
    ang = pos.astype(F32)[:, None] * inv[None, :]
    cos, sin = jnp.cos(ang), jnp.sin(ang)
    lane = jnp.arange(LANES) % HEAD_DIM
    f = lane % half
    cos_t = jnp.where(lane[None, :] < ROT_DIM, cos[:, f], 1.0)
    sin_lo = jnp.where(lane[None, :] < half, -sin[:, f], 0.0)
    sin_hi = jnp.where((lane[None, :] >= half) & (lane[None, :] < ROT_DIM), sin[:, f], 0.0)
    return cos_t.astype(F32), sin_lo.astype(F32), sin_hi.astype(F32)


STEP_TOKENS = 2048
X_SLABS = D_MODEL // LANES
UNITS_PER_PROJ = 4
ONES_ROWS = 16


def _rope_t(blk, cos, sin):
    half = ROT_DIM // 2
    x1, x2 = blk[:half], blk[half:ROT_DIM]
    return jnp.concatenate([x1 * cos - x2 * sin, x2 * cos + x1 * sin, blk[ROT_DIM:]], axis=0)


def _group_attn_kernel(sink_ref, *refs, has_sink, dil, tstep, nstep):
    x_refs = refs[:X_SLABS]
    (wt_ref, cos_ref, sin_ref, o_ref, lse_ref, st_ref,
     kprev_sc, vprev_sc, kv_nat_sc, o_sc, st_sc, pt_sc, bias_sc, xq_sc, ht_sc) = refs[X_SLABS:]
    j = pl.program_id(1)
    n_units = tstep // BLOCK
    is_last = j == nstep - 1

    @pl.when(j == 0)
    def _():
        kprev_sc[...] = jnp.zeros_like(kprev_sc)
        vprev_sc[...] = jnp.zeros_like(vprev_sc)

    scale = HEAD_DIM ** -0.5 * LOG2E
    kidx = lax.broadcasted_iota(jnp.int32, (2 * BLOCK, 2 * BLOCK), 0)
    qidx = lax.broadcasted_iota(jnp.int32, (2 * BLOCK, 2 * BLOCK), 1) % BLOCK
    dist = qidx + BLOCK - kidx
    band = (dist >= 0) & (dist <= BLOCK)
    bias_sc[0] = jnp.where(band, 0.0, NEG_INF)
    bias_sc[1] = jnp.where(band & (kidx >= jnp.where(j > 0, 0, BLOCK)), 0.0, NEG_INF)
    lane2 = lax.broadcasted_iota(jnp.int32, (1, 2 * BLOCK), 1)
    ones_rows = jnp.ones((ONES_ROWS, 2 * BLOCK), F32)
    zeros_half = jnp.zeros((HEAD_DIM, 2 * BLOCK), F32)
    nt = (((1,), (1,)), ((), ()))

    span = BLOCK * dil

    def unit_rows(u):
        start = (u // dil) * span + u % dil
        return pl.ds(start, BLOCK) if dil == 1 else pl.ds(start, BLOCK, stride=dil)

    def state_rows(u):
        return pl.ds(u % dil, BLOCK) if dil == 1 else pl.ds(u % dil, BLOCK, stride=dil)

    n_quads = n_units // UNITS_PER_PROJ
    k_slab = D_MODEL // UNITS_PER_PROJ

    def quad_units(qd):
        return [UNITS_PER_PROJ * qd + w for w in range(UNITS_PER_PROJ)]

    def load_x(qd):
        xq_sc[qd % 2] = jnp.concatenate(
            [jnp.concatenate([x_refs[c][0, unit_rows(u), :] for c in range(X_SLABS)], axis=1)
             for u in quad_units(qd)], axis=0).astype(BF16)

    def project_piece(qd, kp):
        ks = slice(kp * k_slab, (kp + 1) * k_slab)
        part = lax.dot_general(wt_ref[:, ks], xq_sc[qd % 2, :, ks], nt, preferred_element_type=F32)
        if kp == 0:
            ht_sc[qd % 2] = part
        else:
            ht_sc[qd % 2] += part

    load_x(0)
    for kp in range(UNITS_PER_PROJ):
        project_piece(0, kp)
    carried = {}
    for qd in range(n_quads):
        units = quad_units(qd)
        ht = ht_sc[qd % 2]
        cols = slice(qd * UNITS_PER_PROJ * BLOCK, (qd + 1) * UNITS_PER_PROJ * BLOCK)
        cos, sin = cos_ref[:, cols], sin_ref[:, cols]
        qt = [_rope_t(ht[h * HEAD_DIM:(h + 1) * HEAD_DIM], cos, sin) * scale for h in range(Q_HEADS)]
        kt = jnp.concatenate([_rope_t(ht[QW + h * HEAD_DIM:QW + (h + 1) * HEAD_DIM], cos, sin)
                              for h in range(KV_HEADS)], axis=0)
        vt = ht[QW + KW:GW]
        k_rm = kt.T

        if dil > 1 and units[0] >= n_units - dil:
            @pl.when(is_last)
            def _():
                v_rm = vt.T
                for w, u in enumerate(units):
                    kv_nat_sc[0, state_rows(u), :] = k_rm[w * BLOCK:(w + 1) * BLOCK]
                    kv_nat_sc[1, state_rows(u), :] = v_rm[w * BLOCK:(w + 1) * BLOCK]

        def scores(w, u, carried):
            lanes = slice(w * BLOCK, (w + 1) * BLOCK)
            k_cur = k_rm[w * BLOCK:(w + 1) * BLOCK].astype(BF16)
            vt_cur = vt[:, lanes].astype(BF16)
            if u >= dil:
                k_prev, vt_prev = carried[u - dil]
                mask = 0
            else:
                k_prev, vt_prev = kprev_sc[u], vprev_sc[u]
                mask = 1
            k_cat = jnp.concatenate([k_prev, k_cur], axis=0)
            wq = []
            for c in range(QW // LANES):
                top = jnp.concatenate([qt[2 * c][:, lanes], qt[2 * c + 1][:, lanes]], axis=1)
                wq.append(jnp.concatenate([top, zeros_half] if c < GROUP // 2 else [zeros_half, top], axis=0))
            wq = jnp.concatenate(wq, axis=1).astype(BF16)
            st_sc[u % 2] = jnp.dot(k_cat, wq, preferred_element_type=F32)
            return dict(w=w, u=u, lanes=lanes, k_cur=k_cur, vt_cur=vt_cur, vt_prev=vt_prev, mask=mask)

        def softmax(t):
            par = t["u"] % 2
            ms = []
            for c in range(QW // LANES):
                pc = slice(c * 2 * BLOCK, (c + 1) * 2 * BLOCK)
                st = st_sc[par, :, pc] + bias_sc[t["mask"]]
                m = jnp.max(st, axis=0, keepdims=True)
                if has_sink:
                    sink = jnp.where(lane2 < BLOCK, sink_ref[2 * c], sink_ref[2 * c + 1]) * LOG2E
                    m = jnp.maximum(m, sink)
                ms.append(m)
                pt_sc[par, :, pc] = jnp.exp2(st - m).astype(BF16)
            t["ms"] = ms

        def values(t):
            u, par, lanes = t["u"], t["u"] % 2, t["lanes"]
            lse_rows = []
            for hk in range(KV_HEADS):
                hd = slice(hk * HEAD_DIM, (hk + 1) * HEAD_DIM)
                vt_aug = jnp.concatenate(
                    [jnp.concatenate([t["vt_prev"][hd], t["vt_cur"][hd]], axis=1), ones_rows.astype(BF16)], axis=0)
                acc2 = jnp.dot(vt_aug, pt_sc[par, :, hk * 4 * BLOCK:(hk + 1) * 4 * BLOCK],
                               preferred_element_type=F32)
                for i in range(GROUP // 2):
                    c = hk * (GROUP // 2) + i
                    acc = acc2[:, i * 2 * BLOCK:(i + 1) * 2 * BLOCK]
                    m = t["ms"][c]
                    den = acc[HEAD_DIM:HEAD_DIM + 1]
                    if has_sink:
                        sink = jnp.where(lane2 < BLOCK, sink_ref[2 * c], sink_ref[2 * c + 1]) * LOG2E
                        den = den + jnp.exp2(sink - m)
                    ot = acc[:HEAD_DIM] / den
                    lse = m * LN2 + jnp.log(den)
                    lse_rows += [lse[:, :BLOCK], lse[:, BLOCK:]]
                    slab_t = jnp.concatenate([ot[:, :BLOCK], ot[:, BLOCK:]], axis=0)
                    o_sc[c, unit_rows(u), :] = slab_t.T
            lse_t = jnp.concatenate(lse_rows + [jnp.zeros((LANES - Q_HEADS, BLOCK), F32)], axis=0)
            lse_ref[0, unit_rows(u), :] = lse_t.T
            if u >= n_units - dil:
                kprev_sc[u % dil] = t["k_cur"]
                vprev_sc[u % dil] = t["vt_cur"]
                if dil == 1:
                    @pl.when(is_last)
                    def _():
                        st_ref[0, :KW, :] = kt[:, lanes]
                        st_ref[0, KW:, :] = vt[:, lanes]

        prefetch = qd + 1 < n_quads
        pending = None
        for w, u in enumerate(units):
            t = scores(w, u, carried)
            carried[u] = (t["k_cur"], t["vt_cur"])
            if prefetch and w == 0:
                load_x(qd + 1)
            if pending is not None:
                softmax(pending)
            if prefetch:
                project_piece(qd + 1, w)
            if pending is not None:
                values(pending)
            pending = t
        softmax(pending)
        values(pending)

    for c in range(QW // LANES):
        o_ref[0, :, c * LANES:(c + 1) * LANES] = o_sc[c].astype(o_ref.dtype)

    if dil > 1:
        @pl.when(is_last)
        def _():
            st_ref[0, :KW, :] = kv_nat_sc[0].T
            st_ref[0, KW:, :] = kv_nat_sc[1].T


def _group_attn_prompt(x, w_g, sinks, dil, has_sink):
    bn, s, d = x.shape
    span = BLOCK * dil
    tstep = max(STEP_TOKENS, span)
    nstep = s // tstep
    pos = (jnp.arange(s // span)[:, None, None] * span + jnp.arange(dil)[None, :, None]
           + jnp.arange(BLOCK)[None, None, :] * dil).reshape(s)
    half = ROT_DIM // 2
    inv = ROPE_THETA ** (-jnp.arange(half, dtype=F32) / half)
    ang = pos.astype(F32)[:, None] * inv[None, :]
    cos_t, sin_t = jnp.cos(ang).T, jnp.sin(ang).T
    tab_spec = pl.BlockSpec((half, tstep), lambda b, j: (0, j))
    o, lse, st = pl.pallas_call(
        functools.partial(_group_attn_kernel, has_sink=has_sink, dil=dil, tstep=tstep, nstep=nstep),
        grid=(bn, nstep),
        in_specs=([pl.BlockSpec(memory_space=pltpu.SMEM)]
                  + [pl.BlockSpec((1, tstep, LANES), functools.partial(lambda c, b, j: (b, j, c), c))
                     for c in range(X_SLABS)]
                  + [pl.BlockSpec((GW, d), lambda b, j: (0, 0)), tab_spec, tab_spec]),
        out_specs=[
            pl.BlockSpec((1, tstep, QW), lambda b, j: (b, j, 0)),
            pl.BlockSpec((1, tstep, LANES), lambda b, j: (b, j, 0)),
            pl.BlockSpec((1, 2 * KW, span), lambda b, j: (b, 0, 0)),
        ],
        out_shape=[
            jax.ShapeDtypeStruct((bn, s, QW), BF16),
            jax.ShapeDtypeStruct((bn, s, LANES), F32),
            jax.ShapeDtypeStruct((bn, 2 * KW, span), F32),
        ],
        scratch_shapes=[pltpu.VMEM((dil, BLOCK, KW), BF16), pltpu.VMEM((dil, KW, BLOCK), BF16),
                        pltpu.VMEM((2, span, LANES), F32), pltpu.VMEM((QW // LANES, tstep, LANES), F32),
                        pltpu.VMEM((2, 2 * BLOCK, Q_HEADS * BLOCK), F32),
                        pltpu.VMEM((2, 2 * BLOCK, Q_HEADS * BLOCK), BF16),
                        pltpu.VMEM((2, 2 * BLOCK, 2 * BLOCK), F32),
                        pltpu.VMEM((2, UNITS_PER_PROJ * BLOCK, d), BF16),
                        pltpu.VMEM((2, GW, UNITS_PER_PROJ * BLOCK), F32)],
        compiler_params=_cparams(("arbitrary", "arbitrary")),
        name=f"l0_group_attn_d{dil}_{'sink' if has_sink else 'nosink'}",
    )(sinks, *([x] * X_SLABS), w_g.T, cos_t, sin_t)
    st = jnp.transpose(st.reshape(bn, 2, KV_HEADS, HEAD_DIM, span), (0, 4, 1, 2, 3))
    return o.reshape(bn * s, QW), lse.reshape(bn * s, LANES), st


SUB_ROWS = 256


def _l0_out_kernel(x_ref, oa_ref, o0_ref, o1_ref, o2_ref, l0_ref, l1_ref, l2_ref,
                   wz_ref, wo_ref, g_ref, b_ref, y_ref, z_sc, mix_sc):
    tm = x_ref.shape[0]
    n_sub = tm // SUB_ROWS

    def gate_proj(i):
        r = slice(i * SUB_ROWS, (i + 1) * SUB_ROWS)
        z_sc[r, :] = jnp.dot(x_ref[r, :].astype(BF16), wz_ref[...], preferred_element_type=F32)

    def mix_stage(i):
        r = slice(i * SUB_ROWS, (i + 1) * SUB_ROWS)
        l0, l1, l2 = l0_ref[r, :], l1_ref[r, :], l2_ref[r, :]
        m = jnp.maximum(jnp.maximum(l0, l1), l2)
        e0, e1, e2 = jnp.exp(l0 - m), jnp.exp(l1 - m), jnp.exp(l2 - m)
        inv = 1.0 / (e0 + e1 + e2)
        w0, w1, w2 = e0 * inv, e1 * inv, e2 * inv
        parts = []
        for hq in range(Q_HEADS):
            c = slice(hq * HEAD_DIM, (hq + 1) * HEAD_DIM)
            bshape = (SUB_ROWS, HEAD_DIM)
            parts.append(jnp.broadcast_to(w0[:, hq:hq + 1], bshape) * o0_ref[r, c].astype(F32)
                         + jnp.broadcast_to(w1[:, hq:hq + 1], bshape) * o1_ref[r, c].astype(F32)
                         + jnp.broadcast_to(w2[:, hq:hq + 1], bshape) * o2_ref[r, c].astype(F32))
        mix = jnp.concatenate([oa_ref[r, :].astype(F32)] + parts, axis=1)
        z = z_sc[r, :]
        mix_sc[r, :] = (mix * (z * jax.nn.sigmoid(z))).astype(BF16)

    def out_stage(i):
        r = slice(i * SUB_ROWS, (i + 1) * SUB_ROWS)
        y = jnp.dot(mix_sc[r, :], wo_ref[...], preferred_element_type=F32)
        y_ref[r, :] = _layer_norm_rows(DN_ALPHA * x_ref[r, :] + y, g_ref[...], b_ref[...])

    gate_proj(0)
    for i in range(n_sub):
        if i + 1 < n_sub:
            gate_proj(i + 1)
        mix_stage(i)
        out_stage(i)


def _l0_out(x2, oa, obs, lses, wz, wo, g, b):
    n, d = x2.shape
    tm = 512
    row = lambda w: pl.BlockSpec((tm, w), lambda i: (i, 0))
    full = lambda a: pl.BlockSpec(a.shape, lambda i: (0, 0))
    return pl.pallas_call(
        _l0_out_kernel,
        grid=(n // tm,),
        in_specs=[row(d), row(QW), row(QW), row(QW), row(QW), row(LANES), row(LANES), row(LANES),
                  full(wz), full(wo), full(g), full(b)],
        out_specs=row(d),
        out_shape=jax.ShapeDtypeStruct((n, d), F32),
        scratch_shapes=[pltpu.VMEM((tm, MIX), F32), pltpu.VMEM((tm, MIX), BF16)],
        compiler_params=_cparams(("arbitrary",)),
        name="l0_out",
    )(x2, oa, *obs, *lses, wz, wo, g, b)


def _proj_kernel(x_ref, w_ref, cos_ref, slo_ref, shi_ref, h_ref, *, n_rope, n_scaled, tn):
    j = pl.program_id(0)
    h = jnp.dot(x_ref[...].astype(BF16), w_ref[...], preferred_element_type=F32)
    if n_rope:
        cos, slo, shi = cos_ref[...], slo_ref[...], shi_ref[...]
        roped = jnp.concatenate(
            [_rope(h[:, c * LANES:(c + 1) * LANES], cos, slo, shi) for c in range(tn // LANES)], axis=1)
        h = jnp.where(j < n_rope, roped, h)
        h = h * jnp.where(j < n_scaled, HEAD_DIM ** -0.5, 1.0)
    h_ref[...] = h


def _proj(x2, w, tabs, tn, n_rope=0, n_scaled=0):
    n, d = x2.shape
    ncol = w.shape[1]
    tab_spec = pl.BlockSpec((n, LANES), lambda j: (0, 0))
    return pl.pallas_call(
        functools.partial(_proj_kernel, n_rope=n_rope, n_scaled=n_scaled, tn=tn),
        grid=(ncol // tn,),
        in_specs=[pl.BlockSpec((n, d), lambda j: (0, 0)),
                  pl.BlockSpec((d, tn), lambda j: (0, j)),
                  tab_spec, tab_spec, tab_spec],
        out_specs=pl.BlockSpec((n, tn), lambda j: (0, j)),
        out_shape=jax.ShapeDtypeStruct((n, ncol), F32),
        compiler_params=_cparams(("arbitrary",)),
        name=f"decode_proj_{ncol}",
    )(x2, w, *tabs)


def _dec_prepare(h_ref, rows, g):
    lo_half = lax.broadcasted_iota(jnp.int32, (TPAD, LANES), 1) < HEAD_DIM
    k_base = N_GROUPS * QW
    v_base = k_base + N_GROUPS * KW
    blocks = []
    for c in range(QW // LANES):
        slab = h_ref[rows, g * QW + c * LANES: g * QW + (c + 1) * LANES]
        swapped = pltpu.roll(slab, HEAD_DIM, 1)
        if c < QW // LANES // 2:
            blocks += [jnp.where(lo_half, slab, 0.0), jnp.where(lo_half, swapped, 0.0)]
        else:
            blocks += [jnp.where(lo_half, 0.0, swapped), jnp.where(lo_half, 0.0, slab)]
    qm = jnp.concatenate(blocks, axis=0).astype(BF16)
    k_new = h_ref[rows, k_base + g * KW: k_base + (g + 1) * KW]
    v_new = h_ref[rows, v_base + g * KW: v_base + (g + 1) * KW]
    zpad = jnp.zeros((LANES - TPAD, 2 * KW), F32)
    kv_new = jnp.concatenate([jnp.concatenate([k_new, v_new], axis=1), zpad], axis=0)
    return qm, kv_new


def _dec_scores(qm, kv_new, cache_ref, sc_ref, s_i):
    nkeys = cache_ref.shape[2]
    nt = (((1,), (1,)), ((), ()))
    sc_ref[s_i, :, :nkeys] = jnp.dot(qm, cache_ref[s_i, :KW, :].astype(BF16), preferred_element_type=F32)
    sc_ref[s_i, :, nkeys:] = lax.dot_general(qm, kv_new[:, :KW].astype(BF16), nt, preferred_element_type=F32)


def _dec_shift(kv_new, cache_ref, new_ref, s_i, t_new):
    nkeys = cache_ref.shape[2]
    lane_k = lax.broadcasted_iota(jnp.int32, (KW, LANES), 1)
    new_cols = pltpu.roll(kv_new.T, LANES - t_new, 1)
    for half in range(2):
        hrows = slice(half * KW, (half + 1) * KW)
        shifted = pltpu.roll(cache_ref[s_i, hrows, :], nkeys - t_new, 1)
        new_ref[s_i, hrows, :] = shifted
        new_ref[s_i, hrows, nkeys - LANES:] = jnp.where(lane_k >= LANES - t_new, new_cols[hrows],
                                                        shifted[:, nkeys - LANES:])


def _dec_values(kv_new, cache_ref, sc_ref, s_i, rows, dil, t_new, sink_ref, o_ref, l_ref):
    nrow = Q_HEADS * TPAD
    nkeys = cache_ref.shape[2]
    ncol = nkeys + LANES
    nt = (((1,), (1,)), ((), ()))
    lane8 = lax.broadcasted_iota(jnp.int32, (TPAD, LANES), 1)
    lo_half = lane8 < HEAD_DIM
    tok = lax.broadcasted_iota(jnp.int32, (nrow, ncol), 0) % TPAD
    col = lax.broadcasted_iota(jnp.int32, (nrow, ncol), 1)
    new_j = col - nkeys
    valid = (((col < nkeys) & (((col - tok) & (dil - 1)) == 0) & (col >= tok))
             | ((col >= nkeys) & (((tok - new_j) & (dil - 1)) == 0) & (new_j <= tok) & (new_j < t_new)))
    s_all = jnp.where(valid, sc_ref[s_i], NEG_INF)
    m = jnp.max(s_all, axis=-1, keepdims=True)
    if sink_ref is not None:
        sink = sink_ref[:, 0:1]
        m = jnp.maximum(m, sink)
    p = jnp.exp(s_all - m)
    den = jnp.sum(p, axis=-1, keepdims=True)
    if sink_ref is not None:
        den = den + jnp.exp(sink - m)
    pb = p.astype(BF16)
    o = (lax.dot_general(pb[:, :nkeys], cache_ref[s_i, KW:, :].astype(BF16), nt, preferred_element_type=F32)
         + jnp.dot(pb[:, nkeys:], kv_new[:, KW:].astype(BF16), preferred_element_type=F32)) / den
    for c in range(QW // LANES):
        ev = o[(2 * c) * TPAD:(2 * c + 1) * TPAD]
        od = o[(2 * c + 1) * TPAD:(2 * c + 2) * TPAD]
        if c < QW // LANES // 2:
            slab = jnp.where(lo_half, ev, pltpu.roll(od, HEAD_DIM, 1))
        else:
            slab = jnp.where(lo_half, pltpu.roll(ev, HEAD_DIM, 1), od)
        o_ref[rows, c * LANES:(c + 1) * LANES] = slab.astype(o_ref.dtype)
    if l_ref is not None:
        lse = m + jnp.log(den)
        lse_acc = jnp.zeros((TPAD, LANES), F32)
        for hq in range(Q_HEADS):
            lse_acc = jnp.where(lane8 == hq, lse[hq * TPAD:(hq + 1) * TPAD], lse_acc)
        l_ref[rows, :] = lse_acc


def _cache_rows_minor(c):
    nseq, rows = c.shape[1], c.shape[2]
    return jnp.transpose(c[0], (0, 2, 3, 4, 1)).reshape(nseq, 2 * KW, rows)


def _cache_rows_major(c):
    nseq, _, rows = c.shape
    return jnp.transpose(c.reshape(nseq, 2, KV_HEADS, HEAD_DIM, rows), (0, 4, 1, 2, 3))[None]


def _mlstm_chunk(q, k, v, gates, hh, c_mem, c_norm, m_prev, n_valid):
    L = q.shape[0]
    ig_col = gates[:, hh:hh + 1]
    fg = gates[:, C_HEADS + hh:C_HEADS + hh + 1]
    lf_col = jnp.minimum(fg, 0.0) - jnp.log(1.0 + jnp.exp(-jnp.abs(fg)))
    ri = lax.broadcasted_iota(jnp.int32, (L, L), 0)
    ci = lax.broadcasted_iota(jnp.int32, (L, L), 1)
    tril = ri >= ci
    eye = ri == ci
    lf_row = jnp.sum(jnp.where(eye, lf_col, 0.0), axis=0, keepdims=True)
    ig_row = jnp.sum(jnp.where(eye, ig_col, 0.0), axis=0, keepdims=True)
    b_col = jnp.sum(jnp.where(tril, lf_row, 0.0), axis=1, keepdims=True)
    b_row = jnp.sum(jnp.where(ri <= ci, lf_col, 0.0), axis=0, keepdims=True)
    dmat = jnp.where(tril, b_col - b_row + ig_row, NEG_INF)
    inter = b_col + m_prev
    m = jnp.maximum(inter, jnp.max(dmat, axis=1, keepdims=True))
    pw = jnp.exp(dmat - m)
    qb, kb, vb = q.astype(BF16), k.astype(BF16), v.astype(BF16)
    sc = lax.dot_general(qb, kb, (((1,), (1,)), ((), ())), preferred_element_type=F32) * pw
    carry_w = jnp.exp(inter - m)
    num = (jnp.dot(sc.astype(BF16), vb, preferred_element_type=F32)
           + carry_w * jnp.dot(qb, c_mem.astype(BF16), preferred_element_type=F32))
    den = jnp.sum(sc, axis=1, keepdims=True) + carry_w * jnp.sum(q * c_norm, axis=1, keepdims=True)
    h = num / jnp.maximum(jnp.abs(den), jnp.exp(-m))
    last = n_valid - 1
    m_new = m[last:last + 1, :]
    b_last = b_col[last:last + 1, :]
    ws = jnp.exp(b_last - b_col + ig_col - m_new)
    if n_valid < L:
        ws = jnp.where(lax.broadcasted_iota(jnp.int32, (L, 1), 0) < n_valid, ws, 0.0)
    decay = jnp.exp(b_last + m_prev - m_new)
    wk = ws * k
    c_mem_new = decay * c_mem + lax.dot_general(
        wk.astype(BF16), vb, (((0,), (0,)), ((), ())), preferred_element_type=F32)
    c_norm_new = decay * c_norm + jnp.sum(wk, axis=0, keepdims=True)
    return h, c_mem_new, c_norm_new, m_new


AUG = LANES
PROJ_COLS = 256


def _l1_prompt_kernel(x_ref, w_ref, bias_ref, sink_ref, hs_ref, ca_ref, c0_ref, c1_ref, c2_ref,
                      h_ref, cmem_ref, cnorm_ref, cmax_ref,
                      oa_ref, o0_ref, o1_ref, o2_ref, l0_ref, l1_ref, l2_ref, na_ref, n0_ref, n1_ref, n2_ref,
                      caug_sc, m_sc, s_sc, lhs_sc, proj_a, proj_b, xb_sc, sa_sc, s0_sc, s1_sc, s2_sc,
                      *, tm, nblk, dec_sb, t_new):
    i = pl.program_id(0)
    blk = i - 1

    @pl.when(i == 0)
    def _():
        proj_b[...] = jnp.zeros_like(proj_b)

    @pl.when(blk % nblk == 0)
    def _():
        caug_sc[...] = jnp.zeros_like(caug_sc)
        m_sc[...] = jnp.zeros_like(m_sc)

    ri = lax.broadcasted_iota(jnp.int32, (BLOCK, BLOCK), 0)
    ci = lax.broadcasted_iota(jnp.int32, (BLOCK, BLOCK), 1)
    tril = ri >= ci
    upper = (ri <= ci).astype(F32)
    ones_blk = jnp.ones((BLOCK, AUG), BF16)
    kscale = C_HEAD_DIM ** -0.5
    nt = (((1,), (1,)), ((), ()))
    ncol = w_ref.shape[1]
    slabs = [(c0, min(c0 + PROJ_COLS, ncol)) for c0 in range(0, ncol, PROJ_COLS)]
    n_chunks = tm // BLOCK

    def body(src, dst):
        xb_sc[...] = x_ref[...].astype(BF16)
        pieces = iter(slabs)

        def project_piece():
            c = next(pieces, None)
            if c is not None:
                dst[:, c[0]:c[1]] = jnp.dot(xb_sc[...], w_ref[:, c[0]:c[1]], preferred_element_type=F32)

        g_t = (src[:, 3 * MIX:] + bias_ref[...]).T[0:TPAD]
        lf_t = jnp.minimum(g_t, 0.0) - jnp.log(1.0 + jnp.exp(-jnp.abs(g_t)))

        def score_stage(c):
            rows = slice(c * BLOCK, (c + 1) * BLOCK)
            qs, ks, vs = [], [], []
            for hh in range(C_HEADS):
                q = src[rows, hh * C_HEAD_DIM:(hh + 1) * C_HEAD_DIM]
                k = src[rows, MIX + hh * C_HEAD_DIM: MIX + (hh + 1) * C_HEAD_DIM] * kscale
                v = src[rows, 2 * MIX + hh * C_HEAD_DIM: 2 * MIX + (hh + 1) * C_HEAD_DIM]
                qs.append(q)
                ks.append(k)
                vs.append(v.astype(BF16))
                s_sc[c % 2, hh] = lax.dot_general(q.astype(BF16), k.astype(BF16), nt, preferred_element_type=F32)
            return dict(rows=rows, par=c % 2, qs=qs, ks=ks, vs=vs)

        def gate_stage(t, m_prevs):
            rows, par = t["rows"], t["par"]
            b_t = jnp.dot(lf_t[:, rows], upper, preferred_element_type=F32, precision=lax.Precision.HIGHEST)
            stats = []
            for hh in range(C_HEADS):
                ig_row = g_t[hh:hh + 1, rows]
                lf_row = lf_t[C_HEADS + hh:C_HEADS + hh + 1, rows]
                b_row = b_t[C_HEADS + hh:C_HEADS + hh + 1, :]
                m_prev = m_prevs[hh]
                b_col = jnp.sum(jnp.where(tril, lf_row, 0.0), axis=1, keepdims=True)
                dmat = jnp.where(tril, b_col - (b_row - ig_row), NEG_INF)
                inter = b_col + m_prev
                m = jnp.maximum(inter, jnp.max(dmat, axis=1, keepdims=True))
                sc = s_sc[par, hh] * jnp.exp(dmat - m)
                carry_w = jnp.exp(inter - m)
                lhs_sc[par, hh, :, :BLOCK] = sc.astype(BF16)
                lhs_sc[par, hh, :, BLOCK:] = (carry_w * t["qs"][hh]).astype(BF16)
                m_new = m[BLOCK - 1:BLOCK, :]
                b_last = b_col[BLOCK - 1:BLOCK, :]
                ws_row = jnp.exp(b_last - b_row + ig_row - m_new)
                decay = jnp.exp(b_last + m_prev - m_new)
                stats.append((m, m_new, ws_row, decay))
            t["stats"] = stats

        def memory_stage(t):
            rows, par = t["rows"], t["par"]
            for hh in range(C_HEADS):
                cols = slice(hh * C_HEAD_DIM, (hh + 1) * C_HEAD_DIM)
                m, m_new, ws_row, decay = t["stats"][hh]
                caug = caug_sc[hh]
                v_aug = jnp.concatenate([t["vs"][hh], ones_blk], axis=1)
                rhs = jnp.concatenate([v_aug, caug.astype(BF16)], axis=0)
                res = jnp.dot(lhs_sc[par, hh], rhs, preferred_element_type=F32)
                den = jnp.maximum(jnp.abs(res[:, C_HEAD_DIM:]), jnp.exp(-m))
                h_ref[rows, cols] = res[:, :C_HEAD_DIM] / jnp.concatenate([den, den], axis=1)
                wk_t = (t["ks"][hh].T * ws_row).astype(BF16)
                caug_sc[hh] = decay * caug + jnp.dot(wk_t, v_aug, preferred_element_type=F32)

        m_prevs = [m_sc[hh:hh + 1, 0:1] for hh in range(C_HEADS)]
        ts = [score_stage(0), score_stage(1)]
        caches = (ca_ref, c0_ref, c1_ref, c2_ref)
        new_caches = (na_ref, n0_ref, n1_ref, n2_ref)
        o_refs = (oa_ref, o0_ref, o1_ref, o2_ref)
        l_refs = (None, l0_ref, l1_ref, l2_ref)
        sc_refs = (sa_sc, s0_sc, s1_sc, s2_sc)
        dec = []
        for g in range(N_GROUPS):
            for s_i in range(dec_sb):
                rows = slice(s_i * TPAD, (s_i + 1) * TPAD)
                qm, kv_new = _dec_prepare(hs_ref, rows, g)
                _dec_scores(qm, kv_new, caches[g], sc_refs[g], s_i)
                dec.append((g, s_i, rows, kv_new))
        n_dec = len(dec)
        shifts = iter(dec)

        def shift_items(k):
            for _ in range(k):
                d = next(shifts, None)
                if d is not None:
                    _dec_shift(d[3], caches[d[0]], new_caches[d[0]], d[1], t_new)
                    project_piece()

        project_piece()
        for c in range(n_chunks):
            gate_stage(ts[c], m_prevs)
            m_prevs = [st[1] for st in ts[c]["stats"]]
            project_piece()
            if c + 2 < n_chunks:
                ts.append(score_stage(c + 2))
                project_piece()
            shift_items(n_dec // (2 * n_chunks))
            if c >= 1:
                memory_stage(ts[c - 1])
                project_piece()
        memory_stage(ts[n_chunks - 1])
        project_piece()
        shift_items(n_dec)
        for g, s_i, rows, kv_new in dec:
            _dec_values(kv_new, caches[g], sc_refs[g], s_i, rows, DILATIONS[g], t_new,
                        sink_ref if g == 0 else None, o_refs[g], l_refs[g])
            project_piece()
        for _ in slabs:
            project_piece()
        for hh in range(C_HEADS):
            m_sc[hh:hh + 1, :] = jnp.broadcast_to(m_prevs[hh], (1, LANES))

    @pl.when(i % 2 == 0)
    def _():
        body(proj_b, proj_a)

    @pl.when(i % 2 == 1)
    def _():
        body(proj_a, proj_b)

    @pl.when((blk >= 0) & (blk % nblk == nblk - 1))
    def _():
        for hh in range(C_HEADS):
            cmem_ref[0, hh] = caug_sc[hh, :, :C_HEAD_DIM]
            cnorm_ref[0, hh:hh + 1, :] = caug_sc[hh, :, C_HEAD_DIM:].T[0:1, :]
        cmax_ref[0] = m_sc[...]


def _l1_prompt(x, w_qkvg, bias_row, hs, caches, sink_rows, t_new):
    bn, s, d = x.shape
    cts = [_cache_rows_minor(c) for c in caches]
    nseq, ng = cts[0].shape[0], len(cts)
    dec_sb = 2
    ndec = nseq // dec_sb
    tm = (bn * s) // ndec
    nblk = s // tm
    ntot = bn * nblk
    ncol = w_qkvg.shape[1]
    state_blk = lambda i: (jnp.maximum(i - 1, 0) // nblk, 0, 0)
    dec_blk = lambda i: jnp.minimum(i, ndec - 1)
    drow = lambda w: pl.BlockSpec((dec_sb * TPAD, w), lambda i: (dec_blk(i), 0))
    cspec = lambda c: pl.BlockSpec((dec_sb,) + c.shape[1:], lambda i: (dec_blk(i), 0, 0))
    outs = pl.pallas_call(
        functools.partial(_l1_prompt_kernel, tm=tm, nblk=nblk, dec_sb=dec_sb, t_new=t_new),
        grid=(ntot + 1,),
        in_specs=[pl.BlockSpec((tm, d), lambda i: (jnp.minimum(i, ntot - 1), 0)),
                  pl.BlockSpec((d, ncol), lambda i: (0, 0)),
                  pl.BlockSpec((1, LANES), lambda i: (0, 0)),
                  pl.BlockSpec(sink_rows.shape, lambda i: (0, 0)),
                  drow(hs.shape[1])] + [cspec(c) for c in cts],
        out_specs=([pl.BlockSpec((tm, MIX), lambda i: (jnp.maximum(i - 1, 0), 0)),
                    pl.BlockSpec((1, C_HEADS, C_HEAD_DIM, C_HEAD_DIM), lambda i: state_blk(i) + (0,)),
                    pl.BlockSpec((1, C_HEADS, C_HEAD_DIM), state_blk),
                    pl.BlockSpec((1, TPAD, LANES), state_blk)]
                   + [drow(QW)] * ng + [drow(LANES)] * (ng - 1) + [cspec(c) for c in cts]),
        out_shape=([jax.ShapeDtypeStruct((bn * s, MIX), F32),
                    jax.ShapeDtypeStruct((bn, C_HEADS, C_HEAD_DIM, C_HEAD_DIM), F32),
                    jax.ShapeDtypeStruct((bn, C_HEADS, C_HEAD_DIM), F32),
                    jax.ShapeDtypeStruct((bn, TPAD, LANES), F32)]
                   + [jax.ShapeDtypeStruct((nseq * TPAD, QW), BF16)] * ng
                   + [jax.ShapeDtypeStruct((nseq * TPAD, LANES), F32)] * (ng - 1)
                   + [jax.ShapeDtypeStruct(c.shape, F32) for c in cts]),
        scratch_shapes=([pltpu.VMEM((C_HEADS, C_HEAD_DIM, C_HEAD_DIM + AUG), F32),
                         pltpu.VMEM((TPAD, LANES), F32),
                         pltpu.VMEM((2, C_HEADS, BLOCK, BLOCK), F32),
                         pltpu.VMEM((2, C_HEADS, BLOCK, BLOCK + C_HEAD_DIM), BF16),
                         pltpu.VMEM((tm, ncol), F32), pltpu.VMEM((tm, ncol), F32),
                         pltpu.VMEM((tm, d), BF16)]
                        + [pltpu.VMEM((dec_sb, Q_HEADS * TPAD, c.shape[2] + LANES), F32) for c in cts]),
        compiler_params=_cparams(("arbitrary",)),
        name="l1_prompt_mlstm_dec",
    )(x.reshape(bn * s, d), w_qkvg, bias_row, sink_rows, hs, *cts)
    return (outs[:4], outs[4:4 + ng], outs[4 + ng:3 + 2 * ng],
            [_cache_rows_major(c) for c in outs[3 + 2 * ng:]])


def _l1_out_kernel(x_ref, h_ref, woz_ref, mhg_ref, wo_ref, g_ref, b_ref, *refs, dec_sb, t_new):
    if dec_sb:
        (p_ref, bias_ref, cm_ref, cn_ref, cx_ref, y_ref, hd_ref, cmo_ref, cno_ref, cxo_ref,
         oz_sc, mix_sc) = refs
    else:
        y_ref, oz_sc, mix_sc = refs
    tm = x_ref.shape[0]
    n_sub = tm // SUB_ROWS

    def gate_proj(i):
        r = slice(i * SUB_ROWS, (i + 1) * SUB_ROWS)
        oz_sc[r, :] = jnp.dot(x_ref[r, :].astype(BF16), woz_ref[...], preferred_element_type=F32)

    def mix_stage(i):
        r = slice(i * SUB_ROWS, (i + 1) * SUB_ROWS)
        parts = []
        for hh in range(C_HEADS):
            cols = slice(hh * C_HEAD_DIM, (hh + 1) * C_HEAD_DIM)
            hg = jax.nn.sigmoid(oz_sc[r, cols]) * h_ref[r, cols]
            mu = jnp.mean(hg, axis=-1, keepdims=True)
            c = hg - mu
            var = jnp.mean(c * c, axis=-1, keepdims=True)
            parts.append(c * lax.rsqrt(var + MH_EPS))
        hn = jnp.concatenate(parts, axis=1) * mhg_ref[...]
        z = oz_sc[r, MIX:]
        mix_sc[r, :] = (hn * (z * jax.nn.sigmoid(z))).astype(BF16)

    def out_stage(i):
        r = slice(i * SUB_ROWS, (i + 1) * SUB_ROWS)
        y = jnp.dot(mix_sc[r, :], wo_ref[...], preferred_element_type=F32)
        y_ref[r, :] = _layer_norm_rows(DN_ALPHA * x_ref[r, :] + y, g_ref[...], b_ref[...])

    kscale = C_HEAD_DIM ** -0.5
    chunks = [(s_i, hh) for s_i in range(dec_sb) for hh in range(C_HEADS)]
    todo = iter(chunks)

    def decode_chunks(k):
        for _ in range(k):
            it = next(todo, None)
            if it is None:
                return
            s_i, hh = it
            rows = slice(s_i * TPAD, (s_i + 1) * TPAD)
            cols = slice(hh * C_HEAD_DIM, (hh + 1) * C_HEAD_DIM)
            gates = p_ref[rows, 3 * MIX:] + bias_ref[...]
            q = p_ref[rows, cols]
            k_ = p_ref[rows, MIX + hh * C_HEAD_DIM: MIX + (hh + 1) * C_HEAD_DIM] * kscale
            v = p_ref[rows, 2 * MIX + hh * C_HEAD_DIM: 2 * MIX + (hh + 1) * C_HEAD_DIM]
            h, c_new, n_new, m_new = _mlstm_chunk(
                q, k_, v, gates, hh, cm_ref[s_i, hh], cn_ref[s_i, hh:hh + 1, :],
                cx_ref[s_i, hh:hh + 1, 0:1], t_new)
            hd_ref[rows, cols] = h
            cmo_ref[s_i, hh] = c_new
            cno_ref[s_i, hh:hh + 1, :] = n_new
            cxo_ref[s_i, hh:hh + 1, :] = jnp.broadcast_to(m_new, (1, LANES))

    per_slot = -(-len(chunks) // (2 * n_sub + 1))
    gate_proj(0)
    decode_chunks(per_slot)
    for i in range(n_sub):
        if i + 1 < n_sub:
            gate_proj(i + 1)
        mix_stage(i)
        decode_chunks(per_slot)
        out_stage(i)
        decode_chunks(per_slot)
    decode_chunks(len(chunks))


def _l1_out(x2, h2, woz, mhg, wo, g, b, dec=None):
    n, d = x2.shape
    tm = 512
    row = lambda w: pl.BlockSpec((tm, w), lambda i: (i, 0))
    full = lambda a: pl.BlockSpec(a.shape, lambda i: (0, 0))
    in_specs = [row(d), row(MIX), full(woz), full(mhg), full(wo), full(g), full(b)]
    out_specs = [row(d)]
    out_shape = [jax.ShapeDtypeStruct((n, d), F32)]
    args = [x2, h2, woz, mhg, wo, g, b]
    dec_sb, t_new = 0, 0
    if dec is not None:
        proj, bias_row, c_mem, c_norm, c_max_pad, t_new = dec
        nseq = c_mem.shape[0]
        dec_sb = nseq // (n // tm)
        sblk = lambda a: pl.BlockSpec((dec_sb,) + a.shape[1:], lambda i: (i,) + (0,) * (a.ndim - 1))
        in_specs += [pl.BlockSpec((dec_sb * TPAD, proj.shape[1]), lambda i: (i, 0)), full(bias_row),
                     sblk(c_mem), sblk(c_norm), sblk(c_max_pad)]
        out_specs += [pl.BlockSpec((dec_sb * TPAD, MIX), lambda i: (i, 0)), sblk(c_mem), sblk(c_norm), sblk(c_max_pad)]
        out_shape += [jax.ShapeDtypeStruct((nseq * TPAD, MIX), F32), jax.ShapeDtypeStruct(c_mem.shape, F32),
                      jax.ShapeDtypeStruct(c_norm.shape, F32), jax.ShapeDtypeStruct(c_max_pad.shape, F32)]
        args += [proj, bias_row, c_mem, c_norm, c_max_pad]
    outs = pl.pallas_call(
        functools.partial(_l1_out_kernel, dec_sb=dec_sb, t_new=t_new),
        grid=(n // tm,),
        in_specs=in_specs,
        out_specs=out_specs,
        out_shape=out_shape,
        scratch_shapes=[pltpu.VMEM((tm, 2 * MIX), F32), pltpu.VMEM((tm, MIX), BF16)],
        compiler_params=_cparams(("arbitrary",)),
        name="l1_out_dec" if dec is not None else "l1_out",
    )(*args)
    return outs[0] if dec is None else outs


def _group_cols(g):
    if g == 0:
        return [(0, QW), (QW, QW + KW), (QW + KW, GW)]
    b = g - 1
    qb0 = GW
    kb0 = GW + 3 * QW
    vb0 = kb0 + 3 * KW
    return [(qb0 + b * QW, qb0 + (b + 1) * QW), (kb0 + b * KW, kb0 + (b + 1) * KW),
            (vb0 + b * KW, vb0 + (b + 1) * KW)]


def kernel(x_prompt, x_sample, cache_a_kv, cache_b0_kv, cache_b1_kv, cache_b2_kv, state_c_mem,
           state_c_norm, state_c_max, w_in0, sinks0, w_out0, w_in1, b_gates1, mh_norm1, w_out1,
           ln_g, ln_b):
    bn, s, d = x_prompt.shape
    nseq, t_new, _ = x_sample.shape
    w0 = w_in0[0]
    z0 = 4 * GW
    w_groups = [jnp.concatenate([w0[:, a:b] for a, b in _group_cols(g)], axis=1).astype(BF16)
                for g in range(N_GROUPS)]
    wz0 = w0[:, z0:].astype(BF16)
    wo0 = w_out0[0].astype(BF16)
    sinks = sinks0[0].astype(F32)
    g0, b0 = ln_g[0][None, :], ln_b[0][None, :]
    g1, b1 = ln_g[1][None, :], ln_b[1][None, :]

    outs, lses, states_p = [], [], []
    for g in range(N_GROUPS):
        o, lse, st = _group_attn_prompt(x_prompt, w_groups[g], sinks, DILATIONS[g], g == 0)
        outs.append(o)
        lses.append(lse)
        states_p.append(st)
    xp2 = x_prompt.reshape(bn * s, d)
    y0p = _l0_out(xp2, outs[0], outs[1:], lses[1:], wz0, wo0, g0, b0)

    xs2 = jnp.pad(x_sample, ((0, 0), (0, TPAD - t_new), (0, 0))).reshape(nseq * TPAD, d)
    w_dec = jnp.concatenate(
        [w_groups[g][:, :QW] for g in range(N_GROUPS)] + [w_groups[g][:, QW:QW + KW] for g in range(N_GROUPS)]
        + [w_groups[g][:, QW + KW:] for g in range(N_GROUPS)], axis=1)
    tabs_s = tuple(jnp.tile(t, (nseq, 1)) for t in _rope_tables(PAST_LEN + jnp.arange(TPAD)))
    tn = 512
    hs = _proj(xs2, w_dec, tabs_s, tn, n_rope=(N_GROUPS * (QW + KW)) // tn, n_scaled=(N_GROUPS * QW) // tn)
    sink_rows = jnp.broadcast_to(jnp.repeat(sinks, TPAD)[:, None], (Q_HEADS * TPAD, LANES))

    w1 = w_in1[0]
    gate_pad = jnp.zeros((d, LANES - 2 * C_HEADS), F32)
    w_qkvg = jnp.concatenate([w1[:, :3 * MIX], w1[:, 5 * MIX:], gate_pad], axis=1).astype(BF16)
    w_oz = w1[:, 3 * MIX:5 * MIX].astype(BF16)
    wo1 = w_out1[0].astype(BF16)
    bias_row = jnp.pad(b_gates1[0].astype(F32), (0, LANES - 2 * C_HEADS))[None, :]
    mhg = mh_norm1[0][None, :]

    (hp, cmem_p, cnorm_p, cmax_p), dec_o, dec_l, states_s = _l1_prompt(
        y0p.reshape(bn, s, d), w_qkvg, bias_row, hs,
        (cache_a_kv, cache_b0_kv, cache_b1_kv, cache_b2_kv), sink_rows, t_new)

    y0s = _l0_out(xs2, dec_o[0], dec_o[1:], dec_l, wz0, wo0, g0, b0)

    no_tabs = tuple(jnp.zeros((nseq * TPAD, LANES), F32) for _ in range(3))
    ps = _proj(y0s, w_qkvg, no_tabs, 640)
    cmax_pad = jnp.broadcast_to(state_c_max[0][:, :, None], (nseq, C_HEADS, LANES))
    y1p, hsd, cmem_s, cnorm_s, cmax_s = _l1_out(
        y0p, hp, w_oz, mhg, wo1, g1, b1,
        dec=(ps, bias_row, state_c_mem[0], state_c_norm[0], cmax_pad, t_new))
    y1s = _l1_out(y0s, hsd, w_oz, mhg, wo1, g1, b1)

    y_prompt = y1p.reshape(bn, s, d)
    y_sample = y1s.reshape(nseq, TPAD, d)[:, :t_new]
    return (y_prompt, y_sample,
            states_p[0][None], states_p[1][None], states_p[2][None], states_p[3][None],
            cmem_p[None], cnorm_p[None], cmax_p[:, :C_HEADS, 0][None],
            states_s[0], states_s[1], states_s[2], states_s[3],
            cmem_s[None], cnorm_s[None], cmax_s[:, :, 0][None])
```

```python
import functools

import jax
import jax.numpy as jnp
from jax import lax
from jax.experimental import pallas as pl
from jax.experimental.pallas import tpu as pltpu

F32 = jnp.float32
BF16 = jnp.bfloat16

D_MODEL = 1024
PAST_LEN = 16384
HEAD_DIM = 64
ROT_DIM = HEAD_DIM // 4
ROPE_THETA = 500000.0
BLOCK = 128
Q_HEADS = 8
KV_HEADS = 2
GROUP = Q_HEADS // KV_HEADS
QW = Q_HEADS * HEAD_DIM
KW = KV_HEADS * HEAD_DIM
GW = QW + 2 * KW
DILATIONS = (1, 1, 4, 16)
N_GROUPS = 4
C_HEADS = 4
C_HEAD_DIM = 256
MIX = 1024
DEPTH = 2
DN_ALPHA = (2 * DEPTH) ** 0.25
LN_EPS = 1e-5
MH_EPS = 1e-6
NEG_INF = -1e30
LOG2E = 1.4426950408889634
LN2 = 0.6931471805599453
LANES = 128
TPAD = 8
VMEM_LIMIT = 56 * 1024 * 1024


def _cparams(sem):
    return pltpu.CompilerParams(dimension_semantics=sem, vmem_limit_bytes=VMEM_LIMIT)


def _layer_norm_rows(v, g, b):
    mu = jnp.mean(v, axis=-1, keepdims=True)
    c = v - mu
    var = jnp.mean(c * c, axis=-1, keepdims=True)
    return c * lax.rsqrt(var + LN_EPS) * g + b


def _rope(slab, cos, sin_lo, sin_hi):
    return (slab * cos + pltpu.roll(slab, LANES - ROT_DIM // 2, 1) * sin_lo
            + pltpu.roll(slab, ROT_DIM // 2, 1) * sin_hi)


def _rope_tables(pos):
    half = ROT_DIM // 2
    inv = ROPE_THETA ** (-jnp.arange(half, dtype=F32) / half)
    ang = pos.astype(F32)[:, None] * inv[None, :]
    cos, sin = jnp.cos(ang), jnp.sin(ang)
    lane = jnp.arange(LANES) % HEAD_DIM
    f = lane % half
    cos_t = jnp.where(lane[None, :] < ROT_DIM, cos[:, f], 1.0)
    sin_lo = jnp.where(lane[None, :] < half, -sin[:, f], 0.0)
    sin_hi = jnp.where((lane[None, :] >= half) & (lane[None, :] < ROT_DIM), sin[:, f], 0.0)
    return cos_t.astype(F32), sin_lo.astype(F32), sin_hi.astype(F32)


STEP_TOKENS = 2048
X_SLABS = D_MODEL // LANES
UNITS_PER_PROJ = 4
ONES_ROWS = 16


def _rope_t(blk, cos, sin):
    half = ROT_DIM // 2
    x1, x2 = blk[:half], blk[half:ROT_DIM]
    return jnp.concatenate([x1 * cos - x2 * sin, x2 * cos + x1 * sin, blk[ROT_DIM:]], axis=0)


def _group_attn_kernel(sink_ref, *refs, has_sink, dil, tstep, nstep):
    x_refs = refs[:X_SLABS]
    (wt_ref, cos_ref, sin_ref, o_ref, lse_ref, st_ref,
     kprev_sc, vprev_sc, kv_nat_sc, o_sc, st_sc, pt_sc, bias_sc, xq_sc, ht_sc) = refs[X_SLABS:]
    j = pl.program_id(1)
    n_units = tstep // BLOCK
    is_last = j == nstep - 1

    @pl.when(j == 0)
    def _():
        kprev_sc[...] = jnp.zeros_like(kprev_sc)
        vprev_sc[...] = jnp.zeros_like(vprev_sc)

    scale = HEAD_DIM ** -0.5 * LOG2E
    kidx = lax.broadcasted_iota(jnp.int32, (2 * BLOCK, 2 * BLOCK), 0)
    qidx = lax.broadcasted_iota(jnp.int32, (2 * BLOCK, 2 * BLOCK), 1) % BLOCK
    dist = qidx + BLOCK - kidx
    band = (dist >= 0) & (dist <= BLOCK)
    bias_sc[0] = jnp.where(band, 0.0, NEG_INF)
    bias_sc[1] = jnp.where(band & (kidx >= jnp.where(j > 0, 0, BLOCK)), 0.0, NEG_INF)
    lane2 = lax.broadcasted_iota(jnp.int32, (1, 2 * BLOCK), 1)
    ones_rows = jnp.ones((ONES_ROWS, 2 * BLOCK), F32)
    zeros_half = jnp.zeros((HEAD_DIM, 2 * BLOCK), F32)
    nt = (((1,), (1,)), ((), ()))

    span = BLOCK * dil

    def unit_rows(u):
        start = (u // dil) * span + u % dil
        return pl.ds(start, BLOCK) if dil == 1 else pl.ds(start, BLOCK, stride=dil)

    def state_rows(u):
        return pl.ds(u % dil, BLOCK) if dil == 1 else pl.ds(u % dil, BLOCK, stride=dil)

    n_quads = n_units // UNITS_PER_PROJ
    k_slab = D_MODEL // UNITS_PER_PROJ

    def quad_units(qd):
        return [UNITS_PER_PROJ * qd + w for w in range(UNITS_PER_PROJ)]

    def load_x(qd):
        xq_sc[qd % 2] = jnp.concatenate(
            [jnp.concatenate([x_refs[c][0, unit_rows(u), :] for c in range(X_SLABS)], axis=1)
             for u in quad_units(qd)], axis=0).astype(BF16)

    def project_piece(qd, kp):
        ks = slice(kp * k_slab, (kp + 1) * k_slab)
        part = lax.dot_general(wt_ref[:, ks], xq_sc[qd % 2, :, ks], nt, preferred_element_type=F32)
        if kp == 0:
            ht_sc[qd % 2] = part
        else:
            ht_sc[qd % 2] += part

    load_x(0)
    for kp in range(UNITS_PER_PROJ):
        project_piece(0, kp)
    carried = {}
    for qd in range(n_quads):
        units = quad_units(qd)
        ht = ht_sc[qd % 2]
        cols = slice(qd * UNITS_PER_PROJ * BLOCK, (qd + 1) * UNITS_PER_PROJ * BLOCK)
        cos, sin = cos_ref[:, cols], sin_ref[:, cols]
        qt = [_rope_t(ht[h * HEAD_DIM:(h + 1) * HEAD_DIM], cos, sin) * scale for h in range(Q_HEADS)]
        kt = jnp.concatenate([_rope_t(ht[QW + h * HEAD_DIM:QW + (h + 1) * HEAD_DIM], cos, sin)
                              for h in range(KV_HEADS)], axis=0)
        vt = ht[QW + KW:GW]
        k_rm = kt.T

        if dil > 1 and units[0] >= n_units - dil:
            @pl.when(is_last)
            def _():
                v_rm = vt.T
                for w, u in enumerate(units):
                    kv_nat_sc[0, state_rows(u), :] = k_rm[w * BLOCK:(w + 1) * BLOCK]
                    kv_nat_sc[1, state_rows(u), :] = v_rm[w * BLOCK:(w + 1) * BLOCK]

        def scores(w, u, carried):
            lanes = slice(w * BLOCK, (w + 1) * BLOCK)
            k_cur = k_rm[w * BLOCK:(w + 1) * BLOCK].astype(BF16)
            vt_cur = vt[:, lanes].astype(BF16)
            if u >= dil:
                k_prev, vt_prev = carried[u - dil]
                mask = 0
            else:
                k_prev, vt_prev = kprev_sc[u], vprev_sc[u]
                mask = 1
            k_cat = jnp.concatenate([k_prev, k_cur], axis=0)
            wq = []
            for c in range(QW // LANES):
                top = jnp.concatenate([qt[2 * c][:, lanes], qt[2 * c + 1][:, lanes]], axis=1)
                wq.append(jnp.concatenate([top, zeros_half] if c < GROUP // 2 else [zeros_half, top], axis=0))
            wq = jnp.concatenate(wq, axis=1).astype(BF16)
            st_sc[u % 2] = jnp.dot(k_cat, wq, preferred_element_type=F32)
            return dict(w=w, u=u, lanes=lanes, k_cur=k_cur, vt_cur=vt_cur, vt_prev=vt_prev, mask=mask)

        def softmax(t):
            par = t["u"] % 2
            ms = []
            for c in range(QW // LANES):
                pc = slice(c * 2 * BLOCK, (c + 1) * 2 * BLOCK)
                st = st_sc[par, :, pc] + bias_sc[t["mask"]]
                m = jnp.max(st, axis=0, keepdims=True)
                if has_sink:
                    sink = jnp.where(lane2 < BLOCK, sink_ref[2 * c], sink_ref[2 * c + 1]) * LOG2E
                    m = jnp.maximum(m, sink)
                ms.append(m)
                pt_sc[par, :, pc] = jnp.exp2(st - m).astype(BF16)
            t["ms"] = ms

        def values(t):
            u, par, lanes = t["u"], t["u"] % 2, t["lanes"]
            lse_rows = []
            for hk in range(KV_HEADS):
                hd = slice(hk * HEAD_DIM, (hk + 1) * HEAD_DIM)
                vt_aug = jnp.concatenate(
                    [jnp.concatenate([t["vt_prev"][hd], t["vt_cur"][hd]], axis=1), ones_rows.astype(BF16)], axis=0)
                acc2 = jnp.dot(vt_aug, pt_sc[par, :, hk * 4 * BLOCK:(hk + 1) * 4 * BLOCK],
                               preferred_element_type=F32)
                for i in range(GROUP // 2):
                    c = hk * (GROUP // 2) + i
                    acc = acc2[:, i * 2 * BLOCK:(i + 1) * 2 * BLOCK]
                    m = t["ms"][c]
                    den = acc[HEAD_DIM:HEAD_DIM + 1]
                    if has_sink:
                        sink = jnp.where(lane2 < BLOCK, sink_ref[2 * c], sink_ref[2 * c + 1]) * LOG2E
                        den = den + jnp.exp2(sink - m)
                    ot = acc[:HEAD_DIM] / den
                    lse = m * LN2 + jnp.log(den)
                    lse_rows += [lse[:, :BLOCK], lse[:, BLOCK:]]
                    slab_t = jnp.concatenate([ot[:, :BLOCK], ot[:, BLOCK:]], axis=0)
                    o_sc[c, unit_rows(u), :] = slab_t.T
            lse_t = jnp.concatenate(lse_rows + [jnp.zeros((LANES - Q_HEADS, BLOCK), F32)], axis=0)
            lse_ref[0, unit_rows(u), :] = lse_t.T
            if u >= n_units - dil:
                kprev_sc[u % dil] = t["k_cur"]
                vprev_sc[u % dil] = t["vt_cur"]
                if dil == 1:
                    @pl.when(is_last)
                    def _():
                        st_ref[0, :KW, :] = kt[:, lanes]
                        st_ref[0, KW:, :] = vt[:, lanes]

        prefetch = qd + 1 < n_quads
        pending = None
        for w, u in enumerate(units):
            t = scores(w, u, carried)
            carried[u] = (t["k_cur"], t["vt_cur"])
            if prefetch and w == 0:
                load_x(qd + 1)
            if pending is not None:
                softmax(pending)
            if prefetch:
                project_piece(qd + 1, w)
            if pending is not None:
                values(pending)
            pending = t
        softmax(pending)
        values(pending)

    for c in range(QW // LANES):
        o_ref[0, :, c * LANES:(c + 1) * LANES] = o_sc[c].astype(o_ref.dtype)

    if dil > 1:
        @pl.when(is_last)
        def _():
            st_ref[0, :KW, :] = kv_nat_sc[0].T
            st_ref[0, KW:, :] = kv_nat_sc[1].T


def _group_attn_prompt(x, w_g, sinks, dil, has_sink):
    bn, s, d = x.shape
    span = BLOCK * dil
    tstep = max(STEP_TOKENS, span)
    nstep = s // tstep
    pos = (jnp.arange(s // span)[:, None, None] * span + jnp.arange(dil)[None, :, None]
           + jnp.arange(BLOCK)[None, None, :] * dil).reshape(s)
    half = ROT_DIM // 2
    inv = ROPE_THETA ** (-jnp.arange(half, dtype=F32) / half)
    ang = pos.astype(F32)[:, None] * inv[None, :]
    cos_t, sin_t = jnp.cos(ang).T, jnp.sin(ang).T
    tab_spec = pl.BlockSpec((half, tstep), lambda b, j: (0, j))
    o, lse, st = pl.pallas_call(
        functools.partial(_group_attn_kernel, has_sink=has_sink, dil=dil, tstep=tstep, nstep=nstep),
        grid=(bn, nstep),
        in_specs=([pl.BlockSpec(memory_space=pltpu.SMEM)]
                  + [pl.BlockSpec((1, tstep, LANES), functools.partial(lambda c, b, j: (b, j, c), c))
                     for c in range(X_SLABS)]
                  + [pl.BlockSpec((GW, d), lambda b, j: (0, 0)), tab_spec, tab_spec]),
        out_specs=[
            pl.BlockSpec((1, tstep, QW), lambda b, j: (b, j, 0)),
            pl.BlockSpec((1, tstep, LANES), lambda b, j: (b, j, 0)),
            pl.BlockSpec((1, 2 * KW, span), lambda b, j: (b, 0, 0)),
        ],
        out_shape=[
            jax.ShapeDtypeStruct((bn, s, QW), BF16),
            jax.ShapeDtypeStruct((bn, s, LANES), F32),
            jax.ShapeDtypeStruct((bn, 2 * KW, span), F32),
        ],
        scratch_shapes=[pltpu.VMEM((dil, BLOCK, KW), BF16), pltpu.VMEM((dil, KW, BLOCK), BF16),
                        pltpu.VMEM((2, span, LANES), F32), pltpu.VMEM((QW // LANES, tstep, LANES), F32),
                        pltpu.VMEM((2, 2 * BLOCK, Q_HEADS * BLOCK), F32),
                        pltpu.VMEM((2, 2 * BLOCK, Q_HEADS * BLOCK), BF16),
                        pltpu.VMEM((2, 2 * BLOCK, 2 * BLOCK), F32),
                        pltpu.VMEM((2, UNITS_PER_PROJ * BLOCK, d), BF16),
                        pltpu.VMEM((2, GW, UNITS_PER_PROJ * BLOCK), F32)],
        compiler_params=_cparams(("arbitrary", "arbitrary")),
        name=f"l0_group_attn_d{dil}_{'sink' if has_sink else 'nosink'}",
    )(sinks, *([x] * X_SLABS), w_g.T, cos_t, sin_t)
    st = jnp.transpose(st.reshape(bn, 2, KV_HEADS, HEAD_DIM, span), (0, 4, 1, 2, 3))
    return o.reshape(bn * s, QW), lse.reshape(bn * s, LANES), st


SUB_ROWS = 256


def _l0_out_kernel(x_ref, oa_ref, o0_ref, o1_ref, o2_ref, l0_ref, l1_ref, l2_ref,
                   wz_ref, wo_ref, g_ref, b_ref, y_ref, z_sc, mix_sc):
    tm = x_ref.shape[0]
    n_sub = tm // SUB_ROWS

    def gate_proj(i):
        r = slice(i * SUB_ROWS, (i + 1) * SUB_ROWS)
        z_sc[r, :] = jnp.dot(x_ref[r, :].astype(BF16), wz_ref[...], preferred_element_type=F32)

    def mix_stage(i):
        r = slice(i * SUB_ROWS, (i + 1) * SUB_ROWS)
        l0, l1, l2 = l0_ref[r, :], l1_ref[r, :], l2_ref[r, :]
        m = jnp.maximum(jnp.maximum(l0, l1), l2)
        e0, e1, e2 = jnp.exp(l0 - m), jnp.exp(l1 - m), jnp.exp(l2 - m)
        inv = 1.0 / (e0 + e1 + e2)
        w0, w1, w2 = e0 * inv, e1 * inv, e2 * inv
        parts = []
        for hq in range(Q_HEADS):
            c = slice(hq * HEAD_DIM, (hq + 1) * HEAD_DIM)
            bshape = (SUB_ROWS, HEAD_DIM)
            parts.append(jnp.broadcast_to(w0[:, hq:hq + 1], bshape) * o0_ref[r, c].astype(F32)
                         + jnp.broadcast_to(w1[:, hq:hq + 1], bshape) * o1_ref[r, c].astype(F32)
                         + jnp.broadcast_to(w2[:, hq:hq + 1], bshape) * o2_ref[r, c].astype(F32))
        mix = jnp.concatenate([oa_ref[r, :].astype(F32)] + parts, axis=1)
        z = z_sc[r, :]
        mix_sc[r, :] = (mix * (z * jax.nn.sigmoid(z))).astype(BF16)

    def out_stage(i):
        r = slice(i * SUB_ROWS, (i + 1) * SUB_ROWS)
        y = jnp.dot(mix_sc[r, :], wo_ref[...], preferred_element_type=F32)
        y_ref[r, :] = _layer_norm_rows(DN_ALPHA * x_ref[r, :] + y, g_ref[...], b_ref[...])

    gate_proj(0)
    for i in range(n_sub):
        if i + 1 < n_sub:
            gate_proj(i + 1)
        mix_stage(i)
        out_stage(i)


def _l0_out(x2, oa, obs, lses, wz, wo, g, b):
    n, d = x2.shape
    tm = 512
    row = lambda w: pl.BlockSpec((tm, w), lambda i: (i, 0))
    full = lambda a: pl.BlockSpec(a.shape, lambda i: (0, 0))
    return pl.pallas_call(
        _l0_out_kernel,
        grid=(n // tm,),
        in_specs=[row(d), row(QW), row(QW), row(QW), row(QW), row(LANES), row(LANES), row(LANES),
                  full(wz), full(wo), full(g), full(b)],
        out_specs=row(d),
        out_shape=jax.ShapeDtypeStruct((n, d), F32),
        scratch_shapes=[pltpu.VMEM((tm, MIX), F32), pltpu.VMEM((tm, MIX), BF16)],
        compiler_params=_cparams(("arbitrary",)),
        name="l0_out",
    )(x2, oa, *obs, *lses, wz, wo, g, b)


def _proj_kernel(x_ref, w_ref, cos_ref, slo_ref, shi_ref, h_ref, *, n_rope, n_scaled, tn):
    j = pl.program_id(0)
    h = jnp.dot(x_ref[...].astype(BF16), w_ref[...], preferred_element_type=F32)
    if n_rope:
        cos, slo, shi = cos_ref[...], slo_ref[...], shi_ref[...]
        roped = jnp.concatenate(
            [_rope(h[:, c * LANES:(c + 1) * LANES], cos, slo, shi) for c in range(tn // LANES)], axis=1)
        h = jnp.where(j < n_rope, roped, h)
        h = h * jnp.where(j < n_scaled, HEAD_DIM ** -0.5, 1.0)
    h_ref[...] = h


def _proj(x2, w, tabs, tn, n_rope=0, n_scaled=0):
    n, d = x2.shape
    ncol = w.shape[1]
    tab_spec = pl.BlockSpec((n, LANES), lambda j: (0, 0))
    return pl.pallas_call(
        functools.partial(_proj_kernel, n_rope=n_rope, n_scaled=n_scaled, tn=tn),
        grid=(ncol // tn,),
        in_specs=[pl.BlockSpec((n, d), lambda j: (0, 0)),
                  pl.BlockSpec((d, tn), lambda j: (0, j)),
                  tab_spec, tab_spec, tab_spec],
        out_specs=pl.BlockSpec((n, tn), lambda j: (0, j)),
        out_shape=jax.ShapeDtypeStruct((n, ncol), F32),
        compiler_params=_cparams(("arbitrary",)),
        name=f"decode_proj_{ncol}",
    )(x2, w, *tabs)


def _dec_prepare(h_ref, rows, g):
    lo_half = lax.broadcasted_iota(jnp.int32, (TPAD, LANES), 1) < HEAD_DIM
    k_base = N_GROUPS * QW
    v_base = k_base + N_GROUPS * KW
    blocks = []
    for c in range(QW // LANES):
        slab = h_ref[rows, g * QW + c * LANES: g * QW + (c + 1) * LANES]
        swapped = pltpu.roll(slab, HEAD_DIM, 1)
        if c < QW // LANES // 2:
            blocks += [jnp.where(lo_half, slab, 0.0), jnp.where(lo_half, swapped, 0.0)]
        else:
            blocks += [jnp.where(lo_half, 0.0, swapped), jnp.where(lo_half, 0.0, slab)]
    qm = jnp.concatenate(blocks, axis=0).astype(BF16)
    k_new = h_ref[rows, k_base + g * KW: k_base + (g + 1) * KW]
    v_new = h_ref[rows, v_base + g * KW: v_base + (g + 1) * KW]
    zpad = jnp.zeros((LANES - TPAD, 2 * KW), F32)
    kv_new = jnp.concatenate([jnp.concatenate([k_new, v_new], axis=1), zpad], axis=0)
    return qm, kv_new


def _dec_scores(qm, kv_new, cache_ref, sc_ref, s_i):
    nkeys = cache_ref.shape[2]
    nt = (((1,), (1,)), ((), ()))
    sc_ref[s_i, :, :nkeys] = jnp.dot(qm, cache_ref[s_i, :KW, :].astype(BF16), preferred_element_type=F32)
    sc_ref[s_i, :, nkeys:] = lax.dot_general(qm, kv_new[:, :KW].astype(BF16), nt, preferred_element_type=F32)


def _dec_shift(kv_new, cache_ref, new_ref, s_i, t_new):
    nkeys = cache_ref.shape[2]
    lane_k = lax.broadcasted_iota(jnp.int32, (KW, LANES), 1)
    new_cols = pltpu.roll(kv_new.T, LANES - t_new, 1)
    for half in range(2):
        hrows = slice(half * KW, (half + 1) * KW)
        shifted = pltpu.roll(cache_ref[s_i, hrows, :], nkeys - t_new, 1)
        new_ref[s_i, hrows, :] = shifted
        new_ref[s_i, hrows, nkeys - LANES:] = jnp.where(lane_k >= LANES - t_new, new_cols[hrows],
                                                        shifted[:, nkeys - LANES:])


def _dec_values(kv_new, cache_ref, sc_ref, s_i, rows, dil, t_new, sink_ref, o_ref, l_ref):
    nrow = Q_HEADS * TPAD
    nkeys = cache_ref.shape[2]
    ncol = nkeys + LANES
    nt = (((1,), (1,)), ((), ()))
    lane8 = lax.broadcasted_iota(jnp.int32, (TPAD, LANES), 1)
    lo_half = lane8 < HEAD_DIM
    tok = lax.broadcasted_iota(jnp.int32, (nrow, ncol), 0) % TPAD
    col = lax.broadcasted_iota(jnp.int32, (nrow, ncol), 1)
    new_j = col - nkeys
    valid = (((col < nkeys) & (((col - tok) & (dil - 1)) == 0) & (col >= tok))
             | ((col >= nkeys) & (((tok - new_j) & (dil - 1)) == 0) & (new_j <= tok) & (new_j < t_new)))
    s_all = jnp.where(valid, sc_ref[s_i], NEG_INF)
    m = jnp.max(s_all, axis=-1, keepdims=True)
    if sink_ref is not None:
        sink = sink_ref[:, 0:1]
        m = jnp.maximum(m, sink)
    p = jnp.exp(s_all - m)
    den = jnp.sum(p, axis=-1, keepdims=True)
    if sink_ref is not None:
        den = den + jnp.exp(sink - m)
    pb = p.astype(BF16)
    o = (lax.dot_general(pb[:, :nkeys], cache_ref[s_i, KW:, :].astype(BF16), nt, preferred_element_type=F32)
         + jnp.dot(pb[:, nkeys:], kv_new[:, KW:].astype(BF16), preferred_element_type=F32)) / den
    for c in range(QW // LANES):
        ev = o[(2 * c) * TPAD:(2 * c + 1) * TPAD]
        od = o[(2 * c + 1) * TPAD:(2 * c + 2) * TPAD]
        if c < QW // LANES // 2:
            slab = jnp.where(lo_half, ev, pltpu.roll(od, HEAD_DIM, 1))
        else:
            slab = jnp.where(lo_half, pltpu.roll(ev, HEAD_DIM, 1), od)
        o_ref[rows, c * LANES:(c + 1) * LANES] = slab.astype(o_ref.dtype)
    if l_ref is not None:
        lse = m + jnp.log(den)
        lse_acc = jnp.zeros((TPAD, LANES), F32)
        for hq in range(Q_HEADS):
            lse_acc = jnp.where(lane8 == hq, lse[hq * TPAD:(hq + 1) * TPAD], lse_acc)
        l_ref[rows, :] = lse_acc


def _cache_rows_minor(c):
    nseq, rows = c.shape[1], c.shape[2]
    return jnp.transpose(c[0], (0, 2, 3, 4, 1)).reshape(nseq, 2 * KW, rows)


def _cache_rows_major(c):
    nseq, _, rows = c.shape
    return jnp.transpose(c.reshape(nseq, 2, KV_HEADS, HEAD_DIM, rows), (0, 4, 1, 2, 3))[None]


def _mlstm_chunk(q, k, v, gates, hh, c_mem, c_norm, m_prev, n_valid):
    L = q.shape[0]
    ig_col = gates[:, hh:hh + 1]
    fg = gates[:, C_HEADS + hh:C_HEADS + hh + 1]
    lf_col = jnp.minimum(fg, 0.0) - jnp.log(1.0 + jnp.exp(-jnp.abs(fg)))
    ri = lax.broadcasted_iota(jnp.int32, (L, L), 0)
    ci = lax.broadcasted_iota(jnp.int32, (L, L), 1)
    tril = ri >= ci
    eye = ri == ci
    lf_row = jnp.sum(jnp.where(eye, lf_col, 0.0), axis=0, keepdims=True)
    ig_row = jnp.sum(jnp.where(eye, ig_col, 0.0), axis=0, keepdims=True)
    b_col = jnp.sum(jnp.where(tril, lf_row, 0.0), axis=1, keepdims=True)
    b_row = jnp.sum(jnp.where(ri <= ci, lf_col, 0.0), axis=0, keepdims=True)
    dmat = jnp.where(tril, b_col - b_row + ig_row, NEG_INF)
    inter = b_col + m_prev
    m = jnp.maximum(inter, jnp.max(dmat, axis=1, keepdims=True))
    pw = jnp.exp(dmat - m)
    qb, kb, vb = q.astype(BF16), k.astype(BF16), v.astype(BF16)
    sc = lax.dot_general(qb, kb, (((1,), (1,)), ((), ())), preferred_element_type=F32) * pw
    carry_w = jnp.exp(inter - m)
    num = (jnp.dot(sc.astype(BF16), vb, preferred_element_type=F32)
           + carry_w * jnp.dot(qb, c_mem.astype(BF16), preferred_element_type=F32))
    den = jnp.sum(sc, axis=1, keepdims=True) + carry_w * jnp.sum(q * c_norm, axis=1, keepdims=True)
    h = num / jnp.maximum(jnp.abs(den), jnp.exp(-m))
    last = n_valid - 1
    m_new = m[last:last + 1, :]
    b_last = b_col[last:last + 1, :]
    ws = jnp.exp(b_last - b_col + ig_col - m_new)
    if n_valid < L:
        ws = jnp.where(lax.broadcasted_iota(jnp.int32, (L, 1), 0) < n_valid, ws, 0.0)
    decay = jnp.exp(b_last + m_prev - m_new)
    wk = ws * k
    c_mem_new = decay * c_mem + lax.dot_general(
        wk.astype(BF16), vb, (((0,), (0,)), ((), ())), preferred_element_type=F32)
    c_norm_new = decay * c_norm + jnp.sum(wk, axis=0, keepdims=True)
    return h, c_mem_new, c_norm_new, m_new


AUG = LANES
PROJ_COLS = 256


def _l1_prompt_kernel(x_ref, w_ref, bias_ref, sink_ref, hs_ref, ca_ref, c0_ref, c1_ref, c2_ref,
                      h_ref, cmem_ref, cnorm_ref, cmax_ref,
                      oa_ref, o0_ref, o1_ref, o2_ref, l0_ref, l1_ref, l2_ref, na_ref, n0_ref, n1_ref, n2_ref,
                      caug_sc, m_sc, s_sc, lhs_sc, proj_a, proj_b, xb_sc, sa_sc, s0_sc, s1_sc, s2_sc,
                      *, tm, nblk, dec_sb, t_new):
    i = pl.program_id(0)
    blk = i - 1

    @pl.when(i == 0)
    def _():
        proj_b[...] = jnp.zeros_like(proj_b)

    @pl.when(blk % nblk == 0)
    def _():
        caug_sc[...] = jnp.zeros_like(caug_sc)
        m_sc[...] = jnp.zeros_like(m_sc)

    ri = lax.broadcasted_iota(jnp.int32, (BLOCK, BLOCK), 0)
    ci = lax.broadcasted_iota(jnp.int32, (BLOCK, BLOCK), 1)
    tril = ri >= ci
    upper = (ri <= ci).astype(F32)
    ones_blk = jnp.ones((BLOCK, AUG), BF16)
    kscale = C_HEAD_DIM ** -0.5
    nt = (((1,), (1,)), ((), ()))
    ncol = w_ref.shape[1]
    slabs = [(c0, min(c0 + PROJ_COLS, ncol)) for c0 in range(0, ncol, PROJ_COLS)]
    n_chunks = tm // BLOCK

    def body(src, dst):
        xb_sc[...] = x_ref[...].astype(BF16)
        pieces = iter(slabs)

        def project_piece():
            c = next(pieces, None)
            if c is not None:
                dst[:, c[0]:c[1]] = jnp.dot(xb_sc[...], w_ref[:, c[0]:c[1]], preferred_element_type=F32)

        g_t = (src[:, 3 * MIX:] + bias_ref[...]).T[0:TPAD]
        lf_t = jnp.minimum(g_t, 0.0) - jnp.log(1.0 + jnp.exp(-jnp.abs(g_t)))

        def score_stage(c):
            rows = slice(c * BLOCK, (c + 1) * BLOCK)
            qs, ks, vs = [], [], []
            for hh in range(C_HEADS):
                q = src[rows, hh * C_HEAD_DIM:(hh + 1) * C_HEAD_DIM]
                k = src[rows, MIX + hh * C_HEAD_DIM: MIX + (hh + 1) * C_HEAD_DIM] * kscale
                v = src[rows, 2 * MIX + hh * C_HEAD_DIM: 2 * MIX + (hh + 1) * C_HEAD_DIM]
                qs.append(q)
                ks.append(k)
                vs.append(v.astype(BF16))
                s_sc[c % 2, hh] = lax.dot_general(q.astype(BF16), k.astype(BF16), nt, preferred_element_type=F32)
            return dict(rows=rows, par=c % 2, qs=qs, ks=ks, vs=vs)

        def gate_stage(t, m_prevs):
            rows, par = t["rows"], t["par"]
            b_t = jnp.dot(lf_t[:, rows], upper, preferred_element_type=F32, precision=lax.Precision.HIGHEST)
            stats = []
            for hh in range(C_HEADS):
                ig_row = g_t[hh:hh + 1, rows]
                lf_row = lf_t[C_HEADS + hh:C_HEADS + hh + 1, rows]
                b_row = b_t[C_HEADS + hh:C_HEADS + hh + 1, :]
                m_prev = m_prevs[hh]
                b_col = jnp.sum(jnp.where(tril, lf_row, 0.0), axis=1, keepdims=True)
                dmat = jnp.where(tril, b_col - (b_row - ig_row), NEG_INF)
                inter = b_col + m_prev
                m = jnp.maximum(inter, jnp.max(dmat, axis=1, keepdims=True))
                sc = s_sc[par, hh] * jnp.exp(dmat - m)
                carry_w = jnp.exp(inter - m)
                lhs_sc[par, hh, :, :BLOCK] = sc.astype(BF16)
                lhs_sc[par, hh, :, BLOCK:] = (carry_w * t["qs"][hh]).astype(BF16)
                m_new = m[BLOCK - 1:BLOCK, :]
                b_last = b_col[BLOCK - 1:BLOCK, :]
                ws_row = jnp.exp(b_last - b_row + ig_row - m_new)
                decay = jnp.exp(b_last + m_prev - m_new)
                stats.append((m, m_new, ws_row, decay))
            t["stats"] = stats

        def memory_stage(t):
            rows, par = t["rows"], t["par"]
            for hh in range(C_HEADS):
                cols = slice(hh * C_HEAD_DIM, (hh + 1) * C_HEAD_DIM)
                m, m_new, ws_row, decay = t["stats"][hh]
                caug = caug_sc[hh]
                v_aug = jnp.concatenate([t["vs"][hh], ones_blk], axis=1)
                rhs = jnp.concatenate([v_aug, caug.astype(BF16)], axis=0)
                res = jnp.dot(lhs_sc[par, hh], rhs, preferred_element_type=F32)
                den = jnp.maximum(jnp.abs(res[:, C_HEAD_DIM:]), jnp.exp(-m))
                h_ref[rows, cols] = res[:, :C_HEAD_DIM] / jnp.concatenate([den, den], axis=1)
                wk_t = (t["ks"][hh].T * ws_row).astype(BF16)
                caug_sc[hh] = decay * caug + jnp.dot(wk_t, v_aug, preferred_element_type=F32)

        m_prevs = [m_sc[hh:hh + 1, 0:1] for hh in range(C_HEADS)]
        ts = [score_stage(0), score_stage(1)]
        caches = (ca_ref, c0_ref, c1_ref, c2_ref)
        new_caches = (na_ref, n0_ref, n1_ref, n2_ref)
        o_refs = (oa_ref, o0_ref, o1_ref, o2_ref)
        l_refs = (None, l0_ref, l1_ref, l2_ref)
        sc_refs = (sa_sc, s0_sc, s1_sc, s2_sc)
        dec = []
        for g in range(N_GROUPS):
            for s_i in range(dec_sb):
                rows = slice(s_i * TPAD, (s_i + 1) * TPAD)
                qm, kv_new = _dec_prepare(hs_ref, rows, g)
                _dec_scores(qm, kv_new, caches[g], sc_refs[g], s_i)
                dec.append((g, s_i, rows, kv_new))
        n_dec = len(dec)
        shifts = iter(dec)

        def shift_items(k):
            for _ in range(k):
                d = next(shifts, None)
                if d is not None:
                    _dec_shift(d[3], caches[d[0]], new_caches[d[0]], d[1], t_new)
                    project_piece()

        project_piece()
        for c in range(n_chunks):
            gate_stage(ts[c], m_prevs)
            m_prevs = [st[1] for st in ts[c]["stats"]]
            project_piece()
            if c + 2 < n_chunks:
                ts.append(score_stage(c + 2))
                project_piece()
            shift_items(n_dec // (2 * n_chunks))
            if c >= 1:
                memory_stage(ts[c - 1])
                project_piece()
        memory_stage(ts[n_chunks - 1])
        project_piece()
        shift_items(n_dec)
        for g, s_i, rows, kv_new in dec:
            _dec_values(kv_new, caches[g], sc_refs[g], s_i, rows, DILATIONS[g], t_new,
                        sink_ref if g == 0 else None, o_refs[g], l_refs[g])
            project_piece()
        for _ in slabs:
            project_piece()
        for hh in range(C_HEADS):
            m_sc[hh:hh + 1, :] = jnp.broadcast_to(m_prevs[hh], (1, LANES))

    @pl.when(i % 2 == 0)
    def _():
        body(proj_b, proj_a)

    @pl.when(i % 2 == 1)
    def _():
        body(proj_a, proj_b)

    @pl.when((blk >= 0) & (blk % nblk == nblk - 1))
    def _():
        for hh in range(C_HEADS):
            cmem_ref[0, hh] = caug_sc[hh, :, :C_HEAD_DIM]
            cnorm_ref[0, hh:hh + 1, :] = caug_sc[hh, :, C_HEAD_DIM:].T[0:1, :]
        cmax_ref[0] = m_sc[...]


def _l1_prompt(x, w_qkvg, bias_row, hs, caches, sink_rows, t_new):
    bn, s, d = x.shape
    cts = [_cache_rows_minor(c) for c in caches]
    nseq, ng = cts[0].shape[0], len(cts)
    dec_sb = 2
    ndec = nseq // dec_sb
    tm = (bn * s) // ndec
    nblk = s // tm
    ntot = bn * nblk
    ncol = w_qkvg.shape[1]
    state_blk = lambda i: (jnp.maximum(i - 1, 0) // nblk, 0, 0)
    dec_blk = lambda i: jnp.minimum(i, ndec - 1)
    drow = lambda w: pl.BlockSpec((dec_sb * TPAD, w), lambda i: (dec_blk(i), 0))
    cspec = lambda c: pl.BlockSpec((dec_sb,) + c.shape[1:], lambda i: (dec_blk(i), 0, 0))
    outs = pl.pallas_call(
        functools.partial(_l1_prompt_kernel, tm=tm, nblk=nblk, dec_sb=dec_sb, t_new=t_new),
        grid=(ntot + 1,),
        in_specs=[pl.BlockSpec((tm, d), lambda i: (jnp.minimum(i, ntot - 1), 0)),
                  pl.BlockSpec((d, ncol), lambda i: (0, 0)),
                  pl.BlockSpec((1, LANES), lambda i: (0, 0)),
                  pl.BlockSpec(sink_rows.shape, lambda i: (0, 0)),
                  drow(hs.shape[1])] + [cspec(c) for c in cts],
        out_specs=([pl.BlockSpec((tm, MIX), lambda i: (jnp.maximum(i - 1, 0), 0)),
                    pl.BlockSpec((1, C_HEADS, C_HEAD_DIM, C_HEAD_DIM), lambda i: state_blk(i) + (0,)),
                    pl.BlockSpec((1, C_HEADS, C_HEAD_DIM), state_blk),
                    pl.BlockSpec((1, TPAD, LANES), state_blk)]
                   + [drow(QW)] * ng + [drow(LANES)] * (ng - 1) + [cspec(c) for c in cts]),
        out_shape=([jax.ShapeDtypeStruct((bn * s, MIX), F32),
                    jax.ShapeDtypeStruct((bn, C_HEADS, C_HEAD_DIM, C_HEAD_DIM), F32),
                    jax.ShapeDtypeStruct((bn, C_HEADS, C_HEAD_DIM), F32),
                    jax.ShapeDtypeStruct((bn, TPAD, LANES), F32)]
                   + [jax.ShapeDtypeStruct((nseq * TPAD, QW), BF16)] * ng
                   + [jax.ShapeDtypeStruct((nseq * TPAD, LANES), F32)] * (ng - 1)
                   + [jax.ShapeDtypeStruct(c.shape, F32) for c in cts]),
        scratch_shapes=([pltpu.VMEM((C_HEADS, C_HEAD_DIM, C_HEAD_DIM + AUG), F32),
                         pltpu.VMEM((TPAD, LANES), F32),
                         pltpu.VMEM((2, C_HEADS, BLOCK, BLOCK), F32),
                         pltpu.VMEM((2, C_HEADS, BLOCK, BLOCK + C_HEAD_DIM), BF16),
                         pltpu.VMEM((tm, ncol), F32), pltpu.VMEM((tm, ncol), F32),
                         pltpu.VMEM((tm, d), BF16)]
                        + [pltpu.VMEM((dec_sb, Q_HEADS * TPAD, c.shape[2] + LANES), F32) for c in cts]),
        compiler_params=_cparams(("arbitrary",)),
        name="l1_prompt_mlstm_dec",
    )(x.reshape(bn * s, d), w_qkvg, bias_row, sink_rows, hs, *cts)
    return (outs[:4], outs[4:4 + ng], outs[4 + ng:3 + 2 * ng],
            [_cache_rows_major(c) for c in outs[3 + 2 * ng:]])


def _l1_out_kernel(x_ref, h_ref, woz_ref, mhg_ref, wo_ref, g_ref, b_ref, *refs, dec_sb, t_new):
    if dec_sb:
        (p_ref, bias_ref, cm_ref, cn_ref, cx_ref, y_ref, hd_ref, cmo_ref, cno_ref, cxo_ref,
         oz_sc, mix_sc) = refs
    else:
        y_ref, oz_sc, mix_sc = refs
    tm = x_ref.shape[0]
    n_sub = tm // SUB_ROWS

    def gate_proj(i):
        r = slice(i * SUB_ROWS, (i + 1) * SUB_ROWS)
        oz_sc[r, :] = jnp.dot(x_ref[r, :].astype(BF16), woz_ref[...], preferred_element_type=F32)

    def mix_stage(i, between):
        r = slice(i * SUB_ROWS, (i + 1) * SUB_ROWS)
        for hh in range(C_HEADS):
            cols = slice(hh * C_HEAD_DIM, (hh + 1) * C_HEAD_DIM)
            hg = jax.nn.sigmoid(oz_sc[r, cols]) * h_ref[r, cols]
            mu = jnp.mean(hg, axis=-1, keepdims=True)
            c = hg - mu
            var = jnp.mean(c * c, axis=-1, keepdims=True)
            hn = c * lax.rsqrt(var + MH_EPS) * mhg_ref[:, cols]
            z = oz_sc[r, MIX + hh * C_HEAD_DIM:MIX + (hh + 1) * C_HEAD_DIM]
            mix_sc[r, cols] = (hn * (z * jax.nn.sigmoid(z))).astype(BF16)
            between()

    def out_stage(i):
        r = slice(i * SUB_ROWS, (i + 1) * SUB_ROWS)
        y = jnp.dot(mix_sc[r, :], wo_ref[...], preferred_element_type=F32)
        y_ref[r, :] = _layer_norm_rows(DN_ALPHA * x_ref[r, :] + y, g_ref[...], b_ref[...])

    kscale = C_HEAD_DIM ** -0.5
    chunks = [(s_i, hh) for s_i in range(dec_sb) for hh in range(C_HEADS)]
    todo = iter(chunks)

    def decode_chunks(k):
        for _ in range(k):
            it = next(todo, None)
            if it is None:
                return
            s_i, hh = it
            rows = slice(s_i * TPAD, (s_i + 1) * TPAD)
            cols = slice(hh * C_HEAD_DIM, (hh + 1) * C_HEAD_DIM)
            gates = p_ref[rows, 3 * MIX:] + bias_ref[...]
            q = p_ref[rows, cols]
            k_ = p_ref[rows, MIX + hh * C_HEAD_DIM: MIX + (hh + 1) * C_HEAD_DIM] * kscale
            v = p_ref[rows, 2 * MIX + hh * C_HEAD_DIM: 2 * MIX + (hh + 1) * C_HEAD_DIM]
            h, c_new, n_new, m_new = _mlstm_chunk(
                q, k_, v, gates, hh, cm_ref[s_i, hh], cn_ref[s_i, hh:hh + 1, :],
                cx_ref[s_i, hh:hh + 1, 0:1], t_new)
            hd_ref[rows, cols] = h
            cmo_ref[s_i, hh] = c_new
            cno_ref[s_i, hh:hh + 1, :] = n_new
            cxo_ref[s_i, hh:hh + 1, :] = jnp.broadcast_to(m_new, (1, LANES))

    per_slot = -(-len(chunks) // (n_sub * (C_HEADS + 1)))
    gate_proj(0)
    for i in range(n_sub):
        if i + 1 < n_sub:
            gate_proj(i + 1)
        mix_stage(i, lambda: decode_chunks(per_slot))
        out_stage(i)
        decode_chunks(per_slot)
    decode_chunks(len(chunks))


def _l1_out(x2, h2, woz, mhg, wo, g, b, dec=None):
    n, d = x2.shape
    tm = 512
    row = lambda w: pl.BlockSpec((tm, w), lambda i: (i, 0))
    full = lambda a: pl.BlockSpec(a.shape, lambda i: (0, 0))
    in_specs = [row(d), row(MIX), full(woz), full(mhg), full(wo), full(g), full(b)]
    out_specs = [row(d)]
    out_shape = [jax.ShapeDtypeStruct((n, d), F32)]
    args = [x2, h2, woz, mhg, wo, g, b]
    dec_sb, t_new = 0, 0
    if dec is not None:
        proj, bias_row, c_mem, c_norm, c_max_pad, t_new = dec
        nseq = c_mem.shape[0]
        dec_sb = nseq // (n // tm)
        sblk = lambda a: pl.BlockSpec((dec_sb,) + a.shape[1:], lambda i: (i,) + (0,) * (a.ndim - 1))
        in_specs += [pl.BlockSpec((dec_sb * TPAD, proj.shape[1]), lambda i: (i, 0)), full(bias_row),
                     sblk(c_mem), sblk(c_norm), sblk(c_max_pad)]
        out_specs += [pl.BlockSpec((dec_sb * TPAD, MIX), lambda i: (i, 0)), sblk(c_mem), sblk(c_norm), sblk(c_max_pad)]
        out_shape += [jax.ShapeDtypeStruct((nseq * TPAD, MIX), F32), jax.ShapeDtypeStruct(c_mem.shape, F32),
                      jax.ShapeDtypeStruct(c_norm.shape, F32), jax.ShapeDtypeStruct(c_max_pad.shape, F32)]
        args += [proj, bias_row, c_mem, c_norm, c_max_pad]
    outs = pl.pallas_call(
        functools.partial(_l1_out_kernel, dec_sb=dec_sb, t_new=t_new),
        grid=(n // tm,),
        in_specs=in_specs,
        out_specs=out_specs,
        out_shape=out_shape,
        scratch_shapes=[pltpu.VMEM((tm, 2 * MIX), F32), pltpu.VMEM((tm, MIX), BF16)],
        compiler_params=_cparams(("arbitrary",)),
        name="l1_out_dec" if dec is not None else "l1_out",
    )(*args)
    return outs[0] if dec is None else outs


def _group_cols(g):
    if g == 0:
        return [(0, QW), (QW, QW + KW), (QW + KW, GW)]
    b = g - 1
    qb0 = GW
    kb0 = GW + 3 * QW
    vb0 = kb0 + 3 * KW
    return [(qb0 + b * QW, qb0 + (b + 1) * QW), (kb0 + b * KW, kb0 + (b + 1) * KW),
            (vb0 + b * KW, vb0 + (b + 1) * KW)]


def kernel(x_prompt, x_sample, cache_a_kv, cache_b0_kv, cache_b1_kv, cache_b2_kv, state_c_mem,
           state_c_norm, state_c_max, w_in0, sinks0, w_out0, w_in1, b_gates1, mh_norm1, w_out1,
           ln_g, ln_b):
    bn, s, d = x_prompt.shape
    nseq, t_new, _ = x_sample.shape
    w0 = w_in0[0]
    z0 = 4 * GW
    w_groups = [jnp.concatenate([w0[:, a:b] for a, b in _group_cols(g)], axis=1).astype(BF16)
                for g in range(N_GROUPS)]
    wz0 = w0[:, z0:].astype(BF16)
    wo0 = w_out0[0].astype(BF16)
    sinks = sinks0[0].astype(F32)
    g0, b0 = ln_g[0][None, :], ln_b[0][None, :]
    g1, b1 = ln_g[1][None, :], ln_b[1][None, :]

    outs, lses, states_p = [], [], []
    for g in range(N_GROUPS):
        o, lse, st = _group_attn_prompt(x_prompt, w_groups[g], sinks, DILATIONS[g], g == 0)
        outs.append(o)
        lses.append(lse)
        states_p.append(st)
    xp2 = x_prompt.reshape(bn * s, d)
    y0p = _l0_out(xp2, outs[0], outs[1:], lses[1:], wz0, wo0, g0, b0)

    xs2 = jnp.pad(x_sample, ((0, 0), (0, TPAD - t_new), (0, 0))).reshape(nseq * TPAD, d)
    w_dec = jnp.concatenate(
        [w_groups[g][:, :QW] for g in range(N_GROUPS)] + [w_groups[g][:, QW:QW + KW] for g in range(N_GROUPS)]
        + [w_groups[g][:, QW + KW:] for g in range(N_GROUPS)], axis=1)
    tabs_s = tuple(jnp.tile(t, (nseq, 1)) for t in _rope_tables(PAST_LEN + jnp.arange(TPAD)))
    tn = 512
    hs = _proj(xs2, w_dec, tabs_s, tn, n_rope=(N_GROUPS * (QW + KW)) // tn, n_scaled=(N_GROUPS * QW) // tn)
    sink_rows = jnp.broadcast_to(jnp.repeat(sinks, TPAD)[:, None], (Q_HEADS * TPAD, LANES))

    w1 = w_in1[0]
    gate_pad = jnp.zeros((d, LANES - 2 * C_HEADS), F32)
    w_qkvg = jnp.concatenate([w1[:, :3 * MIX], w1[:, 5 * MIX:], gate_pad], axis=1).astype(BF16)
    w_oz = w1[:, 3 * MIX:5 * MIX].astype(BF16)
    wo1 = w_out1[0].astype(BF16)
    bias_row = jnp.pad(b_gates1[0].astype(F32), (0, LANES - 2 * C_HEADS))[None, :]
    mhg = mh_norm1[0][None, :]

    (hp, cmem_p, cnorm_p, cmax_p), dec_o, dec_l, states_s = _l1_prompt(
        y0p.reshape(bn, s, d), w_qkvg, bias_row, hs,
        (cache_a_kv, cache_b0_kv, cache_b1_kv, cache_b2_kv), sink_rows, t_new)

    y0s = _l0_out(xs2, dec_o[0], dec_o[1:], dec_l, wz0, wo0, g0, b0)

    no_tabs = tuple(jnp.zeros((nseq * TPAD, LANES), F32) for _ in range(3))
    ps = _proj(y0s, w_qkvg, no_tabs, 640)
    cmax_pad = jnp.broadcast_to(state_c_max[0][:, :, None], (nseq, C_HEADS, LANES))
    y1p, hsd, cmem_s, cnorm_s, cmax_s = _l1_out(
        y0p, hp, w_oz, mhg, wo1, g1, b1,
        dec=(ps, bias_row, state_c_mem[0], state_c_norm[0], cmax_pad, t_new))
    y1s = _l1_out(y0s, hsd, w_oz, mhg, wo1, g1, b1)

    y_prompt = y1p.reshape(bn, s, d)
    y_sample = y1s.reshape(nseq, TPAD, d)[:, :t_new]
    return (y_prompt, y_sample,
            states_p[0][None], states_p[1][None], states_p[2][None], states_p[3][None],
            cmem_p[None], cnorm_p[None], cmax_p[:, :C_HEADS, 0][None],
            states_s[0], states_s[1], states_s[2], states_s[3],
            cmem_s[None], cnorm_s[None], cmax_s[:, :, 0][None])
```

```python
import functools

import jax
import jax.numpy as jnp
from jax import lax
from jax.experimental import pallas as pl
from jax.experimental.pallas import tpu as pltpu

F32 = jnp.float32
BF16 = jnp.bfloat16

D_MODEL = 1024
PAST_LEN = 16384
HEAD_DIM = 64
ROT_DIM = HEAD_DIM // 4
ROPE_THETA = 500000.0
BLOCK = 128
Q_HEADS = 8
KV_HEADS = 2
GROUP = Q_HEADS // KV_HEADS
QW = Q_HEADS * HEAD_DIM
KW = KV_HEADS * HEAD_DIM
GW = QW + 2 * KW
DILATIONS = (1, 1, 4, 16)
N_GROUPS = 4
C_HEADS = 4
C_HEAD_DIM = 256
MIX = 1024
DEPTH = 2
DN_ALPHA = (2 * DEPTH) ** 0.25
LN_EPS = 1e-5
MH_EPS = 1e-6
NEG_INF = -1e30
LOG2E = 1.4426950408889634
LN2 = 0.6931471805599453
LANES = 128
TPAD = 8
VMEM_LIMIT = 56 * 1024 * 1024


def _cparams(sem):
    return pltpu.CompilerParams(dimension_semantics=sem, vmem_limit_bytes=VMEM_LIMIT)


def _layer_norm_rows(v, g, b):
    mu = jnp.mean(v, axis=-1, keepdims=True)
    c = v - mu
    var = jnp.mean(c * c, axis=-1, keepdims=True)
    return c * lax.rsqrt(var + LN_EPS) * g + b


def _rope(slab, cos, sin_lo, sin_hi):
    return (slab * cos + pltpu.roll(slab, LANES - ROT_DIM // 2, 1) * sin_lo
            + pltpu.roll(slab, ROT_DIM // 2, 1) * sin_hi)


def _rope_tables(pos):
    half = ROT_DIM // 2
    inv = ROPE_THETA ** (-jnp.arange(half, dtype=F32) / half)
    ang = pos.astype(F32)[:, None] * inv[None, :]
    cos, sin = jnp.cos(ang), jnp.sin(ang)
    lane = jnp.arange(LANES) % HEAD_DIM
    f = lane % half
    cos_t = jnp.where(lane[None, :] < ROT_DIM, cos[:, f], 1.0)
    sin_lo = jnp.where(lane[None, :] < half, -sin[:, f], 0.0)
    sin_hi = jnp.where((lane[None, :] >= half) & (lane[None, :] < ROT_DIM), sin[:, f], 0.0)
    return cos_t.astype(F32), sin_lo.astype(F32), sin_hi.astype(F32)


STEP_TOKENS = 2048
X_SLABS = D_MODEL // LANES
UNITS_PER_PROJ = 4
ONES_ROWS = 16


def _rope_t(blk, cos, sin):
    half = ROT_DIM // 2
    x1, x2 = blk[:half], blk[half:ROT_DIM]
    return jnp.concatenate([x1 * cos - x2 * sin, x2 * cos + x1 * sin, blk[ROT_DIM:]], axis=0)


def _group_attn_kernel(sink_ref, *refs, has_sink, dil, tstep, nstep):
    x_refs = refs[:X_SLABS]
    (wt_ref, cos_ref, sin_ref, o_ref, lse_ref, st_ref,
     kprev_sc, vprev_sc, kv_nat_sc, o_sc, st_sc, pt_sc, bias_sc, xq_sc, ht_sc) = refs[X_SLABS:]
    j = pl.program_id(1)
    n_units = tstep // BLOCK
    is_last = j == nstep - 1

    @pl.when(j == 0)
    def _():
        kprev_sc[...] = jnp.zeros_like(kprev_sc)
        vprev_sc[...] = jnp.zeros_like(vprev_sc)

    scale = HEAD_DIM ** -0.5 * LOG2E
    kidx = lax.broadcasted_iota(jnp.int32, (2 * BLOCK, 2 * BLOCK), 0)
    qidx = lax.broadcasted_iota(jnp.int32, (2 * BLOCK, 2 * BLOCK), 1) % BLOCK
    dist = qidx + BLOCK - kidx
    band = (dist >= 0) & (dist <= BLOCK)
    bias_sc[0] = jnp.where(band, 0.0, NEG_INF)
    bias_sc[1] = jnp.where(band & (kidx >= jnp.where(j > 0, 0, BLOCK)), 0.0, NEG_INF)
    lane2 = lax.broadcasted_iota(jnp.int32, (1, 2 * BLOCK), 1)
    ones_rows = jnp.ones((ONES_ROWS, 2 * BLOCK), F32)
    zeros_half = jnp.zeros((HEAD_DIM, 2 * BLOCK), F32)
    nt = (((1,), (1,)), ((), ()))

    span = BLOCK * dil

    def unit_rows(u):
        start = (u // dil) * span + u % dil
        return pl.ds(start, BLOCK) if dil == 1 else pl.ds(start, BLOCK, stride=dil)

    def state_rows(u):
        return pl.ds(u % dil, BLOCK) if dil == 1 else pl.ds(u % dil, BLOCK, stride=dil)

    n_quads = n_units // UNITS_PER_PROJ
    k_slab = D_MODEL // UNITS_PER_PROJ

    def quad_units(qd):
        return [UNITS_PER_PROJ * qd + w for w in range(UNITS_PER_PROJ)]

    def load_x(qd):
        xq_sc[qd % 2] = jnp.concatenate(
            [jnp.concatenate([x_refs[c][0, unit_rows(u), :] for c in range(X_SLABS)], axis=1)
             for u in quad_units(qd)], axis=0).astype(BF16)

    def project_piece(qd, kp):
        ks = slice(kp * k_slab, (kp + 1) * k_slab)
        part = lax.dot_general(wt_ref[:, ks], xq_sc[qd % 2, :, ks], nt, preferred_element_type=F32)
        if kp == 0:
            ht_sc[qd % 2] = part
        else:
            ht_sc[qd % 2] += part

    load_x(0)
    for kp in range(UNITS_PER_PROJ):
        project_piece(0, kp)
    carried = {}
    for qd in range(n_quads):
        units = quad_units(qd)
        ht = ht_sc[qd % 2]
        cols = slice(qd * UNITS_PER_PROJ * BLOCK, (qd + 1) * UNITS_PER_PROJ * BLOCK)
        cos, sin = cos_ref[:, cols], sin_ref[:, cols]
        qt = [_rope_t(ht[h * HEAD_DIM:(h + 1) * HEAD_DIM], cos, sin) * scale for h in range(Q_HEADS)]
        kt = jnp.concatenate([_rope_t(ht[QW + h * HEAD_DIM:QW + (h + 1) * HEAD_DIM], cos, sin)
                              for h in range(KV_HEADS)], axis=0)
        vt = ht[QW + KW:GW]
        k_rm = kt.T

        if dil > 1 and units[0] >= n_units - dil:
            @pl.when(is_last)
            def _():
                v_rm = vt.T
                for w, u in enumerate(units):
                    kv_nat_sc[0, state_rows(u), :] = k_rm[w * BLOCK:(w + 1) * BLOCK]
                    kv_nat_sc[1, state_rows(u), :] = v_rm[w * BLOCK:(w + 1) * BLOCK]

        def scores(w, u, carried):
            lanes = slice(w * BLOCK, (w + 1) * BLOCK)
            k_cur = k_rm[w * BLOCK:(w + 1) * BLOCK].astype(BF16)
            vt_cur = vt[:, lanes].astype(BF16)
            if u >= dil:
                k_prev, vt_prev = carried[u - dil]
                mask = 0
            else:
                k_prev, vt_prev = kprev_sc[u], vprev_sc[u]
                mask = 1
            k_cat = jnp.concatenate([k_prev, k_cur], axis=0)
            wq = []
            for c in range(QW // LANES):
                top = jnp.concatenate([qt[2 * c][:, lanes], qt[2 * c + 1][:, lanes]], axis=1)
                wq.append(jnp.concatenate([top, zeros_half] if c < GROUP // 2 else [zeros_half, top], axis=0))
            wq = jnp.concatenate(wq, axis=1).astype(BF16)
            st_sc[u % 2] = jnp.dot(k_cat, wq, preferred_element_type=F32)
            return dict(w=w, u=u, lanes=lanes, k_cur=k_cur, vt_cur=vt_cur, vt_prev=vt_prev, mask=mask)

        def softmax(t):
            par = t["u"] % 2
            ms = []
            for c in range(QW // LANES):
                pc = slice(c * 2 * BLOCK, (c + 1) * 2 * BLOCK)
                st = st_sc[par, :, pc] + bias_sc[t["mask"]]
                m = jnp.max(st, axis=0, keepdims=True)
                if has_sink:
                    sink = jnp.where(lane2 < BLOCK, sink_ref[2 * c], sink_ref[2 * c + 1]) * LOG2E
                    m = jnp.maximum(m, sink)
                ms.append(m)
                pt_sc[par, :, pc] = jnp.exp2(st - m).astype(BF16)
            t["ms"] = ms

        def values(t):
            u, par, lanes = t["u"], t["u"] % 2, t["lanes"]
            lse_rows = []
            for hk in range(KV_HEADS):
                hd = slice(hk * HEAD_DIM, (hk + 1) * HEAD_DIM)
                vt_aug = jnp.concatenate(
                    [jnp.concatenate([t["vt_prev"][hd], t["vt_cur"][hd]], axis=1), ones_rows.astype(BF16)], axis=0)
                acc2 = jnp.dot(vt_aug, pt_sc[par, :, hk * 4 * BLOCK:(hk + 1) * 4 * BLOCK],
                               preferred_element_type=F32)
                for i in range(GROUP // 2):
                    c = hk * (GROUP // 2) + i
                    acc = acc2[:, i * 2 * BLOCK:(i + 1) * 2 * BLOCK]
                    m = t["ms"][c]
                    den = acc[HEAD_DIM:HEAD_DIM + 1]
                    if has_sink:
                        sink = jnp.where(lane2 < BLOCK, sink_ref[2 * c], sink_ref[2 * c + 1]) * LOG2E
                        den = den + jnp.exp2(sink - m)
                    ot = acc[:HEAD_DIM] / den
                    lse = m * LN2 + jnp.log(den)
                    lse_rows += [lse[:, :BLOCK], lse[:, BLOCK:]]
                    slab_t = jnp.concatenate([ot[:, :BLOCK], ot[:, BLOCK:]], axis=0)
                    o_sc[c, unit_rows(u), :] = slab_t.T
            lse_t = jnp.concatenate(lse_rows + [jnp.zeros((LANES - Q_HEADS, BLOCK), F32)], axis=0)
            lse_ref[0, unit_rows(u), :] = lse_t.T
            if u >= n_units - dil:
                kprev_sc[u % dil] = t["k_cur"]
                vprev_sc[u % dil] = t["vt_cur"]
                if dil == 1:
                    @pl.when(is_last)
                    def _():
                        st_ref[0, :KW, :] = kt[:, lanes]
                        st_ref[0, KW:, :] = vt[:, lanes]

        prefetch = qd + 1 < n_quads
        pending = None
        for w, u in enumerate(units):
            t = scores(w, u, carried)
            carried[u] = (t["k_cur"], t["vt_cur"])
            if prefetch and w == 0:
                load_x(qd + 1)
            if pending is not None:
                softmax(pending)
            if prefetch:
                project_piece(qd + 1, w)
            if pending is not None:
                values(pending)
            pending = t
        softmax(pending)
        values(pending)

    for c in range(QW // LANES):
        o_ref[0, :, c * LANES:(c + 1) * LANES] = o_sc[c].astype(o_ref.dtype)

    if dil > 1:
        @pl.when(is_last)
        def _():
            st_ref[0, :KW, :] = kv_nat_sc[0].T
            st_ref[0, KW:, :] = kv_nat_sc[1].T


def _group_attn_prompt(x, w_g, sinks, dil, has_sink):
    bn, s, d = x.shape
    span = BLOCK * dil
    tstep = max(STEP_TOKENS, span)
    nstep = s // tstep
    pos = (jnp.arange(s // span)[:, None, None] * span + jnp.arange(dil)[None, :, None]
           + jnp.arange(BLOCK)[None, None, :] * dil).reshape(s)
    half = ROT_DIM // 2
    inv = ROPE_THETA ** (-jnp.arange(half, dtype=F32) / half)
    ang = pos.astype(F32)[:, None] * inv[None, :]
    cos_t, sin_t = jnp.cos(ang).T, jnp.sin(ang).T
    tab_spec = pl.BlockSpec((half, tstep), lambda b, j: (0, j))
    o, lse, st = pl.pallas_call(
        functools.partial(_group_attn_kernel, has_sink=has_sink, dil=dil, tstep=tstep, nstep=nstep),
        grid=(bn, nstep),
        in_specs=([pl.BlockSpec(memory_space=pltpu.SMEM)]
                  + [pl.BlockSpec((1, tstep, LANES), functools.partial(lambda c, b, j: (b, j, c), c))
                     for c in range(X_SLABS)]
                  + [pl.BlockSpec((GW, d), lambda b, j: (0, 0)), tab_spec, tab_spec]),
        out_specs=[
            pl.BlockSpec((1, tstep, QW), lambda b, j: (b, j, 0)),
            pl.BlockSpec((1, tstep, LANES), lambda b, j: (b, j, 0)),
            pl.BlockSpec((1, 2 * KW, span), lambda b, j: (b, 0, 0)),
        ],
        out_shape=[
            jax.ShapeDtypeStruct((bn, s, QW), BF16),
            jax.ShapeDtypeStruct((bn, s, LANES), F32),
            jax.ShapeDtypeStruct((bn, 2 * KW, span), F32),
        ],
        scratch_shapes=[pltpu.VMEM((dil, BLOCK, KW), BF16), pltpu.VMEM((dil, KW, BLOCK), BF16),
                        pltpu.VMEM((2, span, LANES), F32), pltpu.VMEM((QW // LANES, tstep, LANES), F32),
                        pltpu.VMEM((2, 2 * BLOCK, Q_HEADS * BLOCK), F32),
                        pltpu.VMEM((2, 2 * BLOCK, Q_HEADS * BLOCK), BF16),
                        pltpu.VMEM((2, 2 * BLOCK, 2 * BLOCK), F32),
                        pltpu.VMEM((2, UNITS_PER_PROJ * BLOCK, d), BF16),
                        pltpu.VMEM((2, GW, UNITS_PER_PROJ * BLOCK), F32)],
        compiler_params=_cparams(("arbitrary", "arbitrary")),
        name=f"l0_group_attn_d{dil}_{'sink' if has_sink else 'nosink'}",
    )(sinks, *([x] * X_SLABS), w_g.T, cos_t, sin_t)
    st = jnp.transpose(st.reshape(bn, 2, KV_HEADS, HEAD_DIM, span), (0, 4, 1, 2, 3))
    return o.reshape(bn * s, QW), lse.reshape(bn * s, LANES), st


SUB_ROWS = 256


def _l0_out_kernel(x_ref, oa_ref, o0_ref, o1_ref, o2_ref, l0_ref, l1_ref, l2_ref,
                   wz_ref, wo_ref, g_ref, b_ref, y_ref, z_sc, mix_sc):
    tm = x_ref.shape[0]
    n_sub = tm // SUB_ROWS

    def gate_proj(i):
        r = slice(i * SUB_ROWS, (i + 1) * SUB_ROWS)
        z_sc[r, :] = jnp.dot(x_ref[r, :].astype(BF16), wz_ref[...], preferred_element_type=F32)

    def mix_stage(i):
        r = slice(i * SUB_ROWS, (i + 1) * SUB_ROWS)
        l0, l1, l2 = l0_ref[r, :], l1_ref[r, :], l2_ref[r, :]
        m = jnp.maximum(jnp.maximum(l0, l1), l2)
        e0, e1, e2 = jnp.exp(l0 - m), jnp.exp(l1 - m), jnp.exp(l2 - m)
        inv = 1.0 / (e0 + e1 + e2)
        w0, w1, w2 = e0 * inv, e1 * inv, e2 * inv
        parts = []
        for hq in range(Q_HEADS):
            c = slice(hq * HEAD_DIM, (hq + 1) * HEAD_DIM)
            bshape = (SUB_ROWS, HEAD_DIM)
            parts.append(jnp.broadcast_to(w0[:, hq:hq + 1], bshape) * o0_ref[r, c].astype(F32)
                         + jnp.broadcast_to(w1[:, hq:hq + 1], bshape) * o1_ref[r, c].astype(F32)
                         + jnp.broadcast_to(w2[:, hq:hq + 1], bshape) * o2_ref[r, c].astype(F32))
        mix = jnp.concatenate([oa_ref[r, :].astype(F32)] + parts, axis=1)
        z = z_sc[r, :]
        mix_sc[r, :] = (mix * (z * jax.nn.sigmoid(z))).astype(BF16)

    def out_stage(i):
        r = slice(i * SUB_ROWS, (i + 1) * SUB_ROWS)
        y = jnp.dot(mix_sc[r, :], wo_ref[...], preferred_element_type=F32)
        y_ref[r, :] = _layer_norm_rows(DN_ALPHA * x_ref[r, :] + y, g_ref[...], b_ref[...])

    gate_proj(0)
    for i in range(n_sub):
        if i + 1 < n_sub:
            gate_proj(i + 1)
        mix_stage(i)
        out_stage(i)


def _l0_out(x2, oa, obs, lses, wz, wo, g, b):
    n, d = x2.shape
    tm = 512
    row = lambda w: pl.BlockSpec((tm, w), lambda i: (i, 0))
    full = lambda a: pl.BlockSpec(a.shape, lambda i: (0, 0))
    return pl.pallas_call(
        _l0_out_kernel,
        grid=(n // tm,),
        in_specs=[row(d), row(QW), row(QW), row(QW), row(QW), row(LANES), row(LANES), row(LANES),
                  full(wz), full(wo), full(g), full(b)],
        out_specs=row(d),
        out_shape=jax.ShapeDtypeStruct((n, d), F32),
        scratch_shapes=[pltpu.VMEM((tm, MIX), F32), pltpu.VMEM((tm, MIX), BF16)],
        compiler_params=_cparams(("arbitrary",)),
        name="l0_out",
    )(x2, oa, *obs, *lses, wz, wo, g, b)


def _proj_kernel(x_ref, w_ref, cos_ref, slo_ref, shi_ref, h_ref, *, n_rope, n_scaled, tn):
    j = pl.program_id(0)
    h = jnp.dot(x_ref[...].astype(BF16), w_ref[...], preferred_element_type=F32)
    if n_rope:
        cos, slo, shi = cos_ref[...], slo_ref[...], shi_ref[...]
        roped = jnp.concatenate(
            [_rope(h[:, c * LANES:(c + 1) * LANES], cos, slo, shi) for c in range(tn // LANES)], axis=1)
        h = jnp.where(j < n_rope, roped, h)
        h = h * jnp.where(j < n_scaled, HEAD_DIM ** -0.5, 1.0)
    h_ref[...] = h


def _proj(x2, w, tabs, tn, n_rope=0, n_scaled=0, ncol=None):
    n, d = x2.shape
    ncol = w.shape[1] if ncol is None else ncol
    tab_spec = pl.BlockSpec((n, LANES), lambda j: (0, 0))
    return pl.pallas_call(
        functools.partial(_proj_kernel, n_rope=n_rope, n_scaled=n_scaled, tn=tn),
        grid=(ncol // tn,),
        in_specs=[pl.BlockSpec((n, d), lambda j: (0, 0)),
                  pl.BlockSpec((d, tn), lambda j: (0, j)),
                  tab_spec, tab_spec, tab_spec],
        out_specs=pl.BlockSpec((n, tn), lambda j: (0, j)),
        out_shape=jax.ShapeDtypeStruct((n, ncol), F32),
        compiler_params=_cparams(("arbitrary",)),
        name=f"decode_proj_{ncol}",
    )(x2, w, *tabs)


def _dec_prepare(h_ref, rows, g):
    lo_half = lax.broadcasted_iota(jnp.int32, (TPAD, LANES), 1) < HEAD_DIM
    k_base = N_GROUPS * QW
    v_base = k_base + N_GROUPS * KW
    blocks = []
    for c in range(QW // LANES):
        slab = h_ref[rows, g * QW + c * LANES: g * QW + (c + 1) * LANES]
        swapped = pltpu.roll(slab, HEAD_DIM, 1)
        if c < QW // LANES // 2:
            blocks += [jnp.where(lo_half, slab, 0.0), jnp.where(lo_half, swapped, 0.0)]
        else:
            blocks += [jnp.where(lo_half, 0.0, swapped), jnp.where(lo_half, 0.0, slab)]
    qm = jnp.concatenate(blocks, axis=0).astype(BF16)
    k_new = h_ref[rows, k_base + g * KW: k_base + (g + 1) * KW]
    v_new = h_ref[rows, v_base + g * KW: v_base + (g + 1) * KW]
    zpad = jnp.zeros((LANES - TPAD, 2 * KW), F32)
    kv_new = jnp.concatenate([jnp.concatenate([k_new, v_new], axis=1), zpad], axis=0)
    return qm, kv_new


def _dec_scores(qm, kv_new, cache_ref, sc_ref, s_i):
    nkeys = cache_ref.shape[2]
    nt = (((1,), (1,)), ((), ()))
    sc_ref[s_i, :, :nkeys] = jnp.dot(qm, cache_ref[s_i, :KW, :].astype(BF16), preferred_element_type=F32)
    sc_ref[s_i, :, nkeys:] = lax.dot_general(qm, kv_new[:, :KW].astype(BF16), nt, preferred_element_type=F32)


def _dec_shift(kv_new, cache_ref, new_ref, s_i, t_new):
    nkeys = cache_ref.shape[2]
    lane_k = lax.broadcasted_iota(jnp.int32, (KW, LANES), 1)
    new_cols = pltpu.roll(kv_new.T, LANES - t_new, 1)
    for half in range(2):
        hrows = slice(half * KW, (half + 1) * KW)
        shifted = pltpu.roll(cache_ref[s_i, hrows, :], nkeys - t_new, 1)
        new_ref[s_i, hrows, :] = shifted
        new_ref[s_i, hrows, nkeys - LANES:] = jnp.where(lane_k >= LANES - t_new, new_cols[hrows],
                                                        shifted[:, nkeys - LANES:])


def _dec_values(kv_new, cache_ref, sc_ref, s_i, rows, dil, t_new, sink_ref, o_ref, l_ref):
    nrow = Q_HEADS * TPAD
    nkeys = cache_ref.shape[2]
    ncol = nkeys + LANES
    nt = (((1,), (1,)), ((), ()))
    lane8 = lax.broadcasted_iota(jnp.int32, (TPAD, LANES), 1)
    lo_half = lane8 < HEAD_DIM
    tok = lax.broadcasted_iota(jnp.int32, (nrow, ncol), 0) % TPAD
    col = lax.broadcasted_iota(jnp.int32, (nrow, ncol), 1)
    new_j = col - nkeys
    valid = (((col < nkeys) & (((col - tok) & (dil - 1)) == 0) & (col >= tok))
             | ((col >= nkeys) & (((tok - new_j) & (dil - 1)) == 0) & (new_j <= tok) & (new_j < t_new)))
    s_all = jnp.where(valid, sc_ref[s_i], NEG_INF)
    m = jnp.max(s_all, axis=-1, keepdims=True)
    if sink_ref is not None:
        sink = sink_ref[:, 0:1]
        m = jnp.maximum(m, sink)
    p = jnp.exp(s_all - m)
    den = jnp.sum(p, axis=-1, keepdims=True)
    if sink_ref is not None:
        den = den + jnp.exp(sink - m)
    pb = p.astype(BF16)
    o = (lax.dot_general(pb[:, :nkeys], cache_ref[s_i, KW:, :].astype(BF16), nt, preferred_element_type=F32)
         + jnp.dot(pb[:, nkeys:], kv_new[:, KW:].astype(BF16), preferred_element_type=F32)) / den
    for c in range(QW // LANES):
        ev = o[(2 * c) * TPAD:(2 * c + 1) * TPAD]
        od = o[(2 * c + 1) * TPAD:(2 * c + 2) * TPAD]
        if c < QW // LANES // 2:
            slab = jnp.where(lo_half, ev, pltpu.roll(od, HEAD_DIM, 1))
        else:
            slab = jnp.where(lo_half, pltpu.roll(ev, HEAD_DIM, 1), od)
        o_ref[rows, c * LANES:(c + 1) * LANES] = slab.astype(o_ref.dtype)
    if l_ref is not None:
        lse = m + jnp.log(den)
        lse_acc = jnp.zeros((TPAD, LANES), F32)
        for hq in range(Q_HEADS):
            lse_acc = jnp.where(lane8 == hq, lse[hq * TPAD:(hq + 1) * TPAD], lse_acc)
        l_ref[rows, :] = lse_acc


def _cache_rows_minor(c):
    nseq, rows = c.shape[1], c.shape[2]
    return jnp.transpose(c[0], (0, 2, 3, 4, 1)).reshape(nseq, 2 * KW, rows)


def _cache_rows_major(c):
    nseq, _, rows = c.shape
    return jnp.transpose(c.reshape(nseq, 2, KV_HEADS, HEAD_DIM, rows), (0, 4, 1, 2, 3))[None]


def _mlstm_chunk(q, k, v, gates, hh, c_mem, c_norm, m_prev, n_valid):
    L = q.shape[0]
    ig_col = gates[:, hh:hh + 1]
    fg = gates[:, C_HEADS + hh:C_HEADS + hh + 1]
    lf_col = jnp.minimum(fg, 0.0) - jnp.log(1.0 + jnp.exp(-jnp.abs(fg)))
    ri = lax.broadcasted_iota(jnp.int32, (L, L), 0)
    ci = lax.broadcasted_iota(jnp.int32, (L, L), 1)
    tril = ri >= ci
    eye = ri == ci
    lf_row = jnp.sum(jnp.where(eye, lf_col, 0.0), axis=0, keepdims=True)
    ig_row = jnp.sum(jnp.where(eye, ig_col, 0.0), axis=0, keepdims=True)
    b_col = jnp.sum(jnp.where(tril, lf_row, 0.0), axis=1, keepdims=True)
    b_row = jnp.sum(jnp.where(ri <= ci, lf_col, 0.0), axis=0, keepdims=True)
    dmat = jnp.where(tril, b_col - b_row + ig_row, NEG_INF)
    inter = b_col + m_prev
    m = jnp.maximum(inter, jnp.max(dmat, axis=1, keepdims=True))
    pw = jnp.exp(dmat - m)
    qb, kb, vb = q.astype(BF16), k.astype(BF16), v.astype(BF16)
    sc = lax.dot_general(qb, kb, (((1,), (1,)), ((), ())), preferred_element_type=F32) * pw
    carry_w = jnp.exp(inter - m)
    num = (jnp.dot(sc.astype(BF16), vb, preferred_element_type=F32)
           + carry_w * jnp.dot(qb, c_mem.astype(BF16), preferred_element_type=F32))
    den = jnp.sum(sc, axis=1, keepdims=True) + carry_w * jnp.sum(q * c_norm, axis=1, keepdims=True)
    h = num / jnp.maximum(jnp.abs(den), jnp.exp(-m))
    last = n_valid - 1
    m_new = m[last:last + 1, :]
    b_last = b_col[last:last + 1, :]
    ws = jnp.exp(b_last - b_col + ig_col - m_new)
    if n_valid < L:
        ws = jnp.where(lax.broadcasted_iota(jnp.int32, (L, 1), 0) < n_valid, ws, 0.0)
    decay = jnp.exp(b_last + m_prev - m_new)
    wk = ws * k
    c_mem_new = decay * c_mem + lax.dot_general(
        wk.astype(BF16), vb, (((0,), (0,)), ((), ())), preferred_element_type=F32)
    c_norm_new = decay * c_norm + jnp.sum(wk, axis=0, keepdims=True)
    return h, c_mem_new, c_norm_new, m_new


AUG = LANES
PROJ_COLS = 256


def _l1_prompt_kernel(x_ref, w_ref, wg_ref, bias_ref, sink_ref, hs_ref, ca_ref, c0_ref, c1_ref, c2_ref,
                      h_ref, cmem_ref, cnorm_ref, cmax_ref,
                      oa_ref, o0_ref, o1_ref, o2_ref, l0_ref, l1_ref, l2_ref, na_ref, n0_ref, n1_ref, n2_ref,
                      caug_sc, m_sc, s_sc, lhs_sc, proj_a, proj_b, xb_sc, sa_sc, s0_sc, s1_sc, s2_sc,
                      *, tm, nblk, dec_sb, t_new):
    i = pl.program_id(0)
    blk = i - 1

    @pl.when(i == 0)
    def _():
        proj_b[...] = jnp.zeros_like(proj_b)

    @pl.when(blk % nblk == 0)
    def _():
        caug_sc[...] = jnp.zeros_like(caug_sc)
        m_sc[...] = jnp.zeros_like(m_sc)

    ri = lax.broadcasted_iota(jnp.int32, (BLOCK, BLOCK), 0)
    ci = lax.broadcasted_iota(jnp.int32, (BLOCK, BLOCK), 1)
    tril = ri >= ci
    upper = (ri <= ci).astype(F32)
    ones_blk = jnp.ones((BLOCK, AUG), BF16)
    kscale = C_HEAD_DIM ** -0.5
    nt = (((1,), (1,)), ((), ()))
    ncol = w_ref.shape[1]
    slabs = [(c0, c0 + PROJ_COLS) for c0 in range(0, ncol, PROJ_COLS)] + [(ncol, ncol + LANES)]
    n_chunks = tm // BLOCK

    def body(src, dst):
        xb_sc[...] = x_ref[...].astype(BF16)
        pieces = iter(slabs)

        def project_piece():
            c = next(pieces, None)
            if c is not None:
                w_piece = w_ref[:, c[0]:c[1]] if c[0] < ncol else wg_ref[...]
                dst[:, c[0]:c[1]] = jnp.dot(xb_sc[...], w_piece, preferred_element_type=F32)

        g_t = (src[:, 3 * MIX:] + bias_ref[...]).T[0:TPAD]
        lf_t = jnp.minimum(g_t, 0.0) - jnp.log(1.0 + jnp.exp(-jnp.abs(g_t)))

        def score_stage(c):
            rows = slice(c * BLOCK, (c + 1) * BLOCK)
            qs, ks, vs = [], [], []
            for hh in range(C_HEADS):
                q = src[rows, hh * C_HEAD_DIM:(hh + 1) * C_HEAD_DIM]
                k = src[rows, MIX + hh * C_HEAD_DIM: MIX + (hh + 1) * C_HEAD_DIM] * kscale
                v = src[rows, 2 * MIX + hh * C_HEAD_DIM: 2 * MIX + (hh + 1) * C_HEAD_DIM]
                qs.append(q)
                ks.append(k)
                vs.append(v.astype(BF16))
                s_sc[c % 2, hh] = lax.dot_general(q.astype(BF16), k.astype(BF16), nt, preferred_element_type=F32)
            return dict(rows=rows, par=c % 2, qs=qs, ks=ks, vs=vs)

        def gate_stage(t, m_prevs):
            rows, par = t["rows"], t["par"]
            b_t = jnp.dot(lf_t[:, rows], upper, preferred_element_type=F32, precision=lax.Precision.HIGHEST)
            stats = []
            for hh in range(C_HEADS):
                ig_row = g_t[hh:hh + 1, rows]
                lf_row = lf_t[C_HEADS + hh:C_HEADS + hh + 1, rows]
                b_row = b_t[C_HEADS + hh:C_HEADS + hh + 1, :]
                m_prev = m_prevs[hh]
                b_col = jnp.sum(jnp.where(tril, lf_row, 0.0), axis=1, keepdims=True)
                dmat = jnp.where(tril, b_col - (b_row - ig_row), NEG_INF)
                inter = b_col + m_prev
                m = jnp.maximum(inter, jnp.max(dmat, axis=1, keepdims=True))
                sc = s_sc[par, hh] * jnp.exp(dmat - m)
                carry_w = jnp.exp(inter - m)
                lhs_sc[par, hh, :, :BLOCK] = sc.astype(BF16)
                lhs_sc[par, hh, :, BLOCK:] = (carry_w * t["qs"][hh]).astype(BF16)
                m_new = m[BLOCK - 1:BLOCK, :]
                b_last = b_col[BLOCK - 1:BLOCK, :]
                ws_row = jnp.exp(b_last - b_row + ig_row - m_new)
                decay = jnp.exp(b_last + m_prev - m_new)
                stats.append((m, m_new, ws_row, decay))
            t["stats"] = stats

        def memory_stage(t):
            rows, par = t["rows"], t["par"]
            for hh in range(C_HEADS):
                cols = slice(hh * C_HEAD_DIM, (hh + 1) * C_HEAD_DIM)
                m, m_new, ws_row, decay = t["stats"][hh]
                caug = caug_sc[hh]
                v_aug = jnp.concatenate([t["vs"][hh], ones_blk], axis=1)
                rhs = jnp.concatenate([v_aug, caug.astype(BF16)], axis=0)
                res = jnp.dot(lhs_sc[par, hh], rhs, preferred_element_type=F32)
                den = jnp.maximum(jnp.abs(res[:, C_HEAD_DIM:]), jnp.exp(-m))
                h_ref[rows, cols] = res[:, :C_HEAD_DIM] / jnp.concatenate([den, den], axis=1)
                wk_t = (t["ks"][hh].T * ws_row).astype(BF16)
                caug_sc[hh] = decay * caug + jnp.dot(wk_t, v_aug, preferred_element_type=F32)

        m_prevs = [m_sc[hh:hh + 1, 0:1] for hh in range(C_HEADS)]
        ts = [score_stage(0), score_stage(1)]
        caches = (ca_ref, c0_ref, c1_ref, c2_ref)
        new_caches = (na_ref, n0_ref, n1_ref, n2_ref)
        o_refs = (oa_ref, o0_ref, o1_ref, o2_ref)
        l_refs = (None, l0_ref, l1_ref, l2_ref)
        sc_refs = (sa_sc, s0_sc, s1_sc, s2_sc)
        dec = []
        for g in range(N_GROUPS):
            for s_i in range(dec_sb):
                rows = slice(s_i * TPAD, (s_i + 1) * TPAD)
                qm, kv_new = _dec_prepare(hs_ref, rows, g)
                _dec_scores(qm, kv_new, caches[g], sc_refs[g], s_i)
                dec.append((g, s_i, rows, kv_new))
        n_dec = len(dec)
        shifts = iter(dec)

        def shift_items(k):
            for _ in range(k):
                d = next(shifts, None)
                if d is not None:
                    _dec_shift(d[3], caches[d[0]], new_caches[d[0]], d[1], t_new)
                    project_piece()

        project_piece()
        for c in range(n_chunks):
            gate_stage(ts[c], m_prevs)
            m_prevs = [st[1] for st in ts[c]["stats"]]
            project_piece()
            if c + 2 < n_chunks:
                ts.append(score_stage(c + 2))
                project_piece()
            shift_items(n_dec // (2 * n_chunks))
            if c >= 1:
                memory_stage(ts[c - 1])
                project_piece()
        memory_stage(ts[n_chunks - 1])
        project_piece()
        shift_items(n_dec)
        for g, s_i, rows, kv_new in dec:
            _dec_values(kv_new, caches[g], sc_refs[g], s_i, rows, DILATIONS[g], t_new,
                        sink_ref if g == 0 else None, o_refs[g], l_refs[g])
            project_piece()
        for _ in slabs:
            project_piece()
        for hh in range(C_HEADS):
            m_sc[hh:hh + 1, :] = jnp.broadcast_to(m_prevs[hh], (1, LANES))

    @pl.when(i % 2 == 0)
    def _():
        body(proj_b, proj_a)

    @pl.when(i % 2 == 1)
    def _():
        body(proj_a, proj_b)

    @pl.when((blk >= 0) & (blk % nblk == nblk - 1))
    def _():
        for hh in range(C_HEADS):
            cmem_ref[0, hh] = caug_sc[hh, :, :C_HEAD_DIM]
            cnorm_ref[0, hh:hh + 1, :] = caug_sc[hh, :, C_HEAD_DIM:].T[0:1, :]
        cmax_ref[0] = m_sc[...]


def _l1_prompt(x, w1b, w_gates, bias_row, hs, caches, sink_rows, t_new):
    bn, s, d = x.shape
    cts = [_cache_rows_minor(c) for c in caches]
    nseq, ng = cts[0].shape[0], len(cts)
    dec_sb = 2
    ndec = nseq // dec_sb
    tm = (bn * s) // ndec
    nblk = s // tm
    ntot = bn * nblk
    ncol = 3 * MIX + LANES
    state_blk = lambda i: (jnp.maximum(i - 1, 0) // nblk, 0, 0)
    dec_blk = lambda i: jnp.minimum(i, ndec - 1)
    drow = lambda w: pl.BlockSpec((dec_sb * TPAD, w), lambda i: (dec_blk(i), 0))
    cspec = lambda c: pl.BlockSpec((dec_sb,) + c.shape[1:], lambda i: (dec_blk(i), 0, 0))
    outs = pl.pallas_call(
        functools.partial(_l1_prompt_kernel, tm=tm, nblk=nblk, dec_sb=dec_sb, t_new=t_new),
        grid=(ntot + 1,),
        in_specs=[pl.BlockSpec((tm, d), lambda i: (jnp.minimum(i, ntot - 1), 0)),
                  pl.BlockSpec((d, 3 * MIX), lambda i: (0, 0)),
                  pl.BlockSpec((d, LANES), lambda i: (0, 0)),
                  pl.BlockSpec((1, LANES), lambda i: (0, 0)),
                  pl.BlockSpec(sink_rows.shape, lambda i: (0, 0)),
                  drow(hs.shape[1])] + [cspec(c) for c in cts],
        out_specs=([pl.BlockSpec((tm, MIX), lambda i: (jnp.maximum(i - 1, 0), 0)),
                    pl.BlockSpec((1, C_HEADS, C_HEAD_DIM, C_HEAD_DIM), lambda i: state_blk(i) + (0,)),
                    pl.BlockSpec((1, C_HEADS, C_HEAD_DIM), state_blk),
                    pl.BlockSpec((1, TPAD, LANES), state_blk)]
                   + [drow(QW)] * ng + [drow(LANES)] * (ng - 1) + [cspec(c) for c in cts]),
        out_shape=([jax.ShapeDtypeStruct((bn * s, MIX), F32),
                    jax.ShapeDtypeStruct((bn, C_HEADS, C_HEAD_DIM, C_HEAD_DIM), F32),
                    jax.ShapeDtypeStruct((bn, C_HEADS, C_HEAD_DIM), F32),
                    jax.ShapeDtypeStruct((bn, TPAD, LANES), F32)]
                   + [jax.ShapeDtypeStruct((nseq * TPAD, QW), BF16)] * ng
                   + [jax.ShapeDtypeStruct((nseq * TPAD, LANES), F32)] * (ng - 1)
                   + [jax.ShapeDtypeStruct(c.shape, F32) for c in cts]),
        scratch_shapes=([pltpu.VMEM((C_HEADS, C_HEAD_DIM, C_HEAD_DIM + AUG), F32),
                         pltpu.VMEM((TPAD, LANES), F32),
                         pltpu.VMEM((2, C_HEADS, BLOCK, BLOCK), F32),
                         pltpu.VMEM((2, C_HEADS, BLOCK, BLOCK + C_HEAD_DIM), BF16),
                         pltpu.VMEM((tm, ncol), F32), pltpu.VMEM((tm, ncol), F32),
                         pltpu.VMEM((tm, d), BF16)]
                        + [pltpu.VMEM((dec_sb, Q_HEADS * TPAD, c.shape[2] + LANES), F32) for c in cts]),
        compiler_params=_cparams(("arbitrary",)),
        name="l1_prompt_mlstm_dec",
    )(x.reshape(bn * s, d), w1b, w_gates, bias_row, sink_rows, hs, *cts)
    return (outs[:4], outs[4:4 + ng], outs[4 + ng:3 + 2 * ng],
            [_cache_rows_major(c) for c in outs[3 + 2 * ng:]])


def _l1_out_kernel(x_ref, h_ref, wog_ref, wz_ref, mhg_ref, wo_ref, g_ref, b_ref, *refs, dec_sb, t_new):
    if dec_sb:
        (p_ref, pg_ref, bias_ref, cm_ref, cn_ref, cx_ref, y_ref, hd_ref, cmo_ref, cno_ref, cxo_ref,
         oz_sc, mix_sc) = refs
    else:
        y_ref, oz_sc, mix_sc = refs
    tm = x_ref.shape[0]
    n_sub = tm // SUB_ROWS

    def gate_proj(i):
        r = slice(i * SUB_ROWS, (i + 1) * SUB_ROWS)
        xb = x_ref[r, :].astype(BF16)
        oz_sc[r, :MIX] = jnp.dot(xb, wog_ref[...], preferred_element_type=F32)
        oz_sc[r, MIX:] = jnp.dot(xb, wz_ref[...], preferred_element_type=F32)

    def mix_stage(i, between):
        r = slice(i * SUB_ROWS, (i + 1) * SUB_ROWS)
        for hh in range(C_HEADS):
            cols = slice(hh * C_HEAD_DIM, (hh + 1) * C_HEAD_DIM)
            hg = jax.nn.sigmoid(oz_sc[r, cols]) * h_ref[r, cols]
            mu = jnp.mean(hg, axis=-1, keepdims=True)
            c = hg - mu
            var = jnp.mean(c * c, axis=-1, keepdims=True)
            hn = c * lax.rsqrt(var + MH_EPS) * mhg_ref[:, cols]
            z = oz_sc[r, MIX + hh * C_HEAD_DIM:MIX + (hh + 1) * C_HEAD_DIM]
            mix_sc[r, cols] = (hn * (z * jax.nn.sigmoid(z))).astype(BF16)
            between()

    def out_stage(i):
        r = slice(i * SUB_ROWS, (i + 1) * SUB_ROWS)
        y = jnp.dot(mix_sc[r, :], wo_ref[...], preferred_element_type=F32)
        y_ref[r, :] = _layer_norm_rows(DN_ALPHA * x_ref[r, :] + y, g_ref[...], b_ref[...])

    kscale = C_HEAD_DIM ** -0.5
    chunks = [(s_i, hh) for s_i in range(dec_sb) for hh in range(C_HEADS)]
    todo = iter(chunks)

    def decode_chunks(k):
        for _ in range(k):
            it = next(todo, None)
            if it is None:
                return
            s_i, hh = it
            rows = slice(s_i * TPAD, (s_i + 1) * TPAD)
            cols = slice(hh * C_HEAD_DIM, (hh + 1) * C_HEAD_DIM)
            gates = pg_ref[rows, :] + bias_ref[...]
            q = p_ref[rows, cols]
            k_ = p_ref[rows, MIX + hh * C_HEAD_DIM: MIX + (hh + 1) * C_HEAD_DIM] * kscale
            v = p_ref[rows, 2 * MIX + hh * C_HEAD_DIM: 2 * MIX + (hh + 1) * C_HEAD_DIM]
            h, c_new, n_new, m_new = _mlstm_chunk(
                q, k_, v, gates, hh, cm_ref[s_i, hh], cn_ref[s_i, hh:hh + 1, :],
                cx_ref[s_i, hh:hh + 1, 0:1], t_new)
            hd_ref[rows, cols] = h
            cmo_ref[s_i, hh] = c_new
            cno_ref[s_i, hh:hh + 1, :] = n_new
            cxo_ref[s_i, hh:hh + 1, :] = jnp.broadcast_to(m_new, (1, LANES))

    per_slot = -(-len(chunks) // (n_sub * (C_HEADS + 1)))
    gate_proj(0)
    for i in range(n_sub):
        if i + 1 < n_sub:
            gate_proj(i + 1)
        mix_stage(i, lambda: decode_chunks(per_slot))
        out_stage(i)
        decode_chunks(per_slot)
    decode_chunks(len(chunks))


def _l1_out(x2, h2, w1b, mhg, wo, g, b, dec=None):
    n, d = x2.shape
    tm = 512
    row = lambda w: pl.BlockSpec((tm, w), lambda i: (i, 0))
    full = lambda a: pl.BlockSpec(a.shape, lambda i: (0, 0))
    wcol = lambda j: pl.BlockSpec((d, MIX), lambda i: (0, j))
    in_specs = [row(d), row(MIX), wcol(3), wcol(4), full(mhg), full(wo), full(g), full(b)]
    out_specs = [row(d)]
    out_shape = [jax.ShapeDtypeStruct((n, d), F32)]
    args = [x2, h2, w1b, w1b, mhg, wo, g, b]
    dec_sb, t_new = 0, 0
    if dec is not None:
        proj, proj_g, bias_row, c_mem, c_norm, c_max_pad, t_new = dec
        nseq = c_mem.shape[0]
        dec_sb = nseq // (n // tm)
        sblk = lambda a: pl.BlockSpec((dec_sb,) + a.shape[1:], lambda i: (i,) + (0,) * (a.ndim - 1))
        in_specs += [pl.BlockSpec((dec_sb * TPAD, proj.shape[1]), lambda i: (i, 0)),
                     pl.BlockSpec((dec_sb * TPAD, LANES), lambda i: (i, 0)), full(bias_row),
                     sblk(c_mem), sblk(c_norm), sblk(c_max_pad)]
        out_specs += [pl.BlockSpec((dec_sb * TPAD, MIX), lambda i: (i, 0)), sblk(c_mem), sblk(c_norm), sblk(c_max_pad)]
        out_shape += [jax.ShapeDtypeStruct((nseq * TPAD, MIX), F32), jax.ShapeDtypeStruct(c_mem.shape, F32),
                      jax.ShapeDtypeStruct(c_norm.shape, F32), jax.ShapeDtypeStruct(c_max_pad.shape, F32)]
        args += [proj, proj_g, bias_row, c_mem, c_norm, c_max_pad]
    outs = pl.pallas_call(
        functools.partial(_l1_out_kernel, dec_sb=dec_sb, t_new=t_new),
        grid=(n // tm,),
        in_specs=in_specs,
        out_specs=out_specs,
        out_shape=out_shape,
        scratch_shapes=[pltpu.VMEM((tm, 2 * MIX), F32), pltpu.VMEM((tm, MIX), BF16)],
        compiler_params=_cparams(("arbitrary",)),
        name="l1_out_dec" if dec is not None else "l1_out",
    )(*args)
    return outs[0] if dec is None else outs


def _group_cols(g):
    if g == 0:
        return [(0, QW), (QW, QW + KW), (QW + KW, GW)]
    b = g - 1
    qb0 = GW
    kb0 = GW + 3 * QW
    vb0 = kb0 + 3 * KW
    return [(qb0 + b * QW, qb0 + (b + 1) * QW), (kb0 + b * KW, kb0 + (b + 1) * KW),
            (vb0 + b * KW, vb0 + (b + 1) * KW)]


def kernel(x_prompt, x_sample, cache_a_kv, cache_b0_kv, cache_b1_kv, cache_b2_kv, state_c_mem,
           state_c_norm, state_c_max, w_in0, sinks0, w_out0, w_in1, b_gates1, mh_norm1, w_out1,
           ln_g, ln_b):
    bn, s, d = x_prompt.shape
    nseq, t_new, _ = x_sample.shape
    w0 = w_in0[0]
    z0 = 4 * GW
    w_groups = [jnp.concatenate([w0[:, a:b] for a, b in _group_cols(g)], axis=1).astype(BF16)
                for g in range(N_GROUPS)]
    wz0 = w0[:, z0:].astype(BF16)
    wo0 = w_out0[0].astype(BF16)
    sinks = sinks0[0].astype(F32)
    g0, b0 = ln_g[0][None, :], ln_b[0][None, :]
    g1, b1 = ln_g[1][None, :], ln_b[1][None, :]

    outs, lses, states_p = [], [], []
    for g in range(N_GROUPS):
        o, lse, st = _group_attn_prompt(x_prompt, w_groups[g], sinks, DILATIONS[g], g == 0)
        outs.append(o)
        lses.append(lse)
        states_p.append(st)
    xp2 = x_prompt.reshape(bn * s, d)
    y0p = _l0_out(xp2, outs[0], outs[1:], lses[1:], wz0, wo0, g0, b0)

    xs2 = jnp.pad(x_sample, ((0, 0), (0, TPAD - t_new), (0, 0))).reshape(nseq * TPAD, d)
    w_dec = jnp.concatenate(
        [w_groups[g][:, :QW] for g in range(N_GROUPS)] + [w_groups[g][:, QW:QW + KW] for g in range(N_GROUPS)]
        + [w_groups[g][:, QW + KW:] for g in range(N_GROUPS)], axis=1)
    tabs_s = tuple(jnp.tile(t, (nseq, 1)) for t in _rope_tables(PAST_LEN + jnp.arange(TPAD)))
    tn = 512
    hs = _proj(xs2, w_dec, tabs_s, tn, n_rope=(N_GROUPS * (QW + KW)) // tn, n_scaled=(N_GROUPS * QW) // tn)
    sink_rows = jnp.broadcast_to(jnp.repeat(sinks, TPAD)[:, None], (Q_HEADS * TPAD, LANES))

    w1b = w_in1[0].astype(BF16)
    w_gates = jnp.pad(w1b[:, 5 * MIX:], ((0, 0), (0, LANES - 2 * C_HEADS)))
    wo1 = w_out1[0].astype(BF16)
    bias_row = jnp.pad(b_gates1[0].astype(F32), (0, LANES - 2 * C_HEADS))[None, :]
    mhg = mh_norm1[0][None, :]

    (hp, cmem_p, cnorm_p, cmax_p), dec_o, dec_l, states_s = _l1_prompt(
        y0p.reshape(bn, s, d), w1b, w_gates, bias_row, hs,
        (cache_a_kv, cache_b0_kv, cache_b1_kv, cache_b2_kv), sink_rows, t_new)

    y0s = _l0_out(xs2, dec_o[0], dec_o[1:], dec_l, wz0, wo0, g0, b0)

    no_tabs = tuple(jnp.zeros((nseq * TPAD, LANES), F32) for _ in range(3))
    ps = _proj(y0s, w1b, no_tabs, 512, ncol=3 * MIX)
    ps_g = _proj(y0s, w_gates, no_tabs, LANES)
    cmax_pad = jnp.broadcast_to(state_c_max[0][:, :, None], (nseq, C_HEADS, LANES))
    y1p, hsd, cmem_s, cnorm_s, cmax_s = _l1_out(
        y0p, hp, w1b, mhg, wo1, g1, b1,
        dec=(ps, ps_g, bias_row, state_c_mem[0], state_c_norm[0], cmax_pad, t_new))
    y1s = _l1_out(y0s, hsd, w1b, mhg, wo1, g1, b1)

    y_prompt = y1p.reshape(bn, s, d)
    y_sample = y1s.reshape(nseq, TPAD, d)[:, :t_new]
    return (y_prompt, y_sample,
            states_p[0][None], states_p[1][None], states_p[2][None], states_p[3][None],
            cmem_p[None], cnorm_p[None], cmax_p[:, :C_HEADS, 0][None],
            states_s[0], states_s[1], states_s[2], states_s[3],
            cmem_s[None], cnorm_s[None], cmax_s[:, :, 0][None])
```

```python
import functools

import jax
import jax.numpy as jnp
from jax import lax
from jax.experimental import pallas as pl
from jax.experimental.pallas import tpu as pltpu

F32 = jnp.float32
BF16 = jnp.bfloat16

D_MODEL = 1024
PAST_LEN = 16384
HEAD_DIM = 64
ROT_DIM = HEAD_DIM // 4
ROPE_THETA = 500000.0
BLOCK = 128
Q_HEADS = 8
KV_HEADS = 2
GROUP = Q_HEADS // KV_HEADS
QW = Q_HEADS * HEAD_DIM
KW = KV_HEADS * HEAD_DIM
GW = QW + 2 * KW
DILATIONS = (1, 1, 4, 16)
N_GROUPS = 4
C_HEADS = 4
C_HEAD_DIM = 256
MIX = 1024
DEPTH = 2
DN_ALPHA = (2 * DEPTH) ** 0.25
LN_EPS = 1e-5
MH_EPS = 1e-6
NEG_INF = -1e30
LOG2E = 1.4426950408889634
LN2 = 0.6931471805599453
LANES = 128
TPAD = 8
VMEM_LIMIT = 56 * 1024 * 1024


def _cparams(sem):
    return pltpu.CompilerParams(dimension_semantics=sem, vmem_limit_bytes=VMEM_LIMIT)


def _layer_norm_rows(v, g, b):
    mu = jnp.mean(v, axis=-1, keepdims=True)
    c = v - mu
    var = jnp.mean(c * c, axis=-1, keepdims=True)
    return c * lax.rsqrt(var + LN_EPS) * g + b


def _rope(slab, cos, sin_lo, sin_hi):
    return (slab * cos + pltpu.roll(slab, LANES - ROT_DIM // 2, 1) * sin_lo
            + pltpu.roll(slab, ROT_DIM // 2, 1) * sin_hi)


def _rope_tables(pos):
    half = ROT_DIM // 2
    inv = ROPE_THETA ** (-jnp.arange(half, dtype=F32) / half)
    ang = pos.astype(F32)[:, None] * inv[None, :]
    cos, sin = jnp.cos(ang), jnp.sin(ang)
    lane = jnp.arange(LANES) % HEAD_DIM
    f = lane % half
    cos_t = jnp.where(lane[None, :] < ROT_DIM, cos[:, f], 1.0)
    sin_lo = jnp.where(lane[None, :] < half, -sin[:, f], 0.0)
    sin_hi = jnp.where((lane[None, :] >= half) & (lane[None, :] < ROT_DIM), sin[:, f], 0.0)
    return cos_t.astype(F32), sin_lo.astype(F32), sin_hi.astype(F32)


STEP_TOKENS = 2048
X_SLABS = D_MODEL // LANES
UNITS_PER_PROJ = 4
ONES_ROWS = 16


def _rope_t(blk, cos, sin):
    half = ROT_DIM // 2
    x1, x2 = blk[:half], blk[half:ROT_DIM]
    return jnp.concatenate([x1 * cos - x2 * sin, x2 * cos + x1 * sin, blk[ROT_DIM:]], axis=0)


def _group_attn_kernel(sink_ref, *refs, has_sink, dil, tstep, nstep):
    x_refs = refs[:X_SLABS]
    (wt_ref, cos_ref, sin_ref, o_ref, lse_ref, st_ref,
     kprev_sc, vprev_sc, kv_nat_sc, o_sc, st_sc, pt_sc, bias_sc, xq_sc, ht_sc) = refs[X_SLABS:]
    j = pl.program_id(1)
    n_units = tstep // BLOCK
    is_last = j == nstep - 1

    @pl.when(j == 0)
    def _():
        kprev_sc[...] = jnp.zeros_like(kprev_sc)
        vprev_sc[...] = jnp.zeros_like(vprev_sc)

    scale = HEAD_DIM ** -0.5 * LOG2E
    kidx = lax.broadcasted_iota(jnp.int32, (2 * BLOCK, 2 * BLOCK), 0)
    qidx = lax.broadcasted_iota(jnp.int32, (2 * BLOCK, 2 * BLOCK), 1) % BLOCK
    dist = qidx + BLOCK - kidx
    band = (dist >= 0) & (dist <= BLOCK)
    bias_sc[0] = jnp.where(band, 0.0, NEG_INF)
    bias_sc[1] = jnp.where(band & (kidx >= jnp.where(j > 0, 0, BLOCK)), 0.0, NEG_INF)
    lane2 = lax.broadcasted_iota(jnp.int32, (1, 2 * BLOCK), 1)
    ones_rows = jnp.ones((ONES_ROWS, 2 * BLOCK), F32)
    zeros_half = jnp.zeros((HEAD_DIM, 2 * BLOCK), F32)
    nt = (((1,), (1,)), ((), ()))

    span = BLOCK * dil

    def unit_rows(u):
        start = (u // dil) * span + u % dil
        return pl.ds(start, BLOCK) if dil == 1 else pl.ds(start, BLOCK, stride=dil)

    def state_rows(u):
        return pl.ds(u % dil, BLOCK) if dil == 1 else pl.ds(u % dil, BLOCK, stride=dil)

    n_quads = n_units // UNITS_PER_PROJ
    k_slab = D_MODEL // UNITS_PER_PROJ

    def quad_units(qd):
        return [UNITS_PER_PROJ * qd + w for w in range(UNITS_PER_PROJ)]

    def load_x(qd):
        xq_sc[qd % 2] = jnp.concatenate(
            [jnp.concatenate([x_refs[c][0, unit_rows(u), :] for c in range(X_SLABS)], axis=1)
             for u in quad_units(qd)], axis=0).astype(BF16)

    def project_piece(qd, kp):
        ks = slice(kp * k_slab, (kp + 1) * k_slab)
        part = lax.dot_general(wt_ref[:, ks], xq_sc[qd % 2, :, ks], nt, preferred_element_type=F32)
        if kp == 0:
            ht_sc[qd % 2] = part
        else:
            ht_sc[qd % 2] += part

    load_x(0)
    for kp in range(UNITS_PER_PROJ):
        project_piece(0, kp)
    carried = {}
    for qd in range(n_quads):
        units = quad_units(qd)
        ht = ht_sc[qd % 2]
        cols = slice(qd * UNITS_PER_PROJ * BLOCK, (qd + 1) * UNITS_PER_PROJ * BLOCK)
        cos, sin = cos_ref[:, cols], sin_ref[:, cols]
        qt = [_rope_t(ht[h * HEAD_DIM:(h + 1) * HEAD_DIM], cos, sin) * scale for h in range(Q_HEADS)]
        kt = jnp.concatenate([_rope_t(ht[QW + h * HEAD_DIM:QW + (h + 1) * HEAD_DIM], cos, sin)
                              for h in range(KV_HEADS)], axis=0)
        vt = ht[QW + KW:GW]
        k_rm = kt.T

        if dil > 1 and units[0] >= n_units - dil:
            @pl.when(is_last)
            def _():
                v_rm = vt.T
                for w, u in enumerate(units):
                    kv_nat_sc[0, state_rows(u), :] = k_rm[w * BLOCK:(w + 1) * BLOCK]
                    kv_nat_sc[1, state_rows(u), :] = v_rm[w * BLOCK:(w + 1) * BLOCK]

        def scores(w, u, carried):
            lanes = slice(w * BLOCK, (w + 1) * BLOCK)
            k_cur = k_rm[w * BLOCK:(w + 1) * BLOCK].astype(BF16)
            vt_cur = vt[:, lanes].astype(BF16)
            if u >= dil:
                k_prev, vt_prev = carried[u - dil]
                mask = 0
            else:
                k_prev, vt_prev = kprev_sc[u], vprev_sc[u]
                mask = 1
            k_cat = jnp.concatenate([k_prev, k_cur], axis=0)
            wq = []
            for c in range(QW // LANES):
                top = jnp.concatenate([qt[2 * c][:, lanes], qt[2 * c + 1][:, lanes]], axis=1)
                wq.append(jnp.concatenate([top, zeros_half] if c < GROUP // 2 else [zeros_half, top], axis=0))
            wq = jnp.concatenate(wq, axis=1).astype(BF16)
            st_sc[u % 2] = jnp.dot(k_cat, wq, preferred_element_type=F32)
            return dict(w=w, u=u, lanes=lanes, k_cur=k_cur, vt_cur=vt_cur, vt_prev=vt_prev, mask=mask)

        def softmax(t):
            par = t["u"] % 2
            ms = []
            for c in range(QW // LANES):
                pc = slice(c * 2 * BLOCK, (c + 1) * 2 * BLOCK)
                st = st_sc[par, :, pc] + bias_sc[t["mask"]]
                m = jnp.max(st, axis=0, keepdims=True)
                if has_sink:
                    sink = jnp.where(lane2 < BLOCK, sink_ref[2 * c], sink_ref[2 * c + 1]) * LOG2E
                    m = jnp.maximum(m, sink)
                ms.append(m)
                pt_sc[par, :, pc] = jnp.exp2(st - m).astype(BF16)
            t["ms"] = ms

        def values(t):
            u, par, lanes = t["u"], t["u"] % 2, t["lanes"]
            lse_rows = []
            for hk in range(KV_HEADS):
                hd = slice(hk * HEAD_DIM, (hk + 1) * HEAD_DIM)
                vt_aug = jnp.concatenate(
                    [jnp.concatenate([t["vt_prev"][hd], t["vt_cur"][hd]], axis=1), ones_rows.astype(BF16)], axis=0)
                acc2 = jnp.dot(vt_aug, pt_sc[par, :, hk * GROUP * BLOCK:(hk + 1) * GROUP * BLOCK],
                               preferred_element_type=F32)
                for i in range(GROUP // 2):
                    c = hk * (GROUP // 2) + i
                    acc = acc2[:, i * 2 * BLOCK:(i + 1) * 2 * BLOCK]
                    m = t["ms"][c]
                    den = acc[HEAD_DIM:HEAD_DIM + 1]
                    if has_sink:
                        sink = jnp.where(lane2 < BLOCK, sink_ref[2 * c], sink_ref[2 * c + 1]) * LOG2E
                        den = den + jnp.exp2(sink - m)
                    ot = acc[:HEAD_DIM] / den
                    lse = m * LN2 + jnp.log(den)
                    lse_rows += [lse[:, :BLOCK], lse[:, BLOCK:]]
                    slab_t = jnp.concatenate([ot[:, :BLOCK], ot[:, BLOCK:]], axis=0)
                    o_sc[c, unit_rows(u), :] = slab_t.T
            lse_t = jnp.concatenate(lse_rows + [jnp.zeros((LANES - Q_HEADS, BLOCK), F32)], axis=0)
            lse_ref[0, unit_rows(u), :] = lse_t.T
            if u >= n_units - dil:
                kprev_sc[u % dil] = t["k_cur"]
                vprev_sc[u % dil] = t["vt_cur"]
                if dil == 1:
                    @pl.when(is_last)
                    def _():
                        st_ref[0, :KW, :] = kt[:, lanes]
                        st_ref[0, KW:, :] = vt[:, lanes]

        prefetch = qd + 1 < n_quads
        pending = None
        for w, u in enumerate(units):
            t = scores(w, u, carried)
            carried[u] = (t["k_cur"], t["vt_cur"])
            if prefetch and w == 0:
                load_x(qd + 1)
            if pending is not None:
                softmax(pending)
            if prefetch:
                project_piece(qd + 1, w)
            if pending is not None:
                values(pending)
            pending = t
        softmax(pending)
        values(pending)

    for c in range(QW // LANES):
        o_ref[0, :, c * LANES:(c + 1) * LANES] = o_sc[c].astype(o_ref.dtype)

    if dil > 1:
        @pl.when(is_last)
        def _():
            st_ref[0, :KW, :] = kv_nat_sc[0].T
            st_ref[0, KW:, :] = kv_nat_sc[1].T


def _group_attn_prompt(x, w_g, sinks, dil, has_sink):
    bn, s, d = x.shape
    span = BLOCK * dil
    tstep = max(STEP_TOKENS, span)
    assert d == D_MODEL and s % tstep == 0 and tstep % (UNITS_PER_PROJ * BLOCK) == 0, (x.shape, dil)
    nstep = s // tstep
    pos = (jnp.arange(s // span)[:, None, None] * span + jnp.arange(dil)[None, :, None]
           + jnp.arange(BLOCK)[None, None, :] * dil).reshape(s)
    half = ROT_DIM // 2
    inv = ROPE_THETA ** (-jnp.arange(half, dtype=F32) / half)
    ang = pos.astype(F32)[:, None] * inv[None, :]
    cos_t, sin_t = jnp.cos(ang).T, jnp.sin(ang).T
    tab_spec = pl.BlockSpec((half, tstep), lambda b, j: (0, j))
    o, lse, st = pl.pallas_call(
        functools.partial(_group_attn_kernel, has_sink=has_sink, dil=dil, tstep=tstep, nstep=nstep),
        grid=(bn, nstep),
        in_specs=([pl.BlockSpec(memory_space=pltpu.SMEM)]
                  + [pl.BlockSpec((1, tstep, LANES), functools.partial(lambda c, b, j: (b, j, c), c))
                     for c in range(X_SLABS)]
                  + [pl.BlockSpec((GW, d), lambda b, j: (0, 0)), tab_spec, tab_spec]),
        out_specs=[
            pl.BlockSpec((1, tstep, QW), lambda b, j: (b, j, 0)),
            pl.BlockSpec((1, tstep, LANES), lambda b, j: (b, j, 0)),
            pl.BlockSpec((1, 2 * KW, span), lambda b, j: (b, 0, 0)),
        ],
        out_shape=[
            jax.ShapeDtypeStruct((bn, s, QW), BF16),
            jax.ShapeDtypeStruct((bn, s, LANES), F32),
            jax.ShapeDtypeStruct((bn, 2 * KW, span), F32),
        ],
        scratch_shapes=[pltpu.VMEM((dil, BLOCK, KW), BF16), pltpu.VMEM((dil, KW, BLOCK), BF16),
                        pltpu.VMEM((2, span, LANES), F32), pltpu.VMEM((QW // LANES, tstep, LANES), F32),
                        pltpu.VMEM((2, 2 * BLOCK, Q_HEADS * BLOCK), F32),
                        pltpu.VMEM((2, 2 * BLOCK, Q_HEADS * BLOCK), BF16),
                        pltpu.VMEM((2, 2 * BLOCK, 2 * BLOCK), F32),
                        pltpu.VMEM((2, UNITS_PER_PROJ * BLOCK, d), BF16),
                        pltpu.VMEM((2, GW, UNITS_PER_PROJ * BLOCK), F32)],
        compiler_params=_cparams(("arbitrary", "arbitrary")),
        name=f"l0_group_attn_d{dil}_{'sink' if has_sink else 'nosink'}",
    )(sinks, *([x] * X_SLABS), w_g.T, cos_t, sin_t)
    st = jnp.transpose(st.reshape(bn, 2, KV_HEADS, HEAD_DIM, span), (0, 4, 1, 2, 3))
    return o.reshape(bn * s, QW), lse.reshape(bn * s, LANES), st


SUB_ROWS = 256


def _l0_out_kernel(x_ref, oa_ref, o0_ref, o1_ref, o2_ref, l0_ref, l1_ref, l2_ref,
                   wz_ref, wo_ref, g_ref, b_ref, y_ref, z_sc, mix_sc):
    tm = x_ref.shape[0]
    n_sub = tm // SUB_ROWS

    def gate_proj(i):
        r = slice(i * SUB_ROWS, (i + 1) * SUB_ROWS)
        z_sc[r, :] = jnp.dot(x_ref[r, :].astype(BF16), wz_ref[...], preferred_element_type=F32)

    def mix_stage(i):
        r = slice(i * SUB_ROWS, (i + 1) * SUB_ROWS)
        l0, l1, l2 = l0_ref[r, :], l1_ref[r, :], l2_ref[r, :]
        m = jnp.maximum(jnp.maximum(l0, l1), l2)
        e0, e1, e2 = jnp.exp(l0 - m), jnp.exp(l1 - m), jnp.exp(l2 - m)
        inv = 1.0 / (e0 + e1 + e2)
        w0, w1, w2 = e0 * inv, e1 * inv, e2 * inv
        parts = []
        for hq in range(Q_HEADS):
            c = slice(hq * HEAD_DIM, (hq + 1) * HEAD_DIM)
            bshape = (SUB_ROWS, HEAD_DIM)
            parts.append(jnp.broadcast_to(w0[:, hq:hq + 1], bshape) * o0_ref[r, c].astype(F32)
                         + jnp.broadcast_to(w1[:, hq:hq + 1], bshape) * o1_ref[r, c].astype(F32)
                         + jnp.broadcast_to(w2[:, hq:hq + 1], bshape) * o2_ref[r, c].astype(F32))
        mix = jnp.concatenate([oa_ref[r, :].astype(F32)] + parts, axis=1)
        z = z_sc[r, :]
        mix_sc[r, :] = (mix * (z * jax.nn.sigmoid(z))).astype(BF16)

    def out_stage(i):
        r = slice(i * SUB_ROWS, (i + 1) * SUB_ROWS)
        y = jnp.dot(mix_sc[r, :], wo_ref[...], preferred_element_type=F32)
        y_ref[r, :] = _layer_norm_rows(DN_ALPHA * x_ref[r, :] + y, g_ref[...], b_ref[...])

    gate_proj(0)
    for i in range(n_sub):
        if i + 1 < n_sub:
            gate_proj(i + 1)
        mix_stage(i)
        out_stage(i)


def _l0_out(x2, oa, obs, lses, wz, wo, g, b):
    n, d = x2.shape
    tm = 512
    assert n % tm == 0 and tm % SUB_ROWS == 0, x2.shape
    row = lambda w: pl.BlockSpec((tm, w), lambda i: (i, 0))
    full = lambda a: pl.BlockSpec(a.shape, lambda i: (0, 0))
    return pl.pallas_call(
        _l0_out_kernel,
        grid=(n // tm,),
        in_specs=[row(d), row(QW), row(QW), row(QW), row(QW), row(LANES), row(LANES), row(LANES),
                  full(wz), full(wo), full(g), full(b)],
        out_specs=row(d),
        out_shape=jax.ShapeDtypeStruct((n, d), F32),
        scratch_shapes=[pltpu.VMEM((tm, MIX), F32), pltpu.VMEM((tm, MIX), BF16)],
        compiler_params=_cparams(("arbitrary",)),
        name="l0_out",
    )(x2, oa, *obs, *lses, wz, wo, g, b)


def _proj_kernel(x_ref, w_ref, cos_ref, slo_ref, shi_ref, h_ref, *, n_rope, n_scaled, tn):
    j = pl.program_id(0)
    h = jnp.dot(x_ref[...].astype(BF16), w_ref[...], preferred_element_type=F32)
    if n_rope:
        cos, slo, shi = cos_ref[...], slo_ref[...], shi_ref[...]
        roped = jnp.concatenate(
            [_rope(h[:, c * LANES:(c + 1) * LANES], cos, slo, shi) for c in range(tn // LANES)], axis=1)
        h = jnp.where(j < n_rope, roped, h)
        h = h * jnp.where(j < n_scaled, HEAD_DIM ** -0.5, 1.0)
    h_ref[...] = h


def _proj(x2, w, tabs, tn, n_rope=0, n_scaled=0, ncol=None):
    n, d = x2.shape
    ncol = w.shape[1] if ncol is None else ncol
    tab_spec = pl.BlockSpec((n, LANES), lambda j: (0, 0))
    return pl.pallas_call(
        functools.partial(_proj_kernel, n_rope=n_rope, n_scaled=n_scaled, tn=tn),
        grid=(ncol // tn,),
        in_specs=[pl.BlockSpec((n, d), lambda j: (0, 0)),
                  pl.BlockSpec((d, tn), lambda j: (0, j)),
                  tab_spec, tab_spec, tab_spec],
        out_specs=pl.BlockSpec((n, tn), lambda j: (0, j)),
        out_shape=jax.ShapeDtypeStruct((n, ncol), F32),
        compiler_params=_cparams(("arbitrary",)),
        name=f"decode_proj_{ncol}",
    )(x2, w, *tabs)


def _dec_prepare(h_ref, rows, g):
    lo_half = lax.broadcasted_iota(jnp.int32, (TPAD, LANES), 1) < HEAD_DIM
    k_base = N_GROUPS * QW
    v_base = k_base + N_GROUPS * KW
    blocks = []
    for c in range(QW // LANES):
        slab = h_ref[rows, g * QW + c * LANES: g * QW + (c + 1) * LANES]
        swapped = pltpu.roll(slab, HEAD_DIM, 1)
        if c < QW // LANES // 2:
            blocks += [jnp.where(lo_half, slab, 0.0), jnp.where(lo_half, swapped, 0.0)]
        else:
            blocks += [jnp.where(lo_half, 0.0, swapped), jnp.where(lo_half, 0.0, slab)]
    qm = jnp.concatenate(blocks, axis=0).astype(BF16)
    k_new = h_ref[rows, k_base + g * KW: k_base + (g + 1) * KW]
    v_new = h_ref[rows, v_base + g * KW: v_base + (g + 1) * KW]
    zpad = jnp.zeros((LANES - TPAD, 2 * KW), F32)
    kv_new = jnp.concatenate([jnp.concatenate([k_new, v_new], axis=1), zpad], axis=0)
    return qm, kv_new


def _dec_scores(qm, kv_new, cache_ref, sc_ref, s_i):
    nkeys = cache_ref.shape[2]
    nt = (((1,), (1,)), ((), ()))
    sc_ref[s_i, :, :nkeys] = jnp.dot(qm, cache_ref[s_i, :KW, :].astype(BF16), preferred_element_type=F32)
    sc_ref[s_i, :, nkeys:] = lax.dot_general(qm, kv_new[:, :KW].astype(BF16), nt, preferred_element_type=F32)


def _dec_shift(kv_new, cache_ref, new_ref, s_i, t_new):
    nkeys = cache_ref.shape[2]
    lane_k = lax.broadcasted_iota(jnp.int32, (KW, LANES), 1)
    new_cols = pltpu.roll(kv_new.T, LANES - t_new, 1)
    for half in range(2):
        hrows = slice(half * KW, (half + 1) * KW)
        shifted = pltpu.roll(cache_ref[s_i, hrows, :], nkeys - t_new, 1)
        new_ref[s_i, hrows, :] = shifted
        new_ref[s_i, hrows, nkeys - LANES:] = jnp.where(lane_k >= LANES - t_new, new_cols[hrows],
                                                        shifted[:, nkeys - LANES:])


def _dec_values(kv_new, cache_ref, sc_ref, s_i, rows, dil, t_new, sink_ref, o_ref, l_ref):
    nrow = Q_HEADS * TPAD
    nkeys = cache_ref.shape[2]
    ncol = nkeys + LANES
    nt = (((1,), (1,)), ((), ()))
    lane8 = lax.broadcasted_iota(jnp.int32, (TPAD, LANES), 1)
    lo_half = lane8 < HEAD_DIM
    tok = lax.broadcasted_iota(jnp.int32, (nrow, ncol), 0) % TPAD
    col = lax.broadcasted_iota(jnp.int32, (nrow, ncol), 1)
    new_j = col - nkeys
    valid = (((col < nkeys) & (((col - tok) & (dil - 1)) == 0) & (col >= tok))
             | ((col >= nkeys) & (((tok - new_j) & (dil - 1)) == 0) & (new_j <= tok) & (new_j < t_new)))
    s_all = jnp.where(valid, sc_ref[s_i], NEG_INF)
    m = jnp.max(s_all, axis=-1, keepdims=True)
    if sink_ref is not None:
        sink = sink_ref[:, 0:1]
        m = jnp.maximum(m, sink)
    p = jnp.exp(s_all - m)
    den = jnp.sum(p, axis=-1, keepdims=True)
    if sink_ref is not None:
        den = den + jnp.exp(sink - m)
    pb = p.astype(BF16)
    o = (lax.dot_general(pb[:, :nkeys], cache_ref[s_i, KW:, :].astype(BF16), nt, preferred_element_type=F32)
         + jnp.dot(pb[:, nkeys:], kv_new[:, KW:].astype(BF16), preferred_element_type=F32)) / den
    for c in range(QW // LANES):
        ev = o[(2 * c) * TPAD:(2 * c + 1) * TPAD]
        od = o[(2 * c + 1) * TPAD:(2 * c + 2) * TPAD]
        if c < QW // LANES // 2:
            slab = jnp.where(lo_half, ev, pltpu.roll(od, HEAD_DIM, 1))
        else:
            slab = jnp.where(lo_half, pltpu.roll(ev, HEAD_DIM, 1), od)
        o_ref[rows, c * LANES:(c + 1) * LANES] = slab.astype(o_ref.dtype)
    if l_ref is not None:
        lse = m + jnp.log(den)
        lse_acc = jnp.zeros((TPAD, LANES), F32)
        for hq in range(Q_HEADS):
            lse_acc = jnp.where(lane8 == hq, lse[hq * TPAD:(hq + 1) * TPAD], lse_acc)
        l_ref[rows, :] = lse_acc


def _cache_rows_minor(c):
    nseq, rows = c.shape[1], c.shape[2]
    return jnp.transpose(c[0], (0, 2, 3, 4, 1)).reshape(nseq, 2 * KW, rows)


def _cache_rows_major(c):
    nseq, _, rows = c.shape
    return jnp.transpose(c.reshape(nseq, 2, KV_HEADS, HEAD_DIM, rows), (0, 4, 1, 2, 3))[None]


def _mlstm_chunk(q, k, v, gates, hh, c_mem, c_norm, m_prev, n_valid):
    L = q.shape[0]
    ig_col = gates[:, hh:hh + 1]
    fg = gates[:, C_HEADS + hh:C_HEADS + hh + 1]
    lf_col = jnp.minimum(fg, 0.0) - jnp.log(1.0 + jnp.exp(-jnp.abs(fg)))
    ri = lax.broadcasted_iota(jnp.int32, (L, L), 0)
    ci = lax.broadcasted_iota(jnp.int32, (L, L), 1)
    tril = ri >= ci
    eye = ri == ci
    lf_row = jnp.sum(jnp.where(eye, lf_col, 0.0), axis=0, keepdims=True)
    ig_row = jnp.sum(jnp.where(eye, ig_col, 0.0), axis=0, keepdims=True)
    b_col = jnp.sum(jnp.where(tril, lf_row, 0.0), axis=1, keepdims=True)
    b_row = jnp.sum(jnp.where(ri <= ci, lf_col, 0.0), axis=0, keepdims=True)
    dmat = jnp.where(tril, b_col - b_row + ig_row, NEG_INF)
    inter = b_col + m_prev
    m = jnp.maximum(inter, jnp.max(dmat, axis=1, keepdims=True))
    pw = jnp.exp(dmat - m)
    qb, kb, vb = q.astype(BF16), k.astype(BF16), v.astype(BF16)
    sc = lax.dot_general(qb, kb, (((1,), (1,)), ((), ())), preferred_element_type=F32) * pw
    carry_w = jnp.exp(inter - m)
    num = (jnp.dot(sc.astype(BF16), vb, preferred_element_type=F32)
           + carry_w * jnp.dot(qb, c_mem.astype(BF16), preferred_element_type=F32))
    den = jnp.sum(sc, axis=1, keepdims=True) + carry_w * jnp.sum(q * c_norm, axis=1, keepdims=True)
    h = num / jnp.maximum(jnp.abs(den), jnp.exp(-m))
    last = n_valid - 1
    m_new = m[last:last + 1, :]
    b_last = b_col[last:last + 1, :]
    ws = jnp.exp(b_last - b_col + ig_col - m_new)
    if n_valid < L:
        ws = jnp.where(lax.broadcasted_iota(jnp.int32, (L, 1), 0) < n_valid, ws, 0.0)
    decay = jnp.exp(b_last + m_prev - m_new)
    wk = ws * k
    c_mem_new = decay * c_mem + lax.dot_general(
        wk.astype(BF16), vb, (((0,), (0,)), ((), ())), preferred_element_type=F32)
    c_norm_new = decay * c_norm + jnp.sum(wk, axis=0, keepdims=True)
    return h, c_mem_new, c_norm_new, m_new


AUG = LANES
PROJ_COLS = 256


def _l1_prompt_kernel(x_ref, w_ref, wg_ref, bias_ref, sink_ref, hs_ref, ca_ref, c0_ref, c1_ref, c2_ref,
                      h_ref, cmem_ref, cnorm_ref, cmax_ref,
                      oa_ref, o0_ref, o1_ref, o2_ref, l0_ref, l1_ref, l2_ref, na_ref, n0_ref, n1_ref, n2_ref,
                      caug_sc, m_sc, s_sc, lhs_sc, proj_a, proj_b, xb_sc, sa_sc, s0_sc, s1_sc, s2_sc,
                      *, tm, nblk, dec_sb, t_new):
    i = pl.program_id(0)
    blk = i - 1

    @pl.when(i == 0)
    def _():
        proj_b[...] = jnp.zeros_like(proj_b)

    @pl.when(blk % nblk == 0)
    def _():
        caug_sc[...] = jnp.zeros_like(caug_sc)
        m_sc[...] = jnp.zeros_like(m_sc)

    ri = lax.broadcasted_iota(jnp.int32, (BLOCK, BLOCK), 0)
    ci = lax.broadcasted_iota(jnp.int32, (BLOCK, BLOCK), 1)
    tril = ri >= ci
    upper = (ri <= ci).astype(F32)
    ones_blk = jnp.ones((BLOCK, AUG), BF16)
    kscale = C_HEAD_DIM ** -0.5
    nt = (((1,), (1,)), ((), ()))
    ncol = w_ref.shape[1]
    slabs = [(c0, c0 + PROJ_COLS) for c0 in range(0, ncol, PROJ_COLS)] + [(ncol, ncol + LANES)]
    n_chunks = tm // BLOCK

    def body(src, dst):
        xb_sc[...] = x_ref[...].astype(BF16)
        pieces = iter(slabs)

        def project_piece():
            c = next(pieces, None)
            if c is not None:
                w_piece = w_ref[:, c[0]:c[1]] if c[0] < ncol else wg_ref[...]
                dst[:, c[0]:c[1]] = jnp.dot(xb_sc[...], w_piece, preferred_element_type=F32)

        g_t = (src[:, 3 * MIX:] + bias_ref[...]).T[0:TPAD]
        lf_t = jnp.minimum(g_t, 0.0) - jnp.log(1.0 + jnp.exp(-jnp.abs(g_t)))

        def score_stage(c):
            rows = slice(c * BLOCK, (c + 1) * BLOCK)
            qs, ks, vs = [], [], []
            for hh in range(C_HEADS):
                q = src[rows, hh * C_HEAD_DIM:(hh + 1) * C_HEAD_DIM]
                k = src[rows, MIX + hh * C_HEAD_DIM: MIX + (hh + 1) * C_HEAD_DIM] * kscale
                v = src[rows, 2 * MIX + hh * C_HEAD_DIM: 2 * MIX + (hh + 1) * C_HEAD_DIM]
                qs.append(q)
                ks.append(k)
                vs.append(v.astype(BF16))
                s_sc[c % 2, hh] = lax.dot_general(q.astype(BF16), k.astype(BF16), nt, preferred_element_type=F32)
            return dict(rows=rows, par=c % 2, qs=qs, ks=ks, vs=vs)

        def gate_stage(t, m_prevs):
            rows, par = t["rows"], t["par"]
            b_t = jnp.dot(lf_t[:, rows], upper, preferred_element_type=F32, precision=lax.Precision.HIGHEST)
            stats = []
            for hh in range(C_HEADS):
                ig_row = g_t[hh:hh + 1, rows]
                lf_row = lf_t[C_HEADS + hh:C_HEADS + hh + 1, rows]
                b_row = b_t[C_HEADS + hh:C_HEADS + hh + 1, :]
                m_prev = m_prevs[hh]
                b_col = jnp.sum(jnp.where(tril, lf_row, 0.0), axis=1, keepdims=True)
                dmat = jnp.where(tril, b_col - (b_row - ig_row), NEG_INF)
                inter = b_col + m_prev
                m = jnp.maximum(inter, jnp.max(dmat, axis=1, keepdims=True))
                sc = s_sc[par, hh] * jnp.exp(dmat - m)
                carry_w = jnp.exp(inter - m)
                lhs_sc[par, hh, :, :BLOCK] = sc.astype(BF16)
                lhs_sc[par, hh, :, BLOCK:] = (carry_w * t["qs"][hh]).astype(BF16)
                m_new = m[BLOCK - 1:BLOCK, :]
                b_last = b_col[BLOCK - 1:BLOCK, :]
                ws_row = jnp.exp(b_last - b_row + ig_row - m_new)
                decay = jnp.exp(b_last + m_prev - m_new)
                stats.append((m, m_new, ws_row, decay))
            t["stats"] = stats

        def memory_stage(t):
            rows, par = t["rows"], t["par"]
            for hh in range(C_HEADS):
                cols = slice(hh * C_HEAD_DIM, (hh + 1) * C_HEAD_DIM)
                m, m_new, ws_row, decay = t["stats"][hh]
                caug = caug_sc[hh]
                v_aug = jnp.concatenate([t["vs"][hh], ones_blk], axis=1)
                rhs = jnp.concatenate([v_aug, caug.astype(BF16)], axis=0)
                res = jnp.dot(lhs_sc[par, hh], rhs, preferred_element_type=F32)
                den = jnp.maximum(jnp.abs(res[:, C_HEAD_DIM:]), jnp.exp(-m))
                h_ref[rows, cols] = res[:, :C_HEAD_DIM] / jnp.concatenate([den, den], axis=1)
                wk_t = (t["ks"][hh].T * ws_row).astype(BF16)
                caug_sc[hh] = decay * caug + jnp.dot(wk_t, v_aug, preferred_element_type=F32)

        m_prevs = [m_sc[hh:hh + 1, 0:1] for hh in range(C_HEADS)]
        ts = [score_stage(0), score_stage(1)]
        caches = (ca_ref, c0_ref, c1_ref, c2_ref)
        new_caches = (na_ref, n0_ref, n1_ref, n2_ref)
        o_refs = (oa_ref, o0_ref, o1_ref, o2_ref)
        l_refs = (None, l0_ref, l1_ref, l2_ref)
        sc_refs = (sa_sc, s0_sc, s1_sc, s2_sc)
        dec = []
        for g in range(N_GROUPS):
            for s_i in range(dec_sb):
                rows = slice(s_i * TPAD, (s_i + 1) * TPAD)
                qm, kv_new = _dec_prepare(hs_ref, rows, g)
                _dec_scores(qm, kv_new, caches[g], sc_refs[g], s_i)
                dec.append((g, s_i, rows, kv_new))
        n_dec = len(dec)
        shifts = iter(dec)

        def shift_items(k):
            for _ in range(k):
                d = next(shifts, None)
                if d is not None:
                    _dec_shift(d[3], caches[d[0]], new_caches[d[0]], d[1], t_new)
                    project_piece()

        project_piece()
        for c in range(n_chunks):
            gate_stage(ts[c], m_prevs)
            m_prevs = [st[1] for st in ts[c]["stats"]]
            project_piece()
            if c + 2 < n_chunks:
                ts.append(score_stage(c + 2))
                project_piece()
            shift_items(n_dec // (2 * n_chunks))
            if c >= 1:
                memory_stage(ts[c - 1])
                project_piece()
        memory_stage(ts[n_chunks - 1])
        project_piece()
        shift_items(n_dec)
        for g, s_i, rows, kv_new in dec:
            _dec_values(kv_new, caches[g], sc_refs[g], s_i, rows, DILATIONS[g], t_new,
                        sink_ref if g == 0 else None, o_refs[g], l_refs[g])
            project_piece()
        for _ in slabs:
            project_piece()
        for hh in range(C_HEADS):
            m_sc[hh:hh + 1, :] = jnp.broadcast_to(m_prevs[hh], (1, LANES))

    @pl.when(i % 2 == 0)
    def _():
        body(proj_b, proj_a)

    @pl.when(i % 2 == 1)
    def _():
        body(proj_a, proj_b)

    @pl.when((blk >= 0) & (blk % nblk == nblk - 1))
    def _():
        for hh in range(C_HEADS):
            cmem_ref[0, hh] = caug_sc[hh, :, :C_HEAD_DIM]
            cnorm_ref[0, hh:hh + 1, :] = caug_sc[hh, :, C_HEAD_DIM:].T[0:1, :]
        cmax_ref[0] = m_sc[...]


def _l1_prompt(x, w1b, w_gates, bias_row, hs, caches, sink_rows, t_new):
    bn, s, d = x.shape
    cts = [_cache_rows_minor(c) for c in caches]
    nseq, ng = cts[0].shape[0], len(cts)
    dec_sb = 2
    ndec = nseq // dec_sb
    tm = (bn * s) // ndec
    assert nseq % dec_sb == 0 and tm * ndec == bn * s and tm % (2 * BLOCK) == 0 and s % tm == 0, (x.shape, nseq)
    nblk = s // tm
    ntot = bn * nblk
    ncol = 3 * MIX + LANES
    state_blk = lambda i: (jnp.maximum(i - 1, 0) // nblk, 0, 0)
    dec_blk = lambda i: jnp.minimum(i, ndec - 1)
    drow = lambda w: pl.BlockSpec((dec_sb * TPAD, w), lambda i: (dec_blk(i), 0))
    cspec = lambda c: pl.BlockSpec((dec_sb,) + c.shape[1:], lambda i: (dec_blk(i), 0, 0))
    outs = pl.pallas_call(
        functools.partial(_l1_prompt_kernel, tm=tm, nblk=nblk, dec_sb=dec_sb, t_new=t_new),
        grid=(ntot + 1,),
        in_specs=[pl.BlockSpec((tm, d), lambda i: (jnp.minimum(i, ntot - 1), 0)),
                  pl.BlockSpec((d, 3 * MIX), lambda i: (0, 0)),
                  pl.BlockSpec((d, LANES), lambda i: (0, 0)),
                  pl.BlockSpec((1, LANES), lambda i: (0, 0)),
                  pl.BlockSpec(sink_rows.shape, lambda i: (0, 0)),
                  drow(hs.shape[1])] + [cspec(c) for c in cts],
        out_specs=([pl.BlockSpec((tm, MIX), lambda i: (jnp.maximum(i - 1, 0), 0)),
                    pl.BlockSpec((1, C_HEADS, C_HEAD_DIM, C_HEAD_DIM), lambda i: state_blk(i) + (0,)),
                    pl.BlockSpec((1, C_HEADS, C_HEAD_DIM), state_blk),
                    pl.BlockSpec((1, TPAD, LANES), state_blk)]
                   + [drow(QW)] * ng + [drow(LANES)] * (ng - 1) + [cspec(c) for c in cts]),
        out_shape=([jax.ShapeDtypeStruct((bn * s, MIX), F32),
                    jax.ShapeDtypeStruct((bn, C_HEADS, C_HEAD_DIM, C_HEAD_DIM), F32),
                    jax.ShapeDtypeStruct((bn, C_HEADS, C_HEAD_DIM), F32),
                    jax.ShapeDtypeStruct((bn, TPAD, LANES), F32)]
                   + [jax.ShapeDtypeStruct((nseq * TPAD, QW), BF16)] * ng
                   + [jax.ShapeDtypeStruct((nseq * TPAD, LANES), F32)] * (ng - 1)
                   + [jax.ShapeDtypeStruct(c.shape, F32) for c in cts]),
        scratch_shapes=([pltpu.VMEM((C_HEADS, C_HEAD_DIM, C_HEAD_DIM + AUG), F32),
                         pltpu.VMEM((TPAD, LANES), F32),
                         pltpu.VMEM((2, C_HEADS, BLOCK, BLOCK), F32),
                         pltpu.VMEM((2, C_HEADS, BLOCK, BLOCK + C_HEAD_DIM), BF16),
                         pltpu.VMEM((tm, ncol), F32), pltpu.VMEM((tm, ncol), F32),
                         pltpu.VMEM((tm, d), BF16)]
                        + [pltpu.VMEM((dec_sb, Q_HEADS * TPAD, c.shape[2] + LANES), F32) for c in cts]),
        compiler_params=_cparams(("arbitrary",)),
        name="l1_prompt_mlstm_dec",
    )(x.reshape(bn * s, d), w1b, w_gates, bias_row, sink_rows, hs, *cts)
    return (outs[:4], outs[4:4 + ng], outs[4 + ng:3 + 2 * ng],
            [_cache_rows_major(c) for c in outs[3 + 2 * ng:]])


def _l1_out_kernel(x_ref, h_ref, wog_ref, wz_ref, mhg_ref, wo_ref, g_ref, b_ref, *refs, dec_sb, t_new):
    if dec_sb:
        (p_ref, pg_ref, bias_ref, cm_ref, cn_ref, cx_ref, y_ref, hd_ref, cmo_ref, cno_ref, cxo_ref,
         oz_sc, mix_sc) = refs
    else:
        y_ref, oz_sc, mix_sc = refs
    tm = x_ref.shape[0]
    n_sub = tm // SUB_ROWS

    def gate_proj(i):
        r = slice(i * SUB_ROWS, (i + 1) * SUB_ROWS)
        xb = x_ref[r, :].astype(BF16)
        oz_sc[r, :MIX] = jnp.dot(xb, wog_ref[...], preferred_element_type=F32)
        oz_sc[r, MIX:] = jnp.dot(xb, wz_ref[...], preferred_element_type=F32)

    def mix_stage(i, between):
        r = slice(i * SUB_ROWS, (i + 1) * SUB_ROWS)
        for hh in range(C_HEADS):
            cols = slice(hh * C_HEAD_DIM, (hh + 1) * C_HEAD_DIM)
            hg = jax.nn.sigmoid(oz_sc[r, cols]) * h_ref[r, cols]
            mu = jnp.mean(hg, axis=-1, keepdims=True)
            c = hg - mu
            var = jnp.mean(c * c, axis=-1, keepdims=True)
            hn = c * lax.rsqrt(var + MH_EPS) * mhg_ref[:, cols]
            z = oz_sc[r, MIX + hh * C_HEAD_DIM:MIX + (hh + 1) * C_HEAD_DIM]
            mix_sc[r, cols] = (hn * (z * jax.nn.sigmoid(z))).astype(BF16)
            between()

    def out_stage(i):
        r = slice(i * SUB_ROWS, (i + 1) * SUB_ROWS)
        y = jnp.dot(mix_sc[r, :], wo_ref[...], preferred_element_type=F32)
        y_ref[r, :] = _layer_norm_rows(DN_ALPHA * x_ref[r, :] + y, g_ref[...], b_ref[...])

    kscale = C_HEAD_DIM ** -0.5
    chunks = [(s_i, hh) for s_i in range(dec_sb) for hh in range(C_HEADS)]
    todo = iter(chunks)

    def decode_chunks(k):
        for _ in range(k):
            it = next(todo, None)
            if it is None:
                return
            s_i, hh = it
            rows = slice(s_i * TPAD, (s_i + 1) * TPAD)
            cols = slice(hh * C_HEAD_DIM, (hh + 1) * C_HEAD_DIM)
            gates = pg_ref[rows, :] + bias_ref[...]
            q = p_ref[rows, cols]
            k_ = p_ref[rows, MIX + hh * C_HEAD_DIM: MIX + (hh + 1) * C_HEAD_DIM] * kscale
            v = p_ref[rows, 2 * MIX + hh * C_HEAD_DIM: 2 * MIX + (hh + 1) * C_HEAD_DIM]
            h, c_new, n_new, m_new = _mlstm_chunk(
                q, k_, v, gates, hh, cm_ref[s_i, hh], cn_ref[s_i, hh:hh + 1, :],
                cx_ref[s_i, hh:hh + 1, 0:1], t_new)
            hd_ref[rows, cols] = h
            cmo_ref[s_i, hh] = c_new
            cno_ref[s_i, hh:hh + 1, :] = n_new
            cxo_ref[s_i, hh:hh + 1, :] = jnp.broadcast_to(m_new, (1, LANES))

    per_slot = -(-len(chunks) // (n_sub * (C_HEADS + 1)))
    gate_proj(0)
    for i in range(n_sub):
        if i + 1 < n_sub:
            gate_proj(i + 1)
        mix_stage(i, lambda: decode_chunks(per_slot))
        out_stage(i)
        decode_chunks(per_slot)
    decode_chunks(len(chunks))


def _l1_out(x2, h2, w1b, mhg, wo, g, b, dec=None):
    n, d = x2.shape
    tm = 512
    row = lambda w: pl.BlockSpec((tm, w), lambda i: (i, 0))
    full = lambda a: pl.BlockSpec(a.shape, lambda i: (0, 0))
    assert n % tm == 0, x2.shape
    wcol = lambda j: pl.BlockSpec((d, MIX), lambda i: (0, j))
    og_blk, z_blk = 3, 4
    in_specs = [row(d), row(MIX), wcol(og_blk), wcol(z_blk), full(mhg), full(wo), full(g), full(b)]
    out_specs = [row(d)]
    out_shape = [jax.ShapeDtypeStruct((n, d), F32)]
    args = [x2, h2, w1b, w1b, mhg, wo, g, b]
    dec_sb, t_new = 0, 0
    if dec is not None:
        proj, proj_g, bias_row, c_mem, c_norm, c_max_pad, t_new = dec
        nseq = c_mem.shape[0]
        dec_sb = nseq // (n // tm)
        assert dec_sb * (n // tm) == nseq, (nseq, n, tm)
        sblk = lambda a: pl.BlockSpec((dec_sb,) + a.shape[1:], lambda i: (i,) + (0,) * (a.ndim - 1))
        in_specs += [pl.BlockSpec((dec_sb * TPAD, proj.shape[1]), lambda i: (i, 0)),
                     pl.BlockSpec((dec_sb * TPAD, LANES), lambda i: (i, 0)), full(bias_row),
                     sblk(c_mem), sblk(c_norm), sblk(c_max_pad)]
        out_specs += [pl.BlockSpec((dec_sb * TPAD, MIX), lambda i: (i, 0)), sblk(c_mem), sblk(c_norm), sblk(c_max_pad)]
        out_shape += [jax.ShapeDtypeStruct((nseq * TPAD, MIX), F32), jax.ShapeDtypeStruct(c_mem.shape, F32),
                      jax.ShapeDtypeStruct(c_norm.shape, F32), jax.ShapeDtypeStruct(c_max_pad.shape, F32)]
        args += [proj, proj_g, bias_row, c_mem, c_norm, c_max_pad]
    outs = pl.pallas_call(
        functools.partial(_l1_out_kernel, dec_sb=dec_sb, t_new=t_new),
        grid=(n // tm,),
        in_specs=in_specs,
        out_specs=out_specs,
        out_shape=out_shape,
        scratch_shapes=[pltpu.VMEM((tm, 2 * MIX), F32), pltpu.VMEM((tm, MIX), BF16)],
        compiler_params=_cparams(("arbitrary",)),
        name="l1_out_dec" if dec is not None else "l1_out",
    )(*args)
    return outs[0] if dec is None else outs


def _group_cols(g):
    if g == 0:
        return [(0, QW), (QW, QW + KW), (QW + KW, GW)]
    b = g - 1
    qb0 = GW
    kb0 = GW + 3 * QW
    vb0 = kb0 + 3 * KW
    return [(qb0 + b * QW, qb0 + (b + 1) * QW), (kb0 + b * KW, kb0 + (b + 1) * KW),
            (vb0 + b * KW, vb0 + (b + 1) * KW)]


def kernel(x_prompt, x_sample, cache_a_kv, cache_b0_kv, cache_b1_kv, cache_b2_kv, state_c_mem,
           state_c_norm, state_c_max, w_in0, sinks0, w_out0, w_in1, b_gates1, mh_norm1, w_out1,
           ln_g, ln_b):
    bn, s, d = x_prompt.shape
    nseq, t_new, _ = x_sample.shape
    w0 = w_in0[0]
    z0 = 4 * GW
    w_groups = [jnp.concatenate([w0[:, a:b] for a, b in _group_cols(g)], axis=1).astype(BF16)
                for g in range(N_GROUPS)]
    wz0 = w0[:, z0:].astype(BF16)
    wo0 = w_out0[0].astype(BF16)
    sinks = sinks0[0].astype(F32)
    g0, b0 = ln_g[0][None, :], ln_b[0][None, :]
    g1, b1 = ln_g[1][None, :], ln_b[1][None, :]

    outs, lses, states_p = [], [], []
    for g in range(N_GROUPS):
        o, lse, st = _group_attn_prompt(x_prompt, w_groups[g], sinks, DILATIONS[g], g == 0)
        outs.append(o)
        lses.append(lse)
        states_p.append(st)
    xp2 = x_prompt.reshape(bn * s, d)
    y0p = _l0_out(xp2, outs[0], outs[1:], lses[1:], wz0, wo0, g0, b0)

    xs2 = jnp.pad(x_sample, ((0, 0), (0, TPAD - t_new), (0, 0))).reshape(nseq * TPAD, d)
    w_dec = jnp.concatenate(
        [w_groups[g][:, :QW] for g in range(N_GROUPS)] + [w_groups[g][:, QW:QW + KW] for g in range(N_GROUPS)]
        + [w_groups[g][:, QW + KW:] for g in range(N_GROUPS)], axis=1)
    tabs_s = tuple(jnp.tile(t, (nseq, 1)) for t in _rope_tables(PAST_LEN + jnp.arange(TPAD)))
    tn = 512
    hs = _proj(xs2, w_dec, tabs_s, tn, n_rope=(N_GROUPS * (QW + KW)) // tn, n_scaled=(N_GROUPS * QW) // tn)
    sink_rows = jnp.broadcast_to(jnp.repeat(sinks, TPAD)[:, None], (Q_HEADS * TPAD, LANES))

    w1b = w_in1[0].astype(BF16)
    w_gates = jnp.pad(w1b[:, 5 * MIX:], ((0, 0), (0, LANES - 2 * C_HEADS)))
    wo1 = w_out1[0].astype(BF16)
    bias_row = jnp.pad(b_gates1[0].astype(F32), (0, LANES - 2 * C_HEADS))[None, :]
    mhg = mh_norm1[0][None, :]

    (hp, cmem_p, cnorm_p, cmax_p), dec_o, dec_l, states_s = _l1_prompt(
        y0p.reshape(bn, s, d), w1b, w_gates, bias_row, hs,
        (cache_a_kv, cache_b0_kv, cache_b1_kv, cache_b2_kv), sink_rows, t_new)

    y0s = _l0_out(xs2, dec_o[0], dec_o[1:], dec_l, wz0, wo0, g0, b0)

    no_tabs = tuple(jnp.zeros((nseq * TPAD, LANES), F32) for _ in range(3))
    ps = _proj(y0s, w1b, no_tabs, 512, ncol=3 * MIX)
    ps_g = _proj(y0s, w_gates, no_tabs, LANES)
    cmax_pad = jnp.broadcast_to(state_c_max[0][:, :, None], (nseq, C_HEADS, LANES))
    y1p, hsd, cmem_s, cnorm_s, cmax_s = _l1_out(
        y0p, hp, w1b, mhg, wo1, g1, b1,
        dec=(ps, ps_g, bias_row, state_c_mem[0], state_c_norm[0], cmax_pad, t_new))
    y1s = _l1_out(y0s, hsd, w1b, mhg, wo1, g1, b1)

    y_prompt = y1p.reshape(bn, s, d)
    y_sample = y1s.reshape(nseq, TPAD, d)[:, :t_new]
    return (y_prompt, y_sample,
            states_p[0][None], states_p[1][None], states_p[2][None], states_p[3][None],
            cmem_p[None], cnorm_p[None], cmax_p[:, :C_HEADS, 0][None],
            states_s[0], states_s[1], states_s[2], states_s[3],
            cmem_s[None], cnorm_s[None], cmax_s[:, :, 0][None])
```

```python
import functools

import jax
import jax.numpy as jnp
from jax import lax
from jax.experimental import pallas as pl
from jax.experimental.pallas import tpu as pltpu

F32 = jnp.float32
BF16 = jnp.bfloat16

D_MODEL = 1024
PAST_LEN = 16384
HEAD_DIM = 64
ROT_DIM = HEAD_DIM // 4
ROPE_THETA = 500000.0
BLOCK = 128
Q_HEADS = 8
KV_HEADS = 2
GROUP = Q_HEADS // KV_HEADS
QW = Q_HEADS * HEAD_DIM
KW = KV_HEADS * HEAD_DIM
GW = QW + 2 * KW
DILATIONS = (1, 1, 4, 16)
N_GROUPS = 4
C_HEADS = 4
C_HEAD_DIM = 256
MIX = 1024
DEPTH = 2
DN_ALPHA = (2 * DEPTH) ** 0.25
LN_EPS = 1e-5
MH_EPS = 1e-6
NEG_INF = -1e30
LOG2E = 1.4426950408889634
LN2 = 0.6931471805599453
LANES = 128
TPAD = 8
VMEM_LIMIT = 56 * 1024 * 1024


def _cparams(sem):
    return pltpu.CompilerParams(dimension_semantics=sem, vmem_limit_bytes=VMEM_LIMIT)


def _layer_norm_rows(v, g, b):
    mu = jnp.mean(v, axis=-1, keepdims=True)
    c = v - mu
    var = jnp.mean(c * c, axis=-1, keepdims=True)
    return c * lax.rsqrt(var + LN_EPS) * g + b


def _rope(slab, cos, sin_lo, sin_hi):
    return (slab * cos + pltpu.roll(slab, LANES - ROT_DIM // 2, 1) * sin_lo
            + pltpu.roll(slab, ROT_DIM // 2, 1) * sin_hi)


def _rope_tables(pos):
    half = ROT_DIM // 2
    inv = ROPE_THETA ** (-jnp.arange(half, dtype=F32) / half)
    ang = pos.astype(F32)[:, None] * inv[None, :]
    cos, sin = jnp.cos(ang), jnp.sin(ang)
    lane = jnp.arange(LANES) % HEAD_DIM
    f = lane % half
    cos_t = jnp.where(lane[None, :] < ROT_DIM, cos[:, f], 1.0)
    sin_lo = jnp.where(lane[None, :] < half, -sin[:, f], 0.0)
    sin_hi = jnp.where((lane[None, :] >= half) & (lane[None, :] < ROT_DIM), sin[:, f], 0.0)
    return cos_t.astype(F32), sin_lo.astype(F32), sin_hi.astype(F32)


STEP_TOKENS = 2048
X_SLABS = D_MODEL // LANES
UNITS_PER_PROJ = 4
ONES_ROWS = 16


def _rope_t(blk, cos, sin):
    half = ROT_DIM // 2
    x1, x2 = blk[:half], blk[half:ROT_DIM]
    return jnp.concatenate([x1 * cos - x2 * sin, x2 * cos + x1 * sin, blk[ROT_DIM:]], axis=0)


def _group_attn_kernel(sink_ref, *refs, has_sink, dil, tstep, nstep):
    x_refs = refs[:X_SLABS]
    (wt_ref, cos_ref, sin_ref, o_ref, lse_ref, st_ref,
     kprev_sc, vprev_sc, kv_nat_sc, o_sc, st_sc, pt_sc, bias_sc, xq_sc, ht_sc) = refs[X_SLABS:]
    j = pl.program_id(1)
    n_units = tstep // BLOCK
    is_last = j == nstep - 1

    @pl.when(j == 0)
    def _():
        kprev_sc[...] = jnp.zeros_like(kprev_sc)
        vprev_sc[...] = jnp.zeros_like(vprev_sc)

    scale = HEAD_DIM ** -0.5 * LOG2E
    kidx = lax.broadcasted_iota(jnp.int32, (2 * BLOCK, 2 * BLOCK), 0)
    qidx = lax.broadcasted_iota(jnp.int32, (2 * BLOCK, 2 * BLOCK), 1) % BLOCK
    dist = qidx + BLOCK - kidx
    band = (dist >= 0) & (dist <= BLOCK)
    bias_sc[0] = jnp.where(band, 0.0, NEG_INF)
    bias_sc[1] = jnp.where(band & (kidx >= jnp.where(j > 0, 0, BLOCK)), 0.0, NEG_INF)
    lane2 = lax.broadcasted_iota(jnp.int32, (1, 2 * BLOCK), 1)
    ones_rows = jnp.ones((ONES_ROWS, 2 * BLOCK), F32)
    zeros_half = jnp.zeros((HEAD_DIM, 2 * BLOCK), F32)
    nt = (((1,), (1,)), ((), ()))

    span = BLOCK * dil

    def unit_rows(u):
        start = (u // dil) * span + u % dil
        return pl.ds(start, BLOCK) if dil == 1 else pl.ds(start, BLOCK, stride=dil)

    def state_rows(u):
        return pl.ds(u % dil, BLOCK) if dil == 1 else pl.ds(u % dil, BLOCK, stride=dil)

    n_quads = n_units // UNITS_PER_PROJ
    k_slab = D_MODEL // UNITS_PER_PROJ

    def quad_units(qd):
        return [UNITS_PER_PROJ * qd + w for w in range(UNITS_PER_PROJ)]

    def load_x(qd):
        xq_sc[qd % 2] = jnp.concatenate(
            [jnp.concatenate([x_refs[c][0, unit_rows(u), :] for c in range(X_SLABS)], axis=1)
             for u in quad_units(qd)], axis=0).astype(BF16)

    def project_piece(qd, kp):
        ks = slice(kp * k_slab, (kp + 1) * k_slab)
        part = lax.dot_general(wt_ref[:, ks], xq_sc[qd % 2, :, ks], nt, preferred_element_type=F32)
        if kp == 0:
            ht_sc[qd % 2] = part
        else:
            ht_sc[qd % 2] += part

    load_x(0)
    for kp in range(UNITS_PER_PROJ):
        project_piece(0, kp)
    carried = {}
    for qd in range(n_quads):
        units = quad_units(qd)
        ht = ht_sc[qd % 2]
        cols = slice(qd * UNITS_PER_PROJ * BLOCK, (qd + 1) * UNITS_PER_PROJ * BLOCK)
        cos, sin = cos_ref[:, cols], sin_ref[:, cols]
        qt = [_rope_t(ht[h * HEAD_DIM:(h + 1) * HEAD_DIM], cos, sin) * scale for h in range(Q_HEADS)]
        kt = jnp.concatenate([_rope_t(ht[QW + h * HEAD_DIM:QW + (h + 1) * HEAD_DIM], cos, sin)
                              for h in range(KV_HEADS)], axis=0)
        vt = ht[QW + KW:GW]
        k_rm = kt.T

        if dil > 1 and units[0] >= n_units - dil:
            @pl.when(is_last)
            def _():
                v_rm = vt.T
                for w, u in enumerate(units):
                    kv_nat_sc[0, state_rows(u), :] = k_rm[w * BLOCK:(w + 1) * BLOCK]
                    kv_nat_sc[1, state_rows(u), :] = v_rm[w * BLOCK:(w + 1) * BLOCK]

        def scores(w, u, carried):
            lanes = slice(w * BLOCK, (w + 1) * BLOCK)
            k_cur = k_rm[w * BLOCK:(w + 1) * BLOCK].astype(BF16)
            vt_cur = vt[:, lanes].astype(BF16)
            if u >= dil:
                k_prev, vt_prev = carried[u - dil]
                mask = 0
            else:
                k_prev, vt_prev = kprev_sc[u], vprev_sc[u]
                mask = 1
            k_cat = jnp.concatenate([k_prev, k_cur], axis=0)
            wq = []
            for c in range(QW // LANES):
                top = jnp.concatenate([qt[2 * c][:, lanes], qt[2 * c + 1][:, lanes]], axis=1)
                wq.append(jnp.concatenate([top, zeros_half] if c < GROUP // 2 else [zeros_half, top], axis=0))
            wq = jnp.concatenate(wq, axis=1).astype(BF16)
            st_sc[u % 2] = jnp.dot(k_cat, wq, preferred_element_type=F32)
            return dict(w=w, u=u, lanes=lanes, k_cur=k_cur, vt_cur=vt_cur, vt_prev=vt_prev, mask=mask)

        def softmax(t):
            par = t["u"] % 2
            ms = []
            for c in range(QW // LANES):
                pc = slice(c * 2 * BLOCK, (c + 1) * 2 * BLOCK)
                st = st_sc[par, :, pc] + bias_sc[t["mask"]]
                m = jnp.max(st, axis=0, keepdims=True)
                if has_sink:
                    sink = jnp.where(lane2 < BLOCK, sink_ref[2 * c], sink_ref[2 * c + 1]) * LOG2E
                    m = jnp.maximum(m, sink)
                ms.append(m)
                pt_sc[par, :, pc] = jnp.exp2(st - m).astype(BF16)
            t["ms"] = ms

        def values(t):
            u, par, lanes = t["u"], t["u"] % 2, t["lanes"]
            lse_rows = []
            for hk in range(KV_HEADS):
                hd = slice(hk * HEAD_DIM, (hk + 1) * HEAD_DIM)
                vt_aug = jnp.concatenate(
                    [jnp.concatenate([t["vt_prev"][hd], t["vt_cur"][hd]], axis=1), ones_rows.astype(BF16)], axis=0)
                acc2 = jnp.dot(vt_aug, pt_sc[par, :, hk * GROUP * BLOCK:(hk + 1) * GROUP * BLOCK],
                               preferred_element_type=F32)
                for i in range(GROUP // 2):
                    c = hk * (GROUP // 2) + i
                    acc = acc2[:, i * 2 * BLOCK:(i + 1) * 2 * BLOCK]
                    m = t["ms"][c]
                    den = acc[HEAD_DIM:HEAD_DIM + 1]
                    if has_sink:
                        sink = jnp.where(lane2 < BLOCK, sink_ref[2 * c], sink_ref[2 * c + 1]) * LOG2E
                        den = den + jnp.exp2(sink - m)
                    ot = acc[:HEAD_DIM] / den
                    lse = m * LN2 + jnp.log(den)
                    lse_rows += [lse[:, :BLOCK], lse[:, BLOCK:]]
                    slab_t = jnp.concatenate([ot[:, :BLOCK], ot[:, BLOCK:]], axis=0)
                    o_sc[c, unit_rows(u), :] = slab_t.T
            lse_t = jnp.concatenate(lse_rows + [jnp.zeros((LANES - Q_HEADS, BLOCK), F32)], axis=0)
            lse_ref[0, unit_rows(u), :] = lse_t.T
            if u >= n_units - dil:
                kprev_sc[u % dil] = t["k_cur"]
                vprev_sc[u % dil] = t["vt_cur"]
                if dil == 1:
                    @pl.when(is_last)
                    def _():
                        st_ref[0, :KW, :] = kt[:, lanes]
                        st_ref[0, KW:, :] = vt[:, lanes]

        prefetch = qd + 1 < n_quads
        pending = None
        for w, u in enumerate(units):
            t = scores(w, u, carried)
            carried[u] = (t["k_cur"], t["vt_cur"])
            if prefetch and w == 0:
                load_x(qd + 1)
            if pending is not None:
                softmax(pending)
            if prefetch:
                project_piece(qd + 1, w)
            if pending is not None:
                values(pending)
            pending = t
        softmax(pending)
        values(pending)

    for c in range(QW // LANES):
        o_ref[0, :, c * LANES:(c + 1) * LANES] = o_sc[c].astype(o_ref.dtype)

    if dil > 1:
        @pl.when(is_last)
        def _():
            st_ref[0, :KW, :] = kv_nat_sc[0].T
            st_ref[0, KW:, :] = kv_nat_sc[1].T


def _group_attn_prompt(x, w_g, sinks, dil, has_sink):
    bn, s, d = x.shape
    span = BLOCK * dil
    tstep = max(STEP_TOKENS, span)
    assert d == D_MODEL and s % tstep == 0 and tstep % (UNITS_PER_PROJ * BLOCK) == 0, (x.shape, dil)
    nstep = s // tstep
    pos = (jnp.arange(s // span)[:, None, None] * span + jnp.arange(dil)[None, :, None]
           + jnp.arange(BLOCK)[None, None, :] * dil).reshape(s)
    half = ROT_DIM // 2
    inv = ROPE_THETA ** (-jnp.arange(half, dtype=F32) / half)
    ang = pos.astype(F32)[:, None] * inv[None, :]
    cos_t, sin_t = jnp.cos(ang).T, jnp.sin(ang).T
    tab_spec = pl.BlockSpec((half, tstep), lambda b, j: (0, j))
    o, lse, st = pl.pallas_call(
        functools.partial(_group_attn_kernel, has_sink=has_sink, dil=dil, tstep=tstep, nstep=nstep),
        grid=(bn, nstep),
        in_specs=([pl.BlockSpec(memory_space=pltpu.SMEM)]
                  + [pl.BlockSpec((1, tstep, LANES), functools.partial(lambda c, b, j: (b, j, c), c))
                     for c in range(X_SLABS)]
                  + [pl.BlockSpec((GW, d), lambda b, j: (0, 0)), tab_spec, tab_spec]),
        out_specs=[
            pl.BlockSpec((1, tstep, QW), lambda b, j: (b, j, 0)),
            pl.BlockSpec((1, tstep, LANES), lambda b, j: (b, j, 0)),
            pl.BlockSpec((1, 2 * KW, span), lambda b, j: (b, 0, 0)),
        ],
        out_shape=[
            jax.ShapeDtypeStruct((bn, s, QW), BF16),
            jax.ShapeDtypeStruct((bn, s, LANES), F32),
            jax.ShapeDtypeStruct((bn, 2 * KW, span), F32),
        ],
        scratch_shapes=[pltpu.VMEM((dil, BLOCK, KW), BF16), pltpu.VMEM((dil, KW, BLOCK), BF16),
                        pltpu.VMEM((2, span, LANES), F32), pltpu.VMEM((QW // LANES, tstep, LANES), F32),
                        pltpu.VMEM((2, 2 * BLOCK, Q_HEADS * BLOCK), F32),
                        pltpu.VMEM((2, 2 * BLOCK, Q_HEADS * BLOCK), BF16),
                        pltpu.VMEM((2, 2 * BLOCK, 2 * BLOCK), F32),
                        pltpu.VMEM((2, UNITS_PER_PROJ * BLOCK, d), BF16),
                        pltpu.VMEM((2, GW, UNITS_PER_PROJ * BLOCK), F32)],
        compiler_params=_cparams(("arbitrary", "arbitrary")),
        name=f"l0_group_attn_d{dil}_{'sink' if has_sink else 'nosink'}",
    )(sinks, *([x] * X_SLABS), w_g.T, cos_t, sin_t)
    st = jnp.transpose(st.reshape(bn, 2, KV_HEADS, HEAD_DIM, span), (0, 4, 1, 2, 3))
    return o.reshape(bn * s, QW), lse.reshape(bn * s, LANES), st


SUB_ROWS = 256


def _l0_out_kernel(x_ref, oa_ref, o0_ref, o1_ref, o2_ref, l0_ref, l1_ref, l2_ref,
                   wz_ref, wo_ref, g_ref, b_ref, y_ref, z_sc, mix_sc):
    tm = x_ref.shape[0]
    n_sub = tm // SUB_ROWS

    def gate_proj(i):
        r = slice(i * SUB_ROWS, (i + 1) * SUB_ROWS)
        z_sc[r, :] = jnp.dot(x_ref[r, :].astype(BF16), wz_ref[...], preferred_element_type=F32)

    def mix_stage(i):
        r = slice(i * SUB_ROWS, (i + 1) * SUB_ROWS)
        l0, l1, l2 = l0_ref[r, :], l1_ref[r, :], l2_ref[r, :]
        m = jnp.maximum(jnp.maximum(l0, l1), l2)
        e0, e1, e2 = jnp.exp(l0 - m), jnp.exp(l1 - m), jnp.exp(l2 - m)
        inv = 1.0 / (e0 + e1 + e2)
        w0, w1, w2 = e0 * inv, e1 * inv, e2 * inv
        parts = []
        for hq in range(Q_HEADS):
            c = slice(hq * HEAD_DIM, (hq + 1) * HEAD_DIM)
            bshape = (SUB_ROWS, HEAD_DIM)
            parts.append(jnp.broadcast_to(w0[:, hq:hq + 1], bshape) * o0_ref[r, c].astype(F32)
                         + jnp.broadcast_to(w1[:, hq:hq + 1], bshape) * o1_ref[r, c].astype(F32)
                         + jnp.broadcast_to(w2[:, hq:hq + 1], bshape) * o2_ref[r, c].astype(F32))
        mix = jnp.concatenate([oa_ref[r, :].astype(F32)] + parts, axis=1)
        z = z_sc[r, :]
        mix_sc[r, :] = (mix * (z * jax.nn.sigmoid(z))).astype(BF16)

    def out_stage(i):
        r = slice(i * SUB_ROWS, (i + 1) * SUB_ROWS)
        y = jnp.dot(mix_sc[r, :], wo_ref[...], preferred_element_type=F32)
        y_ref[r, :] = _layer_norm_rows(DN_ALPHA * x_ref[r, :] + y, g_ref[...], b_ref[...])

    gate_proj(0)
    for i in range(n_sub):
        if i + 1 < n_sub:
            gate_proj(i + 1)
        mix_stage(i)
        out_stage(i)


def _l0_out(x2, oa, obs, lses, wz, wo, g, b):
    n, d = x2.shape
    tm = 512
    assert n % tm == 0 and tm % SUB_ROWS == 0, x2.shape
    row = lambda w: pl.BlockSpec((tm, w), lambda i: (i, 0))
    full = lambda a: pl.BlockSpec(a.shape, lambda i: (0, 0))
    return pl.pallas_call(
        _l0_out_kernel,
        grid=(n // tm,),
        in_specs=[row(d), row(QW), row(QW), row(QW), row(QW), row(LANES), row(LANES), row(LANES),
                  full(wz), full(wo), full(g), full(b)],
        out_specs=row(d),
        out_shape=jax.ShapeDtypeStruct((n, d), F32),
        scratch_shapes=[pltpu.VMEM((tm, MIX), F32), pltpu.VMEM((tm, MIX), BF16)],
        compiler_params=_cparams(("arbitrary",)),
        name="l0_out",
    )(x2, oa, *obs, *lses, wz, wo, g, b)


def _proj_kernel(x_ref, w_ref, cos_ref, slo_ref, shi_ref, h_ref, *, n_rope, n_scaled, tn):
    j = pl.program_id(0)
    h = jnp.dot(x_ref[...].astype(BF16), w_ref[...], preferred_element_type=F32)
    if n_rope:
        cos, slo, shi = cos_ref[...], slo_ref[...], shi_ref[...]
        roped = jnp.concatenate(
            [_rope(h[:, c * LANES:(c + 1) * LANES], cos, slo, shi) for c in range(tn // LANES)], axis=1)
        h = jnp.where(j < n_rope, roped, h)
        h = h * jnp.where(j < n_scaled, HEAD_DIM ** -0.5, 1.0)
    h_ref[...] = h


def _proj(x2, w, tabs, tn, n_rope=0, n_scaled=0, ncol=None):
    n, d = x2.shape
    ncol = w.shape[1] if ncol is None else ncol
    tab_spec = pl.BlockSpec((n, LANES), lambda j: (0, 0))
    return pl.pallas_call(
        functools.partial(_proj_kernel, n_rope=n_rope, n_scaled=n_scaled, tn=tn),
        grid=(ncol // tn,),
        in_specs=[pl.BlockSpec((n, d), lambda j: (0, 0)),
                  pl.BlockSpec((d, tn), lambda j: (0, j)),
                  tab_spec, tab_spec, tab_spec],
        out_specs=pl.BlockSpec((n, tn), lambda j: (0, j)),
        out_shape=jax.ShapeDtypeStruct((n, ncol), F32),
        compiler_params=_cparams(("arbitrary",)),
        name=f"decode_proj_{ncol}",
    )(x2, w, *tabs)


def _dec_prepare(h_ref, rows, g):
    lo_half = lax.broadcasted_iota(jnp.int32, (TPAD, LANES), 1) < HEAD_DIM
    k_base = N_GROUPS * QW
    v_base = k_base + N_GROUPS * KW
    blocks = []
    for c in range(QW // LANES):
        slab = h_ref[rows, g * QW + c * LANES: g * QW + (c + 1) * LANES]
        swapped = pltpu.roll(slab, HEAD_DIM, 1)
        if c < QW // LANES // 2:
            blocks += [jnp.where(lo_half, slab, 0.0), jnp.where(lo_half, swapped, 0.0)]
        else:
            blocks += [jnp.where(lo_half, 0.0, swapped), jnp.where(lo_half, 0.0, slab)]
    qm = jnp.concatenate(blocks, axis=0).astype(BF16)
    k_new = h_ref[rows, k_base + g * KW: k_base + (g + 1) * KW]
    v_new = h_ref[rows, v_base + g * KW: v_base + (g + 1) * KW]
    zpad = jnp.zeros((LANES - TPAD, 2 * KW), F32)
    kv_new = jnp.concatenate([jnp.concatenate([k_new, v_new], axis=1), zpad], axis=0)
    return qm, kv_new


def _dec_scores(qm, kv_new, cache_ref, sc_ref, s_i):
    nkeys = cache_ref.shape[2]
    nt = (((1,), (1,)), ((), ()))
    sc_ref[s_i, :, :nkeys] = jnp.dot(qm, cache_ref[s_i, :KW, :].astype(BF16), preferred_element_type=F32)
    sc_ref[s_i, :, nkeys:] = lax.dot_general(qm, kv_new[:, :KW].astype(BF16), nt, preferred_element_type=F32)


def _dec_shift(kv_new, cache_ref, new_ref, s_i, t_new):
    nkeys = cache_ref.shape[2]
    lane_k = lax.broadcasted_iota(jnp.int32, (KW, LANES), 1)
    new_cols = pltpu.roll(kv_new.T, LANES - t_new, 1)
    for half in range(2):
        hrows = slice(half * KW, (half + 1) * KW)
        shifted = pltpu.roll(cache_ref[s_i, hrows, :], nkeys - t_new, 1)
        new_ref[s_i, hrows, :] = shifted
        new_ref[s_i, hrows, nkeys - LANES:] = jnp.where(lane_k >= LANES - t_new, new_cols[hrows],
                                                        shifted[:, nkeys - LANES:])


def _dec_values(kv_new, cache_ref, sc_ref, s_i, rows, dil, t_new, sink_ref, o_ref, l_ref):
    nrow = Q_HEADS * TPAD
    nkeys = cache_ref.shape[2]
    ncol = nkeys + LANES
    nt = (((1,), (1,)), ((), ()))
    lane8 = lax.broadcasted_iota(jnp.int32, (TPAD, LANES), 1)
    lo_half = lane8 < HEAD_DIM
    tok = lax.broadcasted_iota(jnp.int32, (nrow, ncol), 0) % TPAD
    col = lax.broadcasted_iota(jnp.int32, (nrow, ncol), 1)
    new_j = col - nkeys
    valid = (((col < nkeys) & (((col - tok) & (dil - 1)) == 0) & (col >= tok))
             | ((col >= nkeys) & (((tok - new_j) & (dil - 1)) == 0) & (new_j <= tok) & (new_j < t_new)))
    s_all = jnp.where(valid, sc_ref[s_i], NEG_INF)
    m = jnp.max(s_all, axis=-1, keepdims=True)
    if sink_ref is not None:
        sink = sink_ref[:, 0:1]
        m = jnp.maximum(m, sink)
    p = jnp.exp(s_all - m)
    den = jnp.sum(p, axis=-1, keepdims=True)
    if sink_ref is not None:
        den = den + jnp.exp(sink - m)
    pb = p.astype(BF16)
    o = (lax.dot_general(pb[:, :nkeys], cache_ref[s_i, KW:, :].astype(BF16), nt, preferred_element_type=F32)
         + jnp.dot(pb[:, nkeys:], kv_new[:, KW:].astype(BF16), preferred_element_type=F32)) / den
    for c in range(QW // LANES):
        ev = o[(2 * c) * TPAD:(2 * c + 1) * TPAD]
        od = o[(2 * c + 1) * TPAD:(2 * c + 2) * TPAD]
        if c < QW // LANES // 2:
            slab = jnp.where(lo_half, ev, pltpu.roll(od, HEAD_DIM, 1))
        else:
            slab = jnp.where(lo_half, pltpu.roll(ev, HEAD_DIM, 1), od)
        o_ref[rows, c * LANES:(c + 1) * LANES] = slab.astype(o_ref.dtype)
    if l_ref is not None:
        lse = m + jnp.log(den)
        lse_acc = jnp.zeros((TPAD, LANES), F32)
        for hq in range(Q_HEADS):
            lse_acc = jnp.where(lane8 == hq, lse[hq * TPAD:(hq + 1) * TPAD], lse_acc)
        l_ref[rows, :] = lse_acc


def _cache_rows_minor(c):
    nseq, rows = c.shape[1], c.shape[2]
    return jnp.transpose(c[0], (0, 2, 3, 4, 1)).reshape(nseq, 2 * KW, rows)


def _cache_rows_major(c):
    nseq, _, rows = c.shape
    return jnp.transpose(c.reshape(nseq, 2, KV_HEADS, HEAD_DIM, rows), (0, 4, 1, 2, 3))[None]


def _mlstm_chunk(q, k, v, gates, hh, c_mem, c_norm, m_prev, n_valid):
    L = q.shape[0]
    ig_col = gates[:, hh:hh + 1]
    fg = gates[:, C_HEADS + hh:C_HEADS + hh + 1]
    lf_col = jnp.minimum(fg, 0.0) - jnp.log(1.0 + jnp.exp(-jnp.abs(fg)))
    ri = lax.broadcasted_iota(jnp.int32, (L, L), 0)
    ci = lax.broadcasted_iota(jnp.int32, (L, L), 1)
    tril = ri >= ci
    eye = ri == ci
    lf_row = jnp.sum(jnp.where(eye, lf_col, 0.0), axis=0, keepdims=True)
    ig_row = jnp.sum(jnp.where(eye, ig_col, 0.0), axis=0, keepdims=True)
    b_col = jnp.sum(jnp.where(tril, lf_row, 0.0), axis=1, keepdims=True)
    b_row = jnp.sum(jnp.where(ri <= ci, lf_col, 0.0), axis=0, keepdims=True)
    dmat = jnp.where(tril, b_col - b_row + ig_row, NEG_INF)
    inter = b_col + m_prev
    m = jnp.maximum(inter, jnp.max(dmat, axis=1, keepdims=True))
    pw = jnp.exp(dmat - m)
    qb, kb, vb = q.astype(BF16), k.astype(BF16), v.astype(BF16)
    sc = lax.dot_general(qb, kb, (((1,), (1,)), ((), ())), preferred_element_type=F32) * pw
    carry_w = jnp.exp(inter - m)
    num = (jnp.dot(sc.astype(BF16), vb, preferred_element_type=F32)
           + carry_w * jnp.dot(qb, c_mem.astype(BF16), preferred_element_type=F32))
    den = jnp.sum(sc, axis=1, keepdims=True) + carry_w * jnp.sum(q * c_norm, axis=1, keepdims=True)
    h = num / jnp.maximum(jnp.abs(den), jnp.exp(-m))
    last = n_valid - 1
    m_new = m[last:last + 1, :]
    b_last = b_col[last:last + 1, :]
    ws = jnp.exp(b_last - b_col + ig_col - m_new)
    if n_valid < L:
        ws = jnp.where(lax.broadcasted_iota(jnp.int32, (L, 1), 0) < n_valid, ws, 0.0)
    decay = jnp.exp(b_last + m_prev - m_new)
    wk = ws * k
    c_mem_new = decay * c_mem + lax.dot_general(
        wk.astype(BF16), vb, (((0,), (0,)), ((), ())), preferred_element_type=F32)
    c_norm_new = decay * c_norm + jnp.sum(wk, axis=0, keepdims=True)
    return h, c_mem_new, c_norm_new, m_new


AUG = LANES
PROJ_COLS = 256


def _l1_prompt_kernel(x_ref, w_ref, wg_ref, bias_ref, sink_ref, hs_ref, ca_ref, c0_ref, c1_ref, c2_ref,
                      h_ref, cmem_ref, cnorm_ref, cmax_ref,
                      oa_ref, o0_ref, o1_ref, o2_ref, l0_ref, l1_ref, l2_ref, na_ref, n0_ref, n1_ref, n2_ref,
                      caug_sc, m_sc, s_sc, lhs_sc, proj_a, proj_b, xb_sc, sa_sc, s0_sc, s1_sc, s2_sc,
                      *, tm, nblk, dec_sb, t_new):
    i = pl.program_id(0)
    blk = i - 1

    @pl.when(i == 0)
    def _():
        proj_b[...] = jnp.zeros_like(proj_b)

    @pl.when(blk % nblk == 0)
    def _():
        caug_sc[...] = jnp.zeros_like(caug_sc)
        m_sc[...] = jnp.zeros_like(m_sc)

    ri = lax.broadcasted_iota(jnp.int32, (BLOCK, BLOCK), 0)
    ci = lax.broadcasted_iota(jnp.int32, (BLOCK, BLOCK), 1)
    tril = ri >= ci
    upper = (ri <= ci).astype(F32)
    ones_blk = jnp.ones((BLOCK, AUG), BF16)
    kscale = C_HEAD_DIM ** -0.5
    nt = (((1,), (1,)), ((), ()))
    ncol = w_ref.shape[1]
    slabs = [(c0, c0 + PROJ_COLS) for c0 in range(0, ncol, PROJ_COLS)] + [(ncol, ncol + LANES)]
    n_chunks = tm // BLOCK

    def body(src, dst):
        xb_sc[...] = x_ref[...].astype(BF16)
        pieces = iter(slabs)

        def project_piece():
            c = next(pieces, None)
            if c is not None:
                w_piece = w_ref[:, c[0]:c[1]] if c[0] < ncol else wg_ref[...]
                dst[:, c[0]:c[1]] = jnp.dot(xb_sc[...], w_piece, preferred_element_type=F32)

        g_t = (src[:, 3 * MIX:] + bias_ref[...]).T[0:TPAD]
        lf_t = jnp.minimum(g_t, 0.0) - jnp.log(1.0 + jnp.exp(-jnp.abs(g_t)))

        def score_stage(c):
            rows = slice(c * BLOCK, (c + 1) * BLOCK)
            qs, ks, vs = [], [], []
            for hh in range(C_HEADS):
                q = src[rows, hh * C_HEAD_DIM:(hh + 1) * C_HEAD_DIM]
                k = src[rows, MIX + hh * C_HEAD_DIM: MIX + (hh + 1) * C_HEAD_DIM] * kscale
                v = src[rows, 2 * MIX + hh * C_HEAD_DIM: 2 * MIX + (hh + 1) * C_HEAD_DIM]
                qs.append(q)
                ks.append(k)
                vs.append(v.astype(BF16))
                s_sc[c % 2, hh] = lax.dot_general(q.astype(BF16), k.astype(BF16), nt, preferred_element_type=F32)
            return dict(rows=rows, par=c % 2, qs=qs, ks=ks, vs=vs)

        def gate_stage(t, m_prevs):
            rows, par = t["rows"], t["par"]
            b_t = jnp.dot(lf_t[:, rows], upper, preferred_element_type=F32, precision=lax.Precision.HIGHEST)
            stats = []
            for hh in range(C_HEADS):
                ig_row = g_t[hh:hh + 1, rows]
                lf_row = lf_t[C_HEADS + hh:C_HEADS + hh + 1, rows]
                b_row = b_t[C_HEADS + hh:C_HEADS + hh + 1, :]
                m_prev = m_prevs[hh]
                b_col = jnp.sum(jnp.where(tril, lf_row, 0.0), axis=1, keepdims=True)
                dmat = jnp.where(tril, b_col - (b_row - ig_row), NEG_INF)
                inter = b_col + m_prev
                m = jnp.maximum(inter, jnp.max(dmat, axis=1, keepdims=True))
                sc = s_sc[par, hh] * jnp.exp(dmat - m)
                carry_w = jnp.exp(inter - m)
                lhs_sc[par, hh, :, :BLOCK] = sc.astype(BF16)
                lhs_sc[par, hh, :, BLOCK:] = (carry_w * t["qs"][hh]).astype(BF16)
                m_new = m[BLOCK - 1:BLOCK, :]
                b_last = b_col[BLOCK - 1:BLOCK, :]
                ws_row = jnp.exp(b_last - b_row + ig_row - m_new)
                decay = jnp.exp(b_last + m_prev - m_new)
                stats.append((m, m_new, ws_row, decay))
            t["stats"] = stats

        def memory_stage(t):
            rows, par = t["rows"], t["par"]
            for hh in range(C_HEADS):
                cols = slice(hh * C_HEAD_DIM, (hh + 1) * C_HEAD_DIM)
                m, m_new, ws_row, decay = t["stats"][hh]
                caug = caug_sc[hh]
                v_aug = jnp.concatenate([t["vs"][hh], ones_blk], axis=1)
                rhs = jnp.concatenate([v_aug, caug.astype(BF16)], axis=0)
                res = jnp.dot(lhs_sc[par, hh], rhs, preferred_element_type=F32)
                den = jnp.maximum(jnp.abs(res[:, C_HEAD_DIM:]), jnp.exp(-m))
                h_ref[rows, cols] = res[:, :C_HEAD_DIM] / jnp.concatenate([den, den], axis=1)
                wk_t = (t["ks"][hh].T * ws_row).astype(BF16)
                caug_sc[hh] = decay * caug + jnp.dot(wk_t, v_aug, preferred_element_type=F32)

        m_prevs = [m_sc[hh:hh + 1, 0:1] for hh in range(C_HEADS)]
        ts = [score_stage(0), score_stage(1)]
        caches = (ca_ref, c0_ref, c1_ref, c2_ref)
        new_caches = (na_ref, n0_ref, n1_ref, n2_ref)
        o_refs = (oa_ref, o0_ref, o1_ref, o2_ref)
        l_refs = (None, l0_ref, l1_ref, l2_ref)
        sc_refs = (sa_sc, s0_sc, s1_sc, s2_sc)
        dec = []
        for g in range(N_GROUPS):
            for s_i in range(dec_sb):
                rows = slice(s_i * TPAD, (s_i + 1) * TPAD)
                qm, kv_new = _dec_prepare(hs_ref, rows, g)
                _dec_scores(qm, kv_new, caches[g], sc_refs[g], s_i)
                dec.append((g, s_i, rows, kv_new))
        n_dec = len(dec)
        shifts = iter(dec)

        def shift_items(k):
            for _ in range(k):
                d = next(shifts, None)
                if d is not None:
                    _dec_shift(d[3], caches[d[0]], new_caches[d[0]], d[1], t_new)
                    project_piece()

        project_piece()
        for c in range(n_chunks):
            gate_stage(ts[c], m_prevs)
            m_prevs = [st[1] for st in ts[c]["stats"]]
            project_piece()
            if c + 2 < n_chunks:
                ts.append(score_stage(c + 2))
                project_piece()
            shift_items(n_dec // (2 * n_chunks))
            if c >= 1:
                memory_stage(ts[c - 1])
                project_piece()
        memory_stage(ts[n_chunks - 1])
        project_piece()
        shift_items(n_dec)
        for g, s_i, rows, kv_new in dec:
            _dec_values(kv_new, caches[g], sc_refs[g], s_i, rows, DILATIONS[g], t_new,
                        sink_ref if g == 0 else None, o_refs[g], l_refs[g])
            project_piece()
        for _ in slabs:
            project_piece()
        for hh in range(C_HEADS):
            m_sc[hh:hh + 1, :] = jnp.broadcast_to(m_prevs[hh], (1, LANES))

    @pl.when(i % 2 == 0)
    def _():
        body(proj_b, proj_a)

    @pl.when(i % 2 == 1)
    def _():
        body(proj_a, proj_b)

    @pl.when((blk >= 0) & (blk % nblk == nblk - 1))
    def _():
        for hh in range(C_HEADS):
            cmem_ref[0, hh] = caug_sc[hh, :, :C_HEAD_DIM]
            cnorm_ref[0, hh:hh + 1, :] = caug_sc[hh, :, C_HEAD_DIM:].T[0:1, :]
        cmax_ref[0] = m_sc[...]


def _l1_prompt(x, w1b, w_gates, bias_row, hs, caches, sink_rows, t_new):
    bn, s, d = x.shape
    cts = [_cache_rows_minor(c) for c in caches]
    nseq, ng = cts[0].shape[0], len(cts)
    dec_sb = 2
    ndec = nseq // dec_sb
    tm = (bn * s) // ndec
    assert nseq % dec_sb == 0 and tm * ndec == bn * s and tm % (2 * BLOCK) == 0 and s % tm == 0, (x.shape, nseq)
    nblk = s // tm
    ntot = bn * nblk
    ncol = 3 * MIX + LANES
    state_blk = lambda i: (jnp.maximum(i - 1, 0) // nblk, 0, 0)
    dec_blk = lambda i: jnp.minimum(i, ndec - 1)
    drow = lambda w: pl.BlockSpec((dec_sb * TPAD, w), lambda i: (dec_blk(i), 0))
    cspec = lambda c: pl.BlockSpec((dec_sb,) + c.shape[1:], lambda i: (dec_blk(i), 0, 0))
    outs = pl.pallas_call(
        functools.partial(_l1_prompt_kernel, tm=tm, nblk=nblk, dec_sb=dec_sb, t_new=t_new),
        grid=(ntot + 1,),
        in_specs=[pl.BlockSpec((tm, d), lambda i: (jnp.minimum(i, ntot - 1), 0)),
                  pl.BlockSpec((d, 3 * MIX), lambda i: (0, 0)),
                  pl.BlockSpec((d, LANES), lambda i: (0, 0)),
                  pl.BlockSpec((1, LANES), lambda i: (0, 0)),
                  pl.BlockSpec(sink_rows.shape, lambda i: (0, 0)),
                  drow(hs.shape[1])] + [cspec(c) for c in cts],
        out_specs=([pl.BlockSpec((tm, MIX), lambda i: (jnp.maximum(i - 1, 0), 0)),
                    pl.BlockSpec((1, C_HEADS, C_HEAD_DIM, C_HEAD_DIM), lambda i: state_blk(i) + (0,)),
                    pl.BlockSpec((1, C_HEADS, C_HEAD_DIM), state_blk),
                    pl.BlockSpec((1, TPAD, LANES), state_blk)]
                   + [drow(QW)] * ng + [drow(LANES)] * (ng - 1) + [cspec(c) for c in cts]),
        out_shape=([jax.ShapeDtypeStruct((bn * s, MIX), F32),
                    jax.ShapeDtypeStruct((bn, C_HEADS, C_HEAD_DIM, C_HEAD_DIM), F32),
                    jax.ShapeDtypeStruct((bn, C_HEADS, C_HEAD_DIM), F32),
                    jax.ShapeDtypeStruct((bn, TPAD, LANES), F32)]
                   + [jax.ShapeDtypeStruct((nseq * TPAD, QW), BF16)] * ng
                   + [jax.ShapeDtypeStruct((nseq * TPAD, LANES), F32)] * (ng - 1)
                   + [jax.ShapeDtypeStruct(c.shape, F32) for c in cts]),
        scratch_shapes=([pltpu.VMEM((C_HEADS, C_HEAD_DIM, C_HEAD_DIM + AUG), F32),
                         pltpu.VMEM((TPAD, LANES), F32),
                         pltpu.VMEM((2, C_HEADS, BLOCK, BLOCK), F32),
                         pltpu.VMEM((2, C_HEADS, BLOCK, BLOCK + C_HEAD_DIM), BF16),
                         pltpu.VMEM((tm, ncol), F32), pltpu.VMEM((tm, ncol), F32),
                         pltpu.VMEM((tm, d), BF16)]
                        + [pltpu.VMEM((dec_sb, Q_HEADS * TPAD, c.shape[2] + LANES), F32) for c in cts]),
        compiler_params=_cparams(("arbitrary",)),
        name="l1_prompt_mlstm_dec",
    )(x.reshape(bn * s, d), w1b, w_gates, bias_row, sink_rows, hs, *cts)
    return (outs[:4], outs[4:4 + ng], outs[4 + ng:3 + 2 * ng],
            [_cache_rows_major(c) for c in outs[3 + 2 * ng:]])


CM_SLOTS = 3


def _l1_out_kernel(x_ref, h_ref, wog_ref, wz_ref, mhg_ref, wo_ref, g_ref, b_ref, *refs, dec_sb, t_new):
    if dec_sb:
        (p_ref, pg_ref, bias_ref, cm_hbm, cn_ref, cx_ref, y_ref, hd_ref, cmo_ref, cno_ref, cxo_ref,
         oz_sc, mix_sc, cm_buf, cm_sem) = refs
        step, n_steps = pl.program_id(0), pl.num_programs(0)

        def cm_copy(s):
            slot = s % CM_SLOTS
            return pltpu.make_async_copy(cm_hbm.at[pl.ds(s * dec_sb, dec_sb)], cm_buf.at[slot], cm_sem.at[slot])

        @pl.when(step == 0)
        def _():
            for s in range(CM_SLOTS - 1):
                cm_copy(s).start()

        @pl.when(step + CM_SLOTS - 1 < n_steps)
        def _():
            cm_copy(step + CM_SLOTS - 1).start()

        cm_copy(step).wait()
        cm_slot = step % CM_SLOTS
    else:
        y_ref, oz_sc, mix_sc = refs
    tm = x_ref.shape[0]
    n_sub = tm // SUB_ROWS

    def gate_proj(i):
        r = slice(i * SUB_ROWS, (i + 1) * SUB_ROWS)
        xb = x_ref[r, :].astype(BF16)
        oz_sc[r, :MIX] = jnp.dot(xb, wog_ref[...], preferred_element_type=F32)
        oz_sc[r, MIX:] = jnp.dot(xb, wz_ref[...], preferred_element_type=F32)

    def mix_stage(i, between):
        r = slice(i * SUB_ROWS, (i + 1) * SUB_ROWS)
        for hh in range(C_HEADS):
            cols = slice(hh * C_HEAD_DIM, (hh + 1) * C_HEAD_DIM)
            hg = jax.nn.sigmoid(oz_sc[r, cols]) * h_ref[r, cols]
            mu = jnp.mean(hg, axis=-1, keepdims=True)
            c = hg - mu
            var = jnp.mean(c * c, axis=-1, keepdims=True)
            hn = c * lax.rsqrt(var + MH_EPS) * mhg_ref[:, cols]
            z = oz_sc[r, MIX + hh * C_HEAD_DIM:MIX + (hh + 1) * C_HEAD_DIM]
            mix_sc[r, cols] = (hn * (z * jax.nn.sigmoid(z))).astype(BF16)
            between()

    def out_stage(i):
        r = slice(i * SUB_ROWS, (i + 1) * SUB_ROWS)
        y = jnp.dot(mix_sc[r, :], wo_ref[...], preferred_element_type=F32)
        y_ref[r, :] = _layer_norm_rows(DN_ALPHA * x_ref[r, :] + y, g_ref[...], b_ref[...])

    kscale = C_HEAD_DIM ** -0.5
    chunks = [(s_i, hh) for s_i in range(dec_sb) for hh in range(C_HEADS)]
    todo = iter(chunks)

    def decode_chunks(k):
        for _ in range(k):
            it = next(todo, None)
            if it is None:
                return
            s_i, hh = it
            rows = slice(s_i * TPAD, (s_i + 1) * TPAD)
            cols = slice(hh * C_HEAD_DIM, (hh + 1) * C_HEAD_DIM)
            gates = pg_ref[rows, :] + bias_ref[...]
            q = p_ref[rows, cols]
            k_ = p_ref[rows, MIX + hh * C_HEAD_DIM: MIX + (hh + 1) * C_HEAD_DIM] * kscale
            v = p_ref[rows, 2 * MIX + hh * C_HEAD_DIM: 2 * MIX + (hh + 1) * C_HEAD_DIM]
            h, c_new, n_new, m_new = _mlstm_chunk(
                q, k_, v, gates, hh, cm_buf[cm_slot, s_i, hh], cn_ref[s_i, hh:hh + 1, :],
                cx_ref[s_i, hh:hh + 1, 0:1], t_new)
            hd_ref[rows, cols] = h
            cmo_ref[s_i, hh] = c_new
            cno_ref[s_i, hh:hh + 1, :] = n_new
            cxo_ref[s_i, hh:hh + 1, :] = jnp.broadcast_to(m_new, (1, LANES))

    per_slot = -(-len(chunks) // (n_sub * (C_HEADS + 1)))
    gate_proj(0)
    for i in range(n_sub):
        if i + 1 < n_sub:
            gate_proj(i + 1)
        mix_stage(i, lambda: decode_chunks(per_slot))
        out_stage(i)
        decode_chunks(per_slot)
    decode_chunks(len(chunks))


def _l1_out(x2, h2, w1b, mhg, wo, g, b, dec=None):
    n, d = x2.shape
    tm = 512
    row = lambda w: pl.BlockSpec((tm, w), lambda i: (i, 0))
    full = lambda a: pl.BlockSpec(a.shape, lambda i: (0, 0))
    assert n % tm == 0, x2.shape
    wcol = lambda j: pl.BlockSpec((d, MIX), lambda i: (0, j))
    og_blk, z_blk = 3, 4
    in_specs = [row(d), row(MIX), wcol(og_blk), wcol(z_blk), full(mhg), full(wo), full(g), full(b)]
    out_specs = [row(d)]
    out_shape = [jax.ShapeDtypeStruct((n, d), F32)]
    args = [x2, h2, w1b, w1b, mhg, wo, g, b]
    scratch = [pltpu.VMEM((tm, 2 * MIX), F32), pltpu.VMEM((tm, MIX), BF16)]
    dec_sb, t_new = 0, 0
    if dec is not None:
        proj, proj_g, bias_row, c_mem, c_norm, c_max_pad, t_new = dec
        nseq = c_mem.shape[0]
        dec_sb = nseq // (n // tm)
        assert dec_sb * (n // tm) == nseq, (nseq, n, tm)
        sblk = lambda a: pl.BlockSpec((dec_sb,) + a.shape[1:], lambda i: (i,) + (0,) * (a.ndim - 1))
        in_specs += [pl.BlockSpec((dec_sb * TPAD, proj.shape[1]), lambda i: (i, 0)),
                     pl.BlockSpec((dec_sb * TPAD, LANES), lambda i: (i, 0)), full(bias_row),
                     pl.BlockSpec(memory_space=pl.ANY), sblk(c_norm), sblk(c_max_pad)]
        assert n // tm >= CM_SLOTS - 1
        scratch += [pltpu.VMEM((CM_SLOTS, dec_sb) + c_mem.shape[1:], F32), pltpu.SemaphoreType.DMA((CM_SLOTS,))]
        out_specs += [pl.BlockSpec((dec_sb * TPAD, MIX), lambda i: (i, 0)), sblk(c_mem), sblk(c_norm), sblk(c_max_pad)]
        out_shape += [jax.ShapeDtypeStruct((nseq * TPAD, MIX), F32), jax.ShapeDtypeStruct(c_mem.shape, F32),
                      jax.ShapeDtypeStruct(c_norm.shape, F32), jax.ShapeDtypeStruct(c_max_pad.shape, F32)]
        args += [proj, proj_g, bias_row, c_mem, c_norm, c_max_pad]
    outs = pl.pallas_call(
        functools.partial(_l1_out_kernel, dec_sb=dec_sb, t_new=t_new),
        grid=(n // tm,),
        in_specs=in_specs,
        out_specs=out_specs,
        out_shape=out_shape,
        scratch_shapes=scratch,
        compiler_params=_cparams(("arbitrary",)),
        name="l1_out_dec" if dec is not None else "l1_out",
    )(*args)
    return outs[0] if dec is None else outs


def _group_cols(g):
    if g == 0:
        return [(0, QW), (QW, QW + KW), (QW + KW, GW)]
    b = g - 1
    qb0 = GW
    kb0 = GW + 3 * QW
    vb0 = kb0 + 3 * KW
    return [(qb0 + b * QW, qb0 + (b + 1) * QW), (kb0 + b * KW, kb0 + (b + 1) * KW),
            (vb0 + b * KW, vb0 + (b + 1) * KW)]


def kernel(x_prompt, x_sample, cache_a_kv, cache_b0_kv, cache_b1_kv, cache_b2_kv, state_c_mem,
           state_c_norm, state_c_max, w_in0, sinks0, w_out0, w_in1, b_gates1, mh_norm1, w_out1,
           ln_g, ln_b):
    bn, s, d = x_prompt.shape
    nseq, t_new, _ = x_sample.shape
    w0 = w_in0[0]
    z0 = 4 * GW
    w_groups = [jnp.concatenate([w0[:, a:b] for a, b in _group_cols(g)], axis=1).astype(BF16)
                for g in range(N_GROUPS)]
    wz0 = w0[:, z0:].astype(BF16)
    wo0 = w_out0[0].astype(BF16)
    sinks = sinks0[0].astype(F32)
    g0, b0 = ln_g[0][None, :], ln_b[0][None, :]
    g1, b1 = ln_g[1][None, :], ln_b[1][None, :]

    outs, lses, states_p = [], [], []
    for g in range(N_GROUPS):
        o, lse, st = _group_attn_prompt(x_prompt, w_groups[g], sinks, DILATIONS[g], g == 0)
        outs.append(o)
        lses.append(lse)
        states_p.append(st)
    xp2 = x_prompt.reshape(bn * s, d)
    y0p = _l0_out(xp2, outs[0], outs[1:], lses[1:], wz0, wo0, g0, b0)

    xs2 = jnp.pad(x_sample, ((0, 0), (0, TPAD - t_new), (0, 0))).reshape(nseq * TPAD, d)
    w_dec = jnp.concatenate(
        [w_groups[g][:, :QW] for g in range(N_GROUPS)] + [w_groups[g][:, QW:QW + KW] for g in range(N_GROUPS)]
        + [w_groups[g][:, QW + KW:] for g in range(N_GROUPS)], axis=1)
    tabs_s = tuple(jnp.tile(t, (nseq, 1)) for t in _rope_tables(PAST_LEN + jnp.arange(TPAD)))
    tn = 512
    hs = _proj(xs2, w_dec, tabs_s, tn, n_rope=(N_GROUPS * (QW + KW)) // tn, n_scaled=(N_GROUPS * QW) // tn)
    sink_rows = jnp.broadcast_to(jnp.repeat(sinks, TPAD)[:, None], (Q_HEADS * TPAD, LANES))

    w1b = w_in1[0].astype(BF16)
    w_gates = jnp.pad(w1b[:, 5 * MIX:], ((0, 0), (0, LANES - 2 * C_HEADS)))
    wo1 = w_out1[0].astype(BF16)
    bias_row = jnp.pad(b_gates1[0].astype(F32), (0, LANES - 2 * C_HEADS))[None, :]
    mhg = mh_norm1[0][None, :]

    (hp, cmem_p, cnorm_p, cmax_p), dec_o, dec_l, states_s = _l1_prompt(
        y0p.reshape(bn, s, d), w1b, w_gates, bias_row, hs,
        (cache_a_kv, cache_b0_kv, cache_b1_kv, cache_b2_kv), sink_rows, t_new)

    y0s = _l0_out(xs2, dec_o[0], dec_o[1:], dec_l, wz0, wo0, g0, b0)

    no_tabs = tuple(jnp.zeros((nseq * TPAD, LANES), F32) for _ in range(3))
    ps = _proj(y0s, w1b, no_tabs, 512, ncol=3 * MIX)
    ps_g = _proj(y0s, w_gates, no_tabs, LANES)
    cmax_pad = jnp.broadcast_to(state_c_max[0][:, :, None], (nseq, C_HEADS, LANES))
    y1p, hsd, cmem_s, cnorm_s, cmax_s = _l1_out(
        y0p, hp, w1b, mhg, wo1, g1, b1,
        dec=(ps, ps_g, bias_row, state_c_mem[0], state_c_norm[0], cmax_pad, t_new))
    y1s = _l1_out(y0s, hsd, w1b, mhg, wo1, g1, b1)

    y_prompt = y1p.reshape(bn, s, d)
    y_sample = y1s.reshape(nseq, TPAD, d)[:, :t_new]
    return (y_prompt, y_sample,
            states_p[0][None], states_p[1][None], states_p[2][None], states_p[3][None],
            cmem_p[None], cnorm_p[None], cmax_p[:, :C_HEADS, 0][None],
            states_s[0], states_s[1], states_s[2], states_s[3],
            cmem_s[None], cnorm_s[None], cmax_s[:, :, 0][None])
```

```python
import functools

import jax
import jax.numpy as jnp
from jax import lax
from jax.experimental import pallas as pl
from jax.experimental.pallas import tpu as pltpu

F32 = jnp.float32
BF16 = jnp.bfloat16

D_MODEL = 1024
PAST_LEN = 16384
HEAD_DIM = 64
ROT_DIM = HEAD_DIM // 4
ROPE_THETA = 500000.0
BLOCK = 128
Q_HEADS = 8
KV_HEADS = 2
GROUP = Q_HEADS // KV_HEADS
QW = Q_HEADS * HEAD_DIM
KW = KV_HEADS * HEAD_DIM
GW = QW + 2 * KW
DILATIONS = (1, 1, 4, 16)
N_GROUPS = 4
C_HEADS = 4
C_HEAD_DIM = 256
MIX = 1024
DEPTH = 2
DN_ALPHA = (2 * DEPTH) ** 0.25
LN_EPS = 1e-5
MH_EPS = 1e-6
NEG_INF = -1e30
LOG2E = 1.4426950408889634
LN2 = 0.6931471805599453
LANES = 128
TPAD = 8
VMEM_LIMIT = 56 * 1024 * 1024


SMALL_VMEM_LIMIT = 40 * 1024 * 1024


def _cparams(sem, vmem=VMEM_LIMIT):
    return pltpu.CompilerParams(dimension_semantics=sem, vmem_limit_bytes=vmem)


def _layer_norm_rows(v, g, b):
    mu = jnp.mean(v, axis=-1, keepdims=True)
    c = v - mu
    var = jnp.mean(c * c, axis=-1, keepdims=True)
    return c * lax.rsqrt(var + LN_EPS) * g + b


def _rope(slab, cos, sin_lo, sin_hi):
    return (slab * cos + pltpu.roll(slab, LANES - ROT_DIM // 2, 1) * sin_lo
            + pltpu.roll(slab, ROT_DIM // 2, 1) * sin_hi)


def _rope_tables(pos):
    half = ROT_DIM // 2
    inv = ROPE_THETA ** (-jnp.arange(half, dtype=F32) / half)
    ang = pos.astype(F32)[:, None] * inv[None, :]
    cos, sin = jnp.cos(ang), jnp.sin(ang)
    lane = jnp.arange(LANES) % HEAD_DIM
    f = lane % half
    cos_t = jnp.where(lane[None, :] < ROT_DIM, cos[:, f], 1.0)
    sin_lo = jnp.where(lane[None, :] < half, -sin[:, f], 0.0)
    sin_hi = jnp.where((lane[None, :] >= half) & (lane[None, :] < ROT_DIM), sin[:, f], 0.0)
    return cos_t.astype(F32), sin_lo.astype(F32), sin_hi.astype(F32)


STEP_TOKENS = 2048
X_SLABS = D_MODEL // LANES
UNITS_PER_PROJ = 4
ONES_ROWS = 16


def _rope_t(blk, cos, sin):
    half = ROT_DIM // 2
    x1, x2 = blk[:half], blk[half:ROT_DIM]
    return jnp.concatenate([x1 * cos - x2 * sin, x2 * cos + x1 * sin, blk[ROT_DIM:]], axis=0)


def _group_attn_kernel(sink_ref, *refs, has_sink, dil, tstep, nstep):
    x_refs = refs[:X_SLABS]
    (wt_ref, cos_ref, sin_ref, o_ref, lse_ref, st_ref,
     kprev_sc, vprev_sc, kv_nat_sc, o_sc, st_sc, pt_sc, bias_sc, xq_sc, ht_sc) = refs[X_SLABS:]
    j = pl.program_id(1)
    n_units = tstep // BLOCK
    is_last = j == nstep - 1

    @pl.when(j == 0)
    def _():
        kprev_sc[...] = jnp.zeros_like(kprev_sc)
        vprev_sc[...] = jnp.zeros_like(vprev_sc)

    scale = HEAD_DIM ** -0.5 * LOG2E
    kidx = lax.broadcasted_iota(jnp.int32, (2 * BLOCK, 2 * BLOCK), 0)
    qidx = lax.broadcasted_iota(jnp.int32, (2 * BLOCK, 2 * BLOCK), 1) % BLOCK
    dist = qidx + BLOCK - kidx
    band = (dist >= 0) & (dist <= BLOCK)
    bias_sc[0] = jnp.where(band, 0.0, NEG_INF)
    bias_sc[1] = jnp.where(band & (kidx >= jnp.where(j > 0, 0, BLOCK)), 0.0, NEG_INF)
    lane2 = lax.broadcasted_iota(jnp.int32, (1, 2 * BLOCK), 1)
    ones_rows = jnp.ones((ONES_ROWS, 2 * BLOCK), F32)
    zeros_half = jnp.zeros((HEAD_DIM, 2 * BLOCK), F32)
    nt = (((1,), (1,)), ((), ()))

    span = BLOCK * dil

    def unit_rows(u):
        start = (u // dil) * span + u % dil
        return pl.ds(start, BLOCK) if dil == 1 else pl.ds(start, BLOCK, stride=dil)

    def state_rows(u):
        return pl.ds(u % dil, BLOCK) if dil == 1 else pl.ds(u % dil, BLOCK, stride=dil)

    n_quads = n_units // UNITS_PER_PROJ
    k_slab = D_MODEL // UNITS_PER_PROJ

    def quad_units(qd):
        return [UNITS_PER_PROJ * qd + w for w in range(UNITS_PER_PROJ)]

    def load_x(qd):
        xq_sc[qd % 2] = jnp.concatenate(
            [jnp.concatenate([x_refs[c][0, unit_rows(u), :] for c in range(X_SLABS)], axis=1)
             for u in quad_units(qd)], axis=0).astype(BF16)

    def project_piece(qd, kp):
        ks = slice(kp * k_slab, (kp + 1) * k_slab)
        part = lax.dot_general(wt_ref[:, ks], xq_sc[qd % 2, :, ks], nt, preferred_element_type=F32)
        if kp == 0:
            ht_sc[qd % 2] = part
        else:
            ht_sc[qd % 2] += part

    load_x(0)
    for kp in range(UNITS_PER_PROJ):
        project_piece(0, kp)
    carried = {}
    for qd in range(n_quads):
        units = quad_units(qd)
        ht = ht_sc[qd % 2]
        cols = slice(qd * UNITS_PER_PROJ * BLOCK, (qd + 1) * UNITS_PER_PROJ * BLOCK)
        cos, sin = cos_ref[:, cols], sin_ref[:, cols]
        qt = [_rope_t(ht[h * HEAD_DIM:(h + 1) * HEAD_DIM], cos, sin) * scale for h in range(Q_HEADS)]
        kt = jnp.concatenate([_rope_t(ht[QW + h * HEAD_DIM:QW + (h + 1) * HEAD_DIM], cos, sin)
                              for h in range(KV_HEADS)], axis=0)
        vt = ht[QW + KW:GW]
        k_rm = kt.T

        if dil > 1 and units[0] >= n_units - dil:
            @pl.when(is_last)
            def _():
                v_rm = vt.T
                for w, u in enumerate(units):
                    kv_nat_sc[0, state_rows(u), :] = k_rm[w * BLOCK:(w + 1) * BLOCK]
                    kv_nat_sc[1, state_rows(u), :] = v_rm[w * BLOCK:(w + 1) * BLOCK]

        def scores(w, u, carried):
            lanes = slice(w * BLOCK, (w + 1) * BLOCK)
            k_cur = k_rm[w * BLOCK:(w + 1) * BLOCK].astype(BF16)
            vt_cur = vt[:, lanes].astype(BF16)
            if u >= dil:
                k_prev, vt_prev = carried[u - dil]
                mask = 0
            else:
                k_prev, vt_prev = kprev_sc[u], vprev_sc[u]
                mask = 1
            k_cat = jnp.concatenate([k_prev, k_cur], axis=0)
            wq = []
            for c in range(QW // LANES):
                top = jnp.concatenate([qt[2 * c][:, lanes], qt[2 * c + 1][:, lanes]], axis=1)
                wq.append(jnp.concatenate([top, zeros_half] if c < GROUP // 2 else [zeros_half, top], axis=0))
            wq = jnp.concatenate(wq, axis=1).astype(BF16)
            st_sc[u % 2] = jnp.dot(k_cat, wq, preferred_element_type=F32)
            return dict(w=w, u=u, lanes=lanes, k_cur=k_cur, vt_cur=vt_cur, vt_prev=vt_prev, mask=mask)

        def softmax(t):
            par = t["u"] % 2
            ms = []
            for c in range(QW // LANES):
                pc = slice(c * 2 * BLOCK, (c + 1) * 2 * BLOCK)
                st = st_sc[par, :, pc] + bias_sc[t["mask"]]
                m = jnp.max(st, axis=0, keepdims=True)
                if has_sink:
                    sink = jnp.where(lane2 < BLOCK, sink_ref[2 * c], sink_ref[2 * c + 1]) * LOG2E
                    m = jnp.maximum(m, sink)
                ms.append(m)
                pt_sc[par, :, pc] = jnp.exp2(st - m).astype(BF16)
            t["ms"] = ms

        def values(t):
            u, par, lanes = t["u"], t["u"] % 2, t["lanes"]
            lse_rows = []
            for hk in range(KV_HEADS):
                hd = slice(hk * HEAD_DIM, (hk + 1) * HEAD_DIM)
                vt_aug = jnp.concatenate(
                    [jnp.concatenate([t["vt_prev"][hd], t["vt_cur"][hd]], axis=1), ones_rows.astype(BF16)], axis=0)
                acc2 = jnp.dot(vt_aug, pt_sc[par, :, hk * GROUP * BLOCK:(hk + 1) * GROUP * BLOCK],
                               preferred_element_type=F32)
                for i in range(GROUP // 2):
                    c = hk * (GROUP // 2) + i
                    acc = acc2[:, i * 2 * BLOCK:(i + 1) * 2 * BLOCK]
                    m = t["ms"][c]
                    den = acc[HEAD_DIM:HEAD_DIM + 1]
                    if has_sink:
                        sink = jnp.where(lane2 < BLOCK, sink_ref[2 * c], sink_ref[2 * c + 1]) * LOG2E
                        den = den + jnp.exp2(sink - m)
                    ot = acc[:HEAD_DIM] / den
                    lse = m * LN2 + jnp.log(den)
                    lse_rows += [lse[:, :BLOCK], lse[:, BLOCK:]]
                    slab_t = jnp.concatenate([ot[:, :BLOCK], ot[:, BLOCK:]], axis=0)
                    o_sc[c, unit_rows(u), :] = slab_t.T
            lse_t = jnp.concatenate(lse_rows + [jnp.zeros((LANES - Q_HEADS, BLOCK), F32)], axis=0)
            lse_ref[0, unit_rows(u), :] = lse_t.T
            if u >= n_units - dil:
                kprev_sc[u % dil] = t["k_cur"]
                vprev_sc[u % dil] = t["vt_cur"]
                if dil == 1:
                    @pl.when(is_last)
                    def _():
                        st_ref[0, :KW, :] = kt[:, lanes]
                        st_ref[0, KW:, :] = vt[:, lanes]

        prefetch = qd + 1 < n_quads
        pending = None
        for w, u in enumerate(units):
            t = scores(w, u, carried)
            carried[u] = (t["k_cur"], t["vt_cur"])
            if prefetch and w == 0:
                load_x(qd + 1)
            if pending is not None:
                softmax(pending)
            if prefetch:
                project_piece(qd + 1, w)
            if pending is not None:
                values(pending)
            pending = t
        softmax(pending)
        values(pending)

    for c in range(QW // LANES):
        o_ref[0, :, c * LANES:(c + 1) * LANES] = o_sc[c].astype(o_ref.dtype)

    if dil > 1:
        @pl.when(is_last)
        def _():
            st_ref[0, :KW, :] = kv_nat_sc[0].T
            st_ref[0, KW:, :] = kv_nat_sc[1].T


def _group_attn_prompt(x, w_g, sinks, dil, has_sink):
    bn, s, d = x.shape
    span = BLOCK * dil
    tstep = max(STEP_TOKENS, span)
    assert d == D_MODEL and s % tstep == 0 and tstep % (UNITS_PER_PROJ * BLOCK) == 0, (x.shape, dil)
    nstep = s // tstep
    pos = (jnp.arange(s // span)[:, None, None] * span + jnp.arange(dil)[None, :, None]
           + jnp.arange(BLOCK)[None, None, :] * dil).reshape(s)
    half = ROT_DIM // 2
    inv = ROPE_THETA ** (-jnp.arange(half, dtype=F32) / half)
    ang = pos.astype(F32)[:, None] * inv[None, :]
    cos_t, sin_t = jnp.cos(ang).T, jnp.sin(ang).T
    tab_spec = pl.BlockSpec((half, tstep), lambda b, j: (0, j))
    o, lse, st = pl.pallas_call(
        functools.partial(_group_attn_kernel, has_sink=has_sink, dil=dil, tstep=tstep, nstep=nstep),
        grid=(bn, nstep),
        in_specs=([pl.BlockSpec(memory_space=pltpu.SMEM)]
                  + [pl.BlockSpec((1, tstep, LANES), functools.partial(lambda c, b, j: (b, j, c), c))
                     for c in range(X_SLABS)]
                  + [pl.BlockSpec((GW, d), lambda b, j: (0, 0)), tab_spec, tab_spec]),
        out_specs=[
            pl.BlockSpec((1, tstep, QW), lambda b, j: (b, j, 0)),
            pl.BlockSpec((1, tstep, LANES), lambda b, j: (b, j, 0)),
            pl.BlockSpec((1, 2 * KW, span), lambda b, j: (b, 0, 0)),
        ],
        out_shape=[
            jax.ShapeDtypeStruct((bn, s, QW), BF16),
            jax.ShapeDtypeStruct((bn, s, LANES), F32),
            jax.ShapeDtypeStruct((bn, 2 * KW, span), F32),
        ],
        scratch_shapes=[pltpu.VMEM((dil, BLOCK, KW), BF16), pltpu.VMEM((dil, KW, BLOCK), BF16),
                        pltpu.VMEM((2, span, LANES), F32), pltpu.VMEM((QW // LANES, tstep, LANES), F32),
                        pltpu.VMEM((2, 2 * BLOCK, Q_HEADS * BLOCK), F32),
                        pltpu.VMEM((2, 2 * BLOCK, Q_HEADS * BLOCK), BF16),
                        pltpu.VMEM((2, 2 * BLOCK, 2 * BLOCK), F32),
                        pltpu.VMEM((2, UNITS_PER_PROJ * BLOCK, d), BF16),
                        pltpu.VMEM((2, GW, UNITS_PER_PROJ * BLOCK), F32)],
        compiler_params=_cparams(("arbitrary", "arbitrary")),
        name=f"l0_group_attn_d{dil}_{'sink' if has_sink else 'nosink'}",
    )(sinks, *([x] * X_SLABS), w_g.T, cos_t, sin_t)
    st = jnp.transpose(st.reshape(bn, 2, KV_HEADS, HEAD_DIM, span), (0, 4, 1, 2, 3))
    return o.reshape(bn * s, QW), lse.reshape(bn * s, LANES), st


SUB_ROWS = 256


def _l0_out_kernel(x_ref, oa_ref, o0_ref, o1_ref, o2_ref, l0_ref, l1_ref, l2_ref,
                   wz_ref, wo_ref, g_ref, b_ref, y_ref, z_sc, mix_sc):
    tm = x_ref.shape[0]
    n_sub = tm // SUB_ROWS

    def gate_proj(i):
        r = slice(i * SUB_ROWS, (i + 1) * SUB_ROWS)
        z_sc[r, :] = jnp.dot(x_ref[r, :].astype(BF16), wz_ref[...], preferred_element_type=F32)

    def mix_stage(i):
        r = slice(i * SUB_ROWS, (i + 1) * SUB_ROWS)
        l0, l1, l2 = l0_ref[r, :], l1_ref[r, :], l2_ref[r, :]
        m = jnp.maximum(jnp.maximum(l0, l1), l2)
        e0, e1, e2 = jnp.exp(l0 - m), jnp.exp(l1 - m), jnp.exp(l2 - m)
        inv = 1.0 / (e0 + e1 + e2)
        w0, w1, w2 = e0 * inv, e1 * inv, e2 * inv
        parts = []
        for hq in range(Q_HEADS):
            c = slice(hq * HEAD_DIM, (hq + 1) * HEAD_DIM)
            bshape = (SUB_ROWS, HEAD_DIM)
            parts.append(jnp.broadcast_to(w0[:, hq:hq + 1], bshape) * o0_ref[r, c].astype(F32)
                         + jnp.broadcast_to(w1[:, hq:hq + 1], bshape) * o1_ref[r, c].astype(F32)
                         + jnp.broadcast_to(w2[:, hq:hq + 1], bshape) * o2_ref[r, c].astype(F32))
        mix = jnp.concatenate([oa_ref[r, :].astype(F32)] + parts, axis=1)
        z = z_sc[r, :]
        mix_sc[r, :] = (mix * (z * jax.nn.sigmoid(z))).astype(BF16)

    def out_stage(i):
        r = slice(i * SUB_ROWS, (i + 1) * SUB_ROWS)
        y = jnp.dot(mix_sc[r, :], wo_ref[...], preferred_element_type=F32)
        y_ref[r, :] = _layer_norm_rows(DN_ALPHA * x_ref[r, :] + y, g_ref[...], b_ref[...])

    gate_proj(0)
    for i in range(n_sub):
        if i + 1 < n_sub:
            gate_proj(i + 1)
        mix_stage(i)
        out_stage(i)


def _l0_out(x2, oa, obs, lses, wz, wo, g, b):
    n, d = x2.shape
    tm = 512
    assert n % tm == 0 and tm % SUB_ROWS == 0, x2.shape
    row = lambda w: pl.BlockSpec((tm, w), lambda i: (i, 0))
    full = lambda a: pl.BlockSpec(a.shape, lambda i: (0, 0))
    return pl.pallas_call(
        _l0_out_kernel,
        grid=(n // tm,),
        in_specs=[row(d), row(QW), row(QW), row(QW), row(QW), row(LANES), row(LANES), row(LANES),
                  full(wz), full(wo), full(g), full(b)],
        out_specs=row(d),
        out_shape=jax.ShapeDtypeStruct((n, d), F32),
        scratch_shapes=[pltpu.VMEM((tm, MIX), F32), pltpu.VMEM((tm, MIX), BF16)],
        compiler_params=_cparams(("arbitrary",), SMALL_VMEM_LIMIT),
        name="l0_out",
    )(x2, oa, *obs, *lses, wz, wo, g, b)


def _proj_kernel(x_ref, w_ref, cos_ref, slo_ref, shi_ref, h_ref, *, n_rope, n_scaled, tn):
    j = pl.program_id(0)
    h = jnp.dot(x_ref[...].astype(BF16), w_ref[...], preferred_element_type=F32)
    if n_rope:
        cos, slo, shi = cos_ref[...], slo_ref[...], shi_ref[...]
        roped = jnp.concatenate(
            [_rope(h[:, c * LANES:(c + 1) * LANES], cos, slo, shi) for c in range(tn // LANES)], axis=1)
        h = jnp.where(j < n_rope, roped, h)
        h = h * jnp.where(j < n_scaled, HEAD_DIM ** -0.5, 1.0)
    h_ref[...] = h


def _proj(x2, w, tabs, tn, n_rope=0, n_scaled=0, ncol=None):
    n, d = x2.shape
    ncol = w.shape[1] if ncol is None else ncol
    tab_spec = pl.BlockSpec((n, LANES), lambda j: (0, 0))
    return pl.pallas_call(
        functools.partial(_proj_kernel, n_rope=n_rope, n_scaled=n_scaled, tn=tn),
        grid=(ncol // tn,),
        in_specs=[pl.BlockSpec((n, d), lambda j: (0, 0)),
                  pl.BlockSpec((d, tn), lambda j: (0, j)),
                  tab_spec, tab_spec, tab_spec],
        out_specs=pl.BlockSpec((n, tn), lambda j: (0, j)),
        out_shape=jax.ShapeDtypeStruct((n, ncol), F32),
        compiler_params=_cparams(("arbitrary",), SMALL_VMEM_LIMIT),
        name=f"decode_proj_{ncol}",
    )(x2, w, *tabs)


def _dec_prepare(h_ref, rows, g):
    lo_half = lax.broadcasted_iota(jnp.int32, (TPAD, LANES), 1) < HEAD_DIM
    k_base = N_GROUPS * QW
    v_base = k_base + N_GROUPS * KW
    blocks = []
    for c in range(QW // LANES):
        slab = h_ref[rows, g * QW + c * LANES: g * QW + (c + 1) * LANES]
        swapped = pltpu.roll(slab, HEAD_DIM, 1)
        if c < QW // LANES // 2:
            blocks += [jnp.where(lo_half, slab, 0.0), jnp.where(lo_half, swapped, 0.0)]
        else:
            blocks += [jnp.where(lo_half, 0.0, swapped), jnp.where(lo_half, 0.0, slab)]
    qm = jnp.concatenate(blocks, axis=0).astype(BF16)
    k_new = h_ref[rows, k_base + g * KW: k_base + (g + 1) * KW]
    v_new = h_ref[rows, v_base + g * KW: v_base + (g + 1) * KW]
    zpad = jnp.zeros((LANES - TPAD, 2 * KW), F32)
    kv_new = jnp.concatenate([jnp.concatenate([k_new, v_new], axis=1), zpad], axis=0)
    return qm, kv_new


def _dec_scores(qm, kv_new, cache_ref, sc_ref, s_i):
    nkeys = cache_ref.shape[2]
    nt = (((1,), (1,)), ((), ()))
    sc_ref[s_i, :, :nkeys] = jnp.dot(qm, cache_ref[s_i, :KW, :].astype(BF16), preferred_element_type=F32)
    sc_ref[s_i, :, nkeys:] = lax.dot_general(qm, kv_new[:, :KW].astype(BF16), nt, preferred_element_type=F32)


def _dec_shift(kv_new, cache_ref, new_ref, s_i, t_new):
    nkeys = cache_ref.shape[2]
    lane_k = lax.broadcasted_iota(jnp.int32, (KW, LANES), 1)
    new_cols = pltpu.roll(kv_new.T, LANES - t_new, 1)
    for half in range(2):
        hrows = slice(half * KW, (half + 1) * KW)
        shifted = pltpu.roll(cache_ref[s_i, hrows, :], nkeys - t_new, 1)
        new_ref[s_i, hrows, :] = shifted
        new_ref[s_i, hrows, nkeys - LANES:] = jnp.where(lane_k >= LANES - t_new, new_cols[hrows],
                                                        shifted[:, nkeys - LANES:])


def _dec_values(kv_new, cache_ref, sc_ref, s_i, rows, dil, t_new, sink_ref, o_ref, l_ref):
    nrow = Q_HEADS * TPAD
    nkeys = cache_ref.shape[2]
    ncol = nkeys + LANES
    nt = (((1,), (1,)), ((), ()))
    lane8 = lax.broadcasted_iota(jnp.int32, (TPAD, LANES), 1)
    lo_half = lane8 < HEAD_DIM
    tok = lax.broadcasted_iota(jnp.int32, (nrow, ncol), 0) % TPAD
    col = lax.broadcasted_iota(jnp.int32, (nrow, ncol), 1)
    new_j = col - nkeys
    valid = (((col < nkeys) & (((col - tok) & (dil - 1)) == 0) & (col >= tok))
             | ((col >= nkeys) & (((tok - new_j) & (dil - 1)) == 0) & (new_j <= tok) & (new_j < t_new)))
    s_all = jnp.where(valid, sc_ref[s_i], NEG_INF)
    m = jnp.max(s_all, axis=-1, keepdims=True)
    if sink_ref is not None:
        sink = sink_ref[:, 0:1]
        m = jnp.maximum(m, sink)
    p = jnp.exp(s_all - m)
    den = jnp.sum(p, axis=-1, keepdims=True)
    if sink_ref is not None:
        den = den + jnp.exp(sink - m)
    pb = p.astype(BF16)
    o = (lax.dot_general(pb[:, :nkeys], cache_ref[s_i, KW:, :].astype(BF16), nt, preferred_element_type=F32)
         + jnp.dot(pb[:, nkeys:], kv_new[:, KW:].astype(BF16), preferred_element_type=F32)) / den
    for c in range(QW // LANES):
        ev = o[(2 * c) * TPAD:(2 * c + 1) * TPAD]
        od = o[(2 * c + 1) * TPAD:(2 * c + 2) * TPAD]
        if c < QW // LANES // 2:
            slab = jnp.where(lo_half, ev, pltpu.roll(od, HEAD_DIM, 1))
        else:
            slab = jnp.where(lo_half, pltpu.roll(ev, HEAD_DIM, 1), od)
        o_ref[rows, c * LANES:(c + 1) * LANES] = slab.astype(o_ref.dtype)
    if l_ref is not None:
        lse = m + jnp.log(den)
        lse_acc = jnp.zeros((TPAD, LANES), F32)
        for hq in range(Q_HEADS):
            lse_acc = jnp.where(lane8 == hq, lse[hq * TPAD:(hq + 1) * TPAD], lse_acc)
        l_ref[rows, :] = lse_acc


def _cache_rows_minor(c):
    nseq, rows = c.shape[1], c.shape[2]
    return jnp.transpose(c[0], (0, 2, 3, 4, 1)).reshape(nseq, 2 * KW, rows)


def _cache_rows_major(c):
    nseq, _, rows = c.shape
    return jnp.transpose(c.reshape(nseq, 2, KV_HEADS, HEAD_DIM, rows), (0, 4, 1, 2, 3))[None]


def _mlstm_chunk(q, k, v, gates, hh, c_mem, c_norm, m_prev, n_valid):
    L = q.shape[0]
    ig_col = gates[:, hh:hh + 1]
    fg = gates[:, C_HEADS + hh:C_HEADS + hh + 1]
    lf_col = jnp.minimum(fg, 0.0) - jnp.log(1.0 + jnp.exp(-jnp.abs(fg)))
    ri = lax.broadcasted_iota(jnp.int32, (L, L), 0)
    ci = lax.broadcasted_iota(jnp.int32, (L, L), 1)
    tril = ri >= ci
    eye = ri == ci
    lf_row = jnp.sum(jnp.where(eye, lf_col, 0.0), axis=0, keepdims=True)
    ig_row = jnp.sum(jnp.where(eye, ig_col, 0.0), axis=0, keepdims=True)
    b_col = jnp.sum(jnp.where(tril, lf_row, 0.0), axis=1, keepdims=True)
    b_row = jnp.sum(jnp.where(ri <= ci, lf_col, 0.0), axis=0, keepdims=True)
    dmat = jnp.where(tril, b_col - b_row + ig_row, NEG_INF)
    inter = b_col + m_prev
    m = jnp.maximum(inter, jnp.max(dmat, axis=1, keepdims=True))
    pw = jnp.exp(dmat - m)
    qb, kb, vb = q.astype(BF16), k.astype(BF16), v.astype(BF16)
    sc = lax.dot_general(qb, kb, (((1,), (1,)), ((), ())), preferred_element_type=F32) * pw
    carry_w = jnp.exp(inter - m)
    num = (jnp.dot(sc.astype(BF16), vb, preferred_element_type=F32)
           + carry_w * jnp.dot(qb, c_mem.astype(BF16), preferred_element_type=F32))
    den = jnp.sum(sc, axis=1, keepdims=True) + carry_w * jnp.sum(q * c_norm, axis=1, keepdims=True)
    h = num / jnp.maximum(jnp.abs(den), jnp.exp(-m))
    last = n_valid - 1
    m_new = m[last:last + 1, :]
    b_last = b_col[last:last + 1, :]
    ws = jnp.exp(b_last - b_col + ig_col - m_new)
    if n_valid < L:
        ws = jnp.where(lax.broadcasted_iota(jnp.int32, (L, 1), 0) < n_valid, ws, 0.0)
    decay = jnp.exp(b_last + m_prev - m_new)
    wk = ws * k
    c_mem_new = decay * c_mem + lax.dot_general(
        wk.astype(BF16), vb, (((0,), (0,)), ((), ())), preferred_element_type=F32)
    c_norm_new = decay * c_norm + jnp.sum(wk, axis=0, keepdims=True)
    return h, c_mem_new, c_norm_new, m_new


AUG = LANES
PROJ_COLS = 256


def _l1_prompt_kernel(x_ref, w_ref, wg_ref, bias_ref, sink_ref, hs_ref, ca_ref, c0_ref, c1_ref, c2_ref,
                      h_ref, cmem_ref, cnorm_ref, cmax_ref,
                      oa_ref, o0_ref, o1_ref, o2_ref, l0_ref, l1_ref, l2_ref, na_ref, n0_ref, n1_ref, n2_ref,
                      caug_sc, m_sc, s_sc, lhs_sc, proj_a, proj_b, xb_sc, sa_sc, s0_sc, s1_sc, s2_sc,
                      *, tm, nblk, dec_sb, t_new):
    i = pl.program_id(0)
    blk = i - 1

    @pl.when(i == 0)
    def _():
        proj_b[...] = jnp.zeros_like(proj_b)

    @pl.when(blk % nblk == 0)
    def _():
        caug_sc[...] = jnp.zeros_like(caug_sc)
        m_sc[...] = jnp.zeros_like(m_sc)

    ri = lax.broadcasted_iota(jnp.int32, (BLOCK, BLOCK), 0)
    ci = lax.broadcasted_iota(jnp.int32, (BLOCK, BLOCK), 1)
    tril = ri >= ci
    upper = (ri <= ci).astype(F32)
    ones_blk = jnp.ones((BLOCK, AUG), BF16)
    kscale = C_HEAD_DIM ** -0.5
    nt = (((1,), (1,)), ((), ()))
    ncol = w_ref.shape[1]
    slabs = [(c0, c0 + PROJ_COLS) for c0 in range(0, ncol, PROJ_COLS)] + [(ncol, ncol + LANES)]
    n_chunks = tm // BLOCK

    def body(src, dst):
        xb_sc[...] = x_ref[...].astype(BF16)
        pieces = iter(slabs)

        def project_piece():
            c = next(pieces, None)
            if c is not None:
                w_piece = w_ref[:, c[0]:c[1]] if c[0] < ncol else wg_ref[...]
                dst[:, c[0]:c[1]] = jnp.dot(xb_sc[...], w_piece, preferred_element_type=F32)

        g_t = (src[:, 3 * MIX:] + bias_ref[...]).T[0:TPAD]
        lf_t = jnp.minimum(g_t, 0.0) - jnp.log(1.0 + jnp.exp(-jnp.abs(g_t)))

        def score_stage(c):
            rows = slice(c * BLOCK, (c + 1) * BLOCK)
            qs, ks, vs = [], [], []
            for hh in range(C_HEADS):
                q = src[rows, hh * C_HEAD_DIM:(hh + 1) * C_HEAD_DIM]
                k = src[rows, MIX + hh * C_HEAD_DIM: MIX + (hh + 1) * C_HEAD_DIM] * kscale
                v = src[rows, 2 * MIX + hh * C_HEAD_DIM: 2 * MIX + (hh + 1) * C_HEAD_DIM]
                qs.append(q)
                ks.append(k)
                vs.append(v.astype(BF16))
                s_sc[c % 2, hh] = lax.dot_general(q.astype(BF16), k.astype(BF16), nt, preferred_element_type=F32)
            return dict(rows=rows, par=c % 2, qs=qs, ks=ks, vs=vs)

        def gate_stage(t, m_prevs):
            rows, par = t["rows"], t["par"]
            b_t = jnp.dot(lf_t[:, rows], upper, preferred_element_type=F32, precision=lax.Precision.HIGHEST)
            stats = []
            for hh in range(C_HEADS):
                ig_row = g_t[hh:hh + 1, rows]
                lf_row = lf_t[C_HEADS + hh:C_HEADS + hh + 1, rows]
                b_row = b_t[C_HEADS + hh:C_HEADS + hh + 1, :]
                m_prev = m_prevs[hh]
                b_col = jnp.sum(jnp.where(tril, lf_row, 0.0), axis=1, keepdims=True)
                dmat = jnp.where(tril, b_col - (b_row - ig_row), NEG_INF)
                inter = b_col + m_prev
                m = jnp.maximum(inter, jnp.max(dmat, axis=1, keepdims=True))
                sc = s_sc[par, hh] * jnp.exp(dmat - m)
                carry_w = jnp.exp(inter - m)
                lhs_sc[par, hh, :, :BLOCK] = sc.astype(BF16)
                lhs_sc[par, hh, :, BLOCK:] = (carry_w * t["qs"][hh]).astype(BF16)
                m_new = m[BLOCK - 1:BLOCK, :]
                b_last = b_col[BLOCK - 1:BLOCK, :]
                ws_row = jnp.exp(b_last - b_row + ig_row - m_new)
                decay = jnp.exp(b_last + m_prev - m_new)
                stats.append((m, m_new, ws_row, decay))
            t["stats"] = stats

        def memory_stage(t):
            rows, par = t["rows"], t["par"]
            for hh in range(C_HEADS):
                cols = slice(hh * C_HEAD_DIM, (hh + 1) * C_HEAD_DIM)
                m, m_new, ws_row, decay = t["stats"][hh]
                caug = caug_sc[hh]
                v_aug = jnp.concatenate([t["vs"][hh], ones_blk], axis=1)
                rhs = jnp.concatenate([v_aug, caug.astype(BF16)], axis=0)
                res = jnp.dot(lhs_sc[par, hh], rhs, preferred_element_type=F32)
                den = jnp.maximum(jnp.abs(res[:, C_HEAD_DIM:]), jnp.exp(-m))
                h_ref[rows, cols] = res[:, :C_HEAD_DIM] / jnp.concatenate([den, den], axis=1)
                wk_t = (t["ks"][hh].T * ws_row).astype(BF16)
                caug_sc[hh] = decay * caug + jnp.dot(wk_t, v_aug, preferred_element_type=F32)

        m_prevs = [m_sc[hh:hh + 1, 0:1] for hh in range(C_HEADS)]
        ts = [score_stage(0), score_stage(1)]
        caches = (ca_ref, c0_ref, c1_ref, c2_ref)
        new_caches = (na_ref, n0_ref, n1_ref, n2_ref)
        o_refs = (oa_ref, o0_ref, o1_ref, o2_ref)
        l_refs = (None, l0_ref, l1_ref, l2_ref)
        sc_refs = (sa_sc, s0_sc, s1_sc, s2_sc)
        dec = []
        for g in range(N_GROUPS):
            for s_i in range(dec_sb):
                rows = slice(s_i * TPAD, (s_i + 1) * TPAD)
                qm, kv_new = _dec_prepare(hs_ref, rows, g)
                _dec_scores(qm, kv_new, caches[g], sc_refs[g], s_i)
                dec.append((g, s_i, rows, kv_new))
        n_dec = len(dec)
        shifts = iter(dec)

        def shift_items(k):
            for _ in range(k):
                d = next(shifts, None)
                if d is not None:
                    _dec_shift(d[3], caches[d[0]], new_caches[d[0]], d[1], t_new)
                    project_piece()

        project_piece()
        for c in range(n_chunks):
            gate_stage(ts[c], m_prevs)
            m_prevs = [st[1] for st in ts[c]["stats"]]
            project_piece()
            if c + 2 < n_chunks:
                ts.append(score_stage(c + 2))
                project_piece()
            shift_items(n_dec // (2 * n_chunks))
            if c >= 1:
                memory_stage(ts[c - 1])
                project_piece()
        memory_stage(ts[n_chunks - 1])
        project_piece()
        shift_items(n_dec)
        for g, s_i, rows, kv_new in dec:
            _dec_values(kv_new, caches[g], sc_refs[g], s_i, rows, DILATIONS[g], t_new,
                        sink_ref if g == 0 else None, o_refs[g], l_refs[g])
            project_piece()
        for _ in slabs:
            project_piece()
        for hh in range(C_HEADS):
            m_sc[hh:hh + 1, :] = jnp.broadcast_to(m_prevs[hh], (1, LANES))

    @pl.when(i % 2 == 0)
    def _():
        body(proj_b, proj_a)

    @pl.when(i % 2 == 1)
    def _():
        body(proj_a, proj_b)

    @pl.when((blk >= 0) & (blk % nblk == nblk - 1))
    def _():
        for hh in range(C_HEADS):
            cmem_ref[0, hh] = caug_sc[hh, :, :C_HEAD_DIM]
            cnorm_ref[0, hh:hh + 1, :] = caug_sc[hh, :, C_HEAD_DIM:].T[0:1, :]
        cmax_ref[0] = m_sc[...]


def _l1_prompt(x, w1b, w_gates, bias_row, hs, caches, sink_rows, t_new):
    bn, s, d = x.shape
    cts = [_cache_rows_minor(c) for c in caches]
    nseq, ng = cts[0].shape[0], len(cts)
    dec_sb = 2
    ndec = nseq // dec_sb
    tm = (bn * s) // ndec
    assert nseq % dec_sb == 0 and tm * ndec == bn * s and tm % (2 * BLOCK) == 0 and s % tm == 0, (x.shape, nseq)
    nblk = s // tm
    ntot = bn * nblk
    ncol = 3 * MIX + LANES
    state_blk = lambda i: (jnp.maximum(i - 1, 0) // nblk, 0, 0)
    dec_blk = lambda i: jnp.minimum(i, ndec - 1)
    drow = lambda w: pl.BlockSpec((dec_sb * TPAD, w), lambda i: (dec_blk(i), 0))
    cspec = lambda c: pl.BlockSpec((dec_sb,) + c.shape[1:], lambda i: (dec_blk(i), 0, 0))
    outs = pl.pallas_call(
        functools.partial(_l1_prompt_kernel, tm=tm, nblk=nblk, dec_sb=dec_sb, t_new=t_new),
        grid=(ntot + 1,),
        in_specs=[pl.BlockSpec((tm, d), lambda i: (jnp.minimum(i, ntot - 1), 0)),
                  pl.BlockSpec((d, 3 * MIX), lambda i: (0, 0)),
                  pl.BlockSpec((d, LANES), lambda i: (0, 0)),
                  pl.BlockSpec((1, LANES), lambda i: (0, 0)),
                  pl.BlockSpec(sink_rows.shape, lambda i: (0, 0)),
                  drow(hs.shape[1])] + [cspec(c) for c in cts],
        out_specs=([pl.BlockSpec((tm, MIX), lambda i: (jnp.maximum(i - 1, 0), 0)),
                    pl.BlockSpec((1, C_HEADS, C_HEAD_DIM, C_HEAD_DIM), lambda i: state_blk(i) + (0,)),
                    pl.BlockSpec((1, C_HEADS, C_HEAD_DIM), state_blk),
                    pl.BlockSpec((1, TPAD, LANES), state_blk)]
                   + [drow(QW)] * ng + [drow(LANES)] * (ng - 1) + [cspec(c) for c in cts]),
        out_shape=([jax.ShapeDtypeStruct((bn * s, MIX), F32),
                    jax.ShapeDtypeStruct((bn, C_HEADS, C_HEAD_DIM, C_HEAD_DIM), F32),
                    jax.ShapeDtypeStruct((bn, C_HEADS, C_HEAD_DIM), F32),
                    jax.ShapeDtypeStruct((bn, TPAD, LANES), F32)]
                   + [jax.ShapeDtypeStruct((nseq * TPAD, QW), BF16)] * ng
                   + [jax.ShapeDtypeStruct((nseq * TPAD, LANES), F32)] * (ng - 1)
                   + [jax.ShapeDtypeStruct(c.shape, F32) for c in cts]),
        scratch_shapes=([pltpu.VMEM((C_HEADS, C_HEAD_DIM, C_HEAD_DIM + AUG), F32),
                         pltpu.VMEM((TPAD, LANES), F32),
                         pltpu.VMEM((2, C_HEADS, BLOCK, BLOCK), F32),
                         pltpu.VMEM((2, C_HEADS, BLOCK, BLOCK + C_HEAD_DIM), BF16),
                         pltpu.VMEM((tm, ncol), F32), pltpu.VMEM((tm, ncol), F32),
                         pltpu.VMEM((tm, d), BF16)]
                        + [pltpu.VMEM((dec_sb, Q_HEADS * TPAD, c.shape[2] + LANES), F32) for c in cts]),
        compiler_params=_cparams(("arbitrary",)),
        name="l1_prompt_mlstm_dec",
    )(x.reshape(bn * s, d), w1b, w_gates, bias_row, sink_rows, hs, *cts)
    return (outs[:4], outs[4:4 + ng], outs[4 + ng:3 + 2 * ng],
            [_cache_rows_major(c) for c in outs[3 + 2 * ng:]])


def _l1_out_kernel(x_ref, h_ref, wog_ref, wz_ref, mhg_ref, wo_ref, g_ref, b_ref, *refs, dec_sb, t_new):
    if dec_sb:
        (p_ref, pg_ref, bias_ref, cm_ref, cn_ref, cx_ref, y_ref, hd_ref, cmo_ref, cno_ref, cxo_ref,
         oz_sc, mix_sc) = refs
    else:
        y_ref, oz_sc, mix_sc = refs
    tm = x_ref.shape[0]
    n_sub = tm // SUB_ROWS

    def gate_proj(i):
        r = slice(i * SUB_ROWS, (i + 1) * SUB_ROWS)
        xb = x_ref[r, :].astype(BF16)
        oz_sc[r, :MIX] = jnp.dot(xb, wog_ref[...], preferred_element_type=F32)
        oz_sc[r, MIX:] = jnp.dot(xb, wz_ref[...], preferred_element_type=F32)

    def mix_stage(i, between):
        r = slice(i * SUB_ROWS, (i + 1) * SUB_ROWS)
        for hh in range(C_HEADS):
            cols = slice(hh * C_HEAD_DIM, (hh + 1) * C_HEAD_DIM)
            hg = jax.nn.sigmoid(oz_sc[r, cols]) * h_ref[r, cols]
            mu = jnp.mean(hg, axis=-1, keepdims=True)
            c = hg - mu
            var = jnp.mean(c * c, axis=-1, keepdims=True)
            hn = c * lax.rsqrt(var + MH_EPS) * mhg_ref[:, cols]
            z = oz_sc[r, MIX + hh * C_HEAD_DIM:MIX + (hh + 1) * C_HEAD_DIM]
            mix_sc[r, cols] = (hn * (z * jax.nn.sigmoid(z))).astype(BF16)
            between()

    def out_stage(i):
        r = slice(i * SUB_ROWS, (i + 1) * SUB_ROWS)
        y = jnp.dot(mix_sc[r, :], wo_ref[...], preferred_element_type=F32)
        y_ref[r, :] = _layer_norm_rows(DN_ALPHA * x_ref[r, :] + y, g_ref[...], b_ref[...])

    kscale = C_HEAD_DIM ** -0.5
    chunks = [(s_i, hh) for s_i in range(dec_sb) for hh in range(C_HEADS)]
    todo = iter(chunks)

    def decode_chunks(k):
        for _ in range(k):
            it = next(todo, None)
            if it is None:
                return
            s_i, hh = it
            rows = slice(s_i * TPAD, (s_i + 1) * TPAD)
            cols = slice(hh * C_HEAD_DIM, (hh + 1) * C_HEAD_DIM)
            gates = pg_ref[rows, :] + bias_ref[...]
            q = p_ref[rows, cols]
            k_ = p_ref[rows, MIX + hh * C_HEAD_DIM: MIX + (hh + 1) * C_HEAD_DIM] * kscale
            v = p_ref[rows, 2 * MIX + hh * C_HEAD_DIM: 2 * MIX + (hh + 1) * C_HEAD_DIM]
            h, c_new, n_new, m_new = _mlstm_chunk(
                q, k_, v, gates, hh, cm_ref[s_i, hh], cn_ref[s_i, hh:hh + 1, :],
                cx_ref[s_i, hh:hh + 1, 0:1], t_new)
            hd_ref[rows, cols] = h
            cmo_ref[s_i, hh] = c_new
            cno_ref[s_i, hh:hh + 1, :] = n_new
            cxo_ref[s_i, hh:hh + 1, :] = jnp.broadcast_to(m_new, (1, LANES))

    per_slot = -(-len(chunks) // (n_sub * (C_HEADS + 1)))
    gate_proj(0)
    for i in range(n_sub):
        if i + 1 < n_sub:
            gate_proj(i + 1)
        mix_stage(i, lambda: decode_chunks(per_slot))
        out_stage(i)
        decode_chunks(per_slot)
    decode_chunks(len(chunks))


def _l1_out(x2, h2, w1b, mhg, wo, g, b, dec=None):
    n, d = x2.shape
    tm = 512
    row = lambda w: pl.BlockSpec((tm, w), lambda i: (i, 0))
    full = lambda a: pl.BlockSpec(a.shape, lambda i: (0, 0))
    assert n % tm == 0, x2.shape
    wcol = lambda j: pl.BlockSpec((d, MIX), lambda i: (0, j))
    og_blk, z_blk = 3, 4
    in_specs = [row(d), row(MIX), wcol(og_blk), wcol(z_blk), full(mhg), full(wo), full(g), full(b)]
    out_specs = [row(d)]
    out_shape = [jax.ShapeDtypeStruct((n, d), F32)]
    args = [x2, h2, w1b, w1b, mhg, wo, g, b]
    dec_sb, t_new = 0, 0
    if dec is not None:
        proj, proj_g, bias_row, c_mem, c_norm, c_max_pad, t_new = dec
        nseq = c_mem.shape[0]
        dec_sb = nseq // (n // tm)
        assert dec_sb * (n // tm) == nseq, (nseq, n, tm)
        sblk = lambda a: pl.BlockSpec((dec_sb,) + a.shape[1:], lambda i: (i,) + (0,) * (a.ndim - 1))
        in_specs += [pl.BlockSpec((dec_sb * TPAD, proj.shape[1]), lambda i: (i, 0)),
                     pl.BlockSpec((dec_sb * TPAD, LANES), lambda i: (i, 0)), full(bias_row),
                     sblk(c_mem), sblk(c_norm), sblk(c_max_pad)]
        out_specs += [pl.BlockSpec((dec_sb * TPAD, MIX), lambda i: (i, 0)), sblk(c_mem), sblk(c_norm), sblk(c_max_pad)]
        out_shape += [jax.ShapeDtypeStruct((nseq * TPAD, MIX), F32), jax.ShapeDtypeStruct(c_mem.shape, F32),
                      jax.ShapeDtypeStruct(c_norm.shape, F32), jax.ShapeDtypeStruct(c_max_pad.shape, F32)]
        args += [proj, proj_g, bias_row, c_mem, c_norm, c_max_pad]
    outs = pl.pallas_call(
        functools.partial(_l1_out_kernel, dec_sb=dec_sb, t_new=t_new),
        grid=(n // tm,),
        in_specs=in_specs,
        out_specs=out_specs,
        out_shape=out_shape,
        scratch_shapes=[pltpu.VMEM((tm, 2 * MIX), F32), pltpu.VMEM((tm, MIX), BF16)],
        compiler_params=_cparams(("arbitrary",), VMEM_LIMIT if dec is not None else SMALL_VMEM_LIMIT),
        name="l1_out_dec" if dec is not None else "l1_out",
    )(*args)
    return outs[0] if dec is None else outs


def _group_cols(g):
    if g == 0:
        return [(0, QW), (QW, QW + KW), (QW + KW, GW)]
    b = g - 1
    qb0 = GW
    kb0 = GW + 3 * QW
    vb0 = kb0 + 3 * KW
    return [(qb0 + b * QW, qb0 + (b + 1) * QW), (kb0 + b * KW, kb0 + (b + 1) * KW),
            (vb0 + b * KW, vb0 + (b + 1) * KW)]


def kernel(x_prompt, x_sample, cache_a_kv, cache_b0_kv, cache_b1_kv, cache_b2_kv, state_c_mem,
           state_c_norm, state_c_max, w_in0, sinks0, w_out0, w_in1, b_gates1, mh_norm1, w_out1,
           ln_g, ln_b):
    bn, s, d = x_prompt.shape
    nseq, t_new, _ = x_sample.shape
    w0 = w_in0[0]
    z0 = 4 * GW
    w_groups = [jnp.concatenate([w0[:, a:b] for a, b in _group_cols(g)], axis=1).astype(BF16)
                for g in range(N_GROUPS)]
    wz0 = w0[:, z0:].astype(BF16)
    wo0 = w_out0[0].astype(BF16)
    sinks = sinks0[0].astype(F32)
    g0, b0 = ln_g[0][None, :], ln_b[0][None, :]
    g1, b1 = ln_g[1][None, :], ln_b[1][None, :]

    outs, lses, states_p = [], [], []
    for g in range(N_GROUPS):
        o, lse, st = _group_attn_prompt(x_prompt, w_groups[g], sinks, DILATIONS[g], g == 0)
        outs.append(o)
        lses.append(lse)
        states_p.append(st)
    xp2 = x_prompt.reshape(bn * s, d)
    y0p = _l0_out(xp2, outs[0], outs[1:], lses[1:], wz0, wo0, g0, b0)

    xs2 = jnp.pad(x_sample, ((0, 0), (0, TPAD - t_new), (0, 0))).reshape(nseq * TPAD, d)
    w_dec = jnp.concatenate(
        [w_groups[g][:, :QW] for g in range(N_GROUPS)] + [w_groups[g][:, QW:QW + KW] for g in range(N_GROUPS)]
        + [w_groups[g][:, QW + KW:] for g in range(N_GROUPS)], axis=1)
    tabs_s = tuple(jnp.tile(t, (nseq, 1)) for t in _rope_tables(PAST_LEN + jnp.arange(TPAD)))
    tn = 512
    hs = _proj(xs2, w_dec, tabs_s, tn, n_rope=(N_GROUPS * (QW + KW)) // tn, n_scaled=(N_GROUPS * QW) // tn)
    sink_rows = jnp.broadcast_to(jnp.repeat(sinks, TPAD)[:, None], (Q_HEADS * TPAD, LANES))

    w1b = w_in1[0].astype(BF16)
    w_gates = jnp.pad(w1b[:, 5 * MIX:], ((0, 0), (0, LANES - 2 * C_HEADS)))
    wo1 = w_out1[0].astype(BF16)
    bias_row = jnp.pad(b_gates1[0].astype(F32), (0, LANES - 2 * C_HEADS))[None, :]
    mhg = mh_norm1[0][None, :]

    (hp, cmem_p, cnorm_p, cmax_p), dec_o, dec_l, states_s = _l1_prompt(
        y0p.reshape(bn, s, d), w1b, w_gates, bias_row, hs,
        (cache_a_kv, cache_b0_kv, cache_b1_kv, cache_b2_kv), sink_rows, t_new)

    y0s = _l0_out(xs2, dec_o[0], dec_o[1:], dec_l, wz0, wo0, g0, b0)

    no_tabs = tuple(jnp.zeros((nseq * TPAD, LANES), F32) for _ in range(3))
    ps = _proj(y0s, w1b, no_tabs, 512, ncol=3 * MIX)
    ps_g = _proj(y0s, w_gates, no_tabs, LANES)
    cmax_pad = jnp.broadcast_to(state_c_max[0][:, :, None], (nseq, C_HEADS, LANES))
    y1p, hsd, cmem_s, cnorm_s, cmax_s = _l1_out(
        y0p, hp, w1b, mhg, wo1, g1, b1,
        dec=(ps, ps_g, bias_row, state_c_mem[0], state_c_norm[0], cmax_pad, t_new))
    y1s = _l1_out(y0s, hsd, w1b, mhg, wo1, g1, b1)

    y_prompt = y1p.reshape(bn, s, d)
    y_sample = y1s.reshape(nseq, TPAD, d)[:, :t_new]
    return (y_prompt, y_sample,
            states_p[0][None], states_p[1][None], states_p[2][None], states_p[3][None],
            cmem_p[None], cnorm_p[None], cmax_p[:, :C_HEADS, 0][None],
            states_s[0], states_s[1], states_s[2], states_s[3],
            cmem_s[None], cnorm_s[None], cmax_s[:, :, 0][None])
```
